```python
import jax, jax.numpy as jnp
from jax import lax
import numpy as np

D_MODEL = 1024
BATCH = 8
SEQ = 2048
DEPTH = 4

N_MIXERS = 3
N_SUB = 3
FFN_RES_WEIGHT = 0.5
D_FF = 2816
RMS_EPS = 1e-6
FOX_HEADS = 16
FOX_HEAD_DIM = D_MODEL // FOX_HEADS
FOX_BLOCK = 128
SCONV_WIDTH = 3
LRU_WIDTH = D_MODEL
LRU_BLOCKS = 16
LRU_BLOCK_DIM = LRU_WIDTH // LRU_BLOCKS
LRU_CONV_WIDTH = 4
LRU_C = 8.0
N_FOX = len(range(0, DEPTH, N_MIXERS))
N_SCONV = len(range(1, DEPTH, N_MIXERS))
N_LRU = len(range(2, DEPTH, N_MIXERS))

kernel_name = "hybrid_fox_shortconv_rglru_macaron"


def rmsnorm(x, g):
    x32 = x.astype(jnp.float32)
    y = x32 * lax.rsqrt(jnp.mean(x32 * x32, axis=-1, keepdims=True) + RMS_EPS)
    return y.astype(x.dtype) * g


def causal_depthwise_conv(u, w, b=None):
    k_w, ch = w.shape
    out = lax.conv_general_dilated(
        u, w[:, None, :].astype(u.dtype), window_strides=(1,),
        padding=[(k_w - 1, 0)], dimension_numbers=("NWC", "WIO", "NWC"),
        feature_group_count=ch)
    if b is not None:
        out = out + b
    return out


def swiglu(h, w_in, w_out):
    g, u = jnp.split(h @ w_in, 2, axis=-1)
    return (jax.nn.silu(g) * u) @ w_out


def fox_mixer(h, w_in, b_f, w_out):
    bsz, seq, _ = h.shape
    proj = h @ w_in
    q, k, v, f_logit = jnp.split(proj, [D_MODEL, 2 * D_MODEL, 3 * D_MODEL], axis=-1)
    q = q.reshape(bsz, seq, FOX_HEADS, FOX_HEAD_DIM)
    k = k.reshape(bsz, seq, FOX_HEADS, FOX_HEAD_DIM)
    v = v.reshape(bsz, seq, FOX_HEADS, FOX_HEAD_DIM)
    log_f = jax.nn.log_sigmoid((f_logit + b_f).astype(jnp.float32))
    cum = jnp.cumsum(log_f, axis=1).transpose(0, 2, 1)
    scale = FOX_HEAD_DIM ** -0.5
    outs = []
    for blk in range(seq // FOX_BLOCK):
        s0 = blk * FOX_BLOCK
        s1 = s0 + FOX_BLOCK
        logits = jnp.einsum("bqhd,bkhd->bhqk", q[:, s0:s1], k[:, :s1]).astype(jnp.float32) * scale
        logits = logits + cum[:, :, s0:s1, None] - cum[:, :, None, :s1]
        q_pos = jnp.arange(s0, s1)[:, None]
        k_pos = jnp.arange(s1)[None, :]
        logits = jnp.where(k_pos <= q_pos, logits, -jnp.inf)
        p = jax.nn.softmax(logits, axis=-1).astype(v.dtype)
        outs.append(jnp.einsum("bhqk,bkhd->bqhd", p, v[:, :s1]))
    o = jnp.concatenate(outs, axis=1).reshape(bsz, seq, D_MODEL)
    return o @ w_out


def sconv_mixer(h, w_in, conv_w, w_out):
    b_gate, c_gate, xv = jnp.split(h @ w_in, 3, axis=-1)
    y = b_gate * causal_depthwise_conv(c_gate * xv, conv_w)
    return y @ w_out


def lru_mixer(h, w_in, conv_w, conv_b, w_a, b_a, w_x, b_x, lam, w_out):
    bsz, seq, _ = h.shape
    gate, xb = jnp.split(h @ w_in, 2, axis=-1)
    xb = causal_depthwise_conv(xb, conv_w, conv_b)
    xh = xb.reshape(bsz, seq, LRU_BLOCKS, LRU_BLOCK_DIM)
    r = jax.nn.sigmoid(jnp.einsum("bsni,nij->bsnj", xh, w_a) + b_a).reshape(bsz, seq, LRU_WIDTH)
    i = jax.nn.sigmoid(jnp.einsum("bsni,nij->bsnj", xh, w_x) + b_x).reshape(bsz, seq, LRU_WIDTH)
    log_a = -LRU_C * r.astype(jnp.float32) * jax.nn.softplus(-lam.astype(jnp.float32))
    a = jnp.exp(log_a)
    mult = jnp.sqrt(-jnp.expm1(2.0 * log_a))
    b_term = mult * (i * xb).astype(jnp.float32)

    def combine(left, right):
        a1, b1 = left
        a2, b2 = right
        return a1 * a2, a2 * b1 + b2

    _, hs = lax.associative_scan(combine, (a, b_term), axis=1)
    y = hs.astype(h.dtype) * jax.nn.gelu(gate)
    return y @ w_out


def _fwd_setup_inputs(seed: int = 0) -> dict:
    key = jax.random.key(seed)
    ks = jax.random.split(key, 24)
    f32 = jnp.float32

    def nrm(k, shape, fan_in):
        return jax.random.normal(k, shape, f32) * (fan_in ** -0.5)

    x = jax.random.normal(ks[0], (BATCH, SEQ, D_MODEL), f32)
    c = jax.random.normal(ks[1], (BATCH, D_MODEL), f32)
    w_cond = nrm(ks[2], (DEPTH, D_MODEL, N_SUB * 3 * D_MODEL), D_MODEL)
    b_cond = 0.02 * jax.random.normal(ks[3], (DEPTH, N_SUB * 3 * D_MODEL), f32)
    norm_pre = 1.0 + 0.05 * jax.random.normal(ks[4], (DEPTH, N_SUB, D_MODEL), f32)
    norm_post = 1.0 + 0.05 * jax.random.normal(ks[5], (DEPTH, N_SUB, D_MODEL), f32)
    w_ffn_in = nrm(ks[6], (DEPTH, 2, D_MODEL, 2 * D_FF), D_MODEL)
    w_ffn_out = nrm(ks[7], (DEPTH, 2, D_FF, D_MODEL), D_FF)
    fox_w_in = nrm(ks[8], (N_FOX, D_MODEL, 3 * D_MODEL + FOX_HEADS), D_MODEL)
    fox_b_f = jax.random.uniform(ks[9], (N_FOX, FOX_HEADS), f32, 1.0, 4.0)
    fox_w_out = nrm(ks[10], (N_FOX, D_MODEL, D_MODEL), D_MODEL)
    sconv_w_in = nrm(ks[11], (N_SCONV, D_MODEL, 3 * D_MODEL), D_MODEL)
    sconv_conv_w = nrm(ks[12], (N_SCONV, SCONV_WIDTH, D_MODEL), SCONV_WIDTH)
    sconv_w_out = nrm(ks[13], (N_SCONV, D_MODEL, D_MODEL), D_MODEL)
    lru_w_in = nrm(ks[14], (N_LRU, D_MODEL, 2 * LRU_WIDTH), D_MODEL)
    lru_conv_w = nrm(ks[15], (N_LRU, LRU_CONV_WIDTH, LRU_WIDTH), LRU_CONV_WIDTH)
    lru_conv_b = 0.02 * jax.random.normal(ks[16], (N_LRU, LRU_WIDTH), f32)
    lru_w_a = nrm(ks[17], (N_LRU, LRU_BLOCKS, LRU_BLOCK_DIM, LRU_BLOCK_DIM), LRU_BLOCK_DIM)
    lru_b_a = 0.02 * jax.random.normal(ks[18], (N_LRU, LRU_BLOCKS, LRU_BLOCK_DIM), f32)
    lru_w_x = nrm(ks[19], (N_LRU, LRU_BLOCKS, LRU_BLOCK_DIM, LRU_BLOCK_DIM), LRU_BLOCK_DIM)
    lru_b_x = 0.02 * jax.random.normal(ks[20], (N_LRU, LRU_BLOCKS, LRU_BLOCK_DIM), f32)
    a_c = jax.random.uniform(ks[21], (N_LRU, LRU_WIDTH), f32, 0.9, 0.999)
    s = a_c ** (1.0 / LRU_C)
    lru_lambda = jnp.log(s) - jnp.log1p(-s)
    lru_w_out = nrm(ks[22], (N_LRU, LRU_WIDTH, D_MODEL), LRU_WIDTH)
    return {
        "x": x, "c": c, "w_cond": w_cond, "b_cond": b_cond,
        "norm_pre": norm_pre, "norm_post": norm_post,
        "w_ffn_in": w_ffn_in, "w_ffn_out": w_ffn_out,
        "fox_w_in": fox_w_in, "fox_b_f": fox_b_f, "fox_w_out": fox_w_out,
        "sconv_w_in": sconv_w_in, "sconv_conv_w": sconv_conv_w, "sconv_w_out": sconv_w_out,
        "lru_w_in": lru_w_in, "lru_conv_w": lru_conv_w, "lru_conv_b": lru_conv_b,
        "lru_w_a": lru_w_a, "lru_b_a": lru_b_a, "lru_w_x": lru_w_x, "lru_b_x": lru_b_x,
        "lru_lambda": lru_lambda, "lru_w_out": lru_w_out,
    }


def _fwd_reference(x, c, w_cond, b_cond, norm_pre, norm_post, w_ffn_in, w_ffn_out,
              fox_w_in, fox_b_f, fox_w_out, sconv_w_in, sconv_conv_w, sconv_w_out,
              lru_w_in, lru_conv_w, lru_conv_b, lru_w_a, lru_b_a, lru_w_x, lru_b_x,
              lru_lambda, lru_w_out):
    bsz = x.shape[0]
    c_act = jax.nn.silu(c)
    for i in range(DEPTH):
        mod = (c_act @ w_cond[i] + b_cond[i]).reshape(bsz, N_SUB, 3, D_MODEL)
        shift = mod[:, :, 0, None, :]
        scale = mod[:, :, 1, None, :]
        gate = mod[:, :, 2, None, :]

        def pre(h, s):
            return rmsnorm(h, norm_pre[i, s]) * (1.0 + scale[:, s]) + shift[:, s]

        y = swiglu(pre(x, 0), w_ffn_in[i, 0], w_ffn_out[i, 0])
        x = x + FFN_RES_WEIGHT * gate[:, 0] * rmsnorm(y, norm_post[i, 0])

        h = pre(x, 1)
        kind = i % N_MIXERS
        j = i // N_MIXERS
        if kind == 0:
            y = fox_mixer(h, fox_w_in[j], fox_b_f[j], fox_w_out[j])
        elif kind == 1:
            y = sconv_mixer(h, sconv_w_in[j], sconv_conv_w[j], sconv_w_out[j])
        else:
            y = lru_mixer(h, lru_w_in[j], lru_conv_w[j], lru_conv_b[j], lru_w_a[j], lru_b_a[j],
                          lru_w_x[j], lru_b_x[j], lru_lambda[j], lru_w_out[j])
        x = x + gate[:, 1] * rmsnorm(y, norm_post[i, 1])

        y = swiglu(pre(x, 2), w_ffn_in[i, 1], w_ffn_out[i, 1])
        x = x + FFN_RES_WEIGHT * gate[:, 2] * rmsnorm(y, norm_post[i, 2])
    return x


import jax as _jax
import jax.numpy as _jnp

TWIN_FORMAT = 'train_step'
FWD_PARAMS = ['x', 'c', 'w_cond', 'b_cond', 'norm_pre', 'norm_post', 'w_ffn_in', 'w_ffn_out', 'fox_w_in', 'fox_b_f', 'fox_w_out', 'sconv_w_in', 'sconv_conv_w', 'sconv_w_out', 'lru_w_in', 'lru_conv_w', 'lru_conv_b', 'lru_w_a', 'lru_b_a', 'lru_w_x', 'lru_b_x', 'lru_lambda', 'lru_w_out']
TWIN_WEIGHTS = ['w_cond', 'b_cond', 'norm_pre', 'norm_post', 'w_ffn_in', 'w_ffn_out', 'fox_w_in', 'fox_b_f', 'fox_w_out', 'sconv_w_in', 'sconv_conv_w', 'sconv_w_out', 'lru_w_in', 'lru_conv_w', 'lru_conv_b', 'lru_w_a', 'lru_b_a', 'lru_w_x', 'lru_b_x', 'lru_lambda', 'lru_w_out']
TWIN_DIFF_INPUT = 'x'
TWIN_INPUTS = ['x', 'c', 'w_cond', 'b_cond', 'norm_pre', 'norm_post', 'w_ffn_in', 'w_ffn_out', 'fox_w_in', 'fox_b_f', 'fox_w_out', 'sconv_w_in', 'sconv_conv_w', 'sconv_w_out', 'lru_w_in', 'lru_conv_w', 'lru_conv_b', 'lru_w_a', 'lru_b_a', 'lru_w_x', 'lru_b_x', 'lru_lambda', 'lru_w_out', 'loss_target', 'm_w_cond', 'm_b_cond', 'm_norm_pre', 'm_norm_post', 'm_w_ffn_in', 'm_w_ffn_out', 'm_fox_w_in', 'm_fox_b_f', 'm_fox_w_out', 'm_sconv_w_in', 'm_sconv_conv_w', 'm_sconv_w_out', 'm_lru_w_in', 'm_lru_conv_w', 'm_lru_conv_b', 'm_lru_w_a', 'm_lru_b_a', 'm_lru_w_x', 'm_lru_b_x', 'm_lru_lambda', 'm_lru_w_out', 'v_w_cond', 'v_b_cond', 'v_norm_pre', 'v_norm_post', 'v_w_ffn_in', 'v_w_ffn_out', 'v_fox_w_in', 'v_fox_b_f', 'v_fox_w_out', 'v_sconv_w_in', 'v_sconv_conv_w', 'v_sconv_w_out', 'v_lru_w_in', 'v_lru_conv_w', 'v_lru_conv_b', 'v_lru_w_a', 'v_lru_b_a', 'v_lru_w_x', 'v_lru_b_x', 'v_lru_lambda', 'v_lru_w_out']
TWIN_OUTPUTS = ['loss', 'grad_x', 'grad_w_cond', 'grad_b_cond', 'grad_norm_pre', 'grad_norm_post', 'grad_w_ffn_in', 'grad_w_ffn_out', 'grad_fox_w_in', 'grad_fox_b_f', 'grad_fox_w_out', 'grad_sconv_w_in', 'grad_sconv_conv_w', 'grad_sconv_w_out', 'grad_lru_w_in', 'grad_lru_conv_w', 'grad_lru_conv_b', 'grad_lru_w_a', 'grad_lru_b_a', 'grad_lru_w_x', 'grad_lru_b_x', 'grad_lru_lambda', 'grad_lru_w_out', 'delta_w_cond', 'delta_b_cond', 'delta_norm_pre', 'delta_norm_post', 'delta_w_ffn_in', 'delta_w_ffn_out', 'delta_fox_w_in', 'delta_fox_b_f', 'delta_fox_w_out', 'delta_sconv_w_in', 'delta_sconv_conv_w', 'delta_sconv_w_out', 'delta_lru_w_in', 'delta_lru_conv_w', 'delta_lru_conv_b', 'delta_lru_w_a', 'delta_lru_b_a', 'delta_lru_w_x', 'delta_lru_b_x', 'delta_lru_lambda', 'delta_lru_w_out', 'new_m_w_cond', 'new_m_b_cond', 'new_m_norm_pre', 'new_m_norm_post', 'new_m_w_ffn_in', 'new_m_w_ffn_out', 'new_m_fox_w_in', 'new_m_fox_b_f', 'new_m_fox_w_out', 'new_m_sconv_w_in', 'new_m_sconv_conv_w', 'new_m_sconv_w_out', 'new_m_lru_w_in', 'new_m_lru_conv_w', 'new_m_lru_conv_b', 'new_m_lru_w_a', 'new_m_lru_b_a', 'new_m_lru_w_x', 'new_m_lru_b_x', 'new_m_lru_lambda', 'new_m_lru_w_out', 'new_v_w_cond', 'new_v_b_cond', 'new_v_norm_pre', 'new_v_norm_post', 'new_v_w_ffn_in', 'new_v_w_ffn_out', 'new_v_fox_w_in', 'new_v_fox_b_f', 'new_v_fox_w_out', 'new_v_sconv_w_in', 'new_v_sconv_conv_w', 'new_v_sconv_w_out', 'new_v_lru_w_in', 'new_v_lru_conv_w', 'new_v_lru_conv_b', 'new_v_lru_w_a', 'new_v_lru_b_a', 'new_v_lru_w_x', 'new_v_lru_b_x', 'new_v_lru_lambda', 'new_v_lru_w_out']
TWIN_LEAF_KINDS = {'loss': 'loss', 'grad_x': 'grad_x', 'grad_w_cond': 'grad_w', 'grad_b_cond': 'grad_w', 'grad_norm_pre': 'grad_w', 'grad_norm_post': 'grad_w', 'grad_w_ffn_in': 'grad_w', 'grad_w_ffn_out': 'grad_w', 'grad_fox_w_in': 'grad_w', 'grad_fox_b_f': 'grad_w', 'grad_fox_w_out': 'grad_w', 'grad_sconv_w_in': 'grad_w', 'grad_sconv_conv_w': 'grad_w', 'grad_sconv_w_out': 'grad_w', 'grad_lru_w_in': 'grad_w', 'grad_lru_conv_w': 'grad_w', 'grad_lru_conv_b': 'grad_w', 'grad_lru_w_a': 'grad_w', 'grad_lru_b_a': 'grad_w', 'grad_lru_w_x': 'grad_w', 'grad_lru_b_x': 'grad_w', 'grad_lru_lambda': 'grad_w', 'grad_lru_w_out': 'grad_w', 'delta_w_cond': 'delta_w', 'delta_b_cond': 'delta_w', 'delta_norm_pre': 'delta_w', 'delta_norm_post': 'delta_w', 'delta_w_ffn_in': 'delta_w', 'delta_w_ffn_out': 'delta_w', 'delta_fox_w_in': 'delta_w', 'delta_fox_b_f': 'delta_w', 'delta_fox_w_out': 'delta_w', 'delta_sconv_w_in': 'delta_w', 'delta_sconv_conv_w': 'delta_w', 'delta_sconv_w_out': 'delta_w', 'delta_lru_w_in': 'delta_w', 'delta_lru_conv_w': 'delta_w', 'delta_lru_conv_b': 'delta_w', 'delta_lru_w_a': 'delta_w', 'delta_lru_b_a': 'delta_w', 'delta_lru_w_x': 'delta_w', 'delta_lru_b_x': 'delta_w', 'delta_lru_lambda': 'delta_w', 'delta_lru_w_out': 'delta_w', 'new_m_w_cond': 'new_m', 'new_m_b_cond': 'new_m', 'new_m_norm_pre': 'new_m', 'new_m_norm_post': 'new_m', 'new_m_w_ffn_in': 'new_m', 'new_m_w_ffn_out': 'new_m', 'new_m_fox_w_in': 'new_m', 'new_m_fox_b_f': 'new_m', 'new_m_fox_w_out': 'new_m', 'new_m_sconv_w_in': 'new_m', 'new_m_sconv_conv_w': 'new_m', 'new_m_sconv_w_out': 'new_m', 'new_m_lru_w_in': 'new_m', 'new_m_lru_conv_w': 'new_m', 'new_m_lru_conv_b': 'new_m', 'new_m_lru_w_a': 'new_m', 'new_m_lru_b_a': 'new_m', 'new_m_lru_w_x': 'new_m', 'new_m_lru_b_x': 'new_m', 'new_m_lru_lambda': 'new_m', 'new_m_lru_w_out': 'new_m', 'new_v_w_cond': 'new_v', 'new_v_b_cond': 'new_v', 'new_v_norm_pre': 'new_v', 'new_v_norm_post': 'new_v', 'new_v_w_ffn_in': 'new_v', 'new_v_w_ffn_out': 'new_v', 'new_v_fox_w_in': 'new_v', 'new_v_fox_b_f': 'new_v', 'new_v_fox_w_out': 'new_v', 'new_v_sconv_w_in': 'new_v', 'new_v_sconv_conv_w': 'new_v', 'new_v_sconv_w_out': 'new_v', 'new_v_lru_w_in': 'new_v', 'new_v_lru_conv_w': 'new_v', 'new_v_lru_conv_b': 'new_v', 'new_v_lru_w_a': 'new_v', 'new_v_lru_b_a': 'new_v', 'new_v_lru_w_x': 'new_v', 'new_v_lru_b_x': 'new_v', 'new_v_lru_lambda': 'new_v', 'new_v_lru_w_out': 'new_v'}


def _forward(args):
    return _fwd_reference(*[args[k] for k in FWD_PARAMS])


def _output_shape():
    out = _jax.eval_shape(lambda: _forward(_fwd_setup_inputs(0)))
    return out.shape, out.dtype

N_MICROBATCH = 1
ADAM_LR = 0.001
ADAM_B1 = 0.9
ADAM_B2 = 0.999
ADAM_EPS = 1e-08
ADAM_WD = 0.01
ADAM_STEP = 10
PER_EXAMPLE_BATCH_AXIS = {'x': 0, 'c': 0, 'loss_target': 0}
SHARED_INPUTS = []
_WEIGHT_DTYPES = {'w_cond': _jnp.float32, 'b_cond': _jnp.float32, 'norm_pre': _jnp.float32, 'norm_post': _jnp.float32, 'w_ffn_in': _jnp.float32, 'w_ffn_out': _jnp.float32, 'fox_w_in': _jnp.float32, 'fox_b_f': _jnp.float32, 'fox_w_out': _jnp.float32, 'sconv_w_in': _jnp.float32, 'sconv_conv_w': _jnp.float32, 'sconv_w_out': _jnp.float32, 'lru_w_in': _jnp.float32, 'lru_conv_w': _jnp.float32, 'lru_conv_b': _jnp.float32, 'lru_w_a': _jnp.float32, 'lru_b_a': _jnp.float32, 'lru_w_x': _jnp.float32, 'lru_b_x': _jnp.float32, 'lru_lambda': _jnp.float32, 'lru_w_out': _jnp.float32}
MOMENT_SCALE = {'w_cond': 2.091920e+00, 'b_cond': 3.549910e+00, 'norm_pre': 1.084236e+00, 'norm_post': 5.230814e+00, 'w_ffn_in': 5.044853e-01, 'w_ffn_out': 9.303347e-01, 'fox_w_in': 2.478617e+00, 'fox_b_f': 6.513043e-01, 'fox_w_out': 4.312166e+00, 'sconv_w_in': 9.391645e-01, 'sconv_conv_w': 1.001462e+00, 'sconv_w_out': 9.385969e-01, 'lru_w_in': 2.763874e+00, 'lru_conv_w': 3.109511e+00, 'lru_conv_b': 4.610961e+00, 'lru_w_a': 3.514635e-01, 'lru_b_a': 3.045943e-01, 'lru_w_x': 8.946844e-01, 'lru_b_x': 9.266934e-01, 'lru_lambda': 8.751325e-01, 'lru_w_out': 3.028874e+00}


def _to_microbatches(a, axis):
    t = _jnp.moveaxis(a, axis, 0)
    t = t.reshape((N_MICROBATCH, t.shape[0] // N_MICROBATCH) + t.shape[1:])
    return _jnp.moveaxis(t, 1, axis + 1)


def setup_inputs(seed: int = 0) -> dict:
    inp = _fwd_setup_inputs(seed)
    key = _jax.random.fold_in(_jax.random.key(seed), 7919)
    shape, _ = _output_shape()
    out = dict(inp)
    out["loss_target"] = _jax.random.normal(_jax.random.fold_in(key, 0), shape, _jnp.float32)
    for i, name in enumerate(TWIN_WEIGHTS):
        w = inp[name].astype(_jnp.float32)
        if MOMENT_SCALE is None:
            s = _jnp.sqrt(_jnp.mean(_jnp.square(w)) + 1e-30)
        else:
            s = MOMENT_SCALE[name]
        km, kv = _jax.random.split(_jax.random.fold_in(key, i + 1))
        out[name] = w
        out["m_" + name] = s * _jax.random.normal(km, w.shape, _jnp.float32)
        out["v_" + name] = (s * s) * _jax.random.uniform(kv, w.shape, _jnp.float32, 0.5, 1.5)
    if N_MICROBATCH > 1:
        for name, axis in PER_EXAMPLE_BATCH_AXIS.items():
            out[name] = _to_microbatches(out[name], axis)
    return {'x': out['x'], 'c': out['c'], 'w_cond': out['w_cond'], 'b_cond': out['b_cond'], 'norm_pre': out['norm_pre'], 'norm_post': out['norm_post'], 'w_ffn_in': out['w_ffn_in'], 'w_ffn_out': out['w_ffn_out'], 'fox_w_in': out['fox_w_in'], 'fox_b_f': out['fox_b_f'], 'fox_w_out': out['fox_w_out'], 'sconv_w_in': out['sconv_w_in'], 'sconv_conv_w': out['sconv_conv_w'], 'sconv_w_out': out['sconv_w_out'], 'lru_w_in': out['lru_w_in'], 'lru_conv_w': out['lru_conv_w'], 'lru_conv_b': out['lru_conv_b'], 'lru_w_a': out['lru_w_a'], 'lru_b_a': out['lru_b_a'], 'lru_w_x': out['lru_w_x'], 'lru_b_x': out['lru_b_x'], 'lru_lambda': out['lru_lambda'], 'lru_w_out': out['lru_w_out'], 'loss_target': out['loss_target'], 'm_w_cond': out['m_w_cond'], 'm_b_cond': out['m_b_cond'], 'm_norm_pre': out['m_norm_pre'], 'm_norm_post': out['m_norm_post'], 'm_w_ffn_in': out['m_w_ffn_in'], 'm_w_ffn_out': out['m_w_ffn_out'], 'm_fox_w_in': out['m_fox_w_in'], 'm_fox_b_f': out['m_fox_b_f'], 'm_fox_w_out': out['m_fox_w_out'], 'm_sconv_w_in': out['m_sconv_w_in'], 'm_sconv_conv_w': out['m_sconv_conv_w'], 'm_sconv_w_out': out['m_sconv_w_out'], 'm_lru_w_in': out['m_lru_w_in'], 'm_lru_conv_w': out['m_lru_conv_w'], 'm_lru_conv_b': out['m_lru_conv_b'], 'm_lru_w_a': out['m_lru_w_a'], 'm_lru_b_a': out['m_lru_b_a'], 'm_lru_w_x': out['m_lru_w_x'], 'm_lru_b_x': out['m_lru_b_x'], 'm_lru_lambda': out['m_lru_lambda'], 'm_lru_w_out': out['m_lru_w_out'], 'v_w_cond': out['v_w_cond'], 'v_b_cond': out['v_b_cond'], 'v_norm_pre': out['v_norm_pre'], 'v_norm_post': out['v_norm_post'], 'v_w_ffn_in': out['v_w_ffn_in'], 'v_w_ffn_out': out['v_w_ffn_out'], 'v_fox_w_in': out['v_fox_w_in'], 'v_fox_b_f': out['v_fox_b_f'], 'v_fox_w_out': out['v_fox_w_out'], 'v_sconv_w_in': out['v_sconv_w_in'], 'v_sconv_conv_w': out['v_sconv_conv_w'], 'v_sconv_w_out': out['v_sconv_w_out'], 'v_lru_w_in': out['v_lru_w_in'], 'v_lru_conv_w': out['v_lru_conv_w'], 'v_lru_conv_b': out['v_lru_conv_b'], 'v_lru_w_a': out['v_lru_w_a'], 'v_lru_b_a': out['v_lru_b_a'], 'v_lru_w_x': out['v_lru_w_x'], 'v_lru_b_x': out['v_lru_b_x'], 'v_lru_lambda': out['v_lru_lambda'], 'v_lru_w_out': out['v_lru_w_out']}


def _loss(weights, diff, rest, loss_target):
    with _jax.named_scope("forward"):
        args = {**rest, TWIN_DIFF_INPUT: diff, **{k: w.astype(_WEIGHT_DTYPES[k]) for k, w in weights.items()}}
        y = _forward(args)
    with _jax.named_scope("loss_head"):
        err = _jnp.square(y.astype(_jnp.float32) - loss_target)
        return 0.5 * _jnp.sum(_jnp.mean(err, axis=-1)) if err.ndim else 0.5 * err


def _adamw(w, g, m, v):
    m = ADAM_B1 * m + (1.0 - ADAM_B1) * g
    v = ADAM_B2 * v + (1.0 - ADAM_B2) * _jnp.square(g)
    m_hat = m / (1.0 - ADAM_B1 ** ADAM_STEP)
    v_hat = v / (1.0 - ADAM_B2 ** ADAM_STEP)
    delta = -ADAM_LR * (m_hat / (_jnp.sqrt(v_hat) + ADAM_EPS) + ADAM_WD * w)
    return delta, m, v


def reference(x, c, w_cond, b_cond, norm_pre, norm_post, w_ffn_in, w_ffn_out, fox_w_in, fox_b_f, fox_w_out, sconv_w_in, sconv_conv_w, sconv_w_out, lru_w_in, lru_conv_w, lru_conv_b, lru_w_a, lru_b_a, lru_w_x, lru_b_x, lru_lambda, lru_w_out, loss_target, m_w_cond, m_b_cond, m_norm_pre, m_norm_post, m_w_ffn_in, m_w_ffn_out, m_fox_w_in, m_fox_b_f, m_fox_w_out, m_sconv_w_in, m_sconv_conv_w, m_sconv_w_out, m_lru_w_in, m_lru_conv_w, m_lru_conv_b, m_lru_w_a, m_lru_b_a, m_lru_w_x, m_lru_b_x, m_lru_lambda, m_lru_w_out, v_w_cond, v_b_cond, v_norm_pre, v_norm_post, v_w_ffn_in, v_w_ffn_out, v_fox_w_in, v_fox_b_f, v_fox_w_out, v_sconv_w_in, v_sconv_conv_w, v_sconv_w_out, v_lru_w_in, v_lru_conv_w, v_lru_conv_b, v_lru_w_a, v_lru_b_a, v_lru_w_x, v_lru_b_x, v_lru_lambda, v_lru_w_out):
    given = dict(x=x, c=c, w_cond=w_cond, b_cond=b_cond, norm_pre=norm_pre, norm_post=norm_post, w_ffn_in=w_ffn_in, w_ffn_out=w_ffn_out, fox_w_in=fox_w_in, fox_b_f=fox_b_f, fox_w_out=fox_w_out, sconv_w_in=sconv_w_in, sconv_conv_w=sconv_conv_w, sconv_w_out=sconv_w_out, lru_w_in=lru_w_in, lru_conv_w=lru_conv_w, lru_conv_b=lru_conv_b, lru_w_a=lru_w_a, lru_b_a=lru_b_a, lru_w_x=lru_w_x, lru_b_x=lru_b_x, lru_lambda=lru_lambda, lru_w_out=lru_w_out, loss_target=loss_target, m_w_cond=m_w_cond, m_b_cond=m_b_cond, m_norm_pre=m_norm_pre, m_norm_post=m_norm_post, m_w_ffn_in=m_w_ffn_in, m_w_ffn_out=m_w_ffn_out, m_fox_w_in=m_fox_w_in, m_fox_b_f=m_fox_b_f, m_fox_w_out=m_fox_w_out, m_sconv_w_in=m_sconv_w_in, m_sconv_conv_w=m_sconv_conv_w, m_sconv_w_out=m_sconv_w_out, m_lru_w_in=m_lru_w_in, m_lru_conv_w=m_lru_conv_w, m_lru_conv_b=m_lru_conv_b, m_lru_w_a=m_lru_w_a, m_lru_b_a=m_lru_b_a, m_lru_w_x=m_lru_w_x, m_lru_b_x=m_lru_b_x, m_lru_lambda=m_lru_lambda, m_lru_w_out=m_lru_w_out, v_w_cond=v_w_cond, v_b_cond=v_b_cond, v_norm_pre=v_norm_pre, v_norm_post=v_norm_post, v_w_ffn_in=v_w_ffn_in, v_w_ffn_out=v_w_ffn_out, v_fox_w_in=v_fox_w_in, v_fox_b_f=v_fox_b_f, v_fox_w_out=v_fox_w_out, v_sconv_w_in=v_sconv_w_in, v_sconv_conv_w=v_sconv_conv_w, v_sconv_w_out=v_sconv_w_out, v_lru_w_in=v_lru_w_in, v_lru_conv_w=v_lru_conv_w, v_lru_conv_b=v_lru_conv_b, v_lru_w_a=v_lru_w_a, v_lru_b_a=v_lru_b_a, v_lru_w_x=v_lru_w_x, v_lru_b_x=v_lru_b_x, v_lru_lambda=v_lru_lambda, v_lru_w_out=v_lru_w_out)
    weights = {n: given[n] for n in TWIN_WEIGHTS}
    shared = {n: given[n] for n in SHARED_INPUTS}
    per_example = {n: given[n] for n in ['x', 'c']}
    grad_fn = _jax.value_and_grad(_loss, argnums=(0, 1))

    def one_microbatch(ex, loss_target):
        ex = dict(ex)
        diff = ex.pop(TWIN_DIFF_INPUT)
        return grad_fn(weights, diff, {**shared, **ex}, loss_target)

    if N_MICROBATCH == 1:
        loss, (grad_w, grad_x) = one_microbatch(per_example, given["loss_target"])
    else:
        def body(carry, xs):
            loss_sum, grad_sum = carry
            l_k, (gw_k, gx_k) = one_microbatch(xs[0], xs[1])
            with _jax.named_scope("update"):
                return (loss_sum + l_k, _jax.tree.map(_jnp.add, grad_sum, gw_k)), gx_k

        init = (_jnp.zeros((), _jnp.float32), _jax.tree.map(_jnp.zeros_like, weights))
        (loss, grad_w), grad_x = _jax.lax.scan(body, init, (per_example, given["loss_target"]))
    with _jax.named_scope("update"):
        delta_w, new_m, new_v = {}, {}, {}
        for n in TWIN_WEIGHTS:
            delta_w[n], new_m[n], new_v[n] = _adamw(weights[n], grad_w[n], given["m_" + n], given["v_" + n])
    return (loss, grad_x, *[grad_w[n] for n in TWIN_WEIGHTS], *[delta_w[n] for n in TWIN_WEIGHTS],
            *[new_m[n] for n in TWIN_WEIGHTS], *[new_v[n] for n in TWIN_WEIGHTS])
```

```python
import functools
import math

import jax
import jax.numpy as jnp
from jax import lax
from jax.experimental import pallas as pl
from jax.experimental.pallas import tpu as pltpu

F32 = jnp.float32
BF16 = jnp.bfloat16
NDEV = 8
D = 1024
DFF = 2816
FFB = 704
HEADS = 16
HDIM = 64
DEPTH = 4
RMS_EPS = 1e-6
LRU_C = 8.0
ADAM_LR, ADAM_B1, ADAM_B2, ADAM_EPS, ADAM_WD, ADAM_STEP = 0.001, 0.9, 0.999, 1e-08, 0.01, 10
FOXP = 3200
MESH = pl.DeviceIdType.MESH
HBM = pl.BlockSpec(memory_space=pltpu.HBM)
VMEM_BIG = 48 * 1024 * 1024

_NN = (((1,), (0,)), ((), ()))
_NT = (((1,), (1,)), ((), ()))
_TN = (((0,), (0,)), ((), ()))
_DIMS = {"nn": _NN, "nt": _NT, "tn": _TN}


def _cparams(sem=None, vmem=None):
    kw = {}
    if sem is not None:
        kw["dimension_semantics"] = sem
    if vmem is not None:
        kw["vmem_limit_bytes"] = vmem
    return pltpu.CompilerParams(**kw)


def _sigmoid(x):
    return 1.0 / (1.0 + jnp.exp(-x))


def _softplus(x):
    return jnp.maximum(x, 0.0) + jnp.log(1.0 + jnp.exp(-jnp.abs(x)))


_GELU_C = math.sqrt(2.0 / math.pi)


def _gelu_parts(x):
    u = _GELU_C * (x + 0.044715 * x * x * x)
    t = jnp.tanh(u)
    g = 0.5 * x * (1.0 + t)
    dg = 0.5 * (1.0 + t) + 0.5 * x * (1.0 - t * t) * _GELU_C * (1.0 + 3.0 * 0.044715 * x * x)
    return g, dg


def _mesh_pos():
    ax, ay, ac = lax.axis_index("x"), lax.axis_index("y"), lax.axis_index("c")
    return ax, ay, ac, 4 * ax + 2 * ay + ac


def _peer(ax, ay, ac, d):
    px = 1 - ax if (d >> 2) & 1 else ax
    py = 1 - ay if (d >> 1) & 1 else ay
    pc = 1 - ac if d & 1 else ac
    return (px, py, pc), 4 * px + 2 * py + pc


def _exchange(x, axis, name, gather):
    if gather:
        x = jnp.expand_dims(x, axis)
        oshape = x.shape[:axis] + (NDEV,) + x.shape[axis + 1:]
    else:
        oshape = x.shape
    lead = (slice(None),) * axis

    def blk(ref, k):
        return ref.at[lead + (pl.ds(k, 1),)]

    def body(x_ref, o_ref, send_sems, recv_sems, local_sem):
        ax, ay, ac, me = _mesh_pos()
        src = (lambda k: x_ref) if gather else (lambda k: blk(x_ref, k))
        mine = pltpu.make_async_copy(src(me), blk(o_ref, me), local_sem)
        mine.start()
        sends = []
        for d in range(1, NDEV):
            peer, pidx = _peer(ax, ay, ac, d)
            cp = pltpu.make_async_remote_copy(
                src_ref=src(pidx), dst_ref=blk(o_ref, me), send_sem=send_sems.at[d - 1],
                recv_sem=recv_sems.at[d - 1], device_id=peer, device_id_type=MESH)
            cp.start()
            sends.append(cp)
        for d in range(1, NDEV):
            peer, pidx = _peer(ax, ay, ac, d)
            pltpu.make_async_remote_copy(
                src_ref=src(pidx), dst_ref=blk(o_ref, pidx), send_sem=send_sems.at[d - 1],
                recv_sem=recv_sems.at[d - 1], device_id=peer, device_id_type=MESH).wait_recv()
        for cp in sends:
            cp.wait_send()
        mine.wait()

    return pl.pallas_call(
        body, name=name, out_shape=jax.ShapeDtypeStruct(oshape, x.dtype),
        in_specs=[HBM], out_specs=HBM,
        scratch_shapes=[pltpu.SemaphoreType.DMA((NDEV - 1,)), pltpu.SemaphoreType.DMA((NDEV - 1,)),
                        pltpu.SemaphoreType.DMA(())],
    )(x)


def _all_gather(x, axis, name):
    return _exchange(x, axis, name, True)


def _all_to_all(x, axis, name):
    return _exchange(x, axis, name, False)


def _mm(a, b, *, mode, grid, a_spec, b_spec, o_spec, o_shape, acc_shape, out_dtype, name):
    nred = grid[2]

    def body(a_ref, b_ref, o_ref, *scratch):
        p = lax.dot_general(a_ref[...], b_ref[...], _DIMS[mode], preferred_element_type=F32)
        if nred == 1:
            o_ref[...] = p.astype(o_ref.dtype)
            return
        acc = scratch[0]
        r = pl.program_id(2)

        @pl.when(r == 0)
        def _():
            acc[...] = p

        @pl.when(r > 0)
        def _():
            acc[...] += p

        @pl.when(r == nred - 1)
        def _():
            o_ref[...] = acc[...].astype(o_ref.dtype)

    return pl.pallas_call(
        body, name=name, out_shape=jax.ShapeDtypeStruct(o_shape, out_dtype), grid=grid,
        in_specs=[a_spec, b_spec], out_specs=o_spec,
        scratch_shapes=[] if nred == 1 else [pltpu.VMEM(acc_shape, F32)],
        compiler_params=_cparams(("parallel", "parallel", "arbitrary"), VMEM_BIG),
    )(a, b)


def _tile(n, cands):
    for c in cands:
        if n % c == 0:
            return c
    return n


def _mm_nn(a, b, out_dtype, name):
    m, k = a.shape
    n = b.shape[1]
    tm, tn = _tile(m, (512, 256, 128)), _tile(n, (1024, 640, 512, 256, 128))
    return _mm(a, b, mode="nn", grid=(m // tm, n // tn, 1),
               a_spec=pl.BlockSpec((tm, k), lambda i, j, r: (i, 0)),
               b_spec=pl.BlockSpec((k, tn), lambda i, j, r: (0, j)),
               o_spec=pl.BlockSpec((tm, tn), lambda i, j, r: (i, j)),
               o_shape=(m, n), acc_shape=None, out_dtype=out_dtype, name=name)


def _mm_nt(a, b, out_dtype, name):
    m, n = a.shape
    k = b.shape[0]
    tm, tn = _tile(m, (512, 256, 128)), _tile(n, (1024, 640, 512, 256, 128))
    return _mm(a, b, mode="nt", grid=(m // tm, 1, n // tn),
               a_spec=pl.BlockSpec((tm, tn), lambda i, j, r: (i, r)),
               b_spec=pl.BlockSpec((k, tn), lambda i, j, r: (0, r)),
               o_spec=pl.BlockSpec((tm, k), lambda i, j, r: (i, 0)),
               o_shape=(m, k), acc_shape=(tm, k), out_dtype=out_dtype, name=name)


def _mm_tn(a, b, out_dtype, name):
    s, k = a.shape
    n = b.shape[1]
    ts, tn = _tile(s, (512, 256, 128)), _tile(n, (640, 512, 256, 128))
    return _mm(a, b, mode="tn", grid=(1, n // tn, s // ts),
               a_spec=pl.BlockSpec((ts, k), lambda i, j, r: (r, 0)),
               b_spec=pl.BlockSpec((ts, tn), lambda i, j, r: (r, j)),
               o_spec=pl.BlockSpec((k, tn), lambda i, j, r: (0, j)),
               o_shape=(k, n), acc_shape=(k, tn), out_dtype=out_dtype, name=name)


def _ffn_up(h, wi_all, l, f, name):
    s = h.shape[0]
    tm = _tile(s, (512, 256, 128))

    def body(h_ref, wg_ref, wu_ref, g_ref, u_ref, act_ref):
        hv = h_ref[...]
        g = jnp.dot(hv, wg_ref[...], preferred_element_type=F32)
        u = jnp.dot(hv, wu_ref[...], preferred_element_type=F32)
        g_ref[...] = g.astype(BF16)
        u_ref[...] = u.astype(BF16)
        act_ref[...] = (g * _sigmoid(g) * u).astype(BF16)

    wspec = lambda off: pl.BlockSpec((None, None, None, D, FFB), lambda j, i: (l, f, j + off, 0, 0))
    ospec = pl.BlockSpec((None, tm, FFB), lambda j, i: (j, i, 0))
    shp = jax.ShapeDtypeStruct((4, s, FFB), BF16)
    return pl.pallas_call(
        body, name=name, out_shape=(shp, shp, shp), grid=(4, s // tm),
        in_specs=[pl.BlockSpec((tm, D), lambda j, i: (i, 0)), wspec(0), wspec(4)],
        out_specs=(ospec, ospec, ospec),
        compiler_params=_cparams(("parallel", "parallel"), VMEM_BIG),
    )(h, wi_all, wi_all)


def _ffn_down(act, wo_all, l, f, name):
    s = act.shape[1]
    tm = _tile(s, (512, 256, 128))
    return _mm(act, wo_all, mode="nn", grid=(s // tm, 1, 4),
               a_spec=pl.BlockSpec((None, tm, FFB), lambda i, j, r: (r, i, 0)),
               b_spec=pl.BlockSpec((None, None, None, FFB, D), lambda i, j, r: (l, f, r, 0, 0)),
               o_spec=pl.BlockSpec((tm, D), lambda i, j, r: (i, 0)),
               o_shape=(s, D), acc_shape=(tm, D), out_dtype=F32, name=name)


def _ffn_down_bwd(dy, g, u, wo_all, l, f, name):
    s = dy.shape[0]
    tm = _tile(s, (512, 256, 128))
    ns = s // tm

    def body(dy_ref, g_ref, u_ref, wo_ref, dg_ref, du_ref, dwo_ref, acc):
        i = pl.program_id(1)
        dyv = dy_ref[...]
        dact = lax.dot_general(dyv, wo_ref[...], _NT, preferred_element_type=F32)
        gv = g_ref[...].astype(F32)
        uv = u_ref[...].astype(F32)
        sg = _sigmoid(gv)
        silu = gv * sg
        dg_ref[...] = (dact * uv * (sg * (1.0 + gv * (1.0 - sg)))).astype(BF16)
        du_ref[...] = (dact * silu).astype(BF16)
        p = lax.dot_general((silu * uv).astype(BF16), dyv, _TN, preferred_element_type=F32)

        @pl.when(i == 0)
        def _():
            acc[...] = p

        @pl.when(i > 0)
        def _():
            acc[...] += p

        @pl.when(i == ns - 1)
        def _():
            dwo_ref[...] = acc[...].astype(BF16)

    aspec = pl.BlockSpec((None, tm, FFB), lambda j, i: (j, i, 0))
    shp = jax.ShapeDtypeStruct((4, s, FFB), BF16)
    return pl.pallas_call(
        body, name=name, out_shape=(shp, shp, jax.ShapeDtypeStruct((4, FFB, D), BF16)),
        grid=(4, ns),
        in_specs=[pl.BlockSpec((tm, D), lambda j, i: (i, 0)), aspec, aspec,
                  pl.BlockSpec((None, None, None, FFB, D), lambda j, i: (l, f, j, 0, 0))],
        out_specs=(aspec, aspec, pl.BlockSpec((None, FFB, D), lambda j, i: (j, 0, 0))),
        scratch_shapes=[pltpu.VMEM((FFB, D), F32)],
        compiler_params=_cparams(("parallel", "arbitrary"), VMEM_BIG),
    )(dy, g, u, wo_all)


def _ffn_dh(dg, du, wi_all, l, f, name):
    s = dg.shape[1]
    tm = _tile(s, (512, 256, 128))

    def body(dg_ref, du_ref, wg_ref, wu_ref, o_ref, acc):
        r = pl.program_id(1)
        p = (lax.dot_general(dg_ref[...], wg_ref[...], _NT, preferred_element_type=F32)
             + lax.dot_general(du_ref[...], wu_ref[...], _NT, preferred_element_type=F32))

        @pl.when(r == 0)
        def _():
            acc[...] = p

        @pl.when(r > 0)
        def _():
            acc[...] += p

        @pl.when(r == 3)
        def _():
            o_ref[...] = acc[...]

    aspec = pl.BlockSpec((None, tm, FFB), lambda i, r: (r, i, 0))
    wspec = lambda off: pl.BlockSpec((None, None, None, D, FFB), lambda i, r: (l, f, r + off, 0, 0))
    return pl.pallas_call(
        body, name=name, out_shape=jax.ShapeDtypeStruct((s, D), F32), grid=(s // tm, 4),
        in_specs=[aspec, aspec, wspec(0), wspec(4)],
        out_specs=pl.BlockSpec((tm, D), lambda i, r: (i, 0)),
        scratch_shapes=[pltpu.VMEM((tm, D), F32)],
        compiler_params=_cparams(("parallel", "arbitrary"), VMEM_BIG),
    )(dg, du, wi_all, wi_all)


def _ffn_dwi(h, da, name):
    s = h.shape[0]
    ts = _tile(s, (512, 256, 128))
    return _mm(h, da, mode="tn", grid=(4, 1, s // ts),
               a_spec=pl.BlockSpec((ts, D), lambda k, j, r: (r, 0)),
               b_spec=pl.BlockSpec((None, ts, FFB), lambda k, j, r: (k, r, 0)),
               o_spec=pl.BlockSpec((None, D, FFB), lambda k, j, r: (k, 0, 0)),
               o_shape=(4, D, FFB), acc_shape=(D, FFB), out_dtype=BF16, name=name)


TR = 256


def _rows_spec(s):
    tr = _tile(s, (TR, 128))
    return tr, pl.BlockSpec((tr, D), lambda i: (i, 0))


def _pspec(l, sub):
    return pl.BlockSpec((None, None, 8, D), lambda i: (l, sub, 0, 0))


def _pre_math(x, p_ref):
    r = lax.rsqrt(jnp.mean(x * x, axis=1, keepdims=True) + RMS_EPS)
    return (x * r) * p_ref[0:1, :] * (1.0 + p_ref[1:2, :]) + p_ref[2:3, :]


def _prenorm(x, ptab, l, sub, name):
    s = x.shape[0]
    tr, spec = _rows_spec(s)

    def body(x_ref, p_ref, h_ref):
        h_ref[...] = _pre_math(x_ref[...], p_ref).astype(BF16)

    return pl.pallas_call(
        body, name=name, out_shape=jax.ShapeDtypeStruct((s, D), BF16), grid=(s // tr,),
        in_specs=[spec, _pspec(l, sub)], out_specs=spec,
        compiler_params=_cparams(("parallel",)),
    )(x, ptab)


def _resid(x, y, ptab, l, sub, coef, nxt, name):
    s = x.shape[0]
    tr, spec = _rows_spec(s)

    def body(x_ref, y_ref, p_ref, *rest):
        yv = y_ref[...]
        r = lax.rsqrt(jnp.mean(yv * yv, axis=1, keepdims=True) + RMS_EPS)
        xn = x_ref[...] + (coef * p_ref[4:5, :]) * ((yv * r) * p_ref[3:4, :])
        if nxt is None:
            rest[0][...] = xn
        else:
            rest[1][...] = xn
            rest[2][...] = _pre_math(xn, rest[0]).astype(BF16)

    if nxt is None:
        return pl.pallas_call(
            body, name=name, out_shape=jax.ShapeDtypeStruct((s, D), F32), grid=(s // tr,),
            in_specs=[spec, spec, _pspec(l, sub)], out_specs=spec,
            compiler_params=_cparams(("parallel",)),
        )(x, y, ptab), None
    return pl.pallas_call(
        body, name=name,
        out_shape=(jax.ShapeDtypeStruct((s, D), F32), jax.ShapeDtypeStruct((s, D), BF16)),
        grid=(s // tr,),
        in_specs=[spec, spec, _pspec(l, sub), _pspec(*nxt)], out_specs=(spec, spec),
        compiler_params=_cparams(("parallel",)),
    )(x, y, ptab, ptab)


def _loss_grad(x, tgt, name):
    s = x.shape[0]
    tr, spec = _rows_spec(s)

    def body(x_ref, t_ref, dx_ref, l_ref):
        @pl.when(pl.program_id(0) == 0)
        def _():
            l_ref[...] = jnp.zeros_like(l_ref)

        e = x_ref[...] - t_ref[...]
        dx_ref[...] = e * (1.0 / D)
        l_ref[0:1, :] += jnp.sum(e * e, axis=0, keepdims=True) * (0.5 / D)

    return pl.pallas_call(
        body, name=name,
        out_shape=(jax.ShapeDtypeStruct((s, D), F32), jax.ShapeDtypeStruct((8, D), F32)),
        grid=(s // tr,), in_specs=[spec, spec],
        out_specs=(spec, pl.BlockSpec((8, D), lambda i: (0, 0))),
        compiler_params=_cparams(("arbitrary",)),
    )(x, tgt)


def _post_bwd(dx, y, ptab, l, sub, coef, name):
    s = dx.shape[0]
    tr, spec = _rows_spec(s)

    def body(dx_ref, y_ref, p_ref, dy_ref, red_ref):
        @pl.when(pl.program_id(0) == 0)
        def _():
            red_ref[...] = jnp.zeros_like(red_ref)

        dxv, yv = dx_ref[...], y_ref[...]
        gpost, gate = p_ref[3:4, :], p_ref[4:5, :]
        r = lax.rsqrt(jnp.mean(yv * yv, axis=1, keepdims=True) + RMS_EPS)
        yhat = yv * r
        red_ref[0:1, :] += jnp.sum(dxv * yhat * gpost, axis=0, keepdims=True) * coef
        dn = dxv * (coef * gate)
        red_ref[1:2, :] += jnp.sum(dn * yhat, axis=0, keepdims=True)
        dyh = dn * gpost
        dy_ref[...] = (r * (dyh - yhat * jnp.mean(dyh * yhat, axis=1, keepdims=True))).astype(BF16)

    return pl.pallas_call(
        body, name=name,
        out_shape=(jax.ShapeDtypeStruct((s, D), BF16), jax.ShapeDtypeStruct((8, D), F32)),
        grid=(s // tr,), in_specs=[spec, spec, _pspec(l, sub)],
        out_specs=(spec, pl.BlockSpec((8, D), lambda i: (0, 0))),
        compiler_params=_cparams(("arbitrary",)),
    )(dx, y, ptab)


def _pre_bwd(dx, dh, x, ptab, l, sub, name):
    s = dx.shape[0]
    tr, spec = _rows_spec(s)

    def body(dx_ref, dh_ref, x_ref, p_ref, o_ref, red_ref):
        @pl.when(pl.program_id(0) == 0)
        def _():
            red_ref[...] = jnp.zeros_like(red_ref)

        dhv, xv = dh_ref[...], x_ref[...]
        gpre, scale = p_ref[0:1, :], p_ref[1:2, :]
        r = lax.rsqrt(jnp.mean(xv * xv, axis=1, keepdims=True) + RMS_EPS)
        xhat = xv * r
        red_ref[0:1, :] += jnp.sum(dhv, axis=0, keepdims=True)
        red_ref[1:2, :] += jnp.sum(dhv * xhat * gpre, axis=0, keepdims=True)
        red_ref[2:3, :] += jnp.sum(dhv * xhat * (1.0 + scale), axis=0, keepdims=True)
        dxh = dhv * (gpre * (1.0 + scale))
        o_ref[...] = dx_ref[...] + r * (dxh - xhat * jnp.mean(dxh * xhat, axis=1, keepdims=True))

    return pl.pallas_call(
        body, name=name,
        out_shape=(jax.ShapeDtypeStruct((s, D), F32), jax.ShapeDtypeStruct((8, D), F32)),
        grid=(s // tr,), in_specs=[spec, spec, spec, _pspec(l, sub)],
        out_specs=(spec, pl.BlockSpec((8, D), lambda i: (0, 0))),
        compiler_params=_cparams(("arbitrary",)),
    )(dx, dh, x, ptab)


def _cond_fwd(c_pad, wc, bc, name):
    w = wc.shape[2]

    def body(c_ref, w_ref, b_ref, o_ref, ca_ref):
        cv = c_ref[...]
        ca = (cv * _sigmoid(cv)).astype(BF16)
        ca_ref[...] = ca
        o_ref[...] = jnp.dot(ca, w_ref[...].astype(BF16), preferred_element_type=F32) + b_ref[...]

    return pl.pallas_call(
        body, name=name,
        out_shape=(jax.ShapeDtypeStruct((DEPTH, 128, w), F32), jax.ShapeDtypeStruct((128, D), BF16)),
        grid=(DEPTH,),
        in_specs=[pl.BlockSpec((128, D), lambda i: (0, 0)),
                  pl.BlockSpec((None, D, w), lambda i: (i, 0, 0)),
                  pl.BlockSpec((None, 1, w), lambda i: (i, 0, 0))],
        out_specs=(pl.BlockSpec((None, 128, w), lambda i: (i, 0, 0)),
                   pl.BlockSpec((128, D), lambda i: (0, 0))),
        compiler_params=_cparams(("arbitrary",), VMEM_BIG),
    )(c_pad, wc, bc)


def _cond_bwd(ca_pad, dmod, name):
    w = dmod.shape[2]
    return _mm(ca_pad, dmod, mode="tn", grid=(DEPTH, 1, 1),
               a_spec=pl.BlockSpec((128, D), lambda i, j, r: (0, 0)),
               b_spec=pl.BlockSpec((None, 128, w), lambda i, j, r: (i, 0, 0)),
               o_spec=pl.BlockSpec((None, D, w), lambda i, j, r: (i, 0, 0)),
               o_shape=(DEPTH, D, w), acc_shape=None, out_dtype=F32, name=name)


def _split3(x):
    hi = x.astype(BF16)
    r1 = x - hi.astype(F32)
    mid = r1.astype(BF16)
    lo = (r1 - mid.astype(F32)).astype(BF16)
    return hi, mid, lo


def _tri_dot(t, x):
    hi, mid, lo = _split3(x)
    return (jnp.dot(t, hi, preferred_element_type=F32) + jnp.dot(t, mid, preferred_element_type=F32)
            + jnp.dot(t, lo, preferred_element_type=F32))


def _fox_cum(fl, bf, name):
    s = fl.shape[0]
    tb = _tile(s, (256, 128))

    def body(fl_ref, b_ref, cum_ref):
        row = lax.broadcasted_iota(jnp.int32, (tb, tb), 0)
        col = lax.broadcasted_iota(jnp.int32, (tb, tb), 1)
        tri = (col <= row).astype(BF16)
        carry = jnp.zeros((1, 128), F32)
        for blk in range(s // tb):
            z = fl_ref[blk * tb:(blk + 1) * tb, :] + b_ref[0:1, :]
            lf = jnp.minimum(z, 0.0) - jnp.log(1.0 + jnp.exp(-jnp.abs(z)))
            cum_ref[blk * tb:(blk + 1) * tb, :] = _tri_dot(tri, lf) + carry
            carry = carry + jnp.sum(lf, axis=0, keepdims=True)

    return pl.pallas_call(
        body, name=name, out_shape=jax.ShapeDtypeStruct((s, 128), F32),
    )(fl, bf)


def _fox_cum_bwd(dcum, fl, bf, name):
    s = fl.shape[0]
    tb = _tile(s, (256, 128))

    def body(dc_ref, fl_ref, b_ref, dfl_ref, db_ref):
        row = lax.broadcasted_iota(jnp.int32, (tb, tb), 0)
        col = lax.broadcasted_iota(jnp.int32, (tb, tb), 1)
        tri = (col >= row).astype(BF16)
        carry = jnp.zeros((1, 128), F32)
        dbs = jnp.zeros((1, 128), F32)
        for blk in reversed(range(s // tb)):
            dc = dc_ref[blk * tb:(blk + 1) * tb, :]
            dl = _tri_dot(tri, dc) + carry
            carry = carry + jnp.sum(dc, axis=0, keepdims=True)
            z = fl_ref[blk * tb:(blk + 1) * tb, :] + b_ref[0:1, :]
            dz = dl * _sigmoid(-z)
            dfl_ref[blk * tb:(blk + 1) * tb, :] = dz
            dbs = dbs + jnp.sum(dz, axis=0, keepdims=True)
        db_ref[...] = jnp.broadcast_to(dbs, (8, 128))

    return pl.pallas_call(
        body, name=name,
        out_shape=(jax.ShapeDtypeStruct((s, 128), F32), jax.ShapeDtypeStruct((8, 128), F32)),
    )(dcum, fl, bf)


def _fox_scores(q, k_ref, cq, ck_ref, qi, tq, n):
    sc = lax.dot_general(q, k_ref[0:n, :], _NT, preferred_element_type=F32) * (HDIM ** -0.5)
    sc = sc + cq - ck_ref[:, 0:n]
    row = lax.broadcasted_iota(jnp.int32, (tq, n), 0) + qi * tq
    col = lax.broadcasted_iota(jnp.int32, (tq, n), 1)
    return sc, col <= row


def _fox_attn_fwd(q, k, v, cq, ck, name):
    s = q.shape[1]
    tq = _tile(s, (256, 128))

    def body(q_ref, k_ref, v_ref, cq_ref, ck_ref, o_ref, lse_ref):
        for qi in range(s // tq):
            n = (qi + 1) * tq
            rows = slice(qi * tq, n)
            sc, keep = _fox_scores(q_ref[rows, :], k_ref, cq_ref[rows, :], ck_ref, qi, tq, n)
            sc = jnp.where(keep, sc, -1e30)
            m = jnp.max(sc, axis=1, keepdims=True)
            p = jnp.exp(sc - m)
            lsum = jnp.sum(p, axis=1, keepdims=True)
            o = jnp.dot(p.astype(BF16), v_ref[0:n, :], preferred_element_type=F32) / lsum
            o_ref[rows, :] = o.astype(BF16)
            lse_ref[rows, :] = m + jnp.log(lsum)

    hspec = pl.BlockSpec((None, s, HDIM), lambda h: (h, 0, 0))
    cspec = pl.BlockSpec((None, s, 1), lambda h: (h, 0, 0))
    rspec = pl.BlockSpec((None, 1, s), lambda h: (h, 0, 0))
    return pl.pallas_call(
        body, name=name,
        out_shape=(jax.ShapeDtypeStruct((HEADS, s, HDIM), BF16), jax.ShapeDtypeStruct((HEADS, s, 1), F32)),
        grid=(HEADS,), in_specs=[hspec, hspec, hspec, cspec, rspec], out_specs=(hspec, cspec),
        compiler_params=_cparams(("parallel",), VMEM_BIG),
    )(q, k, v, cq, ck)


def _fox_attn_bwd(q, k, v, do, lse, cq, ck, name):
    s = q.shape[1]
    tq = _tile(s, (256, 128))
    scale = HDIM ** -0.5

    def body(q_ref, k_ref, v_ref, do_ref, lse_ref, cq_ref, ck_ref,
             dq_ref, dk_ref, dv_ref, dcq_ref, dck_ref, dk_acc, dv_acc, dck_acc):
        dk_acc[...] = jnp.zeros_like(dk_acc)
        dv_acc[...] = jnp.zeros_like(dv_acc)
        dck_acc[...] = jnp.zeros_like(dck_acc)
        for qi in range(s // tq):
            n = (qi + 1) * tq
            rows = slice(qi * tq, n)
            qv, dov = q_ref[rows, :], do_ref[rows, :]
            sc, keep = _fox_scores(qv, k_ref, cq_ref[rows, :], ck_ref, qi, tq, n)
            p = jnp.where(keep, jnp.exp(sc - lse_ref[rows, :]), 0.0)
            dp = lax.dot_general(dov, v_ref[0:n, :], _NT, preferred_element_type=F32)
            ds = p * (dp - jnp.sum(p * dp, axis=1, keepdims=True))
            dsb = ds.astype(BF16)
            dq_ref[rows, :] = (jnp.dot(dsb, k_ref[0:n, :], preferred_element_type=F32) * scale).astype(BF16)
            dk_acc[0:n, :] += lax.dot_general(dsb, qv, _TN, preferred_element_type=F32) * scale
            dv_acc[0:n, :] += lax.dot_general(p.astype(BF16), dov, _TN, preferred_element_type=F32)
            dcq_ref[rows, :] = jnp.sum(ds, axis=1, keepdims=True)
            dck_acc[:, 0:n] -= jnp.sum(ds, axis=0, keepdims=True)
        dk_ref[...] = dk_acc[...].astype(BF16)
        dv_ref[...] = dv_acc[...].astype(BF16)
        dck_ref[...] = dck_acc[...]

    hspec = pl.BlockSpec((None, s, HDIM), lambda h: (h, 0, 0))
    cspec = pl.BlockSpec((None, s, 1), lambda h: (h, 0, 0))
    rspec = pl.BlockSpec((None, 1, s), lambda h: (h, 0, 0))
    hs = jax.ShapeDtypeStruct((HEADS, s, HDIM), BF16)
    return pl.pallas_call(
        body, name=name,
        out_shape=(hs, hs, hs, jax.ShapeDtypeStruct((HEADS, s, 1), F32), jax.ShapeDtypeStruct((HEADS, 1, s), F32)),
        grid=(HEADS,), in_specs=[hspec, hspec, hspec, hspec, cspec, cspec, rspec],
        out_specs=(hspec, hspec, hspec, cspec, rspec),
        scratch_shapes=[pltpu.VMEM((s, HDIM), F32), pltpu.VMEM((s, HDIM), F32), pltpu.VMEM((1, s), F32)],
        compiler_params=_cparams(("parallel",), VMEM_BIG),
    )(q, k, v, do, lse, cq, ck)


TC = 256
HALO = 8


def _chunk_specs(s):
    tc = _tile(s, (TC, 128))
    per = tc // HALO
    nblk = s // HALO
    cur = pl.BlockSpec((tc, 8, 128), lambda i: (i, 0, 0))
    past = pl.BlockSpec((HALO, 8, 128), lambda i: (jnp.maximum(i * per - 1, 0), 0, 0))
    future = pl.BlockSpec((HALO, 8, 128), lambda i: (jnp.minimum((i + 1) * per, nblk - 1), 0, 0))
    return tc, cur, past, future


def _vec_spec(n):
    return pl.BlockSpec((n, 8, 128), lambda i: (0, 0, 0))


def _conv_past(buf, w_ref, kw, tc):
    out = w_ref[kw - 1] * buf[HALO:HALO + tc]
    for k in range(kw - 1):
        off = HALO - (kw - 1) + k
        out = out + w_ref[k] * buf[off:off + tc]
    return out


def _sconv_fwd(bg, cg, xv, w, name):
    s = bg.shape[0]
    tc, cur, past, _ = _chunk_specs(s)

    def body(bg_ref, cg_ref, xv_ref, cgp_ref, xvp_ref, w_ref, y_ref, zbuf):
        first = pl.program_id(0) == 0
        zbuf[0:HALO] = jnp.where(first, 0.0, cgp_ref[...] * xvp_ref[...])
        zbuf[HALO:HALO + tc] = cg_ref[...] * xv_ref[...]
        y_ref[...] = bg_ref[...] * _conv_past(zbuf, w_ref, 3, tc)

    return pl.pallas_call(
        body, name=name, out_shape=jax.ShapeDtypeStruct((s, 8, 128), F32), grid=(s // tc,),
        in_specs=[cur, cur, cur, past, past, _vec_spec(3)], out_specs=cur,
        scratch_shapes=[pltpu.VMEM((tc + HALO, 8, 128), F32)],
        compiler_params=_cparams(("parallel",)),
    )(bg, cg, xv, cg, xv, w)


def _sconv_bwd(dy, bg, cg, xv, w, name):
    s = dy.shape[0]
    tc, cur, past, future = _chunk_specs(s)
    nch = s // tc

    def body(dy_ref, bg_ref, cg_ref, xv_ref, cgp_ref, xvp_ref, dyf_ref, bgf_ref, w_ref,
             dbg_ref, dcg_ref, dxv_ref, dw_ref, zbuf, dbuf):
        i = pl.program_id(0)

        @pl.when(i == 0)
        def _():
            dw_ref[...] = jnp.zeros_like(dw_ref)

        z = cg_ref[...] * xv_ref[...]
        zbuf[0:HALO] = jnp.where(i == 0, 0.0, cgp_ref[...] * xvp_ref[...])
        zbuf[HALO:HALO + tc] = z
        dyv = dy_ref[...]
        dbg_ref[...] = dyv * _conv_past(zbuf, w_ref, 3, tc)
        dbuf[0:tc] = dyv * bg_ref[...]
        dbuf[tc:tc + HALO] = jnp.where(i == nch - 1, 0.0, dyf_ref[...] * bgf_ref[...])
        dz = jnp.zeros((tc, 8, 128), F32)
        for k in range(3):
            sh = dbuf[2 - k:2 - k + tc]
            dz = dz + w_ref[k] * sh
            dw_ref[k] += jnp.sum(z * sh, axis=0)
        dcg_ref[...] = dz * xv_ref[...]
        dxv_ref[...] = dz * cg_ref[...]

    shp = jax.ShapeDtypeStruct((s, 8, 128), F32)
    return pl.pallas_call(
        body, name=name, out_shape=(shp, shp, shp, jax.ShapeDtypeStruct((3, 8, 128), F32)),
        grid=(nch,),
        in_specs=[cur, cur, cur, cur, past, past, future, future, _vec_spec(3)],
        out_specs=(cur, cur, cur, _vec_spec(3)),
        scratch_shapes=[pltpu.VMEM((tc + HALO, 8, 128), F32), pltpu.VMEM((tc + HALO, 8, 128), F32)],
        compiler_params=_cparams(("arbitrary",)),
    )(dy, bg, cg, xv, cg, xv, dy, bg, w)


def _lru_conv_fwd(xp, wb, name):
    s = xp.shape[0]
    tc, cur, past, _ = _chunk_specs(s)

    def body(x_ref, xp_ref, w_ref, o_ref, buf):
        buf[0:HALO] = jnp.where(pl.program_id(0) == 0, 0.0, xp_ref[...])
        buf[HALO:HALO + tc] = x_ref[...]
        o_ref[...] = _conv_past(buf, w_ref, 4, tc) + w_ref[4]

    return pl.pallas_call(
        body, name=name, out_shape=jax.ShapeDtypeStruct((s, 8, 128), F32), grid=(s // tc,),
        in_specs=[cur, past, _vec_spec(8)], out_specs=cur,
        scratch_shapes=[pltpu.VMEM((tc + HALO, 8, 128), F32)],
        compiler_params=_cparams(("parallel",)),
    )(xp, xp, wb)


def _lru_conv_bwd(dxb, xp, wb, name):
    s = dxb.shape[0]
    tc, cur, _, future = _chunk_specs(s)
    nch = s // tc

    def body(d_ref, df_ref, x_ref, w_ref, o_ref, red_ref, dbuf):
        i = pl.program_id(0)

        @pl.when(i == 0)
        def _():
            red_ref[...] = jnp.zeros_like(red_ref)

        dv = d_ref[...]
        dbuf[0:tc] = dv
        dbuf[tc:tc + HALO] = jnp.where(i == nch - 1, 0.0, df_ref[...])
        xv = x_ref[...]
        dx = jnp.zeros((tc, 8, 128), F32)
        for k in range(4):
            sh = dbuf[3 - k:3 - k + tc]
            dx = dx + w_ref[k] * sh
            red_ref[k] += jnp.sum(xv * sh, axis=0)
        red_ref[4] += jnp.sum(dv, axis=0)
        o_ref[...] = dx

    return pl.pallas_call(
        body, name=name,
        out_shape=(jax.ShapeDtypeStruct((s, 8, 128), F32), jax.ShapeDtypeStruct((8, 8, 128), F32)),
        grid=(nch,), in_specs=[cur, future, cur, _vec_spec(8)], out_specs=(cur, _vec_spec(8)),
        scratch_shapes=[pltpu.VMEM((tc + HALO, 8, 128), F32)],
        compiler_params=_cparams(("arbitrary",)),
    )(dxb, dxb, xp, wb)


def _lru_gates(ra, ia, pv_ref):
    sp = _softplus(-pv_ref[2])
    r = _sigmoid(ra + pv_ref[0])
    ig = _sigmoid(ia + pv_ref[1])
    log_a = (-LRU_C) * r * sp
    a = jnp.exp(log_a)
    mult = jnp.sqrt(-jnp.tanh(log_a) * (a * a + 1.0))
    return r, ig, a, mult, sp


def _lru_scan_fwd(ra, ia, xb, gate, pv, name):
    s = ra.shape[0]
    tc, cur, _, _ = _chunk_specs(s)

    def body(ra_ref, ia_ref, xb_ref, g_ref, pv_ref, hs_ref, hp_ref, y_ref, abuf, bbuf, hcar):
        @pl.when(pl.program_id(0) == 0)
        def _():
            hcar[...] = jnp.zeros_like(hcar)

        xbv = xb_ref[...]
        _, ig, a, mult, _ = _lru_gates(ra_ref[...], ia_ref[...], pv_ref)
        abuf[...] = a
        bbuf[...] = mult * (ig * xbv)

        def step(t, h):
            hp_ref[t] = h
            h = abuf[t] * h + bbuf[t]
            hs_ref[t] = h
            return h

        hcar[...] = lax.fori_loop(0, tc, step, hcar[...], unroll=8)
        y_ref[...] = hs_ref[...] * _gelu_parts(g_ref[...])[0]

    shp = jax.ShapeDtypeStruct((s, 8, 128), F32)
    return pl.pallas_call(
        body, name=name, out_shape=(shp, shp, shp), grid=(s // tc,),
        in_specs=[cur, cur, cur, cur, _vec_spec(8)], out_specs=(cur, cur, cur),
        scratch_shapes=[pltpu.VMEM((tc, 8, 128), F32), pltpu.VMEM((tc, 8, 128), F32),
                        pltpu.VMEM((8, 128), F32)],
        compiler_params=_cparams(("arbitrary",)),
    )(ra, ia, xb, gate, pv)


def _lru_scan_bwd(dy, ra, ia, xb, gate, hs, hp, pv, name):
    s = dy.shape[0]
    tc = _tile(s, (TC, 128))
    nch = s // tc
    rev = pl.BlockSpec((tc, 8, 128), lambda i: (nch - 1 - i, 0, 0))

    def body(dy_ref, ra_ref, ia_ref, xb_ref, g_ref, hs_ref, hp_ref, pv_ref,
             dg_ref, dra_ref, dia_ref, dxb_ref, red_ref, abuf, dbuf, gbuf, car):
        @pl.when(pl.program_id(0) == 0)
        def _():
            red_ref[...] = jnp.zeros_like(red_ref)
            car[...] = jnp.zeros_like(car)

        xbv = xb_ref[...]
        r, ig, a, mult, sp = _lru_gates(ra_ref[...], ia_ref[...], pv_ref)
        ge, dge = _gelu_parts(g_ref[...])
        dyv = dy_ref[...]
        dg_ref[...] = dyv * hs_ref[...] * dge
        abuf[...] = a
        dbuf[...] = dyv * ge

        def step(k, c):
            t = tc - 1 - k
            g = dbuf[t] + c
            gbuf[t] = g
            return abuf[t] * g

        car[...] = lax.fori_loop(0, tc, step, car[...], unroll=8)
        g = gbuf[...]
        d_a = g * hp_ref[...]
        d_m = g * (ig * xbv)
        d_loga = d_a * a - d_m * (a * a / mult)
        dxb_ref[...] = g * mult * ig
        dra = d_loga * ((-LRU_C) * sp) * r * (1.0 - r)
        dia = g * mult * xbv * ig * (1.0 - ig)
        dra_ref[...] = dra
        dia_ref[...] = dia
        red_ref[0] += jnp.sum(dra, axis=0)
        red_ref[1] += jnp.sum(dia, axis=0)
        red_ref[2] += jnp.sum(d_loga * r, axis=0) * (LRU_C * _sigmoid(-pv_ref[2]))

    shp = jax.ShapeDtypeStruct((s, 8, 128), F32)
    return pl.pallas_call(
        body, name=name, out_shape=(shp, shp, shp, shp, jax.ShapeDtypeStruct((8, 8, 128), F32)),
        grid=(nch,), in_specs=[rev] * 7 + [_vec_spec(8)],
        out_specs=(rev, rev, rev, rev, _vec_spec(8)),
        scratch_shapes=[pltpu.VMEM((tc, 8, 128), F32), pltpu.VMEM((tc, 8, 128), F32),
                        pltpu.VMEM((tc, 8, 128), F32), pltpu.VMEM((8, 128), F32)],
        compiler_params=_cparams(("arbitrary",)),
    )(dy, ra, ia, xb, gate, hs, hp, pv)


def _adamw(src, w, m, v, name):
    nl, n, rr, cc = src.shape
    tr = rr
    for cand in sorted((d for d in range(16, rr + 1, 16) if rr % d == 0), reverse=True):
        if cand * cc <= 192 * 1024:
            tr = cand
            break
    c1 = 1.0 - ADAM_B1 ** ADAM_STEP
    c2 = 1.0 - ADAM_B2 ** ADAM_STEP

    def body(s_ref, w_ref, m_ref, v_ref, g_out, d_out, m_out, v_out):
        g = s_ref[0].astype(F32)
        for k in range(1, n):
            g = g + s_ref[k].astype(F32)
        mn = ADAM_B1 * m_ref[...] + (1.0 - ADAM_B1) * g
        vn = ADAM_B2 * v_ref[...] + (1.0 - ADAM_B2) * (g * g)
        g_out[...] = g
        m_out[...] = mn
        v_out[...] = vn
        d_out[...] = (-ADAM_LR) * ((mn / c1) / (jnp.sqrt(vn / c2) + ADAM_EPS) + ADAM_WD * w_ref[...])

    pspec = pl.BlockSpec((None, tr, cc), lambda l, i: (l, i, 0))
    shp = jax.ShapeDtypeStruct((nl, rr, cc), F32)
    return pl.pallas_call(
        body, name=name, out_shape=(shp, shp, shp, shp), grid=(nl, rr // tr),
        in_specs=[pl.BlockSpec((None, n, tr, cc), lambda l, i: (l, 0, i, 0)), pspec, pspec, pspec],
        out_specs=(pspec, pspec, pspec, pspec),
        compiler_params=_cparams(("parallel", "parallel"), VMEM_BIG),
    )(src, w, m, v)


def _sum_devices(x, name):
    _, rr, cc = x.shape

    def body(x_ref, o_ref):
        acc = x_ref[0]
        for k in range(1, NDEV):
            acc = acc + x_ref[k]
        o_ref[...] = acc

    return pl.pallas_call(
        body, name=name, out_shape=jax.ShapeDtypeStruct((rr, cc), F32), grid=(rr // 8,),
        in_specs=[pl.BlockSpec((NDEV, 8, cc), lambda i: (0, i, 0))],
        out_specs=pl.BlockSpec((8, cc), lambda i: (i, 0)),
        compiler_params=_cparams(("parallel",)),
    )(x)


def _to3(x):
    return x.reshape(x.shape[0], 8, 128)


def _heads(x):
    s = x.shape[0]
    return x.reshape(s, HEADS, HDIM).transpose(1, 0, 2).astype(BF16)


def _unheads(x):
    s = x.shape[1]
    return x.transpose(1, 0, 2).reshape(s, D)


def _block_diag(w):
    eye = jnp.eye(HEADS, dtype=w.dtype)
    return (w[:, :, None, :] * eye[:, None, :, None]).reshape(D, D)


def _diag_blocks(x):
    return jnp.diagonal(x.reshape(HEADS, HDIM, HEADS, HDIM), axis1=0, axis2=2).transpose(2, 0, 1)


def _col_blocks(dw, n):
    return dw.reshape(dw.shape[0], NDEV, n).transpose(1, 0, 2)


def kernel(x, c, w_cond, b_cond, norm_pre, norm_post, w_ffn_in, w_ffn_out, fox_w_in, fox_b_f, fox_w_out, sconv_w_in, sconv_conv_w, sconv_w_out, lru_w_in, lru_conv_w, lru_conv_b, lru_w_a, lru_b_a, lru_w_x, lru_b_x, lru_lambda, lru_w_out, loss_target, m_w_cond, m_b_cond, m_norm_pre, m_norm_post, m_w_ffn_in, m_w_ffn_out, m_fox_w_in, m_fox_b_f, m_fox_w_out, m_sconv_w_in, m_sconv_conv_w, m_sconv_w_out, m_lru_w_in, m_lru_conv_w, m_lru_conv_b, m_lru_w_a, m_lru_b_a, m_lru_w_x, m_lru_b_x, m_lru_lambda, m_lru_w_out, v_w_cond, v_b_cond, v_norm_pre, v_norm_post, v_w_ffn_in, v_w_ffn_out, v_fox_w_in, v_fox_b_f, v_fox_w_out, v_sconv_w_in, v_sconv_conv_w, v_sconv_w_out, v_lru_w_in, v_lru_conv_w, v_lru_conv_b, v_lru_w_a, v_lru_b_a, v_lru_w_x, v_lru_b_x, v_lru_lambda, v_lru_w_out):
    me = 4 * lax.axis_index("x") + 2 * lax.axis_index("y") + lax.axis_index("c")
    s = x.shape[1]
    x0 = x[0]
    tgt = loss_target[0]
    wcw = w_cond.shape[2]
    dsh = norm_pre.shape[2]

    small_parts = [c.reshape(-1), norm_pre.reshape(-1), norm_post.reshape(-1), sconv_conv_w.reshape(-1),
                   lru_conv_w.reshape(-1), lru_conv_b.reshape(-1), lru_lambda.reshape(-1)]
    sizes = [p.shape[0] for p in small_parts]
    flat = jnp.concatenate(small_parts)
    padn = (-flat.shape[0]) % 1024
    flat = jnp.pad(flat, (0, padn)).reshape(-1, 1024)
    sm = _all_gather(flat, 0, "ag_small").reshape(NDEV, -1)
    offs = [0]
    for n_ in sizes:
        offs.append(offs[-1] + n_)
    piece = lambda i: sm[:, offs[i]:offs[i + 1]]
    c_all = piece(0)
    unshard = lambda p, lead: p.reshape((NDEV,) + lead + (dsh,)).transpose(
        tuple(range(1, len(lead) + 1)) + (0, len(lead) + 1)).reshape(lead + (D,))
    npre = unshard(piece(1), (DEPTH, 3))
    npost = unshard(piece(2), (DEPTH, 3))
    scw = unshard(piece(3), (3,))
    lcw = unshard(piece(4), (4,))
    lcb = unshard(piece(5), ())
    llam = unshard(piece(6), ())

    c_pad = jnp.pad(c_all, ((0, 128 - NDEV), (0, 0)))
    bc_mine = lax.dynamic_slice(b_cond, (0, me * wcw), (DEPTH, wcw)).reshape(DEPTH, 1, wcw)
    modc, ca_pad = _cond_fwd(c_pad, w_cond, bc_mine, "cond_fwd")
    modg = _all_gather(modc[:, :NDEV, :], 0, "ag_mod")
    mod = lax.dynamic_index_in_dim(modg, me, axis=2, keepdims=False)
    mod = mod.transpose(1, 0, 2).reshape(DEPTH, 3, 3, D)
    ptab = jnp.stack([npre, mod[:, :, 1], mod[:, :, 0], npost, mod[:, :, 2],
                      jnp.zeros_like(npre), jnp.zeros_like(npre), jnp.zeros_like(npre)], axis=2)

    wi_all = _all_gather(w_ffn_in.astype(BF16), 2, "ag_wi")
    wo_all = _all_gather(w_ffn_out.astype(BF16), 2, "ag_wo").reshape(DEPTH, 2, 4, FFB, D)
    fwi = _all_gather(fox_w_in.astype(BF16), 1, "ag_fwi")
    fwi = fwi.transpose(0, 2, 1, 3).reshape(2, D, NDEV * fox_w_in.shape[2])
    fwi = jnp.pad(fwi, ((0, 0), (0, 0), (0, FOXP - fwi.shape[2])))
    fwo = _all_gather(fox_w_out.astype(BF16), 1, "ag_fwo").reshape(2, D, D)
    swi = _all_gather(sconv_w_in.astype(BF16), 1, "ag_swi")[0]
    swi = swi.transpose(1, 0, 2).reshape(D, 3 * D)
    swo = _all_gather(sconv_w_out.astype(BF16), 1, "ag_swo").reshape(D, D)
    lwi = _all_gather(lru_w_in.astype(BF16), 1, "ag_lwi")[0]
    lwi = lwi.transpose(1, 0, 2).reshape(D, 2 * D)
    lwo = _all_gather(lru_w_out.astype(BF16), 1, "ag_lwo").reshape(D, D)
    wgate = jnp.concatenate([_block_diag(lru_w_a[0]), _block_diag(lru_w_x[0])], axis=1).astype(BF16)
    bf_pad = jnp.pad(fox_b_f, ((0, 0), (0, 128 - HEADS))).reshape(2, 1, 128)
    scw3 = scw.reshape(3, 8, 128)
    lcwb = jnp.concatenate([lcw, lcb[None], jnp.zeros((3, D), F32)], axis=0).reshape(8, 8, 128)
    lpv = jnp.concatenate([lru_b_a.reshape(1, D), lru_b_x.reshape(1, D), llam[None],
                           jnp.zeros((5, D), F32)], axis=0).reshape(8, 8, 128)

    subs = [(l, sub) for l in range(DEPTH) for sub in range(3)]
    coef = lambda sub: 1.0 if sub == 1 else 0.5

    saved = {}
    xcur = x0
    h = _prenorm(xcur, ptab, 0, 0, "prenorm")
    for idx, (l, sub) in enumerate(subs):
        tag = f"{l}{sub}"
        sv = {"x": xcur, "h": h}
        if sub != 1:
            f = 0 if sub == 0 else 1
            g, u, act = _ffn_up(h, wi_all, l, f, "ffn_up_" + tag)
            y = _ffn_down(act, wo_all, l, f, "ffn_down_" + tag)
            sv.update(g=g, u=u)
        elif l % 3 == 0:
            j = l // 3
            proj = _mm_nn(h, fwi[j], F32, "fox_in_" + tag)
            q, k, v = (_heads(proj[:, i * D:(i + 1) * D]) for i in range(3))
            fl = proj[:, 3 * D:]
            cum = _fox_cum(fl, bf_pad[j], "fox_cum_" + tag)
            cumt = cum[:, :HEADS].T
            cq, ck = cumt[:, :, None], cumt[:, None, :]
            o, lse = _fox_attn_fwd(q, k, v, cq, ck, "fox_attn_" + tag)
            o2 = _unheads(o)
            y = _mm_nn(o2, fwo[j], F32, "fox_out_" + tag)
            sv.update(q=q, k=k, v=v, fl=fl, cq=cq, ck=ck, lse=lse, o2=o2)
        elif l % 3 == 1:
            proj = _mm_nn(h, swi, F32, "sconv_in_" + tag)
            bg, cg, xv = (_to3(proj[:, i * D:(i + 1) * D]) for i in range(3))
            y3 = _sconv_fwd(bg, cg, xv, scw3, "sconv_mix_" + tag)
            y2 = y3.reshape(s, D).astype(BF16)
            y = _mm_nn(y2, swo, F32, "sconv_out_" + tag)
            sv.update(bg=bg, cg=cg, xv=xv, y2=y2)
        else:
            proj = _mm_nn(h, lwi, F32, "lru_in_" + tag)
            gate3, xp3 = _to3(proj[:, :D]), _to3(proj[:, D:])
            xb3 = _lru_conv_fwd(xp3, lcwb, "lru_conv_" + tag)
            xb2 = xb3.reshape(s, D).astype(BF16)
            gpre = _mm_nn(xb2, wgate, F32, "lru_gate_" + tag)
            ra3, ia3 = _to3(gpre[:, :D]), _to3(gpre[:, D:])
            hs3, hp3, y3 = _lru_scan_fwd(ra3, ia3, xb3, gate3, lpv, "lru_scan_" + tag)
            y2 = y3.reshape(s, D).astype(BF16)
            y = _mm_nn(y2, lwo, F32, "lru_out_" + tag)
            sv.update(gate3=gate3, xp3=xp3, xb3=xb3, xb2=xb2, ra3=ra3, ia3=ia3, hs3=hs3, hp3=hp3, y2=y2)
        sv["y"] = y
        saved[(l, sub)] = sv
        nxt = subs[idx + 1] if idx + 1 < len(subs) else None
        xcur, h = _resid(xcur, y, ptab, l, sub, coef(sub), nxt, "resid_" + tag)

    dx, loss_cols = _loss_grad(xcur, tgt, "loss_grad")

    zero_d = jnp.zeros((D,), F32)
    d_mod = [[[None] * 3 for _ in range(3)] for _ in range(DEPTH)]
    d_npre = [[None] * 3 for _ in range(DEPTH)]
    d_npost = [[None] * 3 for _ in range(DEPTH)]
    dwi_blocks, dwo_blocks = {}, {}
    d_fwi, d_fwo = [None, None], [None, None]
    d_fbf = [None, None]
    small_g = {}
    for (l, sub) in reversed(subs):
        tag = f"{l}{sub}"
        sv = saved[(l, sub)]
        dy, red = _post_bwd(dx, sv["y"], ptab, l, sub, coef(sub), "post_bwd_" + tag)
        d_mod[l][sub][2] = red[0]
        d_npost[l][sub] = red[1]
        hb = sv["h"]
        if sub != 1:
            f = 0 if sub == 0 else 1
            dg, du, dwo = _ffn_down_bwd(dy, sv["g"], sv["u"], wo_all, l, f, "ffn_down_bwd_" + tag)
            dh = _ffn_dh(dg, du, wi_all, l, f, "ffn_dh_" + tag)
            dwi_blocks[(l, f)] = jnp.concatenate(
                [_ffn_dwi(hb, dg, "ffn_dwg_" + tag), _ffn_dwi(hb, du, "ffn_dwu_" + tag)], axis=0)
            dwo_blocks[(l, f)] = dwo.reshape(NDEV, FFB // 2, D)
        elif l % 3 == 0:
            j = l // 3
            do2 = _mm_nt(dy, fwo[j], BF16, "fox_out_dx_" + tag)
            d_fwo[j] = _mm_tn(sv["o2"], dy, BF16, "fox_out_dw_" + tag).reshape(NDEV, dsh, D)
            dq, dk, dv, dcq, dck = _fox_attn_bwd(sv["q"], sv["k"], sv["v"], _heads(do2), sv["lse"],
                                                 sv["cq"], sv["ck"], "fox_attn_bwd_" + tag)
            dcum = (dcq[:, :, 0] + dck[:, 0, :]).T
            dcum = jnp.pad(dcum, ((0, 0), (0, 128 - HEADS)))
            dfl, dbf = _fox_cum_bwd(dcum, sv["fl"], bf_pad[j], "fox_cum_bwd_" + tag)
            d_fbf[j] = dbf[0, :HEADS]
            dproj = jnp.concatenate([_unheads(dq), _unheads(dk), _unheads(dv), dfl.astype(BF16)], axis=1)
            dh = _mm_nt(dproj, fwi[j], F32, "fox_in_dx_" + tag)
            dwf = _mm_tn(hb, dproj, BF16, "fox_in_dw_" + tag)
            d_fwi[j] = _col_blocks(dwf[:, :NDEV * fox_w_in.shape[2]], fox_w_in.shape[2])
        elif l % 3 == 1:
            dy3 = _to3(_mm_nt(dy, swo, F32, "sconv_out_dx_" + tag))
            d_swo = _mm_tn(sv["y2"], dy, BF16, "sconv_out_dw_" + tag).reshape(NDEV, dsh, D)
            dbg, dcg, dxv, dscw = _sconv_bwd(dy3, sv["bg"], sv["cg"], sv["xv"], scw3, "sconv_mix_bwd_" + tag)
            small_g["sconv_conv_w"] = dscw.reshape(3, D)
            dproj = jnp.concatenate([t.reshape(s, D) for t in (dbg, dcg, dxv)], axis=1).astype(BF16)
            dh = _mm_nt(dproj, swi, F32, "sconv_in_dx_" + tag)
            d_swi = _col_blocks(_mm_tn(hb, dproj, BF16, "sconv_in_dw_" + tag), sconv_w_in.shape[2])
        else:
            dy3 = _to3(_mm_nt(dy, lwo, F32, "lru_out_dx_" + tag))
            d_lwo = _mm_tn(sv["y2"], dy, BF16, "lru_out_dw_" + tag).reshape(NDEV, dsh, D)
            dgate3, dra3, dia3, dxb3, lred = _lru_scan_bwd(
                dy3, sv["ra3"], sv["ia3"], sv["xb3"], sv["gate3"], sv["hs3"], sv["hp3"], lpv, "lru_scan_bwd_" + tag)
            dgp = jnp.concatenate([dra3.reshape(s, D), dia3.reshape(s, D)], axis=1).astype(BF16)
            dxb3 = dxb3 + _to3(_mm_nt(dgp, wgate, F32, "lru_gate_dx_" + tag))
            dwgate = _mm_tn(sv["xb2"], dgp, F32, "lru_gate_dw_" + tag)
            dxp3, cred = _lru_conv_bwd(dxb3, sv["xp3"], lcwb, "lru_conv_bwd_" + tag)
            lred, cred = lred.reshape(8, D), cred.reshape(8, D)
            small_g.update(lru_w_a=_diag_blocks(dwgate[:, :D]), lru_w_x=_diag_blocks(dwgate[:, D:]),
                           lru_b_a=lred[0], lru_b_x=lred[1], lru_lambda=lred[2],
                           lru_conv_w=cred[:4], lru_conv_b=cred[4])
            dproj = jnp.concatenate([dgate3.reshape(s, D), dxp3.reshape(s, D)], axis=1).astype(BF16)
            dh = _mm_nt(dproj, lwi, F32, "lru_in_dx_" + tag)
            d_lwi = _col_blocks(_mm_tn(hb, dproj, BF16, "lru_in_dw_" + tag), lru_w_in.shape[2])
        dx, red = _pre_bwd(dx, dh, sv["x"], ptab, l, sub, "pre_bwd_" + tag)
        d_mod[l][sub][0] = red[0]
        d_mod[l][sub][1] = red[1]
        d_npre[l][sub] = red[2]
    grad_x = dx[None]

    dmod_mine = jnp.stack([jnp.stack([jnp.stack(d_mod[l][sub]) for sub in range(3)]) for l in range(DEPTH)])
    gparts = [loss_cols[0], dmod_mine.reshape(-1),
              jnp.stack([jnp.stack(r_) for r_ in d_npre]).reshape(-1),
              jnp.stack([jnp.stack(r_) for r_ in d_npost]).reshape(-1),
              jnp.stack(d_fbf).reshape(-1), small_g["sconv_conv_w"].reshape(-1),
              small_g["lru_conv_w"].reshape(-1), small_g["lru_conv_b"].reshape(-1),
              small_g["lru_w_a"].reshape(-1), small_g["lru_b_a"].reshape(-1),
              small_g["lru_w_x"].reshape(-1), small_g["lru_b_x"].reshape(-1),
              small_g["lru_lambda"].reshape(-1)]
    gsizes = [p.shape[0] for p in gparts]
    gflat = jnp.concatenate(gparts)
    gflat = jnp.pad(gflat, (0, (-gflat.shape[0]) % (8 * 1024))).reshape(-1, 1024)
    gall = _all_gather(gflat, 0, "ag_smallgrads")
    gsum = _sum_devices(gall, "sum_smallgrads").reshape(-1)
    goffs = [0]
    for n_ in gsizes:
        goffs.append(goffs[-1] + n_)
    gpiece = lambda i: gsum[goffs[i]:goffs[i + 1]]
    loss = jnp.sum(gpiece(0))
    my_ch = lambda g, lead: lax.dynamic_slice_in_dim(g.reshape(lead + (D,)), me * dsh, dsh, axis=len(lead))

    dmod_all = gall.reshape(NDEV, -1)[:, goffs[1]:goffs[2]].reshape(NDEV, DEPTH, 3 * 3 * D)
    dmod_cols = lax.dynamic_slice_in_dim(dmod_all, me * wcw, wcw, axis=2).transpose(1, 0, 2)
    dmod_cols = jnp.pad(dmod_cols, ((0, 0), (0, 128 - NDEV), (0, 0))).astype(BF16)
    g_wcond = _cond_bwd(ca_pad, dmod_cols, "cond_bwd")

    def update(blocks, w, m, v, name):
        got = _all_to_all(blocks, 1, "a2a_" + name)
        shp = w.shape
        w3, m3, v3 = (t.reshape((blocks.shape[0],) + blocks.shape[2:]) for t in (w, m, v))
        return tuple(t.reshape(shp) for t in _adamw(got, w3, m3, v3, "adamw_" + name))

    lf = [(l, f) for l in range(DEPTH) for f in range(2)]
    out = {}
    out["w_ffn_in"] = update(jnp.stack([dwi_blocks[k_] for k_ in lf]), w_ffn_in, m_w_ffn_in, v_w_ffn_in, "wi")
    out["w_ffn_out"] = update(jnp.stack([dwo_blocks[k_] for k_ in lf]), w_ffn_out, m_w_ffn_out, v_w_ffn_out, "wo")
    out["fox_w_in"] = update(jnp.stack(d_fwi), fox_w_in, m_fox_w_in, v_fox_w_in, "fwi")
    out["fox_w_out"] = update(jnp.stack(d_fwo), fox_w_out, m_fox_w_out, v_fox_w_out, "fwo")
    out["sconv_w_in"] = update(d_swi[None], sconv_w_in, m_sconv_w_in, v_sconv_w_in, "swi")
    out["sconv_w_out"] = update(d_swo[None], sconv_w_out, m_sconv_w_out, v_sconv_w_out, "swo")
    out["lru_w_in"] = update(d_lwi[None], lru_w_in, m_lru_w_in, v_lru_w_in, "lwi")
    out["lru_w_out"] = update(d_lwo[None], lru_w_out, m_lru_w_out, v_lru_w_out, "lwo")
    out["w_cond"] = _adamw(g_wcond[:, None], w_cond, m_w_cond, v_w_cond, "adamw_wcond")

    small = [
        ("b_cond", b_cond, m_b_cond, v_b_cond, gpiece(1)),
        ("norm_pre", norm_pre, m_norm_pre, v_norm_pre, my_ch(gpiece(2), (DEPTH, 3))),
        ("norm_post", norm_post, m_norm_post, v_norm_post, my_ch(gpiece(3), (DEPTH, 3))),
        ("fox_b_f", fox_b_f, m_fox_b_f, v_fox_b_f, gpiece(4)),
        ("sconv_conv_w", sconv_conv_w, m_sconv_conv_w, v_sconv_conv_w, my_ch(gpiece(5), (1, 3))),
        ("lru_conv_w", lru_conv_w, m_lru_conv_w, v_lru_conv_w, my_ch(gpiece(6), (1, 4))),
        ("lru_conv_b", lru_conv_b, m_lru_conv_b, v_lru_conv_b, my_ch(gpiece(7), (1,))),
        ("lru_w_a", lru_w_a, m_lru_w_a, v_lru_w_a, gpiece(8)),
        ("lru_b_a", lru_b_a, m_lru_b_a, v_lru_b_a, gpiece(9)),
        ("lru_w_x", lru_w_x, m_lru_w_x, v_lru_w_x, gpiece(10)),
        ("lru_b_x", lru_b_x, m_lru_b_x, v_lru_b_x, gpiece(11)),
        ("lru_lambda", lru_lambda, m_lru_lambda, v_lru_lambda, my_ch(gpiece(12), (1,))),
    ]
    pack = lambda ts: jnp.concatenate([t.reshape(-1) for t in ts])
    ssz = [w_.size for _, w_, _, _, _ in small]
    tot = sum(ssz)
    padr = (-tot) % (16 * 1024)
    pk = lambda ts: jnp.pad(pack(ts), (0, padr)).reshape(1, -1, 1024)
    sg, sd, smm, svv = _adamw(pk([t[4] for t in small])[:, None], pk([t[1] for t in small]),
                              pk([t[2] for t in small]), pk([t[3] for t in small]), "adamw_small")
    soff = 0
    for (name, w_, _, _, _), n_ in zip(small, ssz):
        out[name] = tuple(t.reshape(-1)[soff:soff + n_].reshape(w_.shape) for t in (sg, sd, smm, svv))
        soff += n_

    names = ["w_cond", "b_cond", "norm_pre", "norm_post", "w_ffn_in", "w_ffn_out", "fox_w_in", "fox_b_f",
             "fox_w_out", "sconv_w_in", "sconv_conv_w", "sconv_w_out", "lru_w_in", "lru_conv_w", "lru_conv_b",
             "lru_w_a", "lru_b_a", "lru_w_x", "lru_b_x", "lru_lambda", "lru_w_out"]
    return (loss, grad_x, *[out[n_][0] for n_ in names], *[out[n_][1] for n_ in names],
            *[out[n_][2] for n_ in names], *[out[n_][3] for n_ in names])
```

```python
import functools
import math

import jax
import jax.numpy as jnp
from jax import lax
from jax.experimental import pallas as pl
from jax.experimental.pallas import tpu as pltpu

F32 = jnp.float32
BF16 = jnp.bfloat16
NDEV = 8
D = 1024
DFF = 2816
FFB = 704
HEADS = 16
HDIM = 64
DEPTH = 4
RMS_EPS = 1e-6
LRU_C = 8.0
ADAM_LR, ADAM_B1, ADAM_B2, ADAM_EPS, ADAM_WD, ADAM_STEP = 0.001, 0.9, 0.999, 1e-08, 0.01, 10
FOXP = 3200
MESH = pl.DeviceIdType.MESH
HBM = pl.BlockSpec(memory_space=pltpu.HBM)
VMEM_BIG = 48 * 1024 * 1024

_NN = (((1,), (0,)), ((), ()))
_NT = (((1,), (1,)), ((), ()))
_TN = (((0,), (0,)), ((), ()))
_DIMS = {"nn": _NN, "nt": _NT, "tn": _TN}


def _cparams(sem=None, vmem=None):
    kw = {}
    if sem is not None:
        kw["dimension_semantics"] = sem
    if vmem is not None:
        kw["vmem_limit_bytes"] = vmem
    return pltpu.CompilerParams(**kw)


def _sigmoid(x):
    return 1.0 / (1.0 + jnp.exp(-x))


def _softplus(x):
    return jnp.maximum(x, 0.0) + jnp.log(1.0 + jnp.exp(-jnp.abs(x)))


_GELU_C = math.sqrt(2.0 / math.pi)


def _gelu_parts(x):
    u = _GELU_C * (x + 0.044715 * x * x * x)
    t = jnp.tanh(u)
    g = 0.5 * x * (1.0 + t)
    dg = 0.5 * (1.0 + t) + 0.5 * x * (1.0 - t * t) * _GELU_C * (1.0 + 3.0 * 0.044715 * x * x)
    return g, dg


def _mesh_pos():
    ax, ay, ac = lax.axis_index("x"), lax.axis_index("y"), lax.axis_index("c")
    return ax, ay, ac, 4 * ax + 2 * ay + ac


def _peer(ax, ay, ac, d):
    px = 1 - ax if (d >> 2) & 1 else ax
    py = 1 - ay if (d >> 1) & 1 else ay
    pc = 1 - ac if d & 1 else ac
    return (px, py, pc), 4 * px + 2 * py + pc


def _exchange(x, axis, name, gather):
    if gather:
        x = jnp.expand_dims(x, axis)
        oshape = x.shape[:axis] + (NDEV,) + x.shape[axis + 1:]
    else:
        oshape = x.shape
    lead = (slice(None),) * axis

    def blk(ref, k):
        return ref.at[lead + (pl.ds(k, 1),)]

    def body(x_ref, o_ref, send_sems, recv_sems, local_sem):
        ax, ay, ac, me = _mesh_pos()
        src = (lambda k: x_ref) if gather else (lambda k: blk(x_ref, k))
        mine = pltpu.make_async_copy(src(me), blk(o_ref, me), local_sem)
        mine.start()
        sends = []
        for d in range(1, NDEV):
            peer, pidx = _peer(ax, ay, ac, d)
            cp = pltpu.make_async_remote_copy(
                src_ref=src(pidx), dst_ref=blk(o_ref, me), send_sem=send_sems.at[d - 1],
                recv_sem=recv_sems.at[d - 1], device_id=peer, device_id_type=MESH)
            cp.start()
            sends.append(cp)
        for d in range(1, NDEV):
            peer, pidx = _peer(ax, ay, ac, d)
            pltpu.make_async_remote_copy(
                src_ref=src(pidx), dst_ref=blk(o_ref, pidx), send_sem=send_sems.at[d - 1],
                recv_sem=recv_sems.at[d - 1], device_id=peer, device_id_type=MESH).wait_recv()
        for cp in sends:
            cp.wait_send()
        mine.wait()

    return pl.pallas_call(
        body, name=name, out_shape=jax.ShapeDtypeStruct(oshape, x.dtype),
        in_specs=[HBM], out_specs=HBM,
        scratch_shapes=[pltpu.SemaphoreType.DMA((NDEV - 1,)), pltpu.SemaphoreType.DMA((NDEV - 1,)),
                        pltpu.SemaphoreType.DMA(())],
    )(x)


def _all_gather(x, axis, name):
    return _exchange(x, axis, name, True)


SEM = pl.BlockSpec(memory_space=pltpu.SEMAPHORE)
EFFECT = pltpu.SideEffectType.DATAFLOW_SIDE_EFFECTING


def _exchange_start(x, name, gather):
    if gather:
        x = x[None]
        oshape = (NDEV,) + x.shape[1:]
    else:
        oshape = x.shape

    def blk(ref, k):
        return ref.at[pl.ds(k, 1)]

    def body(x_ref, land_ref, send_sems, recv_sems, x_thru, land_thru, token, local_sem):
        ax, ay, ac, me = _mesh_pos()
        src = (lambda k: x_ref) if gather else (lambda k: blk(x_ref, k))
        mine = pltpu.make_async_copy(src(me), blk(land_ref, me), local_sem)
        mine.start()
        for d in range(1, NDEV):
            peer, pidx = _peer(ax, ay, ac, d)
            pltpu.make_async_remote_copy(
                src_ref=src(pidx), dst_ref=blk(land_ref, me), send_sem=send_sems.at[d - 1],
                recv_sem=recv_sems.at[d - 1], device_id=peer, device_id_type=MESH).start()
        mine.wait()
        token[...] = jnp.zeros_like(token)

    return pl.pallas_call(
        body, name=name,
        out_shape=(pltpu.SemaphoreType.DMA((NDEV - 1,)), pltpu.SemaphoreType.DMA((NDEV - 1,)),
                   pltpu.HBM(x.shape, x.dtype), pltpu.HBM(oshape, x.dtype),
                   jax.ShapeDtypeStruct((8, 128), F32)),
        in_specs=(HBM, HBM), out_specs=(SEM, SEM, HBM, HBM, pl.BlockSpec(memory_space=pltpu.VMEM)),
        input_output_aliases={0: 2, 1: 3},
        scratch_shapes=[pltpu.SemaphoreType.DMA(())],
        compiler_params=pltpu.CompilerParams(has_side_effects=EFFECT),
    )(pltpu.with_memory_space_constraint(x, pltpu.HBM),
      pltpu.with_memory_space_constraint(lax.empty(oshape, x.dtype), pltpu.HBM))


def _exchange_wait(handle, after, name, gather):
    send_sems, recv_sems, x_thru, land_thru = handle

    def blk(ref, k):
        return ref.at[pl.ds(k, 1)]

    def body(x_ref, land_ref, send_sems, recv_sems, after_ref, x_dead, got_ref):
        ax, ay, ac, me = _mesh_pos()
        src = (lambda k: x_ref) if gather else (lambda k: blk(x_ref, k))
        for d in range(1, NDEV):
            peer, pidx = _peer(ax, ay, ac, d)
            cp = pltpu.make_async_remote_copy(
                src_ref=src(pidx), dst_ref=blk(land_ref, pidx), send_sem=send_sems.at[d - 1],
                recv_sem=recv_sems.at[d - 1], device_id=peer, device_id_type=MESH)
            cp.wait_send()
            cp.wait_recv()

    return pl.pallas_call(
        body, name=name,
        out_shape=(pltpu.HBM(x_thru.shape, x_thru.dtype), pltpu.HBM(land_thru.shape, land_thru.dtype)),
        in_specs=(HBM, HBM, SEM, SEM, pl.BlockSpec(memory_space=pl.ANY)), out_specs=(HBM, HBM),
        input_output_aliases={0: 0, 1: 1},
        compiler_params=pltpu.CompilerParams(has_side_effects=EFFECT),
    )(x_thru, land_thru, send_sems, recv_sems, after)[1]


def _mm(a, b, *, mode, grid, a_spec, b_spec, o_spec, o_shape, acc_shape, out_dtype, name):
    nred = grid[2]

    def body(a_ref, b_ref, o_ref, *scratch):
        p = lax.dot_general(a_ref[...], b_ref[...], _DIMS[mode], preferred_element_type=F32)
        if nred == 1:
            o_ref[...] = p.astype(o_ref.dtype)
            return
        acc = scratch[0]
        r = pl.program_id(2)

        @pl.when(r == 0)
        def _():
            acc[...] = p

        @pl.when(r > 0)
        def _():
            acc[...] += p

        @pl.when(r == nred - 1)
        def _():
            o_ref[...] = acc[...].astype(o_ref.dtype)

    return pl.pallas_call(
        body, name=name, out_shape=jax.ShapeDtypeStruct(o_shape, out_dtype), grid=grid,
        in_specs=[a_spec, b_spec], out_specs=o_spec,
        scratch_shapes=[] if nred == 1 else [pltpu.VMEM(acc_shape, F32)],
        compiler_params=_cparams(("parallel", "parallel", "arbitrary"), VMEM_BIG),
    )(a, b)


def _tile(n, cands):
    for c in cands:
        if n % c == 0:
            return c
    return n


def _mm_nn(a, b, out_dtype, name):
    m, k = a.shape
    n = b.shape[1]
    tm, tn = _tile(m, (512, 256, 128)), _tile(n, (1024, 640, 512, 256, 128))
    return _mm(a, b, mode="nn", grid=(m // tm, n // tn, 1),
               a_spec=pl.BlockSpec((tm, k), lambda i, j, r: (i, 0)),
               b_spec=pl.BlockSpec((k, tn), lambda i, j, r: (0, j)),
               o_spec=pl.BlockSpec((tm, tn), lambda i, j, r: (i, j)),
               o_shape=(m, n), acc_shape=None, out_dtype=out_dtype, name=name)


def _mm_nt(a, b, out_dtype, name):
    m, n = a.shape
    k = b.shape[0]
    tm, tn = _tile(m, (512, 256, 128)), _tile(n, (1024, 640, 512, 256, 128))
    return _mm(a, b, mode="nt", grid=(m // tm, 1, n // tn),
               a_spec=pl.BlockSpec((tm, tn), lambda i, j, r: (i, r)),
               b_spec=pl.BlockSpec((k, tn), lambda i, j, r: (0, r)),
               o_spec=pl.BlockSpec((tm, k), lambda i, j, r: (i, 0)),
               o_shape=(m, k), acc_shape=(tm, k), out_dtype=out_dtype, name=name)


def _mm_tn(a, b, out_dtype, name):
    s, k = a.shape
    n = b.shape[1]
    ts, tn = _tile(s, (512, 256, 128)), _tile(n, (640, 512, 256, 128))
    return _mm(a, b, mode="tn", grid=(1, n // tn, s // ts),
               a_spec=pl.BlockSpec((ts, k), lambda i, j, r: (r, 0)),
               b_spec=pl.BlockSpec((ts, tn), lambda i, j, r: (r, j)),
               o_spec=pl.BlockSpec((k, tn), lambda i, j, r: (0, j)),
               o_shape=(k, n), acc_shape=(k, tn), out_dtype=out_dtype, name=name)


def _ffn_up(h, wi, name):
    s = h.shape[0]
    tm = _tile(s, (512, 256, 128))

    def body(h_ref, wg_ref, wu_ref, g_ref, u_ref, act_ref):
        hv = h_ref[...]
        g = jnp.dot(hv, wg_ref[...], preferred_element_type=F32)
        u = jnp.dot(hv, wu_ref[...], preferred_element_type=F32)
        g_ref[...] = g.astype(BF16)
        u_ref[...] = u.astype(BF16)
        act_ref[...] = (g * _sigmoid(g) * u).astype(BF16)

    wspec = lambda off: pl.BlockSpec((None, D, FFB), lambda j, i: (j + off, 0, 0))
    ospec = pl.BlockSpec((None, tm, FFB), lambda j, i: (j, i, 0))
    shp = jax.ShapeDtypeStruct((4, s, FFB), BF16)
    return pl.pallas_call(
        body, name=name, out_shape=(shp, shp, shp), grid=(4, s // tm),
        in_specs=[pl.BlockSpec((tm, D), lambda j, i: (i, 0)), wspec(0), wspec(4)],
        out_specs=(ospec, ospec, ospec),
        compiler_params=_cparams(("parallel", "parallel"), VMEM_BIG),
    )(h, wi, wi)


def _ffn_down(act, wo, name):
    s = act.shape[1]
    tm = _tile(s, (512, 256, 128))
    return _mm(act, wo, mode="nn", grid=(s // tm, 1, 4),
               a_spec=pl.BlockSpec((None, tm, FFB), lambda i, j, r: (r, i, 0)),
               b_spec=pl.BlockSpec((None, FFB, D), lambda i, j, r: (r, 0, 0)),
               o_spec=pl.BlockSpec((tm, D), lambda i, j, r: (i, 0)),
               o_shape=(s, D), acc_shape=(tm, D), out_dtype=F32, name=name)


def _ffn_down_bwd(dy, g, u, wo, name):
    s = dy.shape[0]
    tm = _tile(s, (512, 256, 128))
    ns = s // tm

    def body(dy_ref, g_ref, u_ref, wo_ref, dg_ref, du_ref, dwo_ref, acc):
        i = pl.program_id(1)
        dyv = dy_ref[...]
        dact = lax.dot_general(dyv, wo_ref[...], _NT, preferred_element_type=F32)
        gv = g_ref[...].astype(F32)
        uv = u_ref[...].astype(F32)
        sg = _sigmoid(gv)
        silu = gv * sg
        dg_ref[...] = (dact * uv * (sg * (1.0 + gv * (1.0 - sg)))).astype(BF16)
        du_ref[...] = (dact * silu).astype(BF16)
        p = lax.dot_general((silu * uv).astype(BF16), dyv, _TN, preferred_element_type=F32)

        @pl.when(i == 0)
        def _():
            acc[...] = p

        @pl.when(i > 0)
        def _():
            acc[...] += p

        @pl.when(i == ns - 1)
        def _():
            dwo_ref[...] = acc[...].astype(BF16)

    aspec = pl.BlockSpec((None, tm, FFB), lambda j, i: (j, i, 0))
    shp = jax.ShapeDtypeStruct((4, s, FFB), BF16)
    return pl.pallas_call(
        body, name=name, out_shape=(shp, shp, jax.ShapeDtypeStruct((4, FFB, D), BF16)),
        grid=(4, ns),
        in_specs=[pl.BlockSpec((tm, D), lambda j, i: (i, 0)), aspec, aspec,
                  pl.BlockSpec((None, FFB, D), lambda j, i: (j, 0, 0))],
        out_specs=(aspec, aspec, pl.BlockSpec((None, FFB, D), lambda j, i: (j, 0, 0))),
        scratch_shapes=[pltpu.VMEM((FFB, D), F32)],
        compiler_params=_cparams(("parallel", "arbitrary"), VMEM_BIG),
    )(dy, g, u, wo)


def _ffn_dh(dg, du, wi, name):
    s = dg.shape[1]
    tm = _tile(s, (512, 256, 128))

    def body(dg_ref, du_ref, wg_ref, wu_ref, o_ref, acc):
        r = pl.program_id(1)
        p = (lax.dot_general(dg_ref[...], wg_ref[...], _NT, preferred_element_type=F32)
             + lax.dot_general(du_ref[...], wu_ref[...], _NT, preferred_element_type=F32))

        @pl.when(r == 0)
        def _():
            acc[...] = p

        @pl.when(r > 0)
        def _():
            acc[...] += p

        @pl.when(r == 3)
        def _():
            o_ref[...] = acc[...]

    aspec = pl.BlockSpec((None, tm, FFB), lambda i, r: (r, i, 0))
    wspec = lambda off: pl.BlockSpec((None, D, FFB), lambda i, r: (r + off, 0, 0))
    return pl.pallas_call(
        body, name=name, out_shape=jax.ShapeDtypeStruct((s, D), F32), grid=(s // tm, 4),
        in_specs=[aspec, aspec, wspec(0), wspec(4)],
        out_specs=pl.BlockSpec((tm, D), lambda i, r: (i, 0)),
        scratch_shapes=[pltpu.VMEM((tm, D), F32)],
        compiler_params=_cparams(("parallel", "arbitrary"), VMEM_BIG),
    )(dg, du, wi, wi)


def _ffn_dwi(h, da, name):
    s = h.shape[0]
    ts = _tile(s, (512, 256, 128))
    return _mm(h, da, mode="tn", grid=(4, 1, s // ts),
               a_spec=pl.BlockSpec((ts, D), lambda k, j, r: (r, 0)),
               b_spec=pl.BlockSpec((None, ts, FFB), lambda k, j, r: (k, r, 0)),
               o_spec=pl.BlockSpec((None, D, FFB), lambda k, j, r: (k, 0, 0)),
               o_shape=(4, D, FFB), acc_shape=(D, FFB), out_dtype=BF16, name=name)


TR = 256


def _rows_spec(s):
    tr = _tile(s, (TR, 128))
    return tr, pl.BlockSpec((tr, D), lambda i: (i, 0))


def _pspec(l, sub):
    return pl.BlockSpec((None, None, 8, D), lambda i: (l, sub, 0, 0))


def _pre_math(x, p_ref):
    r = lax.rsqrt(jnp.mean(x * x, axis=1, keepdims=True) + RMS_EPS)
    return (x * r) * p_ref[0:1, :] * (1.0 + p_ref[1:2, :]) + p_ref[2:3, :]


def _prenorm(x, ptab, l, sub, name):
    s = x.shape[0]
    tr, spec = _rows_spec(s)

    def body(x_ref, p_ref, h_ref):
        h_ref[...] = _pre_math(x_ref[...], p_ref).astype(BF16)

    return pl.pallas_call(
        body, name=name, out_shape=jax.ShapeDtypeStruct((s, D), BF16), grid=(s // tr,),
        in_specs=[spec, _pspec(l, sub)], out_specs=spec,
        compiler_params=_cparams(("parallel",)),
    )(x, ptab)


def _resid(x, y, ptab, l, sub, coef, nxt, name):
    s = x.shape[0]
    tr, spec = _rows_spec(s)

    def body(x_ref, y_ref, p_ref, *rest):
        yv = y_ref[...]
        r = lax.rsqrt(jnp.mean(yv * yv, axis=1, keepdims=True) + RMS_EPS)
        xn = x_ref[...] + (coef * p_ref[4:5, :]) * ((yv * r) * p_ref[3:4, :])
        if nxt is None:
            rest[0][...] = xn
        else:
            rest[1][...] = xn
            rest[2][...] = _pre_math(xn, rest[0]).astype(BF16)

    if nxt is None:
        return pl.pallas_call(
            body, name=name, out_shape=jax.ShapeDtypeStruct((s, D), F32), grid=(s // tr,),
            in_specs=[spec, spec, _pspec(l, sub)], out_specs=spec,
            compiler_params=_cparams(("parallel",)),
        )(x, y, ptab), None
    return pl.pallas_call(
        body, name=name,
        out_shape=(jax.ShapeDtypeStruct((s, D), F32), jax.ShapeDtypeStruct((s, D), BF16)),
        grid=(s // tr,),
        in_specs=[spec, spec, _pspec(l, sub), _pspec(*nxt)], out_specs=(spec, spec),
        compiler_params=_cparams(("parallel",)),
    )(x, y, ptab, ptab)


def _loss_grad(x, tgt, name):
    s = x.shape[0]
    tr, spec = _rows_spec(s)

    def body(x_ref, t_ref, dx_ref, l_ref):
        @pl.when(pl.program_id(0) == 0)
        def _():
            l_ref[...] = jnp.zeros_like(l_ref)

        e = x_ref[...] - t_ref[...]
        dx_ref[...] = e * (1.0 / D)
        l_ref[0:1, :] += jnp.sum(e * e, axis=0, keepdims=True) * (0.5 / D)

    return pl.pallas_call(
        body, name=name,
        out_shape=(jax.ShapeDtypeStruct((s, D), F32), jax.ShapeDtypeStruct((8, D), F32)),
        grid=(s // tr,), in_specs=[spec, spec],
        out_specs=(spec, pl.BlockSpec((8, D), lambda i: (0, 0))),
        compiler_params=_cparams(("arbitrary",)),
    )(x, tgt)


def _post_bwd(dx, y, ptab, l, sub, coef, name):
    s = dx.shape[0]
    tr, spec = _rows_spec(s)

    def body(dx_ref, y_ref, p_ref, dy_ref, red_ref):
        @pl.when(pl.program_id(0) == 0)
        def _():
            red_ref[...] = jnp.zeros_like(red_ref)

        dxv, yv = dx_ref[...], y_ref[...]
        gpost, gate = p_ref[3:4, :], p_ref[4:5, :]
        r = lax.rsqrt(jnp.mean(yv * yv, axis=1, keepdims=True) + RMS_EPS)
        yhat = yv * r
        red_ref[0:1, :] += jnp.sum(dxv * yhat * gpost, axis=0, keepdims=True) * coef
        dn = dxv * (coef * gate)
        red_ref[1:2, :] += jnp.sum(dn * yhat, axis=0, keepdims=True)
        dyh = dn * gpost
        dy_ref[...] = (r * (dyh - yhat * jnp.mean(dyh * yhat, axis=1, keepdims=True))).astype(BF16)

    return pl.pallas_call(
        body, name=name,
        out_shape=(jax.ShapeDtypeStruct((s, D), BF16), jax.ShapeDtypeStruct((8, D), F32)),
        grid=(s // tr,), in_specs=[spec, spec, _pspec(l, sub)],
        out_specs=(spec, pl.BlockSpec((8, D), lambda i: (0, 0))),
        compiler_params=_cparams(("arbitrary",)),
    )(dx, y, ptab)


def _pre_bwd(dx, dh, x, ptab, l, sub, name):
    s = dx.shape[0]
    tr, spec = _rows_spec(s)

    def body(dx_ref, dh_ref, x_ref, p_ref, o_ref, red_ref):
        @pl.when(pl.program_id(0) == 0)
        def _():
            red_ref[...] = jnp.zeros_like(red_ref)

        dhv, xv = dh_ref[...], x_ref[...]
        gpre, scale = p_ref[0:1, :], p_ref[1:2, :]
        r = lax.rsqrt(jnp.mean(xv * xv, axis=1, keepdims=True) + RMS_EPS)
        xhat = xv * r
        red_ref[0:1, :] += jnp.sum(dhv, axis=0, keepdims=True)
        red_ref[1:2, :] += jnp.sum(dhv * xhat * gpre, axis=0, keepdims=True)
        red_ref[2:3, :] += jnp.sum(dhv * xhat * (1.0 + scale), axis=0, keepdims=True)
        dxh = dhv * (gpre * (1.0 + scale))
        o_ref[...] = dx_ref[...] + r * (dxh - xhat * jnp.mean(dxh * xhat, axis=1, keepdims=True))

    return pl.pallas_call(
        body, name=name,
        out_shape=(jax.ShapeDtypeStruct((s, D), F32), jax.ShapeDtypeStruct((8, D), F32)),
        grid=(s // tr,), in_specs=[spec, spec, spec, _pspec(l, sub)],
        out_specs=(spec, pl.BlockSpec((8, D), lambda i: (0, 0))),
        compiler_params=_cparams(("arbitrary",)),
    )(dx, dh, x, ptab)


def _cond_fwd(c_pad, wc, bc, name):
    w = wc.shape[2]

    def body(c_ref, w_ref, b_ref, o_ref, ca_ref):
        cv = c_ref[...]
        ca = (cv * _sigmoid(cv)).astype(BF16)
        ca_ref[...] = ca
        o_ref[...] = jnp.dot(ca, w_ref[...].astype(BF16), preferred_element_type=F32) + b_ref[...]

    return pl.pallas_call(
        body, name=name,
        out_shape=(jax.ShapeDtypeStruct((DEPTH, 128, w), F32), jax.ShapeDtypeStruct((128, D), BF16)),
        grid=(DEPTH,),
        in_specs=[pl.BlockSpec((128, D), lambda i: (0, 0)),
                  pl.BlockSpec((None, D, w), lambda i: (i, 0, 0)),
                  pl.BlockSpec((None, 1, w), lambda i: (i, 0, 0))],
        out_specs=(pl.BlockSpec((None, 128, w), lambda i: (i, 0, 0)),
                   pl.BlockSpec((128, D), lambda i: (0, 0))),
        compiler_params=_cparams(("arbitrary",), VMEM_BIG),
    )(c_pad, wc, bc)


def _cond_bwd(ca_pad, dmod, name):
    w = dmod.shape[2]
    return _mm(ca_pad, dmod, mode="tn", grid=(DEPTH, 1, 1),
               a_spec=pl.BlockSpec((128, D), lambda i, j, r: (0, 0)),
               b_spec=pl.BlockSpec((None, 128, w), lambda i, j, r: (i, 0, 0)),
               o_spec=pl.BlockSpec((None, D, w), lambda i, j, r: (i, 0, 0)),
               o_shape=(DEPTH, D, w), acc_shape=None, out_dtype=F32, name=name)


def _split3(x):
    hi = x.astype(BF16)
    r1 = x - hi.astype(F32)
    mid = r1.astype(BF16)
    lo = (r1 - mid.astype(F32)).astype(BF16)
    return hi, mid, lo


def _tri_dot(t, x):
    hi, mid, lo = _split3(x)
    return (jnp.dot(t, hi, preferred_element_type=F32) + jnp.dot(t, mid, preferred_element_type=F32)
            + jnp.dot(t, lo, preferred_element_type=F32))


def _fox_cum(fl, bf, name):
    s = fl.shape[0]
    tb = _tile(s, (256, 128))

    def body(fl_ref, b_ref, cum_ref):
        row = lax.broadcasted_iota(jnp.int32, (tb, tb), 0)
        col = lax.broadcasted_iota(jnp.int32, (tb, tb), 1)
        tri = (col <= row).astype(BF16)
        carry = jnp.zeros((1, 128), F32)
        for blk in range(s // tb):
            z = fl_ref[blk * tb:(blk + 1) * tb, :] + b_ref[0:1, :]
            lf = jnp.minimum(z, 0.0) - jnp.log(1.0 + jnp.exp(-jnp.abs(z)))
            cum_ref[blk * tb:(blk + 1) * tb, :] = _tri_dot(tri, lf) + carry
            carry = carry + jnp.sum(lf, axis=0, keepdims=True)

    return pl.pallas_call(
        body, name=name, out_shape=jax.ShapeDtypeStruct((s, 128), F32),
    )(fl, bf)


def _fox_cum_bwd(dcum, fl, bf, name):
    s = fl.shape[0]
    tb = _tile(s, (256, 128))

    def body(dc_ref, fl_ref, b_ref, dfl_ref, db_ref):
        row = lax.broadcasted_iota(jnp.int32, (tb, tb), 0)
        col = lax.broadcasted_iota(jnp.int32, (tb, tb), 1)
        tri = (col >= row).astype(BF16)
        carry = jnp.zeros((1, 128), F32)
        dbs = jnp.zeros((1, 128), F32)
        for blk in reversed(range(s // tb)):
            dc = dc_ref[blk * tb:(blk + 1) * tb, :]
            dl = _tri_dot(tri, dc) + carry
            carry = carry + jnp.sum(dc, axis=0, keepdims=True)
            z = fl_ref[blk * tb:(blk + 1) * tb, :] + b_ref[0:1, :]
            dz = dl * _sigmoid(-z)
            dfl_ref[blk * tb:(blk + 1) * tb, :] = dz
            dbs = dbs + jnp.sum(dz, axis=0, keepdims=True)
        db_ref[...] = jnp.broadcast_to(dbs, (8, 128))

    return pl.pallas_call(
        body, name=name,
        out_shape=(jax.ShapeDtypeStruct((s, 128), F32), jax.ShapeDtypeStruct((8, 128), F32)),
    )(dcum, fl, bf)


def _fox_scores(q, k_ref, cq, ck_ref, qi, tq, n):
    sc = lax.dot_general(q, k_ref[0:n, :], _NT, preferred_element_type=F32) * (HDIM ** -0.5)
    sc = sc + cq - ck_ref[:, 0:n]
    row = lax.broadcasted_iota(jnp.int32, (tq, n), 0) + qi * tq
    col = lax.broadcasted_iota(jnp.int32, (tq, n), 1)
    return sc, col <= row


def _fox_attn_fwd(q, k, v, cq, ck, name):
    s = q.shape[1]
    tq = _tile(s, (256, 128))

    def body(q_ref, k_ref, v_ref, cq_ref, ck_ref, o_ref, lse_ref):
        for qi in range(s // tq):
            n = (qi + 1) * tq
            rows = slice(qi * tq, n)
            sc, keep = _fox_scores(q_ref[rows, :], k_ref, cq_ref[rows, :], ck_ref, qi, tq, n)
            sc = jnp.where(keep, sc, -1e30)
            m = jnp.max(sc, axis=1, keepdims=True)
            p = jnp.exp(sc - m)
            lsum = jnp.sum(p, axis=1, keepdims=True)
            o = jnp.dot(p.astype(BF16), v_ref[0:n, :], preferred_element_type=F32) / lsum
            o_ref[rows, :] = o.astype(BF16)
            lse_ref[rows, :] = m + jnp.log(lsum)

    hspec = pl.BlockSpec((None, s, HDIM), lambda h: (h, 0, 0))
    cspec = pl.BlockSpec((None, s, 1), lambda h: (h, 0, 0))
    rspec = pl.BlockSpec((None, 1, s), lambda h: (h, 0, 0))
    return pl.pallas_call(
        body, name=name,
        out_shape=(jax.ShapeDtypeStruct((HEADS, s, HDIM), BF16), jax.ShapeDtypeStruct((HEADS, s, 1), F32)),
        grid=(HEADS,), in_specs=[hspec, hspec, hspec, cspec, rspec], out_specs=(hspec, cspec),
        compiler_params=_cparams(("parallel",), VMEM_BIG),
    )(q, k, v, cq, ck)


def _fox_attn_bwd(q, k, v, do, lse, cq, ck, name):
    s = q.shape[1]
    tq = _tile(s, (256, 128))
    scale = HDIM ** -0.5

    def body(q_ref, k_ref, v_ref, do_ref, lse_ref, cq_ref, ck_ref,
             dq_ref, dk_ref, dv_ref, dcq_ref, dck_ref, dk_acc, dv_acc, dck_acc):
        dk_acc[...] = jnp.zeros_like(dk_acc)
        dv_acc[...] = jnp.zeros_like(dv_acc)
        dck_acc[...] = jnp.zeros_like(dck_acc)
        for qi in range(s // tq):
            n = (qi + 1) * tq
            rows = slice(qi * tq, n)
            qv, dov = q_ref[rows, :], do_ref[rows, :]
            sc, keep = _fox_scores(qv, k_ref, cq_ref[rows, :], ck_ref, qi, tq, n)
            p = jnp.where(keep, jnp.exp(sc - lse_ref[rows, :]), 0.0)
            dp = lax.dot_general(dov, v_ref[0:n, :], _NT, preferred_element_type=F32)
            ds = p * (dp - jnp.sum(p * dp, axis=1, keepdims=True))
            dsb = ds.astype(BF16)
            dq_ref[rows, :] = (jnp.dot(dsb, k_ref[0:n, :], preferred_element_type=F32) * scale).astype(BF16)
            dk_acc[0:n, :] += lax.dot_general(dsb, qv, _TN, preferred_element_type=F32) * scale
            dv_acc[0:n, :] += lax.dot_general(p.astype(BF16), dov, _TN, preferred_element_type=F32)
            dcq_ref[rows, :] = jnp.sum(ds, axis=1, keepdims=True)
            dck_acc[:, 0:n] -= jnp.sum(ds, axis=0, keepdims=True)
        dk_ref[...] = dk_acc[...].astype(BF16)
        dv_ref[...] = dv_acc[...].astype(BF16)
        dck_ref[...] = dck_acc[...]

    hspec = pl.BlockSpec((None, s, HDIM), lambda h: (h, 0, 0))
    cspec = pl.BlockSpec((None, s, 1), lambda h: (h, 0, 0))
    rspec = pl.BlockSpec((None, 1, s), lambda h: (h, 0, 0))
    hs = jax.ShapeDtypeStruct((HEADS, s, HDIM), BF16)
    return pl.pallas_call(
        body, name=name,
        out_shape=(hs, hs, hs, jax.ShapeDtypeStruct((HEADS, s, 1), F32), jax.ShapeDtypeStruct((HEADS, 1, s), F32)),
        grid=(HEADS,), in_specs=[hspec, hspec, hspec, hspec, cspec, cspec, rspec],
        out_specs=(hspec, hspec, hspec, cspec, rspec),
        scratch_shapes=[pltpu.VMEM((s, HDIM), F32), pltpu.VMEM((s, HDIM), F32), pltpu.VMEM((1, s), F32)],
        compiler_params=_cparams(("parallel",), VMEM_BIG),
    )(q, k, v, do, lse, cq, ck)


TC = 256
HALO = 8


def _chunk_specs(s):
    tc = _tile(s, (TC, 128))
    per = tc // HALO
    nblk = s // HALO
    cur = pl.BlockSpec((tc, 8, 128), lambda i: (i, 0, 0))
    past = pl.BlockSpec((HALO, 8, 128), lambda i: (jnp.maximum(i * per - 1, 0), 0, 0))
    future = pl.BlockSpec((HALO, 8, 128), lambda i: (jnp.minimum((i + 1) * per, nblk - 1), 0, 0))
    return tc, cur, past, future


def _vec_spec(n):
    return pl.BlockSpec((n, 8, 128), lambda i: (0, 0, 0))


def _conv_past(buf, w_ref, kw, tc):
    out = w_ref[kw - 1] * buf[HALO:HALO + tc]
    for k in range(kw - 1):
        off = HALO - (kw - 1) + k
        out = out + w_ref[k] * buf[off:off + tc]
    return out


def _sconv_fwd(bg, cg, xv, w, name):
    s = bg.shape[0]
    tc, cur, past, _ = _chunk_specs(s)

    def body(bg_ref, cg_ref, xv_ref, cgp_ref, xvp_ref, w_ref, y_ref, zbuf):
        first = pl.program_id(0) == 0
        zbuf[0:HALO] = jnp.where(first, 0.0, cgp_ref[...] * xvp_ref[...])
        zbuf[HALO:HALO + tc] = cg_ref[...] * xv_ref[...]
        y_ref[...] = bg_ref[...] * _conv_past(zbuf, w_ref, 3, tc)

    return pl.pallas_call(
        body, name=name, out_shape=jax.ShapeDtypeStruct((s, 8, 128), F32), grid=(s // tc,),
        in_specs=[cur, cur, cur, past, past, _vec_spec(3)], out_specs=cur,
        scratch_shapes=[pltpu.VMEM((tc + HALO, 8, 128), F32)],
        compiler_params=_cparams(("parallel",)),
    )(bg, cg, xv, cg, xv, w)


def _sconv_bwd(dy, bg, cg, xv, w, name):
    s = dy.shape[0]
    tc, cur, past, future = _chunk_specs(s)
    nch = s // tc

    def body(dy_ref, bg_ref, cg_ref, xv_ref, cgp_ref, xvp_ref, dyf_ref, bgf_ref, w_ref,
             dbg_ref, dcg_ref, dxv_ref, dw_ref, zbuf, dbuf):
        i = pl.program_id(0)

        @pl.when(i == 0)
        def _():
            dw_ref[...] = jnp.zeros_like(dw_ref)

        z = cg_ref[...] * xv_ref[...]
        zbuf[0:HALO] = jnp.where(i == 0, 0.0, cgp_ref[...] * xvp_ref[...])
        zbuf[HALO:HALO + tc] = z
        dyv = dy_ref[...]
        dbg_ref[...] = dyv * _conv_past(zbuf, w_ref, 3, tc)
        dbuf[0:tc] = dyv * bg_ref[...]
        dbuf[tc:tc + HALO] = jnp.where(i == nch - 1, 0.0, dyf_ref[...] * bgf_ref[...])
        dz = jnp.zeros((tc, 8, 128), F32)
        for k in range(3):
            sh = dbuf[2 - k:2 - k + tc]
            dz = dz + w_ref[k] * sh
            dw_ref[k] += jnp.sum(z * sh, axis=0)
        dcg_ref[...] = dz * xv_ref[...]
        dxv_ref[...] = dz * cg_ref[...]

    shp = jax.ShapeDtypeStruct((s, 8, 128), F32)
    return pl.pallas_call(
        body, name=name, out_shape=(shp, shp, shp, jax.ShapeDtypeStruct((3, 8, 128), F32)),
        grid=(nch,),
        in_specs=[cur, cur, cur, cur, past, past, future, future, _vec_spec(3)],
        out_specs=(cur, cur, cur, _vec_spec(3)),
        scratch_shapes=[pltpu.VMEM((tc + HALO, 8, 128), F32), pltpu.VMEM((tc + HALO, 8, 128), F32)],
        compiler_params=_cparams(("arbitrary",)),
    )(dy, bg, cg, xv, cg, xv, dy, bg, w)


def _lru_conv_fwd(xp, wb, name):
    s = xp.shape[0]
    tc, cur, past, _ = _chunk_specs(s)

    def body(x_ref, xp_ref, w_ref, o_ref, buf):
        buf[0:HALO] = jnp.where(pl.program_id(0) == 0, 0.0, xp_ref[...])
        buf[HALO:HALO + tc] = x_ref[...]
        o_ref[...] = _conv_past(buf, w_ref, 4, tc) + w_ref[4]

    return pl.pallas_call(
        body, name=name, out_shape=jax.ShapeDtypeStruct((s, 8, 128), F32), grid=(s // tc,),
        in_specs=[cur, past, _vec_spec(8)], out_specs=cur,
        scratch_shapes=[pltpu.VMEM((tc + HALO, 8, 128), F32)],
        compiler_params=_cparams(("parallel",)),
    )(xp, xp, wb)


def _lru_conv_bwd(dxb, xp, wb, name):
    s = dxb.shape[0]
    tc, cur, _, future = _chunk_specs(s)
    nch = s // tc

    def body(d_ref, df_ref, x_ref, w_ref, o_ref, red_ref, dbuf):
        i = pl.program_id(0)

        @pl.when(i == 0)
        def _():
            red_ref[...] = jnp.zeros_like(red_ref)

        dv = d_ref[...]
        dbuf[0:tc] = dv
        dbuf[tc:tc + HALO] = jnp.where(i == nch - 1, 0.0, df_ref[...])
        xv = x_ref[...]
        dx = jnp.zeros((tc, 8, 128), F32)
        for k in range(4):
            sh = dbuf[3 - k:3 - k + tc]
            dx = dx + w_ref[k] * sh
            red_ref[k] += jnp.sum(xv * sh, axis=0)
        red_ref[4] += jnp.sum(dv, axis=0)
        o_ref[...] = dx

    return pl.pallas_call(
        body, name=name,
        out_shape=(jax.ShapeDtypeStruct((s, 8, 128), F32), jax.ShapeDtypeStruct((8, 8, 128), F32)),
        grid=(nch,), in_specs=[cur, future, cur, _vec_spec(8)], out_specs=(cur, _vec_spec(8)),
        scratch_shapes=[pltpu.VMEM((tc + HALO, 8, 128), F32)],
        compiler_params=_cparams(("arbitrary",)),
    )(dxb, dxb, xp, wb)


def _lru_gates(ra, ia, pv_ref):
    sp = _softplus(-pv_ref[2])
    r = _sigmoid(ra + pv_ref[0])
    ig = _sigmoid(ia + pv_ref[1])
    log_a = (-LRU_C) * r * sp
    a = jnp.exp(log_a)
    mult = jnp.sqrt(-jnp.tanh(log_a) * (a * a + 1.0))
    return r, ig, a, mult, sp


def _lru_scan_fwd(ra, ia, xb, gate, pv, name):
    s = ra.shape[0]
    tc, cur, _, _ = _chunk_specs(s)

    def body(ra_ref, ia_ref, xb_ref, g_ref, pv_ref, hs_ref, hp_ref, y_ref, abuf, bbuf, hcar):
        @pl.when(pl.program_id(0) == 0)
        def _():
            hcar[...] = jnp.zeros_like(hcar)

        xbv = xb_ref[...]
        _, ig, a, mult, _ = _lru_gates(ra_ref[...], ia_ref[...], pv_ref)
        abuf[...] = a
        bbuf[...] = mult * (ig * xbv)

        def step(t, h):
            hp_ref[t] = h
            h = abuf[t] * h + bbuf[t]
            hs_ref[t] = h
            return h

        hcar[...] = lax.fori_loop(0, tc, step, hcar[...], unroll=8)
        y_ref[...] = hs_ref[...] * _gelu_parts(g_ref[...])[0]

    shp = jax.ShapeDtypeStruct((s, 8, 128), F32)
    return pl.pallas_call(
        body, name=name, out_shape=(shp, shp, shp), grid=(s // tc,),
        in_specs=[cur, cur, cur, cur, _vec_spec(8)], out_specs=(cur, cur, cur),
        scratch_shapes=[pltpu.VMEM((tc, 8, 128), F32), pltpu.VMEM((tc, 8, 128), F32),
                        pltpu.VMEM((8, 128), F32)],
        compiler_params=_cparams(("arbitrary",)),
    )(ra, ia, xb, gate, pv)


def _lru_scan_bwd(dy, ra, ia, xb, gate, hs, hp, pv, name):
    s = dy.shape[0]
    tc = _tile(s, (TC, 128))
    nch = s // tc
    rev = pl.BlockSpec((tc, 8, 128), lambda i: (nch - 1 - i, 0, 0))

    def body(dy_ref, ra_ref, ia_ref, xb_ref, g_ref, hs_ref, hp_ref, pv_ref,
             dg_ref, dra_ref, dia_ref, dxb_ref, red_ref, abuf, dbuf, gbuf, car):
        @pl.when(pl.program_id(0) == 0)
        def _():
            red_ref[...] = jnp.zeros_like(red_ref)
            car[...] = jnp.zeros_like(car)

        xbv = xb_ref[...]
        r, ig, a, mult, sp = _lru_gates(ra_ref[...], ia_ref[...], pv_ref)
        ge, dge = _gelu_parts(g_ref[...])
        dyv = dy_ref[...]
        dg_ref[...] = dyv * hs_ref[...] * dge
        abuf[...] = a
        dbuf[...] = dyv * ge

        def step(k, c):
            t = tc - 1 - k
            g = dbuf[t] + c
            gbuf[t] = g
            return abuf[t] * g

        car[...] = lax.fori_loop(0, tc, step, car[...], unroll=8)
        g = gbuf[...]
        d_a = g * hp_ref[...]
        d_m = g * (ig * xbv)
        d_loga = d_a * a - d_m * (a * a / mult)
        dxb_ref[...] = g * mult * ig
        dra = d_loga * ((-LRU_C) * sp) * r * (1.0 - r)
        dia = g * mult * xbv * ig * (1.0 - ig)
        dra_ref[...] = dra
        dia_ref[...] = dia
        red_ref[0] += jnp.sum(dra, axis=0)
        red_ref[1] += jnp.sum(dia, axis=0)
        red_ref[2] += jnp.sum(d_loga * r, axis=0) * (LRU_C * _sigmoid(-pv_ref[2]))

    shp = jax.ShapeDtypeStruct((s, 8, 128), F32)
    return pl.pallas_call(
        body, name=name, out_shape=(shp, shp, shp, shp, jax.ShapeDtypeStruct((8, 8, 128), F32)),
        grid=(nch,), in_specs=[rev] * 7 + [_vec_spec(8)],
        out_specs=(rev, rev, rev, rev, _vec_spec(8)),
        scratch_shapes=[pltpu.VMEM((tc, 8, 128), F32), pltpu.VMEM((tc, 8, 128), F32),
                        pltpu.VMEM((tc, 8, 128), F32), pltpu.VMEM((8, 128), F32)],
        compiler_params=_cparams(("arbitrary",)),
    )(dy, ra, ia, xb, gate, hs, hp, pv)


def _adamw(src, w, m, v, name):
    nl, n, rr, cc = src.shape
    tr = rr
    for cand in sorted((d for d in range(16, rr + 1, 16) if rr % d == 0), reverse=True):
        if cand * cc <= 192 * 1024:
            tr = cand
            break
    c1 = 1.0 - ADAM_B1 ** ADAM_STEP
    c2 = 1.0 - ADAM_B2 ** ADAM_STEP

    def body(s_ref, w_ref, m_ref, v_ref, g_out, d_out, m_out, v_out):
        g = s_ref[0].astype(F32)
        for k in range(1, n):
            g = g + s_ref[k].astype(F32)
        mn = ADAM_B1 * m_ref[...] + (1.0 - ADAM_B1) * g
        vn = ADAM_B2 * v_ref[...] + (1.0 - ADAM_B2) * (g * g)
        g_out[...] = g
        m_out[...] = mn
        v_out[...] = vn
        d_out[...] = (-ADAM_LR) * ((mn / c1) / (jnp.sqrt(vn / c2) + ADAM_EPS) + ADAM_WD * w_ref[...])

    pspec = pl.BlockSpec((None, tr, cc), lambda l, i: (l, i, 0))
    shp = jax.ShapeDtypeStruct((nl, rr, cc), F32)
    return pl.pallas_call(
        body, name=name, out_shape=(shp, shp, shp, shp), grid=(nl, rr // tr),
        in_specs=[pl.BlockSpec((None, n, tr, cc), lambda l, i: (l, 0, i, 0)), pspec, pspec, pspec],
        out_specs=(pspec, pspec, pspec, pspec),
        compiler_params=_cparams(("parallel", "parallel"), VMEM_BIG),
    )(src, w, m, v)


def _sum_devices(x, name):
    _, rr, cc = x.shape

    def body(x_ref, o_ref):
        acc = x_ref[0]
        for k in range(1, NDEV):
            acc = acc + x_ref[k]
        o_ref[...] = acc

    return pl.pallas_call(
        body, name=name, out_shape=jax.ShapeDtypeStruct((rr, cc), F32), grid=(rr // 8,),
        in_specs=[pl.BlockSpec((NDEV, 8, cc), lambda i: (0, i, 0))],
        out_specs=pl.BlockSpec((8, cc), lambda i: (i, 0)),
        compiler_params=_cparams(("parallel",)),
    )(x)


def _to3(x):
    return x.reshape(x.shape[0], 8, 128)


def _heads(x):
    s = x.shape[0]
    return x.reshape(s, HEADS, HDIM).transpose(1, 0, 2).astype(BF16)


def _unheads(x):
    s = x.shape[1]
    return x.transpose(1, 0, 2).reshape(s, D)


def _block_diag(w):
    eye = jnp.eye(HEADS, dtype=w.dtype)
    return (w[:, :, None, :] * eye[:, None, :, None]).reshape(D, D)


def _diag_blocks(x):
    return jnp.diagonal(x.reshape(HEADS, HDIM, HEADS, HDIM), axis1=0, axis2=2).transpose(2, 0, 1)


def _col_blocks(dw, n):
    return dw.reshape(dw.shape[0], NDEV, n).transpose(1, 0, 2)


def kernel(x, c, w_cond, b_cond, norm_pre, norm_post, w_ffn_in, w_ffn_out, fox_w_in, fox_b_f, fox_w_out, sconv_w_in, sconv_conv_w, sconv_w_out, lru_w_in, lru_conv_w, lru_conv_b, lru_w_a, lru_b_a, lru_w_x, lru_b_x, lru_lambda, lru_w_out, loss_target, m_w_cond, m_b_cond, m_norm_pre, m_norm_post, m_w_ffn_in, m_w_ffn_out, m_fox_w_in, m_fox_b_f, m_fox_w_out, m_sconv_w_in, m_sconv_conv_w, m_sconv_w_out, m_lru_w_in, m_lru_conv_w, m_lru_conv_b, m_lru_w_a, m_lru_b_a, m_lru_w_x, m_lru_b_x, m_lru_lambda, m_lru_w_out, v_w_cond, v_b_cond, v_norm_pre, v_norm_post, v_w_ffn_in, v_w_ffn_out, v_fox_w_in, v_fox_b_f, v_fox_w_out, v_sconv_w_in, v_sconv_conv_w, v_sconv_w_out, v_lru_w_in, v_lru_conv_w, v_lru_conv_b, v_lru_w_a, v_lru_b_a, v_lru_w_x, v_lru_b_x, v_lru_lambda, v_lru_w_out):
    me = 4 * lax.axis_index("x") + 2 * lax.axis_index("y") + lax.axis_index("c")
    s = x.shape[1]
    x0 = x[0]
    tgt = loss_target[0]
    wcw = w_cond.shape[2]
    dsh = norm_pre.shape[2]

    small_parts = [c.reshape(-1), norm_pre.reshape(-1), norm_post.reshape(-1), sconv_conv_w.reshape(-1),
                   lru_conv_w.reshape(-1), lru_conv_b.reshape(-1), lru_lambda.reshape(-1)]
    sizes = [p.shape[0] for p in small_parts]
    flat = jnp.concatenate(small_parts)
    padn = (-flat.shape[0]) % 1024
    flat = jnp.pad(flat, (0, padn)).reshape(-1, 1024)
    sm = _all_gather(flat, 0, "ag_small").reshape(NDEV, -1)
    offs = [0]
    for n_ in sizes:
        offs.append(offs[-1] + n_)
    piece = lambda i: sm[:, offs[i]:offs[i + 1]]
    c_all = piece(0)
    unshard = lambda p, lead: p.reshape((NDEV,) + lead + (dsh,)).transpose(
        tuple(range(1, len(lead) + 1)) + (0, len(lead) + 1)).reshape(lead + (D,))
    npre = unshard(piece(1), (DEPTH, 3))
    npost = unshard(piece(2), (DEPTH, 3))
    scw = unshard(piece(3), (3,))
    lcw = unshard(piece(4), (4,))
    lcb = unshard(piece(5), ())
    llam = unshard(piece(6), ())

    c_pad = jnp.pad(c_all, ((0, 128 - NDEV), (0, 0)))
    bc_mine = lax.dynamic_slice(b_cond, (0, me * wcw), (DEPTH, wcw)).reshape(DEPTH, 1, wcw)
    modc, ca_pad = _cond_fwd(c_pad, w_cond, bc_mine, "cond_fwd")
    modg = _all_gather(modc[:, :NDEV, :], 0, "ag_mod")
    mod = lax.dynamic_index_in_dim(modg, me, axis=2, keepdims=False)
    mod = mod.transpose(1, 0, 2).reshape(DEPTH, 3, 3, D)
    ptab = jnp.stack([npre, mod[:, :, 1], mod[:, :, 0], npost, mod[:, :, 2],
                      jnp.zeros_like(npre), jnp.zeros_like(npre), jnp.zeros_like(npre)], axis=2)

    pend = {}
    chain = [jnp.zeros((), F32)]

    def ag_start(key, w):
        hnd = _exchange_start((w + chain[0]).astype(BF16), "ags_" + key, True)
        chain[0] = hnd[4][0, 0]
        pend[key] = hnd[:4]

    def ag_wait(key, after):
        return _exchange_wait(pend.pop(key), after, "agw_" + key, True)

    for l in range(DEPTH):
        ag_start(f"wi{l}0", w_ffn_in[l, 0])
        ag_start(f"wo{l}0", w_ffn_out[l, 0])
        if l % 3 == 0:
            ag_start(f"fwi{l // 3}", fox_w_in[l // 3])
            ag_start(f"fwo{l // 3}", fox_w_out[l // 3])
        elif l % 3 == 1:
            ag_start("swi", sconv_w_in[0])
            ag_start("swo", sconv_w_out[0])
        else:
            ag_start("lwi", lru_w_in[0])
            ag_start("lwo", lru_w_out[0])
        ag_start(f"wi{l}1", w_ffn_in[l, 1])
        ag_start(f"wo{l}1", w_ffn_out[l, 1])
    ptab = ptab + chain[0]
    cols = lambda wg: wg.transpose(1, 0, 2).reshape(D, -1)
    wgate = jnp.concatenate([_block_diag(lru_w_a[0]), _block_diag(lru_w_x[0])], axis=1).astype(BF16)
    bf_pad = jnp.pad(fox_b_f, ((0, 0), (0, 128 - HEADS))).reshape(2, 1, 128)
    scw3 = scw.reshape(3, 8, 128)
    lcwb = jnp.concatenate([lcw, lcb[None], jnp.zeros((3, D), F32)], axis=0).reshape(8, 8, 128)
    lpv = jnp.concatenate([lru_b_a.reshape(1, D), lru_b_x.reshape(1, D), llam[None],
                           jnp.zeros((5, D), F32)], axis=0).reshape(8, 8, 128)

    subs = [(l, sub) for l in range(DEPTH) for sub in range(3)]
    coef = lambda sub: 1.0 if sub == 1 else 0.5

    saved = {}
    xcur = x0
    h = _prenorm(xcur, ptab, 0, 0, "prenorm")
    for idx, (l, sub) in enumerate(subs):
        tag = f"{l}{sub}"
        sv = {"x": xcur, "h": h}
        if sub != 1:
            f = 0 if sub == 0 else 1
            wi = ag_wait(f"wi{l}{f}", h)
            wo = ag_wait(f"wo{l}{f}", h).reshape(4, FFB, D)
            g, u, act = _ffn_up(h, wi, "ffn_up_" + tag)
            y = _ffn_down(act, wo, "ffn_down_" + tag)
            sv.update(g=g, u=u, wi=wi, wo=wo)
        elif l % 3 == 0:
            j = l // 3
            fwi_j = cols(ag_wait(f"fwi{j}", h))
            fwi_j = jnp.pad(fwi_j, ((0, 0), (0, FOXP - fwi_j.shape[1])))
            fwo_j = ag_wait(f"fwo{j}", h).reshape(D, D)
            sv.update(fwi=fwi_j, fwo=fwo_j)
            proj = _mm_nn(h, fwi_j, F32, "fox_in_" + tag)
            q, k, v = (_heads(proj[:, i * D:(i + 1) * D]) for i in range(3))
            fl = proj[:, 3 * D:]
            cum = _fox_cum(fl, bf_pad[j], "fox_cum_" + tag)
            cumt = cum[:, :HEADS].T
            cq, ck = cumt[:, :, None], cumt[:, None, :]
            o, lse = _fox_attn_fwd(q, k, v, cq, ck, "fox_attn_" + tag)
            o2 = _unheads(o)
            y = _mm_nn(o2, fwo_j, F32, "fox_out_" + tag)
            sv.update(q=q, k=k, v=v, fl=fl, cq=cq, ck=ck, lse=lse, o2=o2)
        elif l % 3 == 1:
            swi = cols(ag_wait("swi", h))
            swo = ag_wait("swo", h).reshape(D, D)
            proj = _mm_nn(h, swi, F32, "sconv_in_" + tag)
            bg, cg, xv = (_to3(proj[:, i * D:(i + 1) * D]) for i in range(3))
            y3 = _sconv_fwd(bg, cg, xv, scw3, "sconv_mix_" + tag)
            y2 = y3.reshape(s, D).astype(BF16)
            y = _mm_nn(y2, swo, F32, "sconv_out_" + tag)
            sv.update(bg=bg, cg=cg, xv=xv, y2=y2)
        else:
            lwi = cols(ag_wait("lwi", h))
            lwo = ag_wait("lwo", h).reshape(D, D)
            proj = _mm_nn(h, lwi, F32, "lru_in_" + tag)
            gate3, xp3 = _to3(proj[:, :D]), _to3(proj[:, D:])
            xb3 = _lru_conv_fwd(xp3, lcwb, "lru_conv_" + tag)
            xb2 = xb3.reshape(s, D).astype(BF16)
            gpre = _mm_nn(xb2, wgate, F32, "lru_gate_" + tag)
            ra3, ia3 = _to3(gpre[:, :D]), _to3(gpre[:, D:])
            hs3, hp3, y3 = _lru_scan_fwd(ra3, ia3, xb3, gate3, lpv, "lru_scan_" + tag)
            y2 = y3.reshape(s, D).astype(BF16)
            y = _mm_nn(y2, lwo, F32, "lru_out_" + tag)
            sv.update(gate3=gate3, xp3=xp3, xb3=xb3, xb2=xb2, ra3=ra3, ia3=ia3, hs3=hs3, hp3=hp3, y2=y2)
        sv["y"] = y
        saved[(l, sub)] = sv
        nxt = subs[idx + 1] if idx + 1 < len(subs) else None
        xcur, h = _resid(xcur, y, ptab, l, sub, coef(sub), nxt, "resid_" + tag)

    dx, loss_cols = _loss_grad(xcur, tgt, "loss_grad")

    d_mod = [[[None] * 3 for _ in range(3)] for _ in range(DEPTH)]
    d_npre = [[None] * 3 for _ in range(DEPTH)]
    d_npost = [[None] * 3 for _ in range(DEPTH)]
    d_fbf = [None, None]
    small_g = {}
    rs_pend, rs_order = {}, []
    ptab_b = [ptab]

    def rs_start(key, blocks):
        hnd = _exchange_start(blocks, "rss_" + key, False)
        rs_pend[key] = hnd[:4]
        rs_order.append(key)
        ptab_b[0] = ptab_b[0] + hnd[4][0, 0]

    for (l, sub) in reversed(subs):
        tag = f"{l}{sub}"
        sv = saved[(l, sub)]
        dy, red = _post_bwd(dx, sv["y"], ptab_b[0], l, sub, coef(sub), "post_bwd_" + tag)
        d_mod[l][sub][2] = red[0]
        d_npost[l][sub] = red[1]
        hb = sv["h"]
        if sub != 1:
            f = 0 if sub == 0 else 1
            dg, du, dwo = _ffn_down_bwd(dy, sv["g"], sv["u"], sv["wo"], "ffn_down_bwd_" + tag)
            dh = _ffn_dh(dg, du, sv["wi"], "ffn_dh_" + tag)
            rs_start(f"wo{l}{f}", dwo.reshape(NDEV, FFB // 2, D))
            rs_start(f"wi{l}{f}", jnp.concatenate(
                [_ffn_dwi(hb, dg, "ffn_dwg_" + tag), _ffn_dwi(hb, du, "ffn_dwu_" + tag)], axis=0))
        elif l % 3 == 0:
            j = l // 3
            fwi_j, fwo_j = sv["fwi"], sv["fwo"]
            do2 = _mm_nt(dy, fwo_j, BF16, "fox_out_dx_" + tag)
            rs_start(f"fwo{j}", _mm_tn(sv["o2"], dy, BF16, "fox_out_dw_" + tag).reshape(NDEV, dsh, D))
            dq, dk, dv, dcq, dck = _fox_attn_bwd(sv["q"], sv["k"], sv["v"], _heads(do2), sv["lse"],
                                                 sv["cq"], sv["ck"], "fox_attn_bwd_" + tag)
            dcum = (dcq[:, :, 0] + dck[:, 0, :]).T
            dcum = jnp.pad(dcum, ((0, 0), (0, 128 - HEADS)))
            dfl, dbf = _fox_cum_bwd(dcum, sv["fl"], bf_pad[j], "fox_cum_bwd_" + tag)
            d_fbf[j] = dbf[0, :HEADS]
            dproj = jnp.concatenate([_unheads(dq), _unheads(dk), _unheads(dv), dfl.astype(BF16)], axis=1)
            dh = _mm_nt(dproj, fwi_j, F32, "fox_in_dx_" + tag)
            dwf = _mm_tn(hb, dproj, BF16, "fox_in_dw_" + tag)
            rs_start(f"fwi{j}", _col_blocks(dwf[:, :NDEV * fox_w_in.shape[2]], fox_w_in.shape[2]))
        elif l % 3 == 1:
            dy3 = _to3(_mm_nt(dy, swo, F32, "sconv_out_dx_" + tag))
            rs_start("swo", _mm_tn(sv["y2"], dy, BF16, "sconv_out_dw_" + tag).reshape(NDEV, dsh, D))
            dbg, dcg, dxv, dscw = _sconv_bwd(dy3, sv["bg"], sv["cg"], sv["xv"], scw3, "sconv_mix_bwd_" + tag)
            small_g["sconv_conv_w"] = dscw.reshape(3, D)
            dproj = jnp.concatenate([t.reshape(s, D) for t in (dbg, dcg, dxv)], axis=1).astype(BF16)
            dh = _mm_nt(dproj, swi, F32, "sconv_in_dx_" + tag)
            rs_start("swi", _col_blocks(_mm_tn(hb, dproj, BF16, "sconv_in_dw_" + tag), sconv_w_in.shape[2]))
        else:
            dy3 = _to3(_mm_nt(dy, lwo, F32, "lru_out_dx_" + tag))
            rs_start("lwo", _mm_tn(sv["y2"], dy, BF16, "lru_out_dw_" + tag).reshape(NDEV, dsh, D))
            dgate3, dra3, dia3, dxb3, lred = _lru_scan_bwd(
                dy3, sv["ra3"], sv["ia3"], sv["xb3"], sv["gate3"], sv["hs3"], sv["hp3"], lpv, "lru_scan_bwd_" + tag)
            dgp = jnp.concatenate([dra3.reshape(s, D), dia3.reshape(s, D)], axis=1).astype(BF16)
            dxb3 = dxb3 + _to3(_mm_nt(dgp, wgate, F32, "lru_gate_dx_" + tag))
            dwgate = _mm_tn(sv["xb2"], dgp, F32, "lru_gate_dw_" + tag)
            dxp3, cred = _lru_conv_bwd(dxb3, sv["xp3"], lcwb, "lru_conv_bwd_" + tag)
            lred, cred = lred.reshape(8, D), cred.reshape(8, D)
            small_g.update(lru_w_a=_diag_blocks(dwgate[:, :D]), lru_w_x=_diag_blocks(dwgate[:, D:]),
                           lru_b_a=lred[0], lru_b_x=lred[1], lru_lambda=lred[2],
                           lru_conv_w=cred[:4], lru_conv_b=cred[4])
            dproj = jnp.concatenate([dgate3.reshape(s, D), dxp3.reshape(s, D)], axis=1).astype(BF16)
            dh = _mm_nt(dproj, lwi, F32, "lru_in_dx_" + tag)
            rs_start("lwi", _col_blocks(_mm_tn(hb, dproj, BF16, "lru_in_dw_" + tag), lru_w_in.shape[2]))
        dx, red = _pre_bwd(dx, dh, sv["x"], ptab_b[0], l, sub, "pre_bwd_" + tag)
        d_mod[l][sub][0] = red[0]
        d_mod[l][sub][1] = red[1]
        d_npre[l][sub] = red[2]
    grad_x = dx[None]

    dmod_mine = jnp.stack([jnp.stack([jnp.stack(d_mod[l][sub]) for sub in range(3)]) for l in range(DEPTH)])
    gparts = [loss_cols[0], dmod_mine.reshape(-1),
              jnp.stack([jnp.stack(r_) for r_ in d_npre]).reshape(-1),
              jnp.stack([jnp.stack(r_) for r_ in d_npost]).reshape(-1),
              jnp.stack(d_fbf).reshape(-1), small_g["sconv_conv_w"].reshape(-1),
              small_g["lru_conv_w"].reshape(-1), small_g["lru_conv_b"].reshape(-1),
              small_g["lru_w_a"].reshape(-1), small_g["lru_b_a"].reshape(-1),
              small_g["lru_w_x"].reshape(-1), small_g["lru_b_x"].reshape(-1),
              small_g["lru_lambda"].reshape(-1)]
    gsizes = [p.shape[0] for p in gparts]
    gflat = jnp.concatenate(gparts)
    gflat = jnp.pad(gflat, (0, (-gflat.shape[0]) % (8 * 1024))).reshape(-1, 1024)
    small_hnd = _exchange_start(gflat, "ags_smallgrads", True)

    out = {}
    after = [small_hnd[4]]

    def update(keys, w, m, v, name):
        got = {}
        for k_ in sorted(keys, key=rs_order.index):
            got[k_] = after[0] = _exchange_wait(rs_pend.pop(k_), after[0], "rsw_" + k_, False)
        src = jnp.stack([got[k_] for k_ in keys])
        shp = w.shape
        w3, m3, v3 = (t.reshape((src.shape[0],) + src.shape[2:]) for t in (w, m, v))
        res = tuple(t.reshape(shp) for t in _adamw(src, w3, m3, v3, "adamw_" + name))
        after[0] = res[1]
        return res

    lf = [f"{l}{f}" for l in range(DEPTH) for f in range(2)]
    out["fox_w_in"] = update(["fwi0", "fwi1"], fox_w_in, m_fox_w_in, v_fox_w_in, "fwi")
    out["fox_w_out"] = update(["fwo0", "fwo1"], fox_w_out, m_fox_w_out, v_fox_w_out, "fwo")
    out["lru_w_out"] = update(["lwo"], lru_w_out, m_lru_w_out, v_lru_w_out, "lwo")
    out["lru_w_in"] = update(["lwi"], lru_w_in, m_lru_w_in, v_lru_w_in, "lwi")
    out["sconv_w_out"] = update(["swo"], sconv_w_out, m_sconv_w_out, v_sconv_w_out, "swo")
    out["sconv_w_in"] = update(["swi"], sconv_w_in, m_sconv_w_in, v_sconv_w_in, "swi")
    out["w_ffn_out"] = update(["wo" + k_ for k_ in lf], w_ffn_out, m_w_ffn_out, v_w_ffn_out, "wo")

    gall = _exchange_wait(small_hnd[:4], after[0], "agw_smallgrads", True)
    gsum = _sum_devices(gall, "sum_smallgrads").reshape(-1)
    goffs = [0]
    for n_ in gsizes:
        goffs.append(goffs[-1] + n_)
    gpiece = lambda i: gsum[goffs[i]:goffs[i + 1]]
    loss = jnp.sum(gpiece(0))
    my_ch = lambda g, lead: lax.dynamic_slice_in_dim(g.reshape(lead + (D,)), me * dsh, dsh, axis=len(lead))

    dmod_all = gall.reshape(NDEV, -1)[:, goffs[1]:goffs[2]].reshape(NDEV, DEPTH, 3 * 3 * D)
    dmod_cols = lax.dynamic_slice_in_dim(dmod_all, me * wcw, wcw, axis=2).transpose(1, 0, 2)
    dmod_cols = jnp.pad(dmod_cols, ((0, 0), (0, 128 - NDEV), (0, 0))).astype(BF16)
    g_wcond = _cond_bwd(ca_pad, dmod_cols, "cond_bwd")

    out["w_cond"] = _adamw(g_wcond[:, None], w_cond, m_w_cond, v_w_cond, "adamw_wcond")

    small = [
        ("b_cond", b_cond, m_b_cond, v_b_cond, gpiece(1)),
        ("norm_pre", norm_pre, m_norm_pre, v_norm_pre, my_ch(gpiece(2), (DEPTH, 3))),
        ("norm_post", norm_post, m_norm_post, v_norm_post, my_ch(gpiece(3), (DEPTH, 3))),
        ("fox_b_f", fox_b_f, m_fox_b_f, v_fox_b_f, gpiece(4)),
        ("sconv_conv_w", sconv_conv_w, m_sconv_conv_w, v_sconv_conv_w, my_ch(gpiece(5), (1, 3))),
        ("lru_conv_w", lru_conv_w, m_lru_conv_w, v_lru_conv_w, my_ch(gpiece(6), (1, 4))),
        ("lru_conv_b", lru_conv_b, m_lru_conv_b, v_lru_conv_b, my_ch(gpiece(7), (1,))),
        ("lru_w_a", lru_w_a, m_lru_w_a, v_lru_w_a, gpiece(8)),
        ("lru_b_a", lru_b_a, m_lru_b_a, v_lru_b_a, gpiece(9)),
        ("lru_w_x", lru_w_x, m_lru_w_x, v_lru_w_x, gpiece(10)),
        ("lru_b_x", lru_b_x, m_lru_b_x, v_lru_b_x, gpiece(11)),
        ("lru_lambda", lru_lambda, m_lru_lambda, v_lru_lambda, my_ch(gpiece(12), (1,))),
    ]
    pack = lambda ts: jnp.concatenate([t.reshape(-1) for t in ts])
    ssz = [w_.size for _, w_, _, _, _ in small]
    tot = sum(ssz)
    padr = (-tot) % (16 * 1024)
    pk = lambda ts: jnp.pad(pack(ts), (0, padr)).reshape(1, -1, 1024)
    sg, sd, smm, svv = _adamw(pk([t[4] for t in small])[:, None], pk([t[1] for t in small]),
                              pk([t[2] for t in small]), pk([t[3] for t in small]), "adamw_small")
    soff = 0
    for (name, w_, _, _, _), n_ in zip(small, ssz):
        out[name] = tuple(t.reshape(-1)[soff:soff + n_].reshape(w_.shape) for t in (sg, sd, smm, svv))
        soff += n_

    after[0] = sd
    out["w_ffn_in"] = update(["wi" + k_ for k_ in lf], w_ffn_in, m_w_ffn_in, v_w_ffn_in, "wi")

    names =["w_cond", "b_cond", "norm_pre", "norm_post", "w_ffn_in", "w_ffn_out", "fox_w_in", "fox_b_f",
             "fox_w_out", "sconv_w_in", "sconv_conv_w", "sconv_w_out", "lru_w_in", "lru_conv_w", "lru_conv_b",
             "lru_w_a", "lru_b_a", "lru_w_x", "lru_b_x", "lru_lambda", "lru_w_out"]
    return (loss, grad_x, *[out[n_][0] for n_ in names], *[out[n_][1] for n_ in names],
            *[out[n_][2] for n_ in names], *[out[n_][3] for n_ in names])
```

```python
import functools
import math

import jax
import jax.numpy as jnp
from jax import lax
from jax.experimental import pallas as pl
from jax.experimental.pallas import tpu as pltpu

F32 = jnp.float32
BF16 = jnp.bfloat16
NDEV = 8
D = 1024
DFF = 2816
FFB = 704
HEADS = 16
HDIM = 64
DEPTH = 4
RMS_EPS = 1e-6
LRU_C = 8.0
ADAM_LR, ADAM_B1, ADAM_B2, ADAM_EPS, ADAM_WD, ADAM_STEP = 0.001, 0.9, 0.999, 1e-08, 0.01, 10
FOXP = 3200
MESH = pl.DeviceIdType.MESH
HBM = pl.BlockSpec(memory_space=pltpu.HBM)
VMEM_BIG = 48 * 1024 * 1024

_NN = (((1,), (0,)), ((), ()))
_NT = (((1,), (1,)), ((), ()))
_TN = (((0,), (0,)), ((), ()))
_DIMS = {"nn": _NN, "nt": _NT, "tn": _TN}


def _cparams(sem=None, vmem=None):
    kw = {}
    if sem is not None:
        kw["dimension_semantics"] = sem
    if vmem is not None:
        kw["vmem_limit_bytes"] = vmem
    return pltpu.CompilerParams(**kw)


def _sigmoid(x):
    return 1.0 / (1.0 + jnp.exp(-x))


def _softplus(x):
    return jnp.maximum(x, 0.0) + jnp.log(1.0 + jnp.exp(-jnp.abs(x)))


_GELU_C = math.sqrt(2.0 / math.pi)


def _gelu_parts(x):
    u = _GELU_C * (x + 0.044715 * x * x * x)
    t = jnp.tanh(u)
    g = 0.5 * x * (1.0 + t)
    dg = 0.5 * (1.0 + t) + 0.5 * x * (1.0 - t * t) * _GELU_C * (1.0 + 3.0 * 0.044715 * x * x)
    return g, dg


def _mesh_pos():
    ax, ay, ac = lax.axis_index("x"), lax.axis_index("y"), lax.axis_index("c")
    return ax, ay, ac, 4 * ax + 2 * ay + ac


def _peer(ax, ay, ac, d):
    px = 1 - ax if (d >> 2) & 1 else ax
    py = 1 - ay if (d >> 1) & 1 else ay
    pc = 1 - ac if d & 1 else ac
    return (px, py, pc), 4 * px + 2 * py + pc


def _exchange(x, axis, name, gather):
    if gather:
        x = jnp.expand_dims(x, axis)
        oshape = x.shape[:axis] + (NDEV,) + x.shape[axis + 1:]
    else:
        oshape = x.shape
    lead = (slice(None),) * axis

    def blk(ref, k):
        return ref.at[lead + (pl.ds(k, 1),)]

    def body(x_ref, o_ref, send_sems, recv_sems, local_sem):
        ax, ay, ac, me = _mesh_pos()
        src = (lambda k: x_ref) if gather else (lambda k: blk(x_ref, k))
        mine = pltpu.make_async_copy(src(me), blk(o_ref, me), local_sem)
        mine.start()
        sends = []
        for d in range(1, NDEV):
            peer, pidx = _peer(ax, ay, ac, d)
            cp = pltpu.make_async_remote_copy(
                src_ref=src(pidx), dst_ref=blk(o_ref, me), send_sem=send_sems.at[d - 1],
                recv_sem=recv_sems.at[d - 1], device_id=peer, device_id_type=MESH)
            cp.start()
            sends.append(cp)
        for d in range(1, NDEV):
            peer, pidx = _peer(ax, ay, ac, d)
            pltpu.make_async_remote_copy(
                src_ref=src(pidx), dst_ref=blk(o_ref, pidx), send_sem=send_sems.at[d - 1],
                recv_sem=recv_sems.at[d - 1], device_id=peer, device_id_type=MESH).wait_recv()
        for cp in sends:
            cp.wait_send()
        mine.wait()

    return pl.pallas_call(
        body, name=name, out_shape=jax.ShapeDtypeStruct(oshape, x.dtype),
        in_specs=[HBM], out_specs=HBM,
        scratch_shapes=[pltpu.SemaphoreType.DMA((NDEV - 1,)), pltpu.SemaphoreType.DMA((NDEV - 1,)),
                        pltpu.SemaphoreType.DMA(())],
    )(x)


def _all_gather(x, axis, name):
    return _exchange(x, axis, name, True)


SEM = pl.BlockSpec(memory_space=pltpu.SEMAPHORE)
EFFECT = pltpu.SideEffectType.DATAFLOW_SIDE_EFFECTING


def _exchange_start(x, name, gather):
    me = 4 * lax.axis_index("x") + 2 * lax.axis_index("y") + lax.axis_index("c")
    if gather:
        x = x[None]
        oshape = (NDEV,) + x.shape[1:]
        own = x
    else:
        oshape = x.shape
        own = lax.dynamic_index_in_dim(x, me, 0, keepdims=True)
    land = lax.dynamic_update_index_in_dim(lax.empty(oshape, x.dtype), own, me, 0)

    def blk(ref, k):
        return ref.at[pl.ds(k, 1)]

    def body(x_ref, land_ref, send_sems, recv_sems, x_thru, land_thru, token):
        ax, ay, ac, me = _mesh_pos()
        src = (lambda k: x_ref) if gather else (lambda k: blk(x_ref, k))
        for d in range(1, NDEV):
            peer, pidx = _peer(ax, ay, ac, d)
            pltpu.make_async_remote_copy(
                src_ref=src(pidx), dst_ref=blk(land_ref, me), send_sem=send_sems.at[d - 1],
                recv_sem=recv_sems.at[d - 1], device_id=peer, device_id_type=MESH).start()
        token[...] = jnp.zeros_like(token)

    return pl.pallas_call(
        body, name=name,
        out_shape=(pltpu.SemaphoreType.DMA((NDEV - 1,)), pltpu.SemaphoreType.DMA((NDEV - 1,)),
                   pltpu.HBM(x.shape, x.dtype), pltpu.HBM(oshape, x.dtype),
                   jax.ShapeDtypeStruct((8, 128), F32)),
        in_specs=(HBM, HBM), out_specs=(SEM, SEM, HBM, HBM, pl.BlockSpec(memory_space=pltpu.VMEM)),
        input_output_aliases={0: 2, 1: 3},
        compiler_params=pltpu.CompilerParams(has_side_effects=EFFECT),
    )(pltpu.with_memory_space_constraint(x, pltpu.HBM),
      pltpu.with_memory_space_constraint(land, pltpu.HBM))


def _exchange_wait(handle, after, name, gather):
    send_sems, recv_sems, x_thru, land_thru = handle

    def blk(ref, k):
        return ref.at[pl.ds(k, 1)]

    def body(x_ref, land_ref, send_sems, recv_sems, after_ref, x_dead, got_ref):
        ax, ay, ac, me = _mesh_pos()
        src = (lambda k: x_ref) if gather else (lambda k: blk(x_ref, k))
        for d in range(1, NDEV):
            peer, pidx = _peer(ax, ay, ac, d)
            cp = pltpu.make_async_remote_copy(
                src_ref=src(pidx), dst_ref=blk(land_ref, pidx), send_sem=send_sems.at[d - 1],
                recv_sem=recv_sems.at[d - 1], device_id=peer, device_id_type=MESH)
            cp.wait_send()
            cp.wait_recv()

    return pl.pallas_call(
        body, name=name,
        out_shape=(pltpu.HBM(x_thru.shape, x_thru.dtype), pltpu.HBM(land_thru.shape, land_thru.dtype)),
        in_specs=(HBM, HBM, SEM, SEM, pl.BlockSpec(memory_space=pl.ANY)), out_specs=(HBM, HBM),
        input_output_aliases={0: 0, 1: 1},
        compiler_params=pltpu.CompilerParams(has_side_effects=EFFECT),
    )(x_thru, land_thru, send_sems, recv_sems, after)[1]


def _mm(a, b, *, mode, grid, a_spec, b_spec, o_spec, o_shape, acc_shape, out_dtype, name):
    nred = grid[2]

    def body(a_ref, b_ref, o_ref, *scratch):
        p = lax.dot_general(a_ref[...], b_ref[...], _DIMS[mode], preferred_element_type=F32)
        if nred == 1:
            o_ref[...] = p.astype(o_ref.dtype)
            return
        acc = scratch[0]
        r = pl.program_id(2)

        @pl.when(r == 0)
        def _():
            acc[...] = p

        @pl.when(r > 0)
        def _():
            acc[...] += p

        @pl.when(r == nred - 1)
        def _():
            o_ref[...] = acc[...].astype(o_ref.dtype)

    return pl.pallas_call(
        body, name=name, out_shape=jax.ShapeDtypeStruct(o_shape, out_dtype), grid=grid,
        in_specs=[a_spec, b_spec], out_specs=o_spec,
        scratch_shapes=[] if nred == 1 else [pltpu.VMEM(acc_shape, F32)],
        compiler_params=_cparams(("parallel", "parallel", "arbitrary"), VMEM_BIG),
    )(a, b)


def _tile(n, cands):
    for c in cands:
        if n % c == 0:
            return c
    return n


def _mm_nn(a, b, out_dtype, name):
    m, k = a.shape
    n = b.shape[1]
    tm, tn = _tile(m, (512, 256, 128)), _tile(n, (1024, 640, 512, 256, 128))
    return _mm(a, b, mode="nn", grid=(m // tm, n // tn, 1),
               a_spec=pl.BlockSpec((tm, k), lambda i, j, r: (i, 0)),
               b_spec=pl.BlockSpec((k, tn), lambda i, j, r: (0, j)),
               o_spec=pl.BlockSpec((tm, tn), lambda i, j, r: (i, j)),
               o_shape=(m, n), acc_shape=None, out_dtype=out_dtype, name=name)


def _mm_nt(a, b, out_dtype, name):
    m, n = a.shape
    k = b.shape[0]
    tm, tn = _tile(m, (512, 256, 128)), _tile(n, (1024, 640, 512, 256, 128))
    return _mm(a, b, mode="nt", grid=(m // tm, 1, n // tn),
               a_spec=pl.BlockSpec((tm, tn), lambda i, j, r: (i, r)),
               b_spec=pl.BlockSpec((k, tn), lambda i, j, r: (0, r)),
               o_spec=pl.BlockSpec((tm, k), lambda i, j, r: (i, 0)),
               o_shape=(m, k), acc_shape=(tm, k), out_dtype=out_dtype, name=name)


def _mm_tn(a, b, out_dtype, name):
    s, k = a.shape
    n = b.shape[1]
    ts, tn = _tile(s, (512, 256, 128)), _tile(n, (640, 512, 256, 128))
    return _mm(a, b, mode="tn", grid=(1, n // tn, s // ts),
               a_spec=pl.BlockSpec((ts, k), lambda i, j, r: (r, 0)),
               b_spec=pl.BlockSpec((ts, tn), lambda i, j, r: (r, j)),
               o_spec=pl.BlockSpec((k, tn), lambda i, j, r: (0, j)),
               o_shape=(k, n), acc_shape=(k, tn), out_dtype=out_dtype, name=name)


def _ffn_up(h, wi, name):
    s = h.shape[0]
    tm = _tile(s, (512, 256, 128))

    def body(h_ref, wg_ref, wu_ref, g_ref, u_ref, act_ref):
        hv = h_ref[...]
        g = jnp.dot(hv, wg_ref[...], preferred_element_type=F32)
        u = jnp.dot(hv, wu_ref[...], preferred_element_type=F32)
        g_ref[...] = g.astype(BF16)
        u_ref[...] = u.astype(BF16)
        act_ref[...] = (g * _sigmoid(g) * u).astype(BF16)

    wspec = lambda off: pl.BlockSpec((None, D, FFB), lambda j, i: (j + off, 0, 0))
    ospec = pl.BlockSpec((None, tm, FFB), lambda j, i: (j, i, 0))
    shp = jax.ShapeDtypeStruct((4, s, FFB), BF16)
    return pl.pallas_call(
        body, name=name, out_shape=(shp, shp, shp), grid=(4, s // tm),
        in_specs=[pl.BlockSpec((tm, D), lambda j, i: (i, 0)), wspec(0), wspec(4)],
        out_specs=(ospec, ospec, ospec),
        compiler_params=_cparams(("parallel", "parallel"), VMEM_BIG),
    )(h, wi, wi)


def _ffn_down(act, wo, name):
    s = act.shape[1]
    tm = _tile(s, (512, 256, 128))
    return _mm(act, wo, mode="nn", grid=(s // tm, 1, 4),
               a_spec=pl.BlockSpec((None, tm, FFB), lambda i, j, r: (r, i, 0)),
               b_spec=pl.BlockSpec((None, FFB, D), lambda i, j, r: (r, 0, 0)),
               o_spec=pl.BlockSpec((tm, D), lambda i, j, r: (i, 0)),
               o_shape=(s, D), acc_shape=(tm, D), out_dtype=F32, name=name)


def _ffn_down_bwd(dy, g, u, wo, name):
    s = dy.shape[0]
    tm = _tile(s, (512, 256, 128))
    ns = s // tm

    def body(dy_ref, g_ref, u_ref, wo_ref, dg_ref, du_ref, dwo_ref, acc):
        i = pl.program_id(1)
        dyv = dy_ref[...]
        dact = lax.dot_general(dyv, wo_ref[...], _NT, preferred_element_type=F32)
        gv = g_ref[...].astype(F32)
        uv = u_ref[...].astype(F32)
        sg = _sigmoid(gv)
        silu = gv * sg
        dg_ref[...] = (dact * uv * (sg * (1.0 + gv * (1.0 - sg)))).astype(BF16)
        du_ref[...] = (dact * silu).astype(BF16)
        p = lax.dot_general((silu * uv).astype(BF16), dyv, _TN, preferred_element_type=F32)

        @pl.when(i == 0)
        def _():
            acc[...] = p

        @pl.when(i > 0)
        def _():
            acc[...] += p

        @pl.when(i == ns - 1)
        def _():
            dwo_ref[...] = acc[...].astype(BF16)

    aspec = pl.BlockSpec((None, tm, FFB), lambda j, i: (j, i, 0))
    shp = jax.ShapeDtypeStruct((4, s, FFB), BF16)
    return pl.pallas_call(
        body, name=name, out_shape=(shp, shp, jax.ShapeDtypeStruct((4, FFB, D), BF16)),
        grid=(4, ns),
        in_specs=[pl.BlockSpec((tm, D), lambda j, i: (i, 0)), aspec, aspec,
                  pl.BlockSpec((None, FFB, D), lambda j, i: (j, 0, 0))],
        out_specs=(aspec, aspec, pl.BlockSpec((None, FFB, D), lambda j, i: (j, 0, 0))),
        scratch_shapes=[pltpu.VMEM((FFB, D), F32)],
        compiler_params=_cparams(("parallel", "arbitrary"), VMEM_BIG),
    )(dy, g, u, wo)


def _ffn_dh(dg, du, wi, name):
    s = dg.shape[1]
    tm = _tile(s, (512, 256, 128))

    def body(dg_ref, du_ref, wg_ref, wu_ref, o_ref, acc):
        r = pl.program_id(1)
        p = (lax.dot_general(dg_ref[...], wg_ref[...], _NT, preferred_element_type=F32)
             + lax.dot_general(du_ref[...], wu_ref[...], _NT, preferred_element_type=F32))

        @pl.when(r == 0)
        def _():
            acc[...] = p

        @pl.when(r > 0)
        def _():
            acc[...] += p

        @pl.when(r == 3)
        def _():
            o_ref[...] = acc[...]

    aspec = pl.BlockSpec((None, tm, FFB), lambda i, r: (r, i, 0))
    wspec = lambda off: pl.BlockSpec((None, D, FFB), lambda i, r: (r + off, 0, 0))
    return pl.pallas_call(
        body, name=name, out_shape=jax.ShapeDtypeStruct((s, D), F32), grid=(s // tm, 4),
        in_specs=[aspec, aspec, wspec(0), wspec(4)],
        out_specs=pl.BlockSpec((tm, D), lambda i, r: (i, 0)),
        scratch_shapes=[pltpu.VMEM((tm, D), F32)],
        compiler_params=_cparams(("parallel", "arbitrary"), VMEM_BIG),
    )(dg, du, wi, wi)


def _ffn_dwi(h, da, name):
    s = h.shape[0]
    ts = _tile(s, (512, 256, 128))
    return _mm(h, da, mode="tn", grid=(4, 1, s // ts),
               a_spec=pl.BlockSpec((ts, D), lambda k, j, r: (r, 0)),
               b_spec=pl.BlockSpec((None, ts, FFB), lambda k, j, r: (k, r, 0)),
               o_spec=pl.BlockSpec((None, D, FFB), lambda k, j, r: (k, 0, 0)),
               o_shape=(4, D, FFB), acc_shape=(D, FFB), out_dtype=BF16, name=name)


TR = 256


def _rows_spec(s):
    tr = _tile(s, (TR, 128))
    return tr, pl.BlockSpec((tr, D), lambda i: (i, 0))


def _pspec(l, sub):
    return pl.BlockSpec((None, None, 8, D), lambda i: (l, sub, 0, 0))


def _pre_math(x, p_ref):
    r = lax.rsqrt(jnp.mean(x * x, axis=1, keepdims=True) + RMS_EPS)
    return (x * r) * p_ref[0:1, :] * (1.0 + p_ref[1:2, :]) + p_ref[2:3, :]


ANYSPEC = pl.BlockSpec(memory_space=pl.ANY)


def _prenorm(x, ptab, l, sub, name, toks=()):
    s = x.shape[0]
    tr, spec = _rows_spec(s)

    def body(x_ref, p_ref, *rest):
        rest[-1][...] = _pre_math(x_ref[...], p_ref).astype(BF16)

    return pl.pallas_call(
        body, name=name, out_shape=jax.ShapeDtypeStruct((s, D), BF16), grid=(s // tr,),
        in_specs=[spec, _pspec(l, sub)] + [ANYSPEC] * len(toks), out_specs=spec,
        compiler_params=_cparams(("parallel",)),
    )(x, ptab, *toks)


def _resid(x, y, ptab, l, sub, coef, nxt, name):
    s = x.shape[0]
    tr, spec = _rows_spec(s)

    def body(x_ref, y_ref, p_ref, *rest):
        yv = y_ref[...]
        r = lax.rsqrt(jnp.mean(yv * yv, axis=1, keepdims=True) + RMS_EPS)
        xn = x_ref[...] + (coef * p_ref[4:5, :]) * ((yv * r) * p_ref[3:4, :])
        if nxt is None:
            rest[0][...] = xn
        else:
            rest[1][...] = xn
            rest[2][...] = _pre_math(xn, rest[0]).astype(BF16)

    if nxt is None:
        return pl.pallas_call(
            body, name=name, out_shape=jax.ShapeDtypeStruct((s, D), F32), grid=(s // tr,),
            in_specs=[spec, spec, _pspec(l, sub)], out_specs=spec,
            compiler_params=_cparams(("parallel",)),
        )(x, y, ptab), None
    return pl.pallas_call(
        body, name=name,
        out_shape=(jax.ShapeDtypeStruct((s, D), F32), jax.ShapeDtypeStruct((s, D), BF16)),
        grid=(s // tr,),
        in_specs=[spec, spec, _pspec(l, sub), _pspec(*nxt)], out_specs=(spec, spec),
        compiler_params=_cparams(("parallel",)),
    )(x, y, ptab, ptab)


def _loss_grad(x, tgt, name):
    s = x.shape[0]
    tr, spec = _rows_spec(s)

    def body(x_ref, t_ref, dx_ref, l_ref):
        @pl.when(pl.program_id(0) == 0)
        def _():
            l_ref[...] = jnp.zeros_like(l_ref)

        e = x_ref[...] - t_ref[...]
        dx_ref[...] = e * (1.0 / D)
        l_ref[0:1, :] += jnp.sum(e * e, axis=0, keepdims=True) * (0.5 / D)

    return pl.pallas_call(
        body, name=name,
        out_shape=(jax.ShapeDtypeStruct((s, D), F32), jax.ShapeDtypeStruct((8, D), F32)),
        grid=(s // tr,), in_specs=[spec, spec],
        out_specs=(spec, pl.BlockSpec((8, D), lambda i: (0, 0))),
        compiler_params=_cparams(("arbitrary",)),
    )(x, tgt)


def _post_bwd(dx, y, ptab, l, sub, coef, name, toks=()):
    s = dx.shape[0]
    tr, spec = _rows_spec(s)

    def body(dx_ref, y_ref, p_ref, *rest):
        dy_ref, red_ref = rest[-2:]

        @pl.when(pl.program_id(0) == 0)
        def _():
            red_ref[...] = jnp.zeros_like(red_ref)

        dxv, yv = dx_ref[...], y_ref[...]
        gpost, gate = p_ref[3:4, :], p_ref[4:5, :]
        r = lax.rsqrt(jnp.mean(yv * yv, axis=1, keepdims=True) + RMS_EPS)
        yhat = yv * r
        red_ref[0:1, :] += jnp.sum(dxv * yhat * gpost, axis=0, keepdims=True) * coef
        dn = dxv * (coef * gate)
        red_ref[1:2, :] += jnp.sum(dn * yhat, axis=0, keepdims=True)
        dyh = dn * gpost
        dy_ref[...] = (r * (dyh - yhat * jnp.mean(dyh * yhat, axis=1, keepdims=True))).astype(BF16)

    return pl.pallas_call(
        body, name=name,
        out_shape=(jax.ShapeDtypeStruct((s, D), BF16), jax.ShapeDtypeStruct((8, D), F32)),
        grid=(s // tr,), in_specs=[spec, spec, _pspec(l, sub)] + [ANYSPEC] * len(toks),
        out_specs=(spec, pl.BlockSpec((8, D), lambda i: (0, 0))),
        compiler_params=_cparams(("arbitrary",)),
    )(dx, y, ptab, *toks)


def _pre_bwd(dx, dh, x, ptab, l, sub, name):
    s = dx.shape[0]
    tr, spec = _rows_spec(s)

    def body(dx_ref, dh_ref, x_ref, p_ref, o_ref, red_ref):
        @pl.when(pl.program_id(0) == 0)
        def _():
            red_ref[...] = jnp.zeros_like(red_ref)

        dhv, xv = dh_ref[...], x_ref[...]
        gpre, scale = p_ref[0:1, :], p_ref[1:2, :]
        r = lax.rsqrt(jnp.mean(xv * xv, axis=1, keepdims=True) + RMS_EPS)
        xhat = xv * r
        red_ref[0:1, :] += jnp.sum(dhv, axis=0, keepdims=True)
        red_ref[1:2, :] += jnp.sum(dhv * xhat * gpre, axis=0, keepdims=True)
        red_ref[2:3, :] += jnp.sum(dhv * xhat * (1.0 + scale), axis=0, keepdims=True)
        dxh = dhv * (gpre * (1.0 + scale))
        o_ref[...] = dx_ref[...] + r * (dxh - xhat * jnp.mean(dxh * xhat, axis=1, keepdims=True))

    return pl.pallas_call(
        body, name=name,
        out_shape=(jax.ShapeDtypeStruct((s, D), F32), jax.ShapeDtypeStruct((8, D), F32)),
        grid=(s // tr,), in_specs=[spec, spec, spec, _pspec(l, sub)],
        out_specs=(spec, pl.BlockSpec((8, D), lambda i: (0, 0))),
        compiler_params=_cparams(("arbitrary",)),
    )(dx, dh, x, ptab)


def _cond_fwd(c_pad, wc, bc, name):
    w = wc.shape[2]

    def body(c_ref, w_ref, b_ref, o_ref, ca_ref):
        cv = c_ref[...]
        ca = (cv * _sigmoid(cv)).astype(BF16)
        ca_ref[...] = ca
        o_ref[...] = jnp.dot(ca, w_ref[...].astype(BF16), preferred_element_type=F32) + b_ref[...]

    return pl.pallas_call(
        body, name=name,
        out_shape=(jax.ShapeDtypeStruct((DEPTH, 128, w), F32), jax.ShapeDtypeStruct((128, D), BF16)),
        grid=(DEPTH,),
        in_specs=[pl.BlockSpec((128, D), lambda i: (0, 0)),
                  pl.BlockSpec((None, D, w), lambda i: (i, 0, 0)),
                  pl.BlockSpec((None, 1, w), lambda i: (i, 0, 0))],
        out_specs=(pl.BlockSpec((None, 128, w), lambda i: (i, 0, 0)),
                   pl.BlockSpec((128, D), lambda i: (0, 0))),
        compiler_params=_cparams(("arbitrary",), VMEM_BIG),
    )(c_pad, wc, bc)


def _cond_bwd(ca_pad, dmod, name):
    w = dmod.shape[2]
    return _mm(ca_pad, dmod, mode="tn", grid=(DEPTH, 1, 1),
               a_spec=pl.BlockSpec((128, D), lambda i, j, r: (0, 0)),
               b_spec=pl.BlockSpec((None, 128, w), lambda i, j, r: (i, 0, 0)),
               o_spec=pl.BlockSpec((None, D, w), lambda i, j, r: (i, 0, 0)),
               o_shape=(DEPTH, D, w), acc_shape=None, out_dtype=F32, name=name)


def _split3(x):
    hi = x.astype(BF16)
    r1 = x - hi.astype(F32)
    mid = r1.astype(BF16)
    lo = (r1 - mid.astype(F32)).astype(BF16)
    return hi, mid, lo


def _tri_dot(t, x):
    hi, mid, lo = _split3(x)
    return (jnp.dot(t, hi, preferred_element_type=F32) + jnp.dot(t, mid, preferred_element_type=F32)
            + jnp.dot(t, lo, preferred_element_type=F32))


def _fox_cum(fl, bf, name):
    s = fl.shape[0]
    tb = _tile(s, (256, 128))

    def body(fl_ref, b_ref, cum_ref):
        row = lax.broadcasted_iota(jnp.int32, (tb, tb), 0)
        col = lax.broadcasted_iota(jnp.int32, (tb, tb), 1)
        tri = (col <= row).astype(BF16)
        carry = jnp.zeros((1, 128), F32)
        for blk in range(s // tb):
            z = fl_ref[blk * tb:(blk + 1) * tb, :] + b_ref[0:1, :]
            lf = jnp.minimum(z, 0.0) - jnp.log(1.0 + jnp.exp(-jnp.abs(z)))
            cum_ref[blk * tb:(blk + 1) * tb, :] = _tri_dot(tri, lf) + carry
            carry = carry + jnp.sum(lf, axis=0, keepdims=True)

    return pl.pallas_call(
        body, name=name, out_shape=jax.ShapeDtypeStruct((s, 128), F32),
    )(fl, bf)


def _fox_cum_bwd(dcum, fl, bf, name):
    s = fl.shape[0]
    tb = _tile(s, (256, 128))

    def body(dc_ref, fl_ref, b_ref, dfl_ref, db_ref):
        row = lax.broadcasted_iota(jnp.int32, (tb, tb), 0)
        col = lax.broadcasted_iota(jnp.int32, (tb, tb), 1)
        tri = (col >= row).astype(BF16)
        carry = jnp.zeros((1, 128), F32)
        dbs = jnp.zeros((1, 128), F32)
        for blk in reversed(range(s // tb)):
            dc = dc_ref[blk * tb:(blk + 1) * tb, :]
            dl = _tri_dot(tri, dc) + carry
            carry = carry + jnp.sum(dc, axis=0, keepdims=True)
            z = fl_ref[blk * tb:(blk + 1) * tb, :] + b_ref[0:1, :]
            dz = dl * _sigmoid(-z)
            dfl_ref[blk * tb:(blk + 1) * tb, :] = dz
            dbs = dbs + jnp.sum(dz, axis=0, keepdims=True)
        db_ref[...] = jnp.broadcast_to(dbs, (8, 128))

    return pl.pallas_call(
        body, name=name,
        out_shape=(jax.ShapeDtypeStruct((s, 128), F32), jax.ShapeDtypeStruct((8, 128), F32)),
    )(dcum, fl, bf)


def _fox_scores(q, k_ref, cq, ck_ref, qi, tq, n):
    sc = lax.dot_general(q, k_ref[0:n, :], _NT, preferred_element_type=F32) * (HDIM ** -0.5)
    sc = sc + cq - ck_ref[:, 0:n]
    row = lax.broadcasted_iota(jnp.int32, (tq, n), 0) + qi * tq
    col = lax.broadcasted_iota(jnp.int32, (tq, n), 1)
    return sc, col <= row


def _fox_attn_fwd(q, k, v, cq, ck, name):
    s = q.shape[1]
    tq = _tile(s, (256, 128))

    def body(q_ref, k_ref, v_ref, cq_ref, ck_ref, o_ref, lse_ref):
        for qi in range(s // tq):
            n = (qi + 1) * tq
            rows = slice(qi * tq, n)
            sc, keep = _fox_scores(q_ref[rows, :], k_ref, cq_ref[rows, :], ck_ref, qi, tq, n)
            sc = jnp.where(keep, sc, -1e30)
            m = jnp.max(sc, axis=1, keepdims=True)
            p = jnp.exp(sc - m)
            lsum = jnp.sum(p, axis=1, keepdims=True)
            o = jnp.dot(p.astype(BF16), v_ref[0:n, :], preferred_element_type=F32) / lsum
            o_ref[rows, :] = o.astype(BF16)
            lse_ref[rows, :] = m + jnp.log(lsum)

    hspec = pl.BlockSpec((None, s, HDIM), lambda h: (h, 0, 0))
    cspec = pl.BlockSpec((None, s, 1), lambda h: (h, 0, 0))
    rspec = pl.BlockSpec((None, 1, s), lambda h: (h, 0, 0))
    return pl.pallas_call(
        body, name=name,
        out_shape=(jax.ShapeDtypeStruct((HEADS, s, HDIM), BF16), jax.ShapeDtypeStruct((HEADS, s, 1), F32)),
        grid=(HEADS,), in_specs=[hspec, hspec, hspec, cspec, rspec], out_specs=(hspec, cspec),
        compiler_params=_cparams(("parallel",), VMEM_BIG),
    )(q, k, v, cq, ck)


def _fox_attn_bwd(q, k, v, do, lse, cq, ck, name):
    s = q.shape[1]
    tq = _tile(s, (256, 128))
    scale = HDIM ** -0.5

    def body(q_ref, k_ref, v_ref, do_ref, lse_ref, cq_ref, ck_ref,
             dq_ref, dk_ref, dv_ref, dcq_ref, dck_ref, dk_acc, dv_acc, dck_acc):
        dk_acc[...] = jnp.zeros_like(dk_acc)
        dv_acc[...] = jnp.zeros_like(dv_acc)
        dck_acc[...] = jnp.zeros_like(dck_acc)
        for qi in range(s // tq):
            n = (qi + 1) * tq
            rows = slice(qi * tq, n)
            qv, dov = q_ref[rows, :], do_ref[rows, :]
            sc, keep = _fox_scores(qv, k_ref, cq_ref[rows, :], ck_ref, qi, tq, n)
            p = jnp.where(keep, jnp.exp(sc - lse_ref[rows, :]), 0.0)
            dp = lax.dot_general(dov, v_ref[0:n, :], _NT, preferred_element_type=F32)
            ds = p * (dp - jnp.sum(p * dp, axis=1, keepdims=True))
            dsb = ds.astype(BF16)
            dq_ref[rows, :] = (jnp.dot(dsb, k_ref[0:n, :], preferred_element_type=F32) * scale).astype(BF16)
            dk_acc[0:n, :] += lax.dot_general(dsb, qv, _TN, preferred_element_type=F32) * scale
            dv_acc[0:n, :] += lax.dot_general(p.astype(BF16), dov, _TN, preferred_element_type=F32)
            dcq_ref[rows, :] = jnp.sum(ds, axis=1, keepdims=True)
            dck_acc[:, 0:n] -= jnp.sum(ds, axis=0, keepdims=True)
        dk_ref[...] = dk_acc[...].astype(BF16)
        dv_ref[...] = dv_acc[...].astype(BF16)
        dck_ref[...] = dck_acc[...]

    hspec = pl.BlockSpec((None, s, HDIM), lambda h: (h, 0, 0))
    cspec = pl.BlockSpec((None, s, 1), lambda h: (h, 0, 0))
    rspec = pl.BlockSpec((None, 1, s), lambda h: (h, 0, 0))
    hs = jax.ShapeDtypeStruct((HEADS, s, HDIM), BF16)
    return pl.pallas_call(
        body, name=name,
        out_shape=(hs, hs, hs, jax.ShapeDtypeStruct((HEADS, s, 1), F32), jax.ShapeDtypeStruct((HEADS, 1, s), F32)),
        grid=(HEADS,), in_specs=[hspec, hspec, hspec, hspec, cspec, cspec, rspec],
        out_specs=(hspec, hspec, hspec, cspec, rspec),
        scratch_shapes=[pltpu.VMEM((s, HDIM), F32), pltpu.VMEM((s, HDIM), F32), pltpu.VMEM((1, s), F32)],
        compiler_params=_cparams(("parallel",), VMEM_BIG),
    )(q, k, v, do, lse, cq, ck)


TC = 256
HALO = 8


def _chunk_specs(s):
    tc = _tile(s, (TC, 128))
    per = tc // HALO
    nblk = s // HALO
    cur = pl.BlockSpec((tc, 8, 128), lambda i: (i, 0, 0))
    past = pl.BlockSpec((HALO, 8, 128), lambda i: (jnp.maximum(i * per - 1, 0), 0, 0))
    future = pl.BlockSpec((HALO, 8, 128), lambda i: (jnp.minimum((i + 1) * per, nblk - 1), 0, 0))
    return tc, cur, past, future


def _vec_spec(n):
    return pl.BlockSpec((n, 8, 128), lambda i: (0, 0, 0))


def _conv_past(buf, w_ref, kw, tc):
    out = w_ref[kw - 1] * buf[HALO:HALO + tc]
    for k in range(kw - 1):
        off = HALO - (kw - 1) + k
        out = out + w_ref[k] * buf[off:off + tc]
    return out


def _sconv_fwd(bg, cg, xv, w, name):
    s = bg.shape[0]
    tc, cur, past, _ = _chunk_specs(s)

    def body(bg_ref, cg_ref, xv_ref, cgp_ref, xvp_ref, w_ref, y_ref, zbuf):
        first = pl.program_id(0) == 0
        zbuf[0:HALO] = jnp.where(first, 0.0, cgp_ref[...] * xvp_ref[...])
        zbuf[HALO:HALO + tc] = cg_ref[...] * xv_ref[...]
        y_ref[...] = bg_ref[...] * _conv_past(zbuf, w_ref, 3, tc)

    return pl.pallas_call(
        body, name=name, out_shape=jax.ShapeDtypeStruct((s, 8, 128), F32), grid=(s // tc,),
        in_specs=[cur, cur, cur, past, past, _vec_spec(3)], out_specs=cur,
        scratch_shapes=[pltpu.VMEM((tc + HALO, 8, 128), F32)],
        compiler_params=_cparams(("parallel",)),
    )(bg, cg, xv, cg, xv, w)


def _sconv_bwd(dy, bg, cg, xv, w, name):
    s = dy.shape[0]
    tc, cur, past, future = _chunk_specs(s)
    nch = s // tc

    def body(dy_ref, bg_ref, cg_ref, xv_ref, cgp_ref, xvp_ref, dyf_ref, bgf_ref, w_ref,
             dbg_ref, dcg_ref, dxv_ref, dw_ref, zbuf, dbuf):
        i = pl.program_id(0)

        @pl.when(i == 0)
        def _():
            dw_ref[...] = jnp.zeros_like(dw_ref)

        z = cg_ref[...] * xv_ref[...]
        zbuf[0:HALO] = jnp.where(i == 0, 0.0, cgp_ref[...] * xvp_ref[...])
        zbuf[HALO:HALO + tc] = z
        dyv = dy_ref[...]
        dbg_ref[...] = dyv * _conv_past(zbuf, w_ref, 3, tc)
        dbuf[0:tc] = dyv * bg_ref[...]
        dbuf[tc:tc + HALO] = jnp.where(i == nch - 1, 0.0, dyf_ref[...] * bgf_ref[...])
        dz = jnp.zeros((tc, 8, 128), F32)
        for k in range(3):
            sh = dbuf[2 - k:2 - k + tc]
            dz = dz + w_ref[k] * sh
            dw_ref[k] += jnp.sum(z * sh, axis=0)
        dcg_ref[...] = dz * xv_ref[...]
        dxv_ref[...] = dz * cg_ref[...]

    shp = jax.ShapeDtypeStruct((s, 8, 128), F32)
    return pl.pallas_call(
        body, name=name, out_shape=(shp, shp, shp, jax.ShapeDtypeStruct((3, 8, 128), F32)),
        grid=(nch,),
        in_specs=[cur, cur, cur, cur, past, past, future, future, _vec_spec(3)],
        out_specs=(cur, cur, cur, _vec_spec(3)),
        scratch_shapes=[pltpu.VMEM((tc + HALO, 8, 128), F32), pltpu.VMEM((tc + HALO, 8, 128), F32)],
        compiler_params=_cparams(("arbitrary",)),
    )(dy, bg, cg, xv, cg, xv, dy, bg, w)


def _lru_conv_fwd(xp, wb, name):
    s = xp.shape[0]
    tc, cur, past, _ = _chunk_specs(s)

    def body(x_ref, xp_ref, w_ref, o_ref, buf):
        buf[0:HALO] = jnp.where(pl.program_id(0) == 0, 0.0, xp_ref[...])
        buf[HALO:HALO + tc] = x_ref[...]
        o_ref[...] = _conv_past(buf, w_ref, 4, tc) + w_ref[4]

    return pl.pallas_call(
        body, name=name, out_shape=jax.ShapeDtypeStruct((s, 8, 128), F32), grid=(s // tc,),
        in_specs=[cur, past, _vec_spec(8)], out_specs=cur,
        scratch_shapes=[pltpu.VMEM((tc + HALO, 8, 128), F32)],
        compiler_params=_cparams(("parallel",)),
    )(xp, xp, wb)


def _lru_conv_bwd(dxb, xp, wb, name):
    s = dxb.shape[0]
    tc, cur, _, future = _chunk_specs(s)
    nch = s // tc

    def body(d_ref, df_ref, x_ref, w_ref, o_ref, red_ref, dbuf):
        i = pl.program_id(0)

        @pl.when(i == 0)
        def _():
            red_ref[...] = jnp.zeros_like(red_ref)

        dv = d_ref[...]
        dbuf[0:tc] = dv
        dbuf[tc:tc + HALO] = jnp.where(i == nch - 1, 0.0, df_ref[...])
        xv = x_ref[...]
        dx = jnp.zeros((tc, 8, 128), F32)
        for k in range(4):
            sh = dbuf[3 - k:3 - k + tc]
            dx = dx + w_ref[k] * sh
            red_ref[k] += jnp.sum(xv * sh, axis=0)
        red_ref[4] += jnp.sum(dv, axis=0)
        o_ref[...] = dx

    return pl.pallas_call(
        body, name=name,
        out_shape=(jax.ShapeDtypeStruct((s, 8, 128), F32), jax.ShapeDtypeStruct((8, 8, 128), F32)),
        grid=(nch,), in_specs=[cur, future, cur, _vec_spec(8)], out_specs=(cur, _vec_spec(8)),
        scratch_shapes=[pltpu.VMEM((tc + HALO, 8, 128), F32)],
        compiler_params=_cparams(("arbitrary",)),
    )(dxb, dxb, xp, wb)


def _lru_gates(ra, ia, pv_ref):
    sp = _softplus(-pv_ref[2])
    r = _sigmoid(ra + pv_ref[0])
    ig = _sigmoid(ia + pv_ref[1])
    log_a = (-LRU_C) * r * sp
    a = jnp.exp(log_a)
    mult = jnp.sqrt(-jnp.tanh(log_a) * (a * a + 1.0))
    return r, ig, a, mult, sp


def _lru_scan_fwd(ra, ia, xb, gate, pv, name):
    s = ra.shape[0]
    tc, cur, _, _ = _chunk_specs(s)

    def body(ra_ref, ia_ref, xb_ref, g_ref, pv_ref, hs_ref, hp_ref, y_ref, abuf, bbuf, hcar):
        @pl.when(pl.program_id(0) == 0)
        def _():
            hcar[...] = jnp.zeros_like(hcar)

        xbv = xb_ref[...]
        _, ig, a, mult, _ = _lru_gates(ra_ref[...], ia_ref[...], pv_ref)
        abuf[...] = a
        bbuf[...] = mult * (ig * xbv)

        def step(t, h):
            hp_ref[t] = h
            h = abuf[t] * h + bbuf[t]
            hs_ref[t] = h
            return h

        hcar[...] = lax.fori_loop(0, tc, step, hcar[...], unroll=8)
        y_ref[...] = hs_ref[...] * _gelu_parts(g_ref[...])[0]

    shp = jax.ShapeDtypeStruct((s, 8, 128), F32)
    return pl.pallas_call(
        body, name=name, out_shape=(shp, shp, shp), grid=(s // tc,),
        in_specs=[cur, cur, cur, cur, _vec_spec(8)], out_specs=(cur, cur, cur),
        scratch_shapes=[pltpu.VMEM((tc, 8, 128), F32), pltpu.VMEM((tc, 8, 128), F32),
                        pltpu.VMEM((8, 128), F32)],
        compiler_params=_cparams(("arbitrary",)),
    )(ra, ia, xb, gate, pv)


def _lru_scan_bwd(dy, ra, ia, xb, gate, hs, hp, pv, name):
    s = dy.shape[0]
    tc = _tile(s, (TC, 128))
    nch = s // tc
    rev = pl.BlockSpec((tc, 8, 128), lambda i: (nch - 1 - i, 0, 0))

    def body(dy_ref, ra_ref, ia_ref, xb_ref, g_ref, hs_ref, hp_ref, pv_ref,
             dg_ref, dra_ref, dia_ref, dxb_ref, red_ref, abuf, dbuf, gbuf, car):
        @pl.when(pl.program_id(0) == 0)
        def _():
            red_ref[...] = jnp.zeros_like(red_ref)
            car[...] = jnp.zeros_like(car)

        xbv = xb_ref[...]
        r, ig, a, mult, sp = _lru_gates(ra_ref[...], ia_ref[...], pv_ref)
        ge, dge = _gelu_parts(g_ref[...])
        dyv = dy_ref[...]
        dg_ref[...] = dyv * hs_ref[...] * dge
        abuf[...] = a
        dbuf[...] = dyv * ge

        def step(k, c):
            t = tc - 1 - k
            g = dbuf[t] + c
            gbuf[t] = g
            return abuf[t] * g

        car[...] = lax.fori_loop(0, tc, step, car[...], unroll=8)
        g = gbuf[...]
        d_a = g * hp_ref[...]
        d_m = g * (ig * xbv)
        d_loga = d_a * a - d_m * (a * a / mult)
        dxb_ref[...] = g * mult * ig
        dra = d_loga * ((-LRU_C) * sp) * r * (1.0 - r)
        dia = g * mult * xbv * ig * (1.0 - ig)
        dra_ref[...] = dra
        dia_ref[...] = dia
        red_ref[0] += jnp.sum(dra, axis=0)
        red_ref[1] += jnp.sum(dia, axis=0)
        red_ref[2] += jnp.sum(d_loga * r, axis=0) * (LRU_C * _sigmoid(-pv_ref[2]))

    shp = jax.ShapeDtypeStruct((s, 8, 128), F32)
    return pl.pallas_call(
        body, name=name, out_shape=(shp, shp, shp, shp, jax.ShapeDtypeStruct((8, 8, 128), F32)),
        grid=(nch,), in_specs=[rev] * 7 + [_vec_spec(8)],
        out_specs=(rev, rev, rev, rev, _vec_spec(8)),
        scratch_shapes=[pltpu.VMEM((tc, 8, 128), F32), pltpu.VMEM((tc, 8, 128), F32),
                        pltpu.VMEM((tc, 8, 128), F32), pltpu.VMEM((8, 128), F32)],
        compiler_params=_cparams(("arbitrary",)),
    )(dy, ra, ia, xb, gate, hs, hp, pv)


def _adamw(src, w, m, v, name):
    nl, n, rr, cc = src.shape
    tr = rr
    for cand in sorted((d for d in range(16, rr + 1, 16) if rr % d == 0), reverse=True):
        if cand * cc <= 192 * 1024:
            tr = cand
            break
    c1 = 1.0 - ADAM_B1 ** ADAM_STEP
    c2 = 1.0 - ADAM_B2 ** ADAM_STEP

    def body(s_ref, w_ref, m_ref, v_ref, g_out, d_out, m_out, v_out):
        g = s_ref[0].astype(F32)
        for k in range(1, n):
            g = g + s_ref[k].astype(F32)
        mn = ADAM_B1 * m_ref[...] + (1.0 - ADAM_B1) * g
        vn = ADAM_B2 * v_ref[...] + (1.0 - ADAM_B2) * (g * g)
        g_out[...] = g
        m_out[...] = mn
        v_out[...] = vn
        d_out[...] = (-ADAM_LR) * ((mn / c1) / (jnp.sqrt(vn / c2) + ADAM_EPS) + ADAM_WD * w_ref[...])

    pspec = pl.BlockSpec((None, tr, cc), lambda l, i: (l, i, 0))
    shp = jax.ShapeDtypeStruct((nl, rr, cc), F32)
    return pl.pallas_call(
        body, name=name, out_shape=(shp, shp, shp, shp), grid=(nl, rr // tr),
        in_specs=[pl.BlockSpec((None, n, tr, cc), lambda l, i: (l, 0, i, 0)), pspec, pspec, pspec],
        out_specs=(pspec, pspec, pspec, pspec),
        compiler_params=_cparams(("parallel", "parallel"), VMEM_BIG),
    )(src, w, m, v)


def _adamw_slice(src, w, m, v, bufs, idx, name):
    n, rr, cc = src.shape
    nl = w.shape[0]
    tr = rr
    for cand in sorted((d for d in range(16, rr + 1, 16) if rr % d == 0), reverse=True):
        if cand * cc <= 192 * 1024:
            tr = cand
            break
    c1 = 1.0 - ADAM_B1 ** ADAM_STEP
    c2 = 1.0 - ADAM_B2 ** ADAM_STEP
    if bufs is None:
        bufs = tuple(lax.empty((nl, rr, cc), F32) for _ in range(4))

    def body(s_ref, w_ref, m_ref, v_ref, b0, b1, b2, b3, g_out, d_out, m_out, v_out):
        g = s_ref[0].astype(F32)
        for k in range(1, n):
            g = g + s_ref[k].astype(F32)
        mn = ADAM_B1 * m_ref[...] + (1.0 - ADAM_B1) * g
        vn = ADAM_B2 * v_ref[...] + (1.0 - ADAM_B2) * (g * g)
        g_out[...] = g
        m_out[...] = mn
        v_out[...] = vn
        d_out[...] = (-ADAM_LR) * ((mn / c1) / (jnp.sqrt(vn / c2) + ADAM_EPS) + ADAM_WD * w_ref[...])

    pspec = pl.BlockSpec((None, tr, cc), lambda i: (idx, i, 0))
    anyspec = pl.BlockSpec(memory_space=pl.ANY)
    shp = jax.ShapeDtypeStruct((nl, rr, cc), F32)
    return pl.pallas_call(
        body, name=name, out_shape=(shp, shp, shp, shp), grid=(rr // tr,),
        in_specs=[pl.BlockSpec((n, tr, cc), lambda i: (0, i, 0)), pspec, pspec, pspec,
                  anyspec, anyspec, anyspec, anyspec],
        out_specs=(pspec, pspec, pspec, pspec),
        input_output_aliases={4: 0, 5: 1, 6: 2, 7: 3},
        compiler_params=_cparams(("parallel",), VMEM_BIG),
    )(src, w, m, v, *bufs)


def _sum_devices(x, name):
    _, rr, cc = x.shape

    def body(x_ref, o_ref):
        acc = x_ref[0]
        for k in range(1, NDEV):
            acc = acc + x_ref[k]
        o_ref[...] = acc

    return pl.pallas_call(
        body, name=name, out_shape=jax.ShapeDtypeStruct((rr, cc), F32), grid=(rr // 8,),
        in_specs=[pl.BlockSpec((NDEV, 8, cc), lambda i: (0, i, 0))],
        out_specs=pl.BlockSpec((8, cc), lambda i: (i, 0)),
        compiler_params=_cparams(("parallel",)),
    )(x)


def _to3(x):
    return x.reshape(x.shape[0], 8, 128)


def _heads(x):
    s = x.shape[0]
    return x.reshape(s, HEADS, HDIM).transpose(1, 0, 2).astype(BF16)


def _unheads(x):
    s = x.shape[1]
    return x.transpose(1, 0, 2).reshape(s, D)


def _block_diag(w):
    eye = jnp.eye(HEADS, dtype=w.dtype)
    return (w[:, :, None, :] * eye[:, None, :, None]).reshape(D, D)


def _diag_blocks(x):
    return jnp.diagonal(x.reshape(HEADS, HDIM, HEADS, HDIM), axis1=0, axis2=2).transpose(2, 0, 1)


def _col_blocks(dw, n):
    return dw.reshape(dw.shape[0], NDEV, n).transpose(1, 0, 2)


def kernel(x, c, w_cond, b_cond, norm_pre, norm_post, w_ffn_in, w_ffn_out, fox_w_in, fox_b_f, fox_w_out, sconv_w_in, sconv_conv_w, sconv_w_out, lru_w_in, lru_conv_w, lru_conv_b, lru_w_a, lru_b_a, lru_w_x, lru_b_x, lru_lambda, lru_w_out, loss_target, m_w_cond, m_b_cond, m_norm_pre, m_norm_post, m_w_ffn_in, m_w_ffn_out, m_fox_w_in, m_fox_b_f, m_fox_w_out, m_sconv_w_in, m_sconv_conv_w, m_sconv_w_out, m_lru_w_in, m_lru_conv_w, m_lru_conv_b, m_lru_w_a, m_lru_b_a, m_lru_w_x, m_lru_b_x, m_lru_lambda, m_lru_w_out, v_w_cond, v_b_cond, v_norm_pre, v_norm_post, v_w_ffn_in, v_w_ffn_out, v_fox_w_in, v_fox_b_f, v_fox_w_out, v_sconv_w_in, v_sconv_conv_w, v_sconv_w_out, v_lru_w_in, v_lru_conv_w, v_lru_conv_b, v_lru_w_a, v_lru_b_a, v_lru_w_x, v_lru_b_x, v_lru_lambda, v_lru_w_out):
    me = 4 * lax.axis_index("x") + 2 * lax.axis_index("y") + lax.axis_index("c")
    s = x.shape[1]
    x0 = x[0]
    tgt = loss_target[0]
    wcw = w_cond.shape[2]
    dsh = norm_pre.shape[2]

    small_parts = [c.reshape(-1), norm_pre.reshape(-1), norm_post.reshape(-1), sconv_conv_w.reshape(-1),
                   lru_conv_w.reshape(-1), lru_conv_b.reshape(-1), lru_lambda.reshape(-1)]
    sizes = [p.shape[0] for p in small_parts]
    flat = jnp.concatenate(small_parts)
    padn = (-flat.shape[0]) % 1024
    flat = jnp.pad(flat, (0, padn)).reshape(-1, 1024)
    sm = _all_gather(flat, 0, "ag_small").reshape(NDEV, -1)
    offs = [0]
    for n_ in sizes:
        offs.append(offs[-1] + n_)
    piece = lambda i: sm[:, offs[i]:offs[i + 1]]
    c_all = piece(0)
    unshard = lambda p, lead: p.reshape((NDEV,) + lead + (dsh,)).transpose(
        tuple(range(1, len(lead) + 1)) + (0, len(lead) + 1)).reshape(lead + (D,))
    npre = unshard(piece(1), (DEPTH, 3))
    npost = unshard(piece(2), (DEPTH, 3))
    scw = unshard(piece(3), (3,))
    lcw = unshard(piece(4), (4,))
    lcb = unshard(piece(5), ())
    llam = unshard(piece(6), ())

    c_pad = jnp.pad(c_all, ((0, 128 - NDEV), (0, 0)))
    bc_mine = lax.dynamic_slice(b_cond, (0, me * wcw), (DEPTH, wcw)).reshape(DEPTH, 1, wcw)
    modc, ca_pad = _cond_fwd(c_pad, w_cond, bc_mine, "cond_fwd")
    modg = _all_gather(modc[:, :NDEV, :], 0, "ag_mod")
    mod = lax.dynamic_index_in_dim(modg, me, axis=2, keepdims=False)
    mod = mod.transpose(1, 0, 2).reshape(DEPTH, 3, 3, D)
    ptab = jnp.stack([npre, mod[:, :, 1], mod[:, :, 0], npost, mod[:, :, 2],
                      jnp.zeros_like(npre), jnp.zeros_like(npre), jnp.zeros_like(npre)], axis=2)

    pend = {}
    chain = [jnp.zeros((8, 128), F32)]

    def ag_start(key, w):
        hnd = _exchange_start((w + chain[0][0, 0]).astype(BF16), "ags_" + key, True)
        chain[0] = hnd[4]
        pend[key] = hnd[:4]

    def ag_wait(key, after):
        return _exchange_wait(pend.pop(key), after, "agw_" + key, True)

    for l in range(DEPTH):
        ag_start(f"wi{l}0", w_ffn_in[l, 0])
        ag_start(f"wo{l}0", w_ffn_out[l, 0])
        if l % 3 == 0:
            ag_start(f"fwi{l // 3}", fox_w_in[l // 3])
            ag_start(f"fwo{l // 3}", fox_w_out[l // 3])
        elif l % 3 == 1:
            ag_start("swi", sconv_w_in[0])
            ag_start("swo", sconv_w_out[0])
        else:
            ag_start("lwi", lru_w_in[0])
            ag_start("lwo", lru_w_out[0])
        ag_start(f"wi{l}1", w_ffn_in[l, 1])
        ag_start(f"wo{l}1", w_ffn_out[l, 1])
    cols = lambda wg: wg.transpose(1, 0, 2).reshape(D, -1)
    wgate = jnp.concatenate([_block_diag(lru_w_a[0]), _block_diag(lru_w_x[0])], axis=1).astype(BF16)
    bf_pad = jnp.pad(fox_b_f, ((0, 0), (0, 128 - HEADS))).reshape(2, 1, 128)
    scw3 = scw.reshape(3, 8, 128)
    lcwb = jnp.concatenate([lcw, lcb[None], jnp.zeros((3, D), F32)], axis=0).reshape(8, 8, 128)
    lpv = jnp.concatenate([lru_b_a.reshape(1, D), lru_b_x.reshape(1, D), llam[None],
                           jnp.zeros((5, D), F32)], axis=0).reshape(8, 8, 128)

    subs = [(l, sub) for l in range(DEPTH) for sub in range(3)]
    coef = lambda sub: 1.0 if sub == 1 else 0.5

    saved = {}
    xcur = x0
    h = _prenorm(xcur, ptab, 0, 0, "prenorm", toks=(chain[0],))
    for idx, (l, sub) in enumerate(subs):
        tag = f"{l}{sub}"
        sv = {"x": xcur, "h": h}
        if sub != 1:
            f = 0 if sub == 0 else 1
            wi = ag_wait(f"wi{l}{f}", h)
            wo = ag_wait(f"wo{l}{f}", h).reshape(4, FFB, D)
            g, u, act = _ffn_up(h, wi, "ffn_up_" + tag)
            y = _ffn_down(act, wo, "ffn_down_" + tag)
            sv.update(g=g, u=u, wi=wi, wo=wo)
        elif l % 3 == 0:
            j = l // 3
            fwi_j = cols(ag_wait(f"fwi{j}", h))
            fwi_j = jnp.pad(fwi_j, ((0, 0), (0, FOXP - fwi_j.shape[1])))
            fwo_j = ag_wait(f"fwo{j}", h).reshape(D, D)
            sv.update(fwi=fwi_j, fwo=fwo_j)
            proj = _mm_nn(h, fwi_j, F32, "fox_in_" + tag)
            q, k, v = (_heads(proj[:, i * D:(i + 1) * D]) for i in range(3))
            fl = proj[:, 3 * D:]
            cum = _fox_cum(fl, bf_pad[j], "fox_cum_" + tag)
            cumt = cum[:, :HEADS].T
            cq, ck = cumt[:, :, None], cumt[:, None, :]
            o, lse = _fox_attn_fwd(q, k, v, cq, ck, "fox_attn_" + tag)
            o2 = _unheads(o)
            y = _mm_nn(o2, fwo_j, F32, "fox_out_" + tag)
            sv.update(q=q, k=k, v=v, fl=fl, cq=cq, ck=ck, lse=lse, o2=o2)
        elif l % 3 == 1:
            swi = cols(ag_wait("swi", h))
            swo = ag_wait("swo", h).reshape(D, D)
            proj = _mm_nn(h, swi, F32, "sconv_in_" + tag)
            bg, cg, xv = (_to3(proj[:, i * D:(i + 1) * D]) for i in range(3))
            y3 = _sconv_fwd(bg, cg, xv, scw3, "sconv_mix_" + tag)
            y2 = y3.reshape(s, D).astype(BF16)
            y = _mm_nn(y2, swo, F32, "sconv_out_" + tag)
            sv.update(bg=bg, cg=cg, xv=xv, y2=y2)
        else:
            lwi = cols(ag_wait("lwi", h))
            lwo = ag_wait("lwo", h).reshape(D, D)
            proj = _mm_nn(h, lwi, F32, "lru_in_" + tag)
            gate3, xp3 = _to3(proj[:, :D]), _to3(proj[:, D:])
            xb3 = _lru_conv_fwd(xp3, lcwb, "lru_conv_" + tag)
            xb2 = xb3.reshape(s, D).astype(BF16)
            gpre = _mm_nn(xb2, wgate, F32, "lru_gate_" + tag)
            ra3, ia3 = _to3(gpre[:, :D]), _to3(gpre[:, D:])
            hs3, hp3, y3 = _lru_scan_fwd(ra3, ia3, xb3, gate3, lpv, "lru_scan_" + tag)
            y2 = y3.reshape(s, D).astype(BF16)
            y = _mm_nn(y2, lwo, F32, "lru_out_" + tag)
            sv.update(gate3=gate3, xp3=xp3, xb3=xb3, xb2=xb2, ra3=ra3, ia3=ia3, hs3=hs3, hp3=hp3, y2=y2)
        sv["y"] = y
        saved[(l, sub)] = sv
        nxt = subs[idx + 1] if idx + 1 < len(subs) else None
        xcur, h = _resid(xcur, y, ptab, l, sub, coef(sub), nxt, "resid_" + tag)

    dx, loss_cols = _loss_grad(xcur, tgt, "loss_grad")

    d_mod = [[[None] * 3 for _ in range(3)] for _ in range(DEPTH)]
    d_npre = [[None] * 3 for _ in range(DEPTH)]
    d_npost = [[None] * 3 for _ in range(DEPTH)]
    d_fbf = [None, None]
    small_g = {}
    rs_pend, rs_order = {}, []
    toks = []

    def rs_start(key, blocks):
        hnd = _exchange_start(blocks, "rss_" + key, False)
        rs_pend[key] = hnd[:4]
        rs_order.append(key)
        toks.append(hnd[4])

    for (l, sub) in reversed(subs):
        tag = f"{l}{sub}"
        sv = saved[(l, sub)]
        dy, red = _post_bwd(dx, sv["y"], ptab, l, sub, coef(sub), "post_bwd_" + tag, toks=tuple(toks))
        del toks[:]
        d_mod[l][sub][2] = red[0]
        d_npost[l][sub] = red[1]
        hb = sv["h"]
        if sub != 1:
            f = 0 if sub == 0 else 1
            dg, du, dwo = _ffn_down_bwd(dy, sv["g"], sv["u"], sv["wo"], "ffn_down_bwd_" + tag)
            dh = _ffn_dh(dg, du, sv["wi"], "ffn_dh_" + tag)
            rs_start(f"wo{l}{f}", dwo.reshape(NDEV, FFB // 2, D))
            dwi = lambda hb=hb, dg=dg, du=du, tag=tag: jnp.concatenate(
                [_ffn_dwi(hb, dg, "ffn_dwg_" + tag), _ffn_dwi(hb, du, "ffn_dwu_" + tag)], axis=0)
            if (l, sub) != subs[0]:
                rs_start(f"wi{l}{f}", dwi())
        elif l % 3 == 0:
            j = l // 3
            fwi_j, fwo_j = sv["fwi"], sv["fwo"]
            do2 = _mm_nt(dy, fwo_j, BF16, "fox_out_dx_" + tag)
            rs_start(f"fwo{j}", _mm_tn(sv["o2"], dy, BF16, "fox_out_dw_" + tag).reshape(NDEV, dsh, D))
            dq, dk, dv, dcq, dck = _fox_attn_bwd(sv["q"], sv["k"], sv["v"], _heads(do2), sv["lse"],
                                                 sv["cq"], sv["ck"], "fox_attn_bwd_" + tag)
            dcum = (dcq[:, :, 0] + dck[:, 0, :]).T
            dcum = jnp.pad(dcum, ((0, 0), (0, 128 - HEADS)))
            dfl, dbf = _fox_cum_bwd(dcum, sv["fl"], bf_pad[j], "fox_cum_bwd_" + tag)
            d_fbf[j] = dbf[0, :HEADS]
            dproj = jnp.concatenate([_unheads(dq), _unheads(dk), _unheads(dv), dfl.astype(BF16)], axis=1)
            dh = _mm_nt(dproj, fwi_j, F32, "fox_in_dx_" + tag)
            dwf = _mm_tn(hb, dproj, BF16, "fox_in_dw_" + tag)
            rs_start(f"fwi{j}", _col_blocks(dwf[:, :NDEV * fox_w_in.shape[2]], fox_w_in.shape[2]))
        elif l % 3 == 1:
            dy3 = _to3(_mm_nt(dy, swo, F32, "sconv_out_dx_" + tag))
            rs_start("swo", _mm_tn(sv["y2"], dy, BF16, "sconv_out_dw_" + tag).reshape(NDEV, dsh, D))
            dbg, dcg, dxv, dscw = _sconv_bwd(dy3, sv["bg"], sv["cg"], sv["xv"], scw3, "sconv_mix_bwd_" + tag)
            small_g["sconv_conv_w"] = dscw.reshape(3, D)
            dproj = jnp.concatenate([t.reshape(s, D) for t in (dbg, dcg, dxv)], axis=1).astype(BF16)
            dh = _mm_nt(dproj, swi, F32, "sconv_in_dx_" + tag)
            rs_start("swi", _col_blocks(_mm_tn(hb, dproj, BF16, "sconv_in_dw_" + tag), sconv_w_in.shape[2]))
        else:
            dy3 = _to3(_mm_nt(dy, lwo, F32, "lru_out_dx_" + tag))
            rs_start("lwo", _mm_tn(sv["y2"], dy, BF16, "lru_out_dw_" + tag).reshape(NDEV, dsh, D))
            dgate3, dra3, dia3, dxb3, lred = _lru_scan_bwd(
                dy3, sv["ra3"], sv["ia3"], sv["xb3"], sv["gate3"], sv["hs3"], sv["hp3"], lpv, "lru_scan_bwd_" + tag)
            dgp = jnp.concatenate([dra3.reshape(s, D), dia3.reshape(s, D)], axis=1).astype(BF16)
            dxb3 = dxb3 + _to3(_mm_nt(dgp, wgate, F32, "lru_gate_dx_" + tag))
            dwgate = _mm_tn(sv["xb2"], dgp, F32, "lru_gate_dw_" + tag)
            dxp3, cred = _lru_conv_bwd(dxb3, sv["xp3"], lcwb, "lru_conv_bwd_" + tag)
            lred, cred = lred.reshape(8, D), cred.reshape(8, D)
            small_g.update(lru_w_a=_diag_blocks(dwgate[:, :D]), lru_w_x=_diag_blocks(dwgate[:, D:]),
                           lru_b_a=lred[0], lru_b_x=lred[1], lru_lambda=lred[2],
                           lru_conv_w=cred[:4], lru_conv_b=cred[4])
            dproj = jnp.concatenate([dgate3.reshape(s, D), dxp3.reshape(s, D)], axis=1).astype(BF16)
            dh = _mm_nt(dproj, lwi, F32, "lru_in_dx_" + tag)
            rs_start("lwi", _col_blocks(_mm_tn(hb, dproj, BF16, "lru_in_dw_" + tag), lru_w_in.shape[2]))
        dx, red = _pre_bwd(dx, dh, sv["x"], ptab, l, sub, "pre_bwd_" + tag)
        d_mod[l][sub][0] = red[0]
        d_mod[l][sub][1] = red[1]
        d_npre[l][sub] = red[2]
    grad_x = dx[None]

    dmod_mine = jnp.stack([jnp.stack([jnp.stack(d_mod[l][sub]) for sub in range(3)]) for l in range(DEPTH)])
    gparts = [loss_cols[0], dmod_mine.reshape(-1),
              jnp.stack([jnp.stack(r_) for r_ in d_npre]).reshape(-1),
              jnp.stack([jnp.stack(r_) for r_ in d_npost]).reshape(-1),
              jnp.stack(d_fbf).reshape(-1), small_g["sconv_conv_w"].reshape(-1),
              small_g["lru_conv_w"].reshape(-1), small_g["lru_conv_b"].reshape(-1),
              small_g["lru_w_a"].reshape(-1), small_g["lru_b_a"].reshape(-1),
              small_g["lru_w_x"].reshape(-1), small_g["lru_b_x"].reshape(-1),
              small_g["lru_lambda"].reshape(-1)]
    gsizes = [p.shape[0] for p in gparts]
    gflat = jnp.concatenate(gparts)
    gflat = jnp.pad(gflat, (0, (-gflat.shape[0]) % (8 * 1024))).reshape(-1, 1024)
    small_hnd = _exchange_start(gflat, "ags_smallgrads", True)
    rs_start("wi00", dwi())

    out = {}
    after = [small_hnd[4]]

    def update(keys, w, m, v, name):
        got = {}
        for k_ in sorted(keys, key=rs_order.index):
            got[k_] = after[0] = _exchange_wait(rs_pend.pop(k_), after[0], "rsw_" + k_, False)
        src = jnp.stack([got[k_] for k_ in keys])
        shp = w.shape
        w3, m3, v3 = (t.reshape((src.shape[0],) + src.shape[2:]) for t in (w, m, v))
        res = tuple(t.reshape(shp) for t in _adamw(src, w3, m3, v3, "adamw_" + name))
        after[0] = res[1]
        return res

    lf = [f"{l}{f}" for l in range(DEPTH) for f in range(2)]
    out["fox_w_in"] = update(["fwi0", "fwi1"], fox_w_in, m_fox_w_in, v_fox_w_in, "fwi")
    out["fox_w_out"] = update(["fwo0", "fwo1"], fox_w_out, m_fox_w_out, v_fox_w_out, "fwo")
    out["lru_w_out"] = update(["lwo"], lru_w_out, m_lru_w_out, v_lru_w_out, "lwo")
    out["lru_w_in"] = update(["lwi"], lru_w_in, m_lru_w_in, v_lru_w_in, "lwi")
    out["sconv_w_out"] = update(["swo"], sconv_w_out, m_sconv_w_out, v_sconv_w_out, "swo")
    out["sconv_w_in"] = update(["swi"], sconv_w_in, m_sconv_w_in, v_sconv_w_in, "swi")

    stacked = {"wi": [t.reshape((-1,) + w_ffn_in.shape[2:]) for t in (w_ffn_in, m_w_ffn_in, v_w_ffn_in)],
               "wo": [t.reshape((-1,) + w_ffn_out.shape[2:]) for t in (w_ffn_out, m_w_ffn_out, v_w_ffn_out)]}
    bufs = {"wi": None, "wo": None}

    def update_slice(k_):
        kind = k_[:2]
        src = after[0] = _exchange_wait(rs_pend.pop(k_), after[0], "rsw_" + k_, False)
        bufs[kind] = _adamw_slice(src, *stacked[kind], bufs[kind], lf.index(k_[2:]), "adamw_" + k_)
        after[0] = bufs[kind][1]

    for k_ in rs_order:
        if k_[:2] in ("wi", "wo") and k_ != "wi00":
            update_slice(k_)

    gall = _exchange_wait(small_hnd[:4], after[0], "agw_smallgrads", True)
    gsum = _sum_devices(gall, "sum_smallgrads").reshape(-1)
    goffs = [0]
    for n_ in gsizes:
        goffs.append(goffs[-1] + n_)
    gpiece = lambda i: gsum[goffs[i]:goffs[i + 1]]
    loss = jnp.sum(gpiece(0))
    my_ch = lambda g, lead: lax.dynamic_slice_in_dim(g.reshape(lead + (D,)), me * dsh, dsh, axis=len(lead))

    dmod_all = gall.reshape(NDEV, -1)[:, goffs[1]:goffs[2]].reshape(NDEV, DEPTH, 3 * 3 * D)
    dmod_cols = lax.dynamic_slice_in_dim(dmod_all, me * wcw, wcw, axis=2).transpose(1, 0, 2)
    dmod_cols = jnp.pad(dmod_cols, ((0, 0), (0, 128 - NDEV), (0, 0))).astype(BF16)
    g_wcond = _cond_bwd(ca_pad, dmod_cols, "cond_bwd")

    out["w_cond"] = _adamw(g_wcond[:, None], w_cond, m_w_cond, v_w_cond, "adamw_wcond")

    small = [
        ("b_cond", b_cond, m_b_cond, v_b_cond, gpiece(1)),
        ("norm_pre", norm_pre, m_norm_pre, v_norm_pre, my_ch(gpiece(2), (DEPTH, 3))),
        ("norm_post", norm_post, m_norm_post, v_norm_post, my_ch(gpiece(3), (DEPTH, 3))),
        ("fox_b_f", fox_b_f, m_fox_b_f, v_fox_b_f, gpiece(4)),
        ("sconv_conv_w", sconv_conv_w, m_sconv_conv_w, v_sconv_conv_w, my_ch(gpiece(5), (1, 3))),
        ("lru_conv_w", lru_conv_w, m_lru_conv_w, v_lru_conv_w, my_ch(gpiece(6), (1, 4))),
        ("lru_conv_b", lru_conv_b, m_lru_conv_b, v_lru_conv_b, my_ch(gpiece(7), (1,))),
        ("lru_w_a", lru_w_a, m_lru_w_a, v_lru_w_a, gpiece(8)),
        ("lru_b_a", lru_b_a, m_lru_b_a, v_lru_b_a, gpiece(9)),
        ("lru_w_x", lru_w_x, m_lru_w_x, v_lru_w_x, gpiece(10)),
        ("lru_b_x", lru_b_x, m_lru_b_x, v_lru_b_x, gpiece(11)),
        ("lru_lambda", lru_lambda, m_lru_lambda, v_lru_lambda, my_ch(gpiece(12), (1,))),
    ]
    pack = lambda ts: jnp.concatenate([t.reshape(-1) for t in ts])
    ssz = [w_.size for _, w_, _, _, _ in small]
    tot = sum(ssz)
    padr = (-tot) % (16 * 1024)
    pk = lambda ts: jnp.pad(pack(ts), (0, padr)).reshape(1, -1, 1024)
    sg, sd, smm, svv = _adamw(pk([t[4] for t in small])[:, None], pk([t[1] for t in small]),
                              pk([t[2] for t in small]), pk([t[3] for t in small]), "adamw_small")
    soff = 0
    for (name, w_, _, _, _), n_ in zip(small, ssz):
        out[name] = tuple(t.reshape(-1)[soff:soff + n_].reshape(w_.shape) for t in (sg, sd, smm, svv))
        soff += n_

    after[0] = sd
    update_slice("wi00")
    out["w_ffn_in"] = tuple(t.reshape(w_ffn_in.shape) for t in bufs["wi"])
    out["w_ffn_out"] = tuple(t.reshape(w_ffn_out.shape) for t in bufs["wo"])

    names =["w_cond", "b_cond", "norm_pre", "norm_post", "w_ffn_in", "w_ffn_out", "fox_w_in", "fox_b_f",
             "fox_w_out", "sconv_w_in", "sconv_conv_w", "sconv_w_out", "lru_w_in", "lru_conv_w", "lru_conv_b",
             "lru_w_a", "lru_b_a", "lru_w_x", "lru_b_x", "lru_lambda", "lru_w_out"]
    return (loss, grad_x, *[out[n_][0] for n_ in names], *[out[n_][1] for n_ in names],
            *[out[n_][2] for n_ in names], *[out[n_][3] for n_ in names])
```

```python
import functools
import math

import jax
import jax.numpy as jnp
from jax import lax
from jax.experimental import pallas as pl
from jax.experimental.pallas import tpu as pltpu

F32 = jnp.float32
BF16 = jnp.bfloat16
NDEV = 8
D = 1024
DFF = 2816
FFB = 704
HEADS = 16
HDIM = 64
DEPTH = 4
RMS_EPS = 1e-6
LRU_C = 8.0
ADAM_LR, ADAM_B1, ADAM_B2, ADAM_EPS, ADAM_WD, ADAM_STEP = 0.001, 0.9, 0.999, 1e-08, 0.01, 10
FOXP = 3200
MESH = pl.DeviceIdType.MESH
HBM = pl.BlockSpec(memory_space=pltpu.HBM)
VMEM_BIG = 48 * 1024 * 1024

_NN = (((1,), (0,)), ((), ()))
_NT = (((1,), (1,)), ((), ()))
_TN = (((0,), (0,)), ((), ()))
_DIMS = {"nn": _NN, "nt": _NT, "tn": _TN}


def _cparams(sem=None, vmem=None):
    kw = {}
    if sem is not None:
        kw["dimension_semantics"] = sem
    if vmem is not None:
        kw["vmem_limit_bytes"] = vmem
    return pltpu.CompilerParams(**kw)


def _sigmoid(x):
    return 1.0 / (1.0 + jnp.exp(-x))


def _softplus(x):
    return jnp.maximum(x, 0.0) + jnp.log(1.0 + jnp.exp(-jnp.abs(x)))


_GELU_C = math.sqrt(2.0 / math.pi)


def _gelu_parts(x):
    u = _GELU_C * (x + 0.044715 * x * x * x)
    t = jnp.tanh(u)
    g = 0.5 * x * (1.0 + t)
    dg = 0.5 * (1.0 + t) + 0.5 * x * (1.0 - t * t) * _GELU_C * (1.0 + 3.0 * 0.044715 * x * x)
    return g, dg


def _mesh_pos():
    ax, ay, ac = lax.axis_index("x"), lax.axis_index("y"), lax.axis_index("c")
    return ax, ay, ac, 4 * ax + 2 * ay + ac


def _peer(ax, ay, ac, d):
    px = 1 - ax if (d >> 2) & 1 else ax
    py = 1 - ay if (d >> 1) & 1 else ay
    pc = 1 - ac if d & 1 else ac
    return (px, py, pc), 4 * px + 2 * py + pc


def _exchange(x, axis, name, gather):
    if gather:
        x = jnp.expand_dims(x, axis)
        oshape = x.shape[:axis] + (NDEV,) + x.shape[axis + 1:]
    else:
        oshape = x.shape
    lead = (slice(None),) * axis

    def blk(ref, k):
        return ref.at[lead + (pl.ds(k, 1),)]

    def body(x_ref, o_ref, send_sems, recv_sems, local_sem):
        ax, ay, ac, me = _mesh_pos()
        src = (lambda k: x_ref) if gather else (lambda k: blk(x_ref, k))
        mine = pltpu.make_async_copy(src(me), blk(o_ref, me), local_sem)
        mine.start()
        sends = []
        for d in range(1, NDEV):
            peer, pidx = _peer(ax, ay, ac, d)
            cp = pltpu.make_async_remote_copy(
                src_ref=src(pidx), dst_ref=blk(o_ref, me), send_sem=send_sems.at[d - 1],
                recv_sem=recv_sems.at[d - 1], device_id=peer, device_id_type=MESH)
            cp.start()
            sends.append(cp)
        for d in range(1, NDEV):
            peer, pidx = _peer(ax, ay, ac, d)
            pltpu.make_async_remote_copy(
                src_ref=src(pidx), dst_ref=blk(o_ref, pidx), send_sem=send_sems.at[d - 1],
                recv_sem=recv_sems.at[d - 1], device_id=peer, device_id_type=MESH).wait_recv()
        for cp in sends:
            cp.wait_send()
        mine.wait()

    return pl.pallas_call(
        body, name=name, out_shape=jax.ShapeDtypeStruct(oshape, x.dtype),
        in_specs=[HBM], out_specs=HBM,
        scratch_shapes=[pltpu.SemaphoreType.DMA((NDEV - 1,)), pltpu.SemaphoreType.DMA((NDEV - 1,)),
                        pltpu.SemaphoreType.DMA(())],
    )(x)


def _all_gather(x, axis, name):
    return _exchange(x, axis, name, True)


SEM = pl.BlockSpec(memory_space=pltpu.SEMAPHORE)
EFFECT = pltpu.SideEffectType.DATAFLOW_SIDE_EFFECTING


def _exchange_start(x, name, gather):
    me = 4 * lax.axis_index("x") + 2 * lax.axis_index("y") + lax.axis_index("c")
    if gather:
        x = x[None]
        oshape = (NDEV,) + x.shape[1:]
        own = x
    else:
        oshape = x.shape
        own = lax.dynamic_index_in_dim(x, me, 0, keepdims=True)
    land = lax.dynamic_update_index_in_dim(lax.empty(oshape, x.dtype), own, me, 0)

    def blk(ref, k):
        return ref.at[pl.ds(k, 1)]

    def body(x_ref, land_ref, send_sems, recv_sems, x_thru, land_thru, token):
        ax, ay, ac, me = _mesh_pos()
        src = (lambda k: x_ref) if gather else (lambda k: blk(x_ref, k))
        for d in range(1, NDEV):
            peer, pidx = _peer(ax, ay, ac, d)
            pltpu.make_async_remote_copy(
                src_ref=src(pidx), dst_ref=blk(land_ref, me), send_sem=send_sems.at[d - 1],
                recv_sem=recv_sems.at[d - 1], device_id=peer, device_id_type=MESH).start()
        token[...] = jnp.zeros_like(token)

    return pl.pallas_call(
        body, name=name,
        out_shape=(pltpu.SemaphoreType.DMA((NDEV - 1,)), pltpu.SemaphoreType.DMA((NDEV - 1,)),
                   pltpu.HBM(x.shape, x.dtype), pltpu.HBM(oshape, x.dtype),
                   jax.ShapeDtypeStruct((8, 128), F32)),
        in_specs=(HBM, HBM), out_specs=(SEM, SEM, HBM, HBM, pl.BlockSpec(memory_space=pltpu.VMEM)),
        input_output_aliases={0: 2, 1: 3},
        compiler_params=pltpu.CompilerParams(has_side_effects=EFFECT),
    )(pltpu.with_memory_space_constraint(x, pltpu.HBM),
      pltpu.with_memory_space_constraint(land, pltpu.HBM))


def _exchange_wait(handle, after, name, gather):
    send_sems, recv_sems, x_thru, land_thru = handle

    def blk(ref, k):
        return ref.at[pl.ds(k, 1)]

    def body(x_ref, land_ref, send_sems, recv_sems, after_ref, x_dead, got_ref):
        ax, ay, ac, me = _mesh_pos()
        src = (lambda k: x_ref) if gather else (lambda k: blk(x_ref, k))
        for d in range(1, NDEV):
            peer, pidx = _peer(ax, ay, ac, d)
            cp = pltpu.make_async_remote_copy(
                src_ref=src(pidx), dst_ref=blk(land_ref, pidx), send_sem=send_sems.at[d - 1],
                recv_sem=recv_sems.at[d - 1], device_id=peer, device_id_type=MESH)
            cp.wait_send()
            cp.wait_recv()

    return pl.pallas_call(
        body, name=name,
        out_shape=(pltpu.HBM(x_thru.shape, x_thru.dtype), pltpu.HBM(land_thru.shape, land_thru.dtype)),
        in_specs=(HBM, HBM, SEM, SEM, pl.BlockSpec(memory_space=pl.ANY)), out_specs=(HBM, HBM),
        input_output_aliases={0: 0, 1: 1},
        compiler_params=pltpu.CompilerParams(has_side_effects=EFFECT),
    )(x_thru, land_thru, send_sems, recv_sems, after)[1]


def _mm(a, b, *, mode, grid, a_spec, b_spec, o_spec, o_shape, acc_shape, out_dtype, name):
    nred = grid[2]

    def body(a_ref, b_ref, o_ref, *scratch):
        p = lax.dot_general(a_ref[...], b_ref[...], _DIMS[mode], preferred_element_type=F32)
        if nred == 1:
            o_ref[...] = p.astype(o_ref.dtype)
            return
        acc = scratch[0]
        r = pl.program_id(2)

        @pl.when(r == 0)
        def _():
            acc[...] = p

        @pl.when(r > 0)
        def _():
            acc[...] += p

        @pl.when(r == nred - 1)
        def _():
            o_ref[...] = acc[...].astype(o_ref.dtype)

    return pl.pallas_call(
        body, name=name, out_shape=jax.ShapeDtypeStruct(o_shape, out_dtype), grid=grid,
        in_specs=[a_spec, b_spec], out_specs=o_spec,
        scratch_shapes=[] if nred == 1 else [pltpu.VMEM(acc_shape, F32)],
        compiler_params=_cparams(("parallel", "parallel", "arbitrary"), VMEM_BIG),
    )(a, b)


def _tile(n, cands):
    for c in cands:
        if n % c == 0:
            return c
    return n


def _mm_nn(a, b, out_dtype, name):
    m, k = a.shape
    n = b.shape[1]
    tm, tn = _tile(m, (512, 256, 128)), _tile(n, (1024, 640, 512, 256, 128))
    return _mm(a, b, mode="nn", grid=(m // tm, n // tn, 1),
               a_spec=pl.BlockSpec((tm, k), lambda i, j, r: (i, 0)),
               b_spec=pl.BlockSpec((k, tn), lambda i, j, r: (0, j)),
               o_spec=pl.BlockSpec((tm, tn), lambda i, j, r: (i, j)),
               o_shape=(m, n), acc_shape=None, out_dtype=out_dtype, name=name)


def _mm_nt(a, b, out_dtype, name):
    m, n = a.shape
    k = b.shape[0]
    tm, tn = _tile(m, (512, 256, 128)), _tile(n, (1024, 640, 512, 256, 128))
    return _mm(a, b, mode="nt", grid=(m // tm, 1, n // tn),
               a_spec=pl.BlockSpec((tm, tn), lambda i, j, r: (i, r)),
               b_spec=pl.BlockSpec((k, tn), lambda i, j, r: (0, r)),
               o_spec=pl.BlockSpec((tm, k), lambda i, j, r: (i, 0)),
               o_shape=(m, k), acc_shape=(tm, k), out_dtype=out_dtype, name=name)


def _mm_tn(a, b, out_dtype, name):
    s, k = a.shape
    n = b.shape[1]
    ts, tn = _tile(s, (512, 256, 128)), _tile(n, (640, 512, 256, 128))
    return _mm(a, b, mode="tn", grid=(1, n // tn, s // ts),
               a_spec=pl.BlockSpec((ts, k), lambda i, j, r: (r, 0)),
               b_spec=pl.BlockSpec((ts, tn), lambda i, j, r: (r, j)),
               o_spec=pl.BlockSpec((k, tn), lambda i, j, r: (0, j)),
               o_shape=(k, n), acc_shape=(k, tn), out_dtype=out_dtype, name=name)


def _gu_spec(tm, idx):
    return pl.BlockSpec((2, None, tm, FFB), idx)


def _ffn_up(h, wi, name):
    s = h.shape[0]
    tm = _tile(s, (1024, 512, 256, 128))

    def body(h_ref, wg_ref, wu_ref, gu_ref, act_ref):
        hv = h_ref[...]
        g = jnp.dot(hv, wg_ref[...], preferred_element_type=F32)
        u = jnp.dot(hv, wu_ref[...], preferred_element_type=F32)
        gu_ref[0] = g.astype(BF16)
        gu_ref[1] = u.astype(BF16)
        act_ref[...] = (g * _sigmoid(g) * u).astype(BF16)

    wspec = lambda off: pl.BlockSpec((None, D, FFB), lambda j, i: (j + off, 0, 0))
    return pl.pallas_call(
        body, name=name,
        out_shape=(jax.ShapeDtypeStruct((2, 4, s, FFB), BF16), jax.ShapeDtypeStruct((4, s, FFB), BF16)),
        grid=(4, s // tm),
        in_specs=[pl.BlockSpec((tm, D), lambda j, i: (i, 0)), wspec(0), wspec(4)],
        out_specs=(_gu_spec(tm, lambda j, i: (0, j, i, 0)), pl.BlockSpec((None, tm, FFB), lambda j, i: (j, i, 0))),
        compiler_params=_cparams(("parallel", "parallel"), VMEM_BIG),
    )(h, wi, wi)


def _ffn_down(act, wo, name):
    s = act.shape[1]
    tm = _tile(s, (1024, 512, 256, 128))
    return _mm(act, wo, mode="nn", grid=(s // tm, 1, 4),
               a_spec=pl.BlockSpec((None, tm, FFB), lambda i, j, r: (r, i, 0)),
               b_spec=pl.BlockSpec((None, FFB, D), lambda i, j, r: (r, 0, 0)),
               o_spec=pl.BlockSpec((tm, D), lambda i, j, r: (i, 0)),
               o_shape=(s, D), acc_shape=(tm, D), out_dtype=F32, name=name)


def _ffn_down_bwd(dy, gu, wo, name):
    s = dy.shape[0]
    tm = _tile(s, (512, 256, 128))
    ns = s // tm

    def body(dy_ref, gu_ref, wo_ref, da_ref, dwo_ref, acc):
        i = pl.program_id(1)
        dyv = dy_ref[...]
        dact = lax.dot_general(dyv, wo_ref[...], _NT, preferred_element_type=F32)
        gv = gu_ref[0].astype(F32)
        uv = gu_ref[1].astype(F32)
        sg = _sigmoid(gv)
        silu = gv * sg
        da_ref[0] = (dact * uv * (sg * (1.0 + gv * (1.0 - sg)))).astype(BF16)
        da_ref[1] = (dact * silu).astype(BF16)
        p = lax.dot_general((silu * uv).astype(BF16), dyv, _TN, preferred_element_type=F32)

        @pl.when(i == 0)
        def _():
            acc[...] = p

        @pl.when(i > 0)
        def _():
            acc[...] += p

        @pl.when(i == ns - 1)
        def _():
            dwo_ref[...] = acc[...].astype(BF16)

    aspec = _gu_spec(tm, lambda j, i: (0, j, i, 0))
    return pl.pallas_call(
        body, name=name,
        out_shape=(jax.ShapeDtypeStruct((2, 4, s, FFB), BF16), jax.ShapeDtypeStruct((4, FFB, D), BF16)),
        grid=(4, ns),
        in_specs=[pl.BlockSpec((tm, D), lambda j, i: (i, 0)), aspec,
                  pl.BlockSpec((None, FFB, D), lambda j, i: (j, 0, 0))],
        out_specs=(aspec, pl.BlockSpec((None, FFB, D), lambda j, i: (j, 0, 0))),
        scratch_shapes=[pltpu.VMEM((FFB, D), F32)],
        compiler_params=_cparams(("parallel", "arbitrary"), VMEM_BIG),
    )(dy, gu, wo)


def _ffn_dh(da, wi, name):
    s = da.shape[2]
    tm = _tile(s, (1024, 512, 256, 128))

    def body(da_ref, wg_ref, wu_ref, o_ref, acc):
        r = pl.program_id(1)
        p = (lax.dot_general(da_ref[0], wg_ref[...], _NT, preferred_element_type=F32)
             + lax.dot_general(da_ref[1], wu_ref[...], _NT, preferred_element_type=F32))

        @pl.when(r == 0)
        def _():
            acc[...] = p

        @pl.when(r > 0)
        def _():
            acc[...] += p

        @pl.when(r == 3)
        def _():
            o_ref[...] = acc[...]

    wspec = lambda off: pl.BlockSpec((None, D, FFB), lambda i, r: (r + off, 0, 0))
    return pl.pallas_call(
        body, name=name, out_shape=jax.ShapeDtypeStruct((s, D), F32), grid=(s // tm, 4),
        in_specs=[_gu_spec(tm, lambda i, r: (0, r, i, 0)), wspec(0), wspec(4)],
        out_specs=pl.BlockSpec((tm, D), lambda i, r: (i, 0)),
        scratch_shapes=[pltpu.VMEM((tm, D), F32)],
        compiler_params=_cparams(("parallel", "arbitrary"), VMEM_BIG),
    )(da, wi, wi)


def _ffn_dwi(h, da, name):
    s = h.shape[0]
    ts = _tile(s, (512, 256, 128))
    return _mm(h, da, mode="tn", grid=(NDEV, 1, s // ts),
               a_spec=pl.BlockSpec((ts, D), lambda k, j, r: (r, 0)),
               b_spec=pl.BlockSpec((None, ts, FFB), lambda k, j, r: (k, r, 0)),
               o_spec=pl.BlockSpec((None, D, FFB), lambda k, j, r: (k, 0, 0)),
               o_shape=(NDEV, D, FFB), acc_shape=(D, FFB), out_dtype=BF16, name=name)


TR = 256


def _rows_spec(s):
    tr = _tile(s, (TR, 128))
    return tr, pl.BlockSpec((tr, D), lambda i: (i, 0))


def _pspec(l, sub):
    return pl.BlockSpec((None, None, 8, D), lambda i: (l, sub, 0, 0))


def _pre_math(x, p_ref):
    r = lax.rsqrt(jnp.mean(x * x, axis=1, keepdims=True) + RMS_EPS)
    return (x * r) * p_ref[0:1, :] * (1.0 + p_ref[1:2, :]) + p_ref[2:3, :]


ANYSPEC = pl.BlockSpec(memory_space=pl.ANY)


def _prenorm(x, ptab, l, sub, name, toks=()):
    s = x.shape[0]
    tr, spec = _rows_spec(s)

    def body(x_ref, p_ref, *rest):
        rest[-1][...] = _pre_math(x_ref[...], p_ref).astype(BF16)

    return pl.pallas_call(
        body, name=name, out_shape=jax.ShapeDtypeStruct((s, D), BF16), grid=(s // tr,),
        in_specs=[spec, _pspec(l, sub)] + [ANYSPEC] * len(toks), out_specs=spec,
        compiler_params=_cparams(("parallel",)),
    )(x, ptab, *toks)


def _resid(x, y, ptab, l, sub, coef, nxt, name):
    s = x.shape[0]
    tr, spec = _rows_spec(s)

    def body(x_ref, y_ref, p_ref, *rest):
        yv = y_ref[...]
        r = lax.rsqrt(jnp.mean(yv * yv, axis=1, keepdims=True) + RMS_EPS)
        xn = x_ref[...] + (coef * p_ref[4:5, :]) * ((yv * r) * p_ref[3:4, :])
        if nxt is None:
            rest[0][...] = xn
        else:
            rest[1][...] = xn
            rest[2][...] = _pre_math(xn, rest[0]).astype(BF16)

    if nxt is None:
        return pl.pallas_call(
            body, name=name, out_shape=jax.ShapeDtypeStruct((s, D), F32), grid=(s // tr,),
            in_specs=[spec, spec, _pspec(l, sub)], out_specs=spec,
            compiler_params=_cparams(("parallel",)),
        )(x, y, ptab), None
    return pl.pallas_call(
        body, name=name,
        out_shape=(jax.ShapeDtypeStruct((s, D), F32), jax.ShapeDtypeStruct((s, D), BF16)),
        grid=(s // tr,),
        in_specs=[spec, spec, _pspec(l, sub), _pspec(*nxt)], out_specs=(spec, spec),
        compiler_params=_cparams(("parallel",)),
    )(x, y, ptab, ptab)


def _loss_grad(x, tgt, name):
    s = x.shape[0]
    tr, spec = _rows_spec(s)

    def body(x_ref, t_ref, dx_ref, l_ref):
        @pl.when(pl.program_id(0) == 0)
        def _():
            l_ref[...] = jnp.zeros_like(l_ref)

        e = x_ref[...] - t_ref[...]
        dx_ref[...] = e * (1.0 / D)
        l_ref[0:1, :] += jnp.sum(e * e, axis=0, keepdims=True) * (0.5 / D)

    return pl.pallas_call(
        body, name=name,
        out_shape=(jax.ShapeDtypeStruct((s, D), F32), jax.ShapeDtypeStruct((8, D), F32)),
        grid=(s // tr,), in_specs=[spec, spec],
        out_specs=(spec, pl.BlockSpec((8, D), lambda i: (0, 0))),
        compiler_params=_cparams(("arbitrary",)),
    )(x, tgt)


def _post_bwd(dx, y, ptab, l, sub, coef, name, toks=()):
    s = dx.shape[0]
    tr, spec = _rows_spec(s)

    def body(dx_ref, y_ref, p_ref, *rest):
        dy_ref, red_ref = rest[-2:]

        @pl.when(pl.program_id(0) == 0)
        def _():
            red_ref[...] = jnp.zeros_like(red_ref)

        dxv, yv = dx_ref[...], y_ref[...]
        gpost, gate = p_ref[3:4, :], p_ref[4:5, :]
        r = lax.rsqrt(jnp.mean(yv * yv, axis=1, keepdims=True) + RMS_EPS)
        yhat = yv * r
        red_ref[0:1, :] += jnp.sum(dxv * yhat * gpost, axis=0, keepdims=True) * coef
        dn = dxv * (coef * gate)
        red_ref[1:2, :] += jnp.sum(dn * yhat, axis=0, keepdims=True)
        dyh = dn * gpost
        dy_ref[...] = (r * (dyh - yhat * jnp.mean(dyh * yhat, axis=1, keepdims=True))).astype(BF16)

    return pl.pallas_call(
        body, name=name,
        out_shape=(jax.ShapeDtypeStruct((s, D), BF16), jax.ShapeDtypeStruct((8, D), F32)),
        grid=(s // tr,), in_specs=[spec, spec, _pspec(l, sub)] + [ANYSPEC] * len(toks),
        out_specs=(spec, pl.BlockSpec((8, D), lambda i: (0, 0))),
        compiler_params=_cparams(("arbitrary",)),
    )(dx, y, ptab, *toks)


def _pre_bwd(dx, dh, x, ptab, l, sub, name, dh2=None):
    s = dx.shape[0]
    tr, spec = _rows_spec(s)
    extra = [] if dh2 is None else [dh2]

    def body(dx_ref, dh_ref, x_ref, p_ref, *rest):
        o_ref, red_ref = rest[-2:]

        @pl.when(pl.program_id(0) == 0)
        def _():
            red_ref[...] = jnp.zeros_like(red_ref)

        dhv, xv = dh_ref[...], x_ref[...]
        if extra:
            dhv = dhv + rest[0][...]
        gpre, scale = p_ref[0:1, :], p_ref[1:2, :]
        r = lax.rsqrt(jnp.mean(xv * xv, axis=1, keepdims=True) + RMS_EPS)
        xhat = xv * r
        red_ref[0:1, :] += jnp.sum(dhv, axis=0, keepdims=True)
        red_ref[1:2, :] += jnp.sum(dhv * xhat * gpre, axis=0, keepdims=True)
        red_ref[2:3, :] += jnp.sum(dhv * xhat * (1.0 + scale), axis=0, keepdims=True)
        dxh = dhv * (gpre * (1.0 + scale))
        o_ref[...] = dx_ref[...] + r * (dxh - xhat * jnp.mean(dxh * xhat, axis=1, keepdims=True))

    return pl.pallas_call(
        body, name=name,
        out_shape=(jax.ShapeDtypeStruct((s, D), F32), jax.ShapeDtypeStruct((8, D), F32)),
        grid=(s // tr,), in_specs=[spec, spec, spec, _pspec(l, sub)] + [spec] * len(extra),
        out_specs=(spec, pl.BlockSpec((8, D), lambda i: (0, 0))),
        compiler_params=_cparams(("arbitrary",)),
    )(dx, dh, x, ptab, *extra)


def _cond_fwd(c_pad, wc, bc, name):
    w = wc.shape[2]

    def body(c_ref, w_ref, b_ref, o_ref, ca_ref):
        cv = c_ref[...]
        ca = (cv * _sigmoid(cv)).astype(BF16)
        ca_ref[...] = ca
        o_ref[...] = jnp.dot(ca, w_ref[...].astype(BF16), preferred_element_type=F32) + b_ref[...]

    return pl.pallas_call(
        body, name=name,
        out_shape=(jax.ShapeDtypeStruct((DEPTH, 128, w), F32), jax.ShapeDtypeStruct((128, D), BF16)),
        grid=(DEPTH,),
        in_specs=[pl.BlockSpec((128, D), lambda i: (0, 0)),
                  pl.BlockSpec((None, D, w), lambda i: (i, 0, 0)),
                  pl.BlockSpec((None, 1, w), lambda i: (i, 0, 0))],
        out_specs=(pl.BlockSpec((None, 128, w), lambda i: (i, 0, 0)),
                   pl.BlockSpec((128, D), lambda i: (0, 0))),
        compiler_params=_cparams(("arbitrary",), VMEM_BIG),
    )(c_pad, wc, bc)


def _cond_bwd(ca_pad, dmod, name):
    w = dmod.shape[2]
    return _mm(ca_pad, dmod, mode="tn", grid=(DEPTH, 1, 1),
               a_spec=pl.BlockSpec((128, D), lambda i, j, r: (0, 0)),
               b_spec=pl.BlockSpec((None, 128, w), lambda i, j, r: (i, 0, 0)),
               o_spec=pl.BlockSpec((None, D, w), lambda i, j, r: (i, 0, 0)),
               o_shape=(DEPTH, D, w), acc_shape=None, out_dtype=F32, name=name)


def _split3(x):
    hi = x.astype(BF16)
    r1 = x - hi.astype(F32)
    mid = r1.astype(BF16)
    lo = (r1 - mid.astype(F32)).astype(BF16)
    return hi, mid, lo


def _tri_dot(t, x):
    hi, mid, lo = _split3(x)
    return (jnp.dot(t, hi, preferred_element_type=F32) + jnp.dot(t, mid, preferred_element_type=F32)
            + jnp.dot(t, lo, preferred_element_type=F32))


def _fox_cum(fl, bf, name):
    s = fl.shape[0]
    tb = _tile(s, (256, 128))

    def body(fl_ref, b_ref, cum_ref):
        row = lax.broadcasted_iota(jnp.int32, (tb, tb), 0)
        col = lax.broadcasted_iota(jnp.int32, (tb, tb), 1)
        tri = (col <= row).astype(BF16)
        carry = jnp.zeros((1, 128), F32)
        for blk in range(s // tb):
            z = fl_ref[blk * tb:(blk + 1) * tb, :] + b_ref[0:1, :]
            lf = jnp.minimum(z, 0.0) - jnp.log(1.0 + jnp.exp(-jnp.abs(z)))
            cum_ref[blk * tb:(blk + 1) * tb, :] = _tri_dot(tri, lf) + carry
            carry = carry + jnp.sum(lf, axis=0, keepdims=True)

    return pl.pallas_call(
        body, name=name, out_shape=jax.ShapeDtypeStruct((s, 128), F32),
    )(fl, bf)


def _fox_cum_bwd(dcum, fl, bf, name):
    s = fl.shape[0]
    tb = _tile(s, (256, 128))

    def body(dc_ref, fl_ref, b_ref, dfl_ref, db_ref):
        row = lax.broadcasted_iota(jnp.int32, (tb, tb), 0)
        col = lax.broadcasted_iota(jnp.int32, (tb, tb), 1)
        tri = (col >= row).astype(BF16)
        carry = jnp.zeros((1, 128), F32)
        dbs = jnp.zeros((1, 128), F32)
        for blk in reversed(range(s // tb)):
            dc = dc_ref[blk * tb:(blk + 1) * tb, :]
            dl = _tri_dot(tri, dc) + carry
            carry = carry + jnp.sum(dc, axis=0, keepdims=True)
            z = fl_ref[blk * tb:(blk + 1) * tb, :] + b_ref[0:1, :]
            dz = dl * _sigmoid(-z)
            dfl_ref[blk * tb:(blk + 1) * tb, :] = dz
            dbs = dbs + jnp.sum(dz, axis=0, keepdims=True)
        db_ref[...] = jnp.broadcast_to(dbs, (8, 128))

    return pl.pallas_call(
        body, name=name,
        out_shape=(jax.ShapeDtypeStruct((s, 128), F32), jax.ShapeDtypeStruct((8, 128), F32)),
    )(dcum, fl, bf)


def _fox_qkv(h, w_qkv, name):
    s = h.shape[0]
    tm = _tile(s, (512, 256, 128))
    return _mm(h, w_qkv, mode="nn", grid=(s // tm, 3, 1),
               a_spec=pl.BlockSpec((tm, D), lambda i, j, r: (i, 0)),
               b_spec=pl.BlockSpec((D, D), lambda i, j, r: (0, j)),
               o_spec=pl.BlockSpec((None, tm, D), lambda i, j, r: (j, i, 0)),
               o_shape=(3, s, D), acc_shape=None, out_dtype=BF16, name=name)


def _fox_qkv_dx(dqkv, w_qkv, name):
    s = dqkv.shape[1]
    tm = _tile(s, (512, 256, 128))
    return _mm(dqkv, w_qkv, mode="nt", grid=(s // tm, 1, 3),
               a_spec=pl.BlockSpec((None, tm, D), lambda i, j, r: (r, i, 0)),
               b_spec=pl.BlockSpec((D, D), lambda i, j, r: (0, r)),
               o_spec=pl.BlockSpec((tm, D), lambda i, j, r: (i, 0)),
               o_shape=(s, D), acc_shape=(tm, D), out_dtype=F32, name=name)


def _fox_qkv_dw(h, dqkv, name):
    s = h.shape[0]
    ts = _tile(s, (512, 256, 128))
    return _mm(h, dqkv, mode="tn", grid=(3, 1, s // ts),
               a_spec=pl.BlockSpec((ts, D), lambda i, j, r: (r, 0)),
               b_spec=pl.BlockSpec((None, ts, D), lambda i, j, r: (i, r, 0)),
               o_spec=pl.BlockSpec((None, D, D), lambda i, j, r: (i, 0, 0)),
               o_shape=(3, D, D), acc_shape=(D, D), out_dtype=BF16, name=name)


HPAIRS = HEADS // 2


def _first_head():
    return lax.broadcasted_iota(jnp.int32, (1, 2 * HDIM), 1) < HDIM


def _one_head(x, sel):
    return jnp.where(sel, x, jnp.zeros_like(x))


def _fox_scores(qm, k_ref, cq, ck, qi, tq, n):
    sc = lax.dot_general(qm, k_ref[0:n, :], _NT, preferred_element_type=F32) * (HDIM ** -0.5)
    sc = sc + cq - ck
    row = lax.broadcasted_iota(jnp.int32, (tq, n), 0) + qi * tq
    col = lax.broadcasted_iota(jnp.int32, (tq, n), 1)
    return sc, col <= row


def _fox_specs(s):
    sect = lambda i: pl.BlockSpec((None, s, 2 * HDIM), lambda p: (i, 0, p))
    ospec = pl.BlockSpec((s, 2 * HDIM), lambda p: (0, p))
    cspec = pl.BlockSpec((None, 2, s, 1), lambda p: (p, 0, 0, 0))
    rspec = pl.BlockSpec((None, 2, 1, s), lambda p: (p, 0, 0, 0))
    return sect, ospec, cspec, rspec


def _fox_attn_fwd(qkv, cq, ck, name):
    s = qkv.shape[1]
    tq = _tile(s, (256, 128))

    def body(q_ref, k_ref, v_ref, cq_ref, ck_ref, o_ref, lse_ref):
        first = _first_head()
        for qi in range(s // tq):
            n = (qi + 1) * tq
            rows = slice(qi * tq, n)
            q2 = q_ref[rows, :]
            outs = []
            for hh in range(2):
                sel = first if hh == 0 else jnp.logical_not(first)
                sc, keep = _fox_scores(_one_head(q2, sel), k_ref, cq_ref[hh, rows, :], ck_ref[hh, :, 0:n], qi, tq, n)
                sc = jnp.where(keep, sc, -1e30)
                m = jnp.max(sc, axis=1, keepdims=True)
                p = jnp.exp(sc - m)
                lsum = jnp.sum(p, axis=1, keepdims=True)
                outs.append(jnp.dot(p.astype(BF16), v_ref[0:n, :], preferred_element_type=F32) / lsum)
                lse_ref[hh, rows, :] = m + jnp.log(lsum)
            o_ref[rows, :] = jnp.where(first, outs[0], outs[1]).astype(BF16)

    sect, ospec, cspec, rspec = _fox_specs(s)
    return pl.pallas_call(
        body, name=name,
        out_shape=(jax.ShapeDtypeStruct((s, D), BF16), jax.ShapeDtypeStruct((HPAIRS, 2, s, 1), F32)),
        grid=(HPAIRS,), in_specs=[sect(0), sect(1), sect(2), cspec, rspec], out_specs=(ospec, cspec),
        compiler_params=_cparams(("parallel",), VMEM_BIG),
    )(qkv, qkv, qkv, cq, ck)


def _fox_attn_bwd(qkv, do, lse, cq, ck, name):
    s = qkv.shape[1]
    tq = _tile(s, (256, 128))
    scale = HDIM ** -0.5

    def body(q_ref, k_ref, v_ref, do_ref, lse_ref, cq_ref, ck_ref,
             dqkv_ref, dcq_ref, dck_ref, dk_acc, dv_acc, dck_acc):
        first = _first_head()
        dk_acc[...] = jnp.zeros_like(dk_acc)
        dv_acc[...] = jnp.zeros_like(dv_acc)
        dck_acc[...] = jnp.zeros_like(dck_acc)
        for qi in range(s // tq):
            n = (qi + 1) * tq
            rows = slice(qi * tq, n)
            q2, do2 = q_ref[rows, :], do_ref[rows, :]
            dq, dk, dv = [], [], []
            for hh in range(2):
                sel = first if hh == 0 else jnp.logical_not(first)
                sc, keep = _fox_scores(_one_head(q2, sel), k_ref, cq_ref[hh, rows, :], ck_ref[hh, :, 0:n], qi, tq, n)
                p = jnp.where(keep, jnp.exp(sc - lse_ref[hh, rows, :]), 0.0)
                dp = lax.dot_general(_one_head(do2, sel), v_ref[0:n, :], _NT, preferred_element_type=F32)
                ds = p * (dp - jnp.sum(p * dp, axis=1, keepdims=True))
                dsb = ds.astype(BF16)
                dq.append(jnp.dot(dsb, k_ref[0:n, :], preferred_element_type=F32))
                dk.append(lax.dot_general(dsb, q2, _TN, preferred_element_type=F32))
                dv.append(lax.dot_general(p.astype(BF16), do2, _TN, preferred_element_type=F32))
                dcq_ref[hh, rows, :] = jnp.sum(ds, axis=1, keepdims=True)
                dck_acc[hh, :, 0:n] -= jnp.sum(ds, axis=0, keepdims=True)
            dqkv_ref[0, rows, :] = (jnp.where(first, dq[0], dq[1]) * scale).astype(BF16)
            dk_acc[0:n, :] += jnp.where(first, dk[0], dk[1]) * scale
            dv_acc[0:n, :] += jnp.where(first, dv[0], dv[1])
        dqkv_ref[1] = dk_acc[...].astype(BF16)
        dqkv_ref[2] = dv_acc[...].astype(BF16)
        dck_ref[...] = dck_acc[...]

    sect, ospec, cspec, rspec = _fox_specs(s)
    return pl.pallas_call(
        body, name=name,
        out_shape=(jax.ShapeDtypeStruct((3, s, D), BF16), jax.ShapeDtypeStruct((HPAIRS, 2, s, 1), F32),
                   jax.ShapeDtypeStruct((HPAIRS, 2, 1, s), F32)),
        grid=(HPAIRS,), in_specs=[sect(0), sect(1), sect(2), ospec, cspec, cspec, rspec],
        out_specs=(pl.BlockSpec((3, s, 2 * HDIM), lambda p: (0, 0, p)), cspec, rspec),
        scratch_shapes=[pltpu.VMEM((s, 2 * HDIM), F32), pltpu.VMEM((s, 2 * HDIM), F32), pltpu.VMEM((2, 1, s), F32)],
        compiler_params=_cparams(("parallel",), VMEM_BIG),
    )(qkv, qkv, qkv, do, lse, cq, ck)


TC = 256
HALO = 8


def _chunk_specs(s):
    tc = _tile(s, (TC, 128))
    per = tc // HALO
    nblk = s // HALO
    cur = pl.BlockSpec((tc, 8, 128), lambda i: (i, 0, 0))
    past = pl.BlockSpec((HALO, 8, 128), lambda i: (jnp.maximum(i * per - 1, 0), 0, 0))
    future = pl.BlockSpec((HALO, 8, 128), lambda i: (jnp.minimum((i + 1) * per, nblk - 1), 0, 0))
    return tc, cur, past, future


def _vec_spec(n):
    return pl.BlockSpec((n, 8, 128), lambda i: (0, 0, 0))


def _conv_past(buf, w_ref, kw, tc):
    out = w_ref[kw - 1] * buf[HALO:HALO + tc]
    for k in range(kw - 1):
        off = HALO - (kw - 1) + k
        out = out + w_ref[k] * buf[off:off + tc]
    return out


def _sconv_fwd(bg, cg, xv, w, name):
    s = bg.shape[0]
    tc, cur, past, _ = _chunk_specs(s)

    def body(bg_ref, cg_ref, xv_ref, cgp_ref, xvp_ref, w_ref, y_ref, zbuf):
        first = pl.program_id(0) == 0
        zbuf[0:HALO] = jnp.where(first, 0.0, cgp_ref[...] * xvp_ref[...])
        zbuf[HALO:HALO + tc] = cg_ref[...] * xv_ref[...]
        y_ref[...] = bg_ref[...] * _conv_past(zbuf, w_ref, 3, tc)

    return pl.pallas_call(
        body, name=name, out_shape=jax.ShapeDtypeStruct((s, 8, 128), F32), grid=(s // tc,),
        in_specs=[cur, cur, cur, past, past, _vec_spec(3)], out_specs=cur,
        scratch_shapes=[pltpu.VMEM((tc + HALO, 8, 128), F32)],
        compiler_params=_cparams(("parallel",)),
    )(bg, cg, xv, cg, xv, w)


def _sconv_bwd(dy, bg, cg, xv, w, name):
    s = dy.shape[0]
    tc, cur, past, future = _chunk_specs(s)
    nch = s // tc

    def body(dy_ref, bg_ref, cg_ref, xv_ref, cgp_ref, xvp_ref, dyf_ref, bgf_ref, w_ref,
             dbg_ref, dcg_ref, dxv_ref, dw_ref, zbuf, dbuf):
        i = pl.program_id(0)

        @pl.when(i == 0)
        def _():
            dw_ref[...] = jnp.zeros_like(dw_ref)

        z = cg_ref[...] * xv_ref[...]
        zbuf[0:HALO] = jnp.where(i == 0, 0.0, cgp_ref[...] * xvp_ref[...])
        zbuf[HALO:HALO + tc] = z
        dyv = dy_ref[...]
        dbg_ref[...] = dyv * _conv_past(zbuf, w_ref, 3, tc)
        dbuf[0:tc] = dyv * bg_ref[...]
        dbuf[tc:tc + HALO] = jnp.where(i == nch - 1, 0.0, dyf_ref[...] * bgf_ref[...])
        dz = jnp.zeros((tc, 8, 128), F32)
        for k in range(3):
            sh = dbuf[2 - k:2 - k + tc]
            dz = dz + w_ref[k] * sh
            dw_ref[k] += jnp.sum(z * sh, axis=0)
        dcg_ref[...] = dz * xv_ref[...]
        dxv_ref[...] = dz * cg_ref[...]

    shp = jax.ShapeDtypeStruct((s, 8, 128), F32)
    return pl.pallas_call(
        body, name=name, out_shape=(shp, shp, shp, jax.ShapeDtypeStruct((3, 8, 128), F32)),
        grid=(nch,),
        in_specs=[cur, cur, cur, cur, past, past, future, future, _vec_spec(3)],
        out_specs=(cur, cur, cur, _vec_spec(3)),
        scratch_shapes=[pltpu.VMEM((tc + HALO, 8, 128), F32), pltpu.VMEM((tc + HALO, 8, 128), F32)],
        compiler_params=_cparams(("arbitrary",)),
    )(dy, bg, cg, xv, cg, xv, dy, bg, w)


def _lru_conv_fwd(xp, wb, name):
    s = xp.shape[0]
    tc, cur, past, _ = _chunk_specs(s)

    def body(x_ref, xp_ref, w_ref, o_ref, buf):
        buf[0:HALO] = jnp.where(pl.program_id(0) == 0, 0.0, xp_ref[...])
        buf[HALO:HALO + tc] = x_ref[...]
        o_ref[...] = _conv_past(buf, w_ref, 4, tc) + w_ref[4]

    return pl.pallas_call(
        body, name=name, out_shape=jax.ShapeDtypeStruct((s, 8, 128), F32), grid=(s // tc,),
        in_specs=[cur, past, _vec_spec(8)], out_specs=cur,
        scratch_shapes=[pltpu.VMEM((tc + HALO, 8, 128), F32)],
        compiler_params=_cparams(("parallel",)),
    )(xp, xp, wb)


def _lru_conv_bwd(dxb, xp, wb, name):
    s = dxb.shape[0]
    tc, cur, _, future = _chunk_specs(s)
    nch = s // tc

    def body(d_ref, df_ref, x_ref, w_ref, o_ref, red_ref, dbuf):
        i = pl.program_id(0)

        @pl.when(i == 0)
        def _():
            red_ref[...] = jnp.zeros_like(red_ref)

        dv = d_ref[...]
        dbuf[0:tc] = dv
        dbuf[tc:tc + HALO] = jnp.where(i == nch - 1, 0.0, df_ref[...])
        xv = x_ref[...]
        dx = jnp.zeros((tc, 8, 128), F32)
        for k in range(4):
            sh = dbuf[3 - k:3 - k + tc]
            dx = dx + w_ref[k] * sh
            red_ref[k] += jnp.sum(xv * sh, axis=0)
        red_ref[4] += jnp.sum(dv, axis=0)
        o_ref[...] = dx

    return pl.pallas_call(
        body, name=name,
        out_shape=(jax.ShapeDtypeStruct((s, 8, 128), F32), jax.ShapeDtypeStruct((8, 8, 128), F32)),
        grid=(nch,), in_specs=[cur, future, cur, _vec_spec(8)], out_specs=(cur, _vec_spec(8)),
        scratch_shapes=[pltpu.VMEM((tc + HALO, 8, 128), F32)],
        compiler_params=_cparams(("arbitrary",)),
    )(dxb, dxb, xp, wb)


def _lru_gates(ra, ia, pv_ref):
    sp = _softplus(-pv_ref[2])
    r = _sigmoid(ra + pv_ref[0])
    ig = _sigmoid(ia + pv_ref[1])
    log_a = (-LRU_C) * r * sp
    a = jnp.exp(log_a)
    mult = jnp.sqrt(-jnp.tanh(log_a) * (a * a + 1.0))
    return r, ig, a, mult, sp


def _lru_scan_fwd(ra, ia, xb, gate, pv, name):
    s = ra.shape[0]
    tc, cur, _, _ = _chunk_specs(s)

    def body(ra_ref, ia_ref, xb_ref, g_ref, pv_ref, hs_ref, hp_ref, y_ref, abuf, bbuf, hcar):
        @pl.when(pl.program_id(0) == 0)
        def _():
            hcar[...] = jnp.zeros_like(hcar)

        xbv = xb_ref[...]
        _, ig, a, mult, _ = _lru_gates(ra_ref[...], ia_ref[...], pv_ref)
        abuf[...] = a
        bbuf[...] = mult * (ig * xbv)

        def step(t, h):
            hp_ref[t] = h
            h = abuf[t] * h + bbuf[t]
            hs_ref[t] = h
            return h

        hcar[...] = lax.fori_loop(0, tc, step, hcar[...], unroll=8)
        y_ref[...] = hs_ref[...] * _gelu_parts(g_ref[...])[0]

    shp = jax.ShapeDtypeStruct((s, 8, 128), F32)
    return pl.pallas_call(
        body, name=name, out_shape=(shp, shp, shp), grid=(s // tc,),
        in_specs=[cur, cur, cur, cur, _vec_spec(8)], out_specs=(cur, cur, cur),
        scratch_shapes=[pltpu.VMEM((tc, 8, 128), F32), pltpu.VMEM((tc, 8, 128), F32),
                        pltpu.VMEM((8, 128), F32)],
        compiler_params=_cparams(("arbitrary",)),
    )(ra, ia, xb, gate, pv)


def _lru_scan_bwd(dy, ra, ia, xb, gate, hs, hp, pv, name):
    s = dy.shape[0]
    tc = _tile(s, (TC, 128))
    nch = s // tc
    rev = pl.BlockSpec((tc, 8, 128), lambda i: (nch - 1 - i, 0, 0))

    def body(dy_ref, ra_ref, ia_ref, xb_ref, g_ref, hs_ref, hp_ref, pv_ref,
             dg_ref, dra_ref, dia_ref, dxb_ref, red_ref, abuf, dbuf, gbuf, car):
        @pl.when(pl.program_id(0) == 0)
        def _():
            red_ref[...] = jnp.zeros_like(red_ref)
            car[...] = jnp.zeros_like(car)

        xbv = xb_ref[...]
        r, ig, a, mult, sp = _lru_gates(ra_ref[...], ia_ref[...], pv_ref)
        ge, dge = _gelu_parts(g_ref[...])
        dyv = dy_ref[...]
        dg_ref[...] = dyv * hs_ref[...] * dge
        abuf[...] = a
        dbuf[...] = dyv * ge

        def step(k, c):
            t = tc - 1 - k
            g = dbuf[t] + c
            gbuf[t] = g
            return abuf[t] * g

        car[...] = lax.fori_loop(0, tc, step, car[...], unroll=8)
        g = gbuf[...]
        d_a = g * hp_ref[...]
        d_m = g * (ig * xbv)
        d_loga = d_a * a - d_m * (a * a / mult)
        dxb_ref[...] = g * mult * ig
        dra = d_loga * ((-LRU_C) * sp) * r * (1.0 - r)
        dia = g * mult * xbv * ig * (1.0 - ig)
        dra_ref[...] = dra
        dia_ref[...] = dia
        red_ref[0] += jnp.sum(dra, axis=0)
        red_ref[1] += jnp.sum(dia, axis=0)
        red_ref[2] += jnp.sum(d_loga * r, axis=0) * (LRU_C * _sigmoid(-pv_ref[2]))

    shp = jax.ShapeDtypeStruct((s, 8, 128), F32)
    return pl.pallas_call(
        body, name=name, out_shape=(shp, shp, shp, shp, jax.ShapeDtypeStruct((8, 8, 128), F32)),
        grid=(nch,), in_specs=[rev] * 7 + [_vec_spec(8)],
        out_specs=(rev, rev, rev, rev, _vec_spec(8)),
        scratch_shapes=[pltpu.VMEM((tc, 8, 128), F32), pltpu.VMEM((tc, 8, 128), F32),
                        pltpu.VMEM((tc, 8, 128), F32), pltpu.VMEM((8, 128), F32)],
        compiler_params=_cparams(("arbitrary",)),
    )(dy, ra, ia, xb, gate, hs, hp, pv)


def _adamw(src, w, m, v, name):
    nl, n, rr, cc = src.shape
    tr = rr
    for cand in sorted((d for d in range(16, rr + 1, 16) if rr % d == 0), reverse=True):
        if cand * cc <= 192 * 1024:
            tr = cand
            break
    c1 = 1.0 - ADAM_B1 ** ADAM_STEP
    c2 = 1.0 - ADAM_B2 ** ADAM_STEP

    def body(s_ref, w_ref, m_ref, v_ref, g_out, d_out, m_out, v_out):
        g = s_ref[0].astype(F32)
        for k in range(1, n):
            g = g + s_ref[k].astype(F32)
        mn = ADAM_B1 * m_ref[...] + (1.0 - ADAM_B1) * g
        vn = ADAM_B2 * v_ref[...] + (1.0 - ADAM_B2) * (g * g)
        g_out[...] = g
        m_out[...] = mn
        v_out[...] = vn
        d_out[...] = (-ADAM_LR) * ((mn / c1) / (jnp.sqrt(vn / c2) + ADAM_EPS) + ADAM_WD * w_ref[...])

    pspec = pl.BlockSpec((None, tr, cc), lambda l, i: (l, i, 0))
    shp = jax.ShapeDtypeStruct((nl, rr, cc), F32)
    return pl.pallas_call(
        body, name=name, out_shape=(shp, shp, shp, shp), grid=(nl, rr // tr),
        in_specs=[pl.BlockSpec((None, n, tr, cc), lambda l, i: (l, 0, i, 0)), pspec, pspec, pspec],
        out_specs=(pspec, pspec, pspec, pspec),
        compiler_params=_cparams(("parallel", "parallel"), VMEM_BIG),
    )(src, w, m, v)


def _adamw_slice(src, w, m, v, bufs, idx, name):
    n, rr, cc = src.shape
    nl = w.shape[0]
    tr = rr
    for cand in sorted((d for d in range(16, rr + 1, 16) if rr % d == 0), reverse=True):
        if cand * cc <= 192 * 1024:
            tr = cand
            break
    c1 = 1.0 - ADAM_B1 ** ADAM_STEP
    c2 = 1.0 - ADAM_B2 ** ADAM_STEP
    if bufs is None:
        bufs = tuple(lax.empty((nl, rr, cc), F32) for _ in range(4))

    def body(s_ref, w_ref, m_ref, v_ref, b0, b1, b2, b3, g_out, d_out, m_out, v_out):
        g = s_ref[0].astype(F32)
        for k in range(1, n):
            g = g + s_ref[k].astype(F32)
        mn = ADAM_B1 * m_ref[...] + (1.0 - ADAM_B1) * g
        vn = ADAM_B2 * v_ref[...] + (1.0 - ADAM_B2) * (g * g)
        g_out[...] = g
        m_out[...] = mn
        v_out[...] = vn
        d_out[...] = (-ADAM_LR) * ((mn / c1) / (jnp.sqrt(vn / c2) + ADAM_EPS) + ADAM_WD * w_ref[...])

    pspec = pl.BlockSpec((None, tr, cc), lambda i: (idx, i, 0))
    anyspec = pl.BlockSpec(memory_space=pl.ANY)
    shp = jax.ShapeDtypeStruct((nl, rr, cc), F32)
    return pl.pallas_call(
        body, name=name, out_shape=(shp, shp, shp, shp), grid=(rr // tr,),
        in_specs=[pl.BlockSpec((n, tr, cc), lambda i: (0, i, 0)), pspec, pspec, pspec,
                  anyspec, anyspec, anyspec, anyspec],
        out_specs=(pspec, pspec, pspec, pspec),
        input_output_aliases={4: 0, 5: 1, 6: 2, 7: 3},
        compiler_params=_cparams(("parallel",), VMEM_BIG),
    )(src, w, m, v, *bufs)


def _sum_devices(x, name):
    _, rr, cc = x.shape

    def body(x_ref, o_ref):
        acc = x_ref[0]
        for k in range(1, NDEV):
            acc = acc + x_ref[k]
        o_ref[...] = acc

    return pl.pallas_call(
        body, name=name, out_shape=jax.ShapeDtypeStruct((rr, cc), F32), grid=(rr // 8,),
        in_specs=[pl.BlockSpec((NDEV, 8, cc), lambda i: (0, i, 0))],
        out_specs=pl.BlockSpec((8, cc), lambda i: (i, 0)),
        compiler_params=_cparams(("parallel",)),
    )(x)


def _to3(x):
    return x.reshape(x.shape[0], 8, 128)


def _heads(x):
    s = x.shape[0]
    return x.reshape(s, HEADS, HDIM).transpose(1, 0, 2).astype(BF16)


def _unheads(x):
    s = x.shape[1]
    return x.transpose(1, 0, 2).reshape(s, D)


def _block_diag(w):
    eye = jnp.eye(HEADS, dtype=w.dtype)
    return (w[:, :, None, :] * eye[:, None, :, None]).reshape(D, D)


def _diag_blocks(x):
    return jnp.diagonal(x.reshape(HEADS, HDIM, HEADS, HDIM), axis1=0, axis2=2).transpose(2, 0, 1)


def _col_blocks(dw, n):
    return dw.reshape(dw.shape[0], NDEV, n).transpose(1, 0, 2)


def kernel(x, c, w_cond, b_cond, norm_pre, norm_post, w_ffn_in, w_ffn_out, fox_w_in, fox_b_f, fox_w_out, sconv_w_in, sconv_conv_w, sconv_w_out, lru_w_in, lru_conv_w, lru_conv_b, lru_w_a, lru_b_a, lru_w_x, lru_b_x, lru_lambda, lru_w_out, loss_target, m_w_cond, m_b_cond, m_norm_pre, m_norm_post, m_w_ffn_in, m_w_ffn_out, m_fox_w_in, m_fox_b_f, m_fox_w_out, m_sconv_w_in, m_sconv_conv_w, m_sconv_w_out, m_lru_w_in, m_lru_conv_w, m_lru_conv_b, m_lru_w_a, m_lru_b_a, m_lru_w_x, m_lru_b_x, m_lru_lambda, m_lru_w_out, v_w_cond, v_b_cond, v_norm_pre, v_norm_post, v_w_ffn_in, v_w_ffn_out, v_fox_w_in, v_fox_b_f, v_fox_w_out, v_sconv_w_in, v_sconv_conv_w, v_sconv_w_out, v_lru_w_in, v_lru_conv_w, v_lru_conv_b, v_lru_w_a, v_lru_b_a, v_lru_w_x, v_lru_b_x, v_lru_lambda, v_lru_w_out):
    me = 4 * lax.axis_index("x") + 2 * lax.axis_index("y") + lax.axis_index("c")
    s = x.shape[1]
    x0 = x[0]
    tgt = loss_target[0]
    wcw = w_cond.shape[2]
    dsh = norm_pre.shape[2]

    small_parts = [c.reshape(-1), norm_pre.reshape(-1), norm_post.reshape(-1), sconv_conv_w.reshape(-1),
                   lru_conv_w.reshape(-1), lru_conv_b.reshape(-1), lru_lambda.reshape(-1)]
    sizes = [p.shape[0] for p in small_parts]
    flat = jnp.concatenate(small_parts)
    padn = (-flat.shape[0]) % 1024
    flat = jnp.pad(flat, (0, padn)).reshape(-1, 1024)
    sm = _all_gather(flat, 0, "ag_small").reshape(NDEV, -1)
    offs = [0]
    for n_ in sizes:
        offs.append(offs[-1] + n_)
    piece = lambda i: sm[:, offs[i]:offs[i + 1]]
    c_all = piece(0)
    unshard = lambda p, lead: p.reshape((NDEV,) + lead + (dsh,)).transpose(
        tuple(range(1, len(lead) + 1)) + (0, len(lead) + 1)).reshape(lead + (D,))
    npre = unshard(piece(1), (DEPTH, 3))
    npost = unshard(piece(2), (DEPTH, 3))
    scw = unshard(piece(3), (3,))
    lcw = unshard(piece(4), (4,))
    lcb = unshard(piece(5), ())
    llam = unshard(piece(6), ())

    c_pad = jnp.pad(c_all, ((0, 128 - NDEV), (0, 0)))
    bc_mine = lax.dynamic_slice(b_cond, (0, me * wcw), (DEPTH, wcw)).reshape(DEPTH, 1, wcw)
    modc, ca_pad = _cond_fwd(c_pad, w_cond, bc_mine, "cond_fwd")
    modg = _all_gather(modc[:, :NDEV, :], 0, "ag_mod")
    mod = lax.dynamic_index_in_dim(modg, me, axis=2, keepdims=False)
    mod = mod.transpose(1, 0, 2).reshape(DEPTH, 3, 3, D)
    ptab = jnp.stack([npre, mod[:, :, 1], mod[:, :, 0], npost, mod[:, :, 2],
                      jnp.zeros_like(npre), jnp.zeros_like(npre), jnp.zeros_like(npre)], axis=2)

    pend = {}
    chain = [jnp.zeros((8, 128), F32)]

    def ag_start(key, w):
        hnd = _exchange_start((w + chain[0][0, 0]).astype(BF16), "ags_" + key, True)
        chain[0] = hnd[4]
        pend[key] = hnd[:4]

    def ag_wait(key, after):
        return _exchange_wait(pend.pop(key), after, "agw_" + key, True)

    for l in range(DEPTH):
        ag_start(f"wi{l}0", w_ffn_in[l, 0])
        ag_start(f"wo{l}0", w_ffn_out[l, 0])
        if l % 3 == 0:
            ag_start(f"fwi{l // 3}", fox_w_in[l // 3])
            ag_start(f"fwo{l // 3}", fox_w_out[l // 3])
        elif l % 3 == 1:
            ag_start("swi", sconv_w_in[0])
            ag_start("swo", sconv_w_out[0])
        else:
            ag_start("lwi", lru_w_in[0])
            ag_start("lwo", lru_w_out[0])
        ag_start(f"wi{l}1", w_ffn_in[l, 1])
        ag_start(f"wo{l}1", w_ffn_out[l, 1])
    cols = lambda wg: wg.transpose(1, 0, 2).reshape(D, -1)
    wgate = jnp.concatenate([_block_diag(lru_w_a[0]), _block_diag(lru_w_x[0])], axis=1).astype(BF16)
    bf_pad = jnp.pad(fox_b_f, ((0, 0), (0, 128 - HEADS))).reshape(2, 1, 128)
    scw3 = scw.reshape(3, 8, 128)
    lcwb = jnp.concatenate([lcw, lcb[None], jnp.zeros((3, D), F32)], axis=0).reshape(8, 8, 128)
    lpv = jnp.concatenate([lru_b_a.reshape(1, D), lru_b_x.reshape(1, D), llam[None],
                           jnp.zeros((5, D), F32)], axis=0).reshape(8, 8, 128)

    subs = [(l, sub) for l in range(DEPTH) for sub in range(3)]
    coef = lambda sub: 1.0 if sub == 1 else 0.5

    saved = {}
    xcur = x0
    h = _prenorm(xcur, ptab, 0, 0, "prenorm", toks=(chain[0],))
    for idx, (l, sub) in enumerate(subs):
        tag = f"{l}{sub}"
        sv = {"x": xcur, "h": h}
        if sub != 1:
            f = 0 if sub == 0 else 1
            wi = ag_wait(f"wi{l}{f}", h)
            wo = ag_wait(f"wo{l}{f}", h).reshape(4, FFB, D)
            gu, act = _ffn_up(h, wi, "ffn_up_" + tag)
            y = _ffn_down(act, wo, "ffn_down_" + tag)
            sv.update(gu=gu, wi=wi, wo=wo)
        elif l % 3 == 0:
            j = l // 3
            fwi_j = cols(ag_wait(f"fwi{j}", h))
            w_qkv = fwi_j[:, :3 * D]
            w_f = jnp.pad(fwi_j[:, 3 * D:], ((0, 0), (0, 128 - HEADS)))
            fwo_j = ag_wait(f"fwo{j}", h).reshape(D, D)
            qkv = _fox_qkv(h, w_qkv, "fox_qkv_" + tag)
            fl = _mm_nn(h, w_f, F32, "fox_f_" + tag)
            cum = _fox_cum(fl, bf_pad[j], "fox_cum_" + tag)
            cumt = cum[:, :HEADS].T.reshape(HPAIRS, 2, s)
            cq, ck = cumt[:, :, :, None], cumt[:, :, None, :]
            o2, lse = _fox_attn_fwd(qkv, cq, ck, "fox_attn_" + tag)
            y = _mm_nn(o2, fwo_j, F32, "fox_out_" + tag)
            sv.update(w_qkv=w_qkv, w_f=w_f, fwo=fwo_j, qkv=qkv, fl=fl, cq=cq, ck=ck, lse=lse, o2=o2)
        elif l % 3 == 1:
            swi = cols(ag_wait("swi", h))
            swo = ag_wait("swo", h).reshape(D, D)
            proj = _mm_nn(h, swi, F32, "sconv_in_" + tag)
            bg, cg, xv = (_to3(proj[:, i * D:(i + 1) * D]) for i in range(3))
            y3 = _sconv_fwd(bg, cg, xv, scw3, "sconv_mix_" + tag)
            y2 = y3.reshape(s, D).astype(BF16)
            y = _mm_nn(y2, swo, F32, "sconv_out_" + tag)
            sv.update(bg=bg, cg=cg, xv=xv, y2=y2)
        else:
            lwi = cols(ag_wait("lwi", h))
            lwo = ag_wait("lwo", h).reshape(D, D)
            proj = _mm_nn(h, lwi, F32, "lru_in_" + tag)
            gate3, xp3 = _to3(proj[:, :D]), _to3(proj[:, D:])
            xb3 = _lru_conv_fwd(xp3, lcwb, "lru_conv_" + tag)
            xb2 = xb3.reshape(s, D).astype(BF16)
            gpre = _mm_nn(xb2, wgate, F32, "lru_gate_" + tag)
            ra3, ia3 = _to3(gpre[:, :D]), _to3(gpre[:, D:])
            hs3, hp3, y3 = _lru_scan_fwd(ra3, ia3, xb3, gate3, lpv, "lru_scan_" + tag)
            y2 = y3.reshape(s, D).astype(BF16)
            y = _mm_nn(y2, lwo, F32, "lru_out_" + tag)
            sv.update(gate3=gate3, xp3=xp3, xb3=xb3, xb2=xb2, ra3=ra3, ia3=ia3, hs3=hs3, hp3=hp3, y2=y2)
        sv["y"] = y
        saved[(l, sub)] = sv
        nxt = subs[idx + 1] if idx + 1 < len(subs) else None
        xcur, h = _resid(xcur, y, ptab, l, sub, coef(sub), nxt, "resid_" + tag)

    dx, loss_cols = _loss_grad(xcur, tgt, "loss_grad")

    d_mod = [[[None] * 3 for _ in range(3)] for _ in range(DEPTH)]
    d_npre = [[None] * 3 for _ in range(DEPTH)]
    d_npost = [[None] * 3 for _ in range(DEPTH)]
    d_fbf = [None, None]
    small_g = {}
    rs_pend, rs_order = {}, []
    toks = []
    dh2 = None

    def rs_start(key, blocks):
        hnd = _exchange_start(blocks, "rss_" + key, False)
        rs_pend[key] = hnd[:4]
        rs_order.append(key)
        toks.append(hnd[4])

    for (l, sub) in reversed(subs):
        tag = f"{l}{sub}"
        sv = saved[(l, sub)]
        dy, red = _post_bwd(dx, sv["y"], ptab, l, sub, coef(sub), "post_bwd_" + tag, toks=tuple(toks))
        del toks[:]
        d_mod[l][sub][2] = red[0]
        d_npost[l][sub] = red[1]
        hb = sv["h"]
        if sub != 1:
            f = 0 if sub == 0 else 1
            da, dwo = _ffn_down_bwd(dy, sv["gu"], sv["wo"], "ffn_down_bwd_" + tag)
            dh = _ffn_dh(da, sv["wi"], "ffn_dh_" + tag)
            rs_start(f"wo{l}{f}", dwo.reshape(NDEV, FFB // 2, D))
            dwi = lambda hb=hb, da=da, tag=tag: _ffn_dwi(hb, da.reshape(NDEV, s, FFB), "ffn_dwi_" + tag)
            if (l, sub) != subs[0]:
                rs_start(f"wi{l}{f}", dwi())
        elif l % 3 == 0:
            j = l // 3
            do2 = _mm_nt(dy, sv["fwo"], BF16, "fox_out_dx_" + tag)
            rs_start(f"fwo{j}", _mm_tn(sv["o2"], dy, BF16, "fox_out_dw_" + tag).reshape(NDEV, dsh, D))
            dqkv, dcq, dck = _fox_attn_bwd(sv["qkv"], do2, sv["lse"], sv["cq"], sv["ck"], "fox_attn_bwd_" + tag)
            dcum = (dcq[..., 0] + dck[:, :, 0, :]).reshape(HEADS, s).T
            dcum = jnp.pad(dcum, ((0, 0), (0, 128 - HEADS)))
            dfl, dbf = _fox_cum_bwd(dcum, sv["fl"], bf_pad[j], "fox_cum_bwd_" + tag)
            d_fbf[j] = dbf[0, :HEADS]
            dflb = dfl.astype(BF16)
            dh = _fox_qkv_dx(dqkv, sv["w_qkv"], "fox_qkv_dx_" + tag)
            dh2 = _mm_nt(dflb, sv["w_f"], F32, "fox_f_dx_" + tag)
            dw3 = _fox_qkv_dw(hb, dqkv, "fox_qkv_dw_" + tag)
            dwf = _mm_tn(hb, dflb, BF16, "fox_f_dw_" + tag)
            dwfox = jnp.concatenate([dw3[0], dw3[1], dw3[2], dwf[:, :HEADS]], axis=1)
            rs_start(f"fwi{j}", _col_blocks(dwfox, fox_w_in.shape[2]))
        elif l % 3 == 1:
            dy3 = _to3(_mm_nt(dy, swo, F32, "sconv_out_dx_" + tag))
            rs_start("swo", _mm_tn(sv["y2"], dy, BF16, "sconv_out_dw_" + tag).reshape(NDEV, dsh, D))
            dbg, dcg, dxv, dscw = _sconv_bwd(dy3, sv["bg"], sv["cg"], sv["xv"], scw3, "sconv_mix_bwd_" + tag)
            small_g["sconv_conv_w"] = dscw.reshape(3, D)
            dproj = jnp.concatenate([t.reshape(s, D) for t in (dbg, dcg, dxv)], axis=1).astype(BF16)
            dh = _mm_nt(dproj, swi, F32, "sconv_in_dx_" + tag)
            rs_start("swi", _col_blocks(_mm_tn(hb, dproj, BF16, "sconv_in_dw_" + tag), sconv_w_in.shape[2]))
        else:
            dy3 = _to3(_mm_nt(dy, lwo, F32, "lru_out_dx_" + tag))
            rs_start("lwo", _mm_tn(sv["y2"], dy, BF16, "lru_out_dw_" + tag).reshape(NDEV, dsh, D))
            dgate3, dra3, dia3, dxb3, lred = _lru_scan_bwd(
                dy3, sv["ra3"], sv["ia3"], sv["xb3"], sv["gate3"], sv["hs3"], sv["hp3"], lpv, "lru_scan_bwd_" + tag)
            dgp = jnp.concatenate([dra3.reshape(s, D), dia3.reshape(s, D)], axis=1).astype(BF16)
            dxb3 = dxb3 + _to3(_mm_nt(dgp, wgate, F32, "lru_gate_dx_" + tag))
            dwgate = _mm_tn(sv["xb2"], dgp, F32, "lru_gate_dw_" + tag)
            dxp3, cred = _lru_conv_bwd(dxb3, sv["xp3"], lcwb, "lru_conv_bwd_" + tag)
            lred, cred = lred.reshape(8, D), cred.reshape(8, D)
            small_g.update(lru_w_a=_diag_blocks(dwgate[:, :D]), lru_w_x=_diag_blocks(dwgate[:, D:]),
                           lru_b_a=lred[0], lru_b_x=lred[1], lru_lambda=lred[2],
                           lru_conv_w=cred[:4], lru_conv_b=cred[4])
            dproj = jnp.concatenate([dgate3.reshape(s, D), dxp3.reshape(s, D)], axis=1).astype(BF16)
            dh = _mm_nt(dproj, lwi, F32, "lru_in_dx_" + tag)
            rs_start("lwi", _col_blocks(_mm_tn(hb, dproj, BF16, "lru_in_dw_" + tag), lru_w_in.shape[2]))
        dx, red = _pre_bwd(dx, dh, sv["x"], ptab, l, sub, "pre_bwd_" + tag, dh2=dh2)
        dh2 = None
        d_mod[l][sub][0] = red[0]
        d_mod[l][sub][1] = red[1]
        d_npre[l][sub] = red[2]
    grad_x = dx[None]

    dmod_mine = jnp.stack([jnp.stack([jnp.stack(d_mod[l][sub]) for sub in range(3)]) for l in range(DEPTH)])
    gparts = [loss_cols[0], dmod_mine.reshape(-1),
              jnp.stack([jnp.stack(r_) for r_ in d_npre]).reshape(-1),
              jnp.stack([jnp.stack(r_) for r_ in d_npost]).reshape(-1),
              jnp.stack(d_fbf).reshape(-1), small_g["sconv_conv_w"].reshape(-1),
              small_g["lru_conv_w"].reshape(-1), small_g["lru_conv_b"].reshape(-1),
              small_g["lru_w_a"].reshape(-1), small_g["lru_b_a"].reshape(-1),
              small_g["lru_w_x"].reshape(-1), small_g["lru_b_x"].reshape(-1),
              small_g["lru_lambda"].reshape(-1)]
    gsizes = [p.shape[0] for p in gparts]
    gflat = jnp.concatenate(gparts)
    gflat = jnp.pad(gflat, (0, (-gflat.shape[0]) % (8 * 1024))).reshape(-1, 1024)
    small_hnd = _exchange_start(gflat, "ags_smallgrads", True)
    rs_start("wi00", dwi())

    out = {}
    after = [small_hnd[4]]

    def update(keys, w, m, v, name):
        got = {}
        for k_ in sorted(keys, key=rs_order.index):
            got[k_] = after[0] = _exchange_wait(rs_pend.pop(k_), after[0], "rsw_" + k_, False)
        src = jnp.stack([got[k_] for k_ in keys])
        shp = w.shape
        w3, m3, v3 = (t.reshape((src.shape[0],) + src.shape[2:]) for t in (w, m, v))
        res = tuple(t.reshape(shp) for t in _adamw(src, w3, m3, v3, "adamw_" + name))
        after[0] = res[1]
        return res

    lf = [f"{l}{f}" for l in range(DEPTH) for f in range(2)]
    out["fox_w_in"] = update(["fwi0", "fwi1"], fox_w_in, m_fox_w_in, v_fox_w_in, "fwi")
    out["fox_w_out"] = update(["fwo0", "fwo1"], fox_w_out, m_fox_w_out, v_fox_w_out, "fwo")
    out["lru_w_out"] = update(["lwo"], lru_w_out, m_lru_w_out, v_lru_w_out, "lwo")
    out["lru_w_in"] = update(["lwi"], lru_w_in, m_lru_w_in, v_lru_w_in, "lwi")
    out["sconv_w_out"] = update(["swo"], sconv_w_out, m_sconv_w_out, v_sconv_w_out, "swo")
    out["sconv_w_in"] = update(["swi"], sconv_w_in, m_sconv_w_in, v_sconv_w_in, "swi")

    stacked = {"wi": [t.reshape((-1,) + w_ffn_in.shape[2:]) for t in (w_ffn_in, m_w_ffn_in, v_w_ffn_in)],
               "wo": [t.reshape((-1,) + w_ffn_out.shape[2:]) for t in (w_ffn_out, m_w_ffn_out, v_w_ffn_out)]}
    bufs = {"wi": None, "wo": None}

    def update_slice(k_):
        kind = k_[:2]
        src = after[0] = _exchange_wait(rs_pend.pop(k_), after[0], "rsw_" + k_, False)
        bufs[kind] = _adamw_slice(src, *stacked[kind], bufs[kind], lf.index(k_[2:]), "adamw_" + k_)
        after[0] = bufs[kind][1]

    for k_ in rs_order:
        if k_[:2] in ("wi", "wo") and k_ != "wi00":
            update_slice(k_)

    gall = _exchange_wait(small_hnd[:4], after[0], "agw_smallgrads", True)
    gsum = _sum_devices(gall, "sum_smallgrads").reshape(-1)
    goffs = [0]
    for n_ in gsizes:
        goffs.append(goffs[-1] + n_)
    gpiece = lambda i: gsum[goffs[i]:goffs[i + 1]]
    loss = jnp.sum(gpiece(0))
    my_ch = lambda g, lead: lax.dynamic_slice_in_dim(g.reshape(lead + (D,)), me * dsh, dsh, axis=len(lead))

    dmod_all = gall.reshape(NDEV, -1)[:, goffs[1]:goffs[2]].reshape(NDEV, DEPTH, 3 * 3 * D)
    dmod_cols = lax.dynamic_slice_in_dim(dmod_all, me * wcw, wcw, axis=2).transpose(1, 0, 2)
    dmod_cols = jnp.pad(dmod_cols, ((0, 0), (0, 128 - NDEV), (0, 0))).astype(BF16)
    g_wcond = _cond_bwd(ca_pad, dmod_cols, "cond_bwd")

    out["w_cond"] = _adamw(g_wcond[:, None], w_cond, m_w_cond, v_w_cond, "adamw_wcond")

    small = [
        ("b_cond", b_cond, m_b_cond, v_b_cond, gpiece(1)),
        ("norm_pre", norm_pre, m_norm_pre, v_norm_pre, my_ch(gpiece(2), (DEPTH, 3))),
        ("norm_post", norm_post, m_norm_post, v_norm_post, my_ch(gpiece(3), (DEPTH, 3))),
        ("fox_b_f", fox_b_f, m_fox_b_f, v_fox_b_f, gpiece(4)),
        ("sconv_conv_w", sconv_conv_w, m_sconv_conv_w, v_sconv_conv_w, my_ch(gpiece(5), (1, 3))),
        ("lru_conv_w", lru_conv_w, m_lru_conv_w, v_lru_conv_w, my_ch(gpiece(6), (1, 4))),
        ("lru_conv_b", lru_conv_b, m_lru_conv_b, v_lru_conv_b, my_ch(gpiece(7), (1,))),
        ("lru_w_a", lru_w_a, m_lru_w_a, v_lru_w_a, gpiece(8)),
        ("lru_b_a", lru_b_a, m_lru_b_a, v_lru_b_a, gpiece(9)),
        ("lru_w_x", lru_w_x, m_lru_w_x, v_lru_w_x, gpiece(10)),
        ("lru_b_x", lru_b_x, m_lru_b_x, v_lru_b_x, gpiece(11)),
        ("lru_lambda", lru_lambda, m_lru_lambda, v_lru_lambda, my_ch(gpiece(12), (1,))),
    ]
    pack = lambda ts: jnp.concatenate([t.reshape(-1) for t in ts])
    ssz = [w_.size for _, w_, _, _, _ in small]
    tot = sum(ssz)
    padr = (-tot) % (16 * 1024)
    pk = lambda ts: jnp.pad(pack(ts), (0, padr)).reshape(1, -1, 1024)
    sg, sd, smm, svv = _adamw(pk([t[4] for t in small])[:, None], pk([t[1] for t in small]),
                              pk([t[2] for t in small]), pk([t[3] for t in small]), "adamw_small")
    soff = 0
    for (name, w_, _, _, _), n_ in zip(small, ssz):
        out[name] = tuple(t.reshape(-1)[soff:soff + n_].reshape(w_.shape) for t in (sg, sd, smm, svv))
        soff += n_

    after[0] = sd
    update_slice("wi00")
    out["w_ffn_in"] = tuple(t.reshape(w_ffn_in.shape) for t in bufs["wi"])
    out["w_ffn_out"] = tuple(t.reshape(w_ffn_out.shape) for t in bufs["wo"])

    names =["w_cond", "b_cond", "norm_pre", "norm_post", "w_ffn_in", "w_ffn_out", "fox_w_in", "fox_b_f",
             "fox_w_out", "sconv_w_in", "sconv_conv_w", "sconv_w_out", "lru_w_in", "lru_conv_w", "lru_conv_b",
             "lru_w_a", "lru_b_a", "lru_w_x", "lru_b_x", "lru_lambda", "lru_w_out"]
    return (loss, grad_x, *[out[n_][0] for n_ in names], *[out[n_][1] for n_ in names],
            *[out[n_][2] for n_ in names], *[out[n_][3] for n_ in names])
```

```python
import functools
import math

import jax
import jax.numpy as jnp
from jax import lax
from jax.experimental import pallas as pl
from jax.experimental.pallas import tpu as pltpu

F32 = jnp.float32
BF16 = jnp.bfloat16
NDEV = 8
D = 1024
DFF = 2816
FFB = 704
HEADS = 16
HDIM = 64
DEPTH = 4
RMS_EPS = 1e-6
LRU_C = 8.0
ADAM_LR, ADAM_B1, ADAM_B2, ADAM_EPS, ADAM_WD, ADAM_STEP = 0.001, 0.9, 0.999, 1e-08, 0.01, 10
FOXP = 3200
MESH = pl.DeviceIdType.MESH
HBM = pl.BlockSpec(memory_space=pltpu.HBM)
VMEM_BIG = 48 * 1024 * 1024

_NN = (((1,), (0,)), ((), ()))
_NT = (((1,), (1,)), ((), ()))
_TN = (((0,), (0,)), ((), ()))
_DIMS = {"nn": _NN, "nt": _NT, "tn": _TN}


def _cparams(sem=None, vmem=None):
    kw = {}
    if sem is not None:
        kw["dimension_semantics"] = sem
    if vmem is not None:
        kw["vmem_limit_bytes"] = vmem
    return pltpu.CompilerParams(**kw)


def _sigmoid(x):
    return 1.0 / (1.0 + jnp.exp(-x))


def _softplus(x):
    return jnp.maximum(x, 0.0) + jnp.log(1.0 + jnp.exp(-jnp.abs(x)))


_GELU_C = math.sqrt(2.0 / math.pi)


def _gelu_parts(x):
    u = _GELU_C * (x + 0.044715 * x * x * x)
    t = jnp.tanh(u)
    g = 0.5 * x * (1.0 + t)
    dg = 0.5 * (1.0 + t) + 0.5 * x * (1.0 - t * t) * _GELU_C * (1.0 + 3.0 * 0.044715 * x * x)
    return g, dg


def _mesh_pos():
    ax, ay, ac = lax.axis_index("x"), lax.axis_index("y"), lax.axis_index("c")
    return ax, ay, ac, 4 * ax + 2 * ay + ac


def _peer(ax, ay, ac, d):
    px = 1 - ax if (d >> 2) & 1 else ax
    py = 1 - ay if (d >> 1) & 1 else ay
    pc = 1 - ac if d & 1 else ac
    return (px, py, pc), 4 * px + 2 * py + pc


def _exchange(x, axis, name, gather):
    if gather:
        x = jnp.expand_dims(x, axis)
        oshape = x.shape[:axis] + (NDEV,) + x.shape[axis + 1:]
    else:
        oshape = x.shape
    lead = (slice(None),) * axis

    def blk(ref, k):
        return ref.at[lead + (pl.ds(k, 1),)]

    def body(x_ref, o_ref, send_sems, recv_sems, local_sem):
        ax, ay, ac, me = _mesh_pos()
        src = (lambda k: x_ref) if gather else (lambda k: blk(x_ref, k))
        mine = pltpu.make_async_copy(src(me), blk(o_ref, me), local_sem)
        mine.start()
        sends = []
        for d in range(1, NDEV):
            peer, pidx = _peer(ax, ay, ac, d)
            cp = pltpu.make_async_remote_copy(
                src_ref=src(pidx), dst_ref=blk(o_ref, me), send_sem=send_sems.at[d - 1],
                recv_sem=recv_sems.at[d - 1], device_id=peer, device_id_type=MESH)
            cp.start()
            sends.append(cp)
        for d in range(1, NDEV):
            peer, pidx = _peer(ax, ay, ac, d)
            pltpu.make_async_remote_copy(
                src_ref=src(pidx), dst_ref=blk(o_ref, pidx), send_sem=send_sems.at[d - 1],
                recv_sem=recv_sems.at[d - 1], device_id=peer, device_id_type=MESH).wait_recv()
        for cp in sends:
            cp.wait_send()
        mine.wait()

    return pl.pallas_call(
        body, name=name, out_shape=jax.ShapeDtypeStruct(oshape, x.dtype),
        in_specs=[HBM], out_specs=HBM,
        scratch_shapes=[pltpu.SemaphoreType.DMA((NDEV - 1,)), pltpu.SemaphoreType.DMA((NDEV - 1,)),
                        pltpu.SemaphoreType.DMA(())],
    )(x)


def _all_gather(x, axis, name):
    return _exchange(x, axis, name, True)


SEM = pl.BlockSpec(memory_space=pltpu.SEMAPHORE)
EFFECT = pltpu.SideEffectType.DATAFLOW_SIDE_EFFECTING


def _exchange_start(x, name, gather, land=None):
    me = 4 * lax.axis_index("x") + 2 * lax.axis_index("y") + lax.axis_index("c")
    if gather:
        x = x[None]
        oshape = (NDEV,) + x.shape[1:]
        own = x
    else:
        oshape = x.shape
        own = lax.dynamic_index_in_dim(x, me, 0, keepdims=True)
    land = lax.empty(oshape, x.dtype) if land is None else land
    land = lax.dynamic_update_index_in_dim(land, own, me, 0)

    def blk(ref, k):
        return ref.at[pl.ds(k, 1)]

    def body(x_ref, land_ref, send_sems, recv_sems, x_thru, land_thru, token):
        ax, ay, ac, me = _mesh_pos()
        src = (lambda k: x_ref) if gather else (lambda k: blk(x_ref, k))
        for d in range(1, NDEV):
            peer, pidx = _peer(ax, ay, ac, d)
            pltpu.make_async_remote_copy(
                src_ref=src(pidx), dst_ref=blk(land_ref, me), send_sem=send_sems.at[d - 1],
                recv_sem=recv_sems.at[d - 1], device_id=peer, device_id_type=MESH).start()
        token[...] = jnp.zeros_like(token)

    return pl.pallas_call(
        body, name=name,
        out_shape=(pltpu.SemaphoreType.DMA((NDEV - 1,)), pltpu.SemaphoreType.DMA((NDEV - 1,)),
                   pltpu.HBM(x.shape, x.dtype), pltpu.HBM(oshape, x.dtype),
                   jax.ShapeDtypeStruct((8, 128), F32)),
        in_specs=(HBM, HBM), out_specs=(SEM, SEM, HBM, HBM, pl.BlockSpec(memory_space=pltpu.VMEM)),
        input_output_aliases={0: 2, 1: 3},
        compiler_params=pltpu.CompilerParams(has_side_effects=EFFECT),
    )(pltpu.with_memory_space_constraint(x, pltpu.HBM),
      pltpu.with_memory_space_constraint(land, pltpu.HBM))


def _exchange_wait(handle, after, name, gather):
    send_sems, recv_sems, x_thru, land_thru = handle

    def blk(ref, k):
        return ref.at[pl.ds(k, 1)]

    def body(x_ref, land_ref, send_sems, recv_sems, after_ref, x_dead, got_ref):
        ax, ay, ac, me = _mesh_pos()
        src = (lambda k: x_ref) if gather else (lambda k: blk(x_ref, k))
        for d in range(1, NDEV):
            peer, pidx = _peer(ax, ay, ac, d)
            cp = pltpu.make_async_remote_copy(
                src_ref=src(pidx), dst_ref=blk(land_ref, pidx), send_sem=send_sems.at[d - 1],
                recv_sem=recv_sems.at[d - 1], device_id=peer, device_id_type=MESH)
            cp.wait_send()
            cp.wait_recv()

    return pl.pallas_call(
        body, name=name,
        out_shape=(pltpu.HBM(x_thru.shape, x_thru.dtype), pltpu.HBM(land_thru.shape, land_thru.dtype)),
        in_specs=(HBM, HBM, SEM, SEM, pl.BlockSpec(memory_space=pl.ANY)), out_specs=(HBM, HBM),
        input_output_aliases={0: 0, 1: 1},
        compiler_params=pltpu.CompilerParams(has_side_effects=EFFECT),
    )(x_thru, land_thru, send_sems, recv_sems, after)[1]


_HOPS = (1, 2, 4, 6)


def _blk(ref, k):
    return ref.at[pl.ds(k, 1)]


def _alloc_many(shapes, dtype, name):
    def body(*refs):
        pass

    return pl.pallas_call(
        body, name=name, out_shape=tuple(pltpu.HBM(s_, dtype) for s_ in shapes),
        out_specs=tuple(HBM for _ in shapes),
    )()


def _gather_start(x, land, name):
    me = 4 * lax.axis_index("x") + 2 * lax.axis_index("y") + lax.axis_index("c")
    x = x[None]
    oshape = (NDEV,) + x.shape[1:]
    land = lax.dynamic_update_index_in_dim(land, x, me, 0)

    def body(x_ref, land_ref, send_sems, recv_sems, x_thru, land_thru, token):
        ax, ay, ac, me = _mesh_pos()
        for i, d in enumerate(_HOPS):
            peer, _ = _peer(ax, ay, ac, d)
            pltpu.make_async_remote_copy(
                src_ref=x_ref, dst_ref=_blk(land_ref, me), send_sem=send_sems.at[i],
                recv_sem=recv_sems.at[i], device_id=peer, device_id_type=MESH).start()
        token[...] = jnp.zeros_like(token)

    send1, recv1, x_thru, land_thru, token = pl.pallas_call(
        body, name=name,
        out_shape=(pltpu.SemaphoreType.DMA((4,)), pltpu.SemaphoreType.DMA((4,)),
                   pltpu.HBM(x.shape, x.dtype), pltpu.HBM(oshape, x.dtype),
                   jax.ShapeDtypeStruct((8, 128), F32)),
        in_specs=(HBM, HBM), out_specs=(SEM, SEM, HBM, HBM, pl.BlockSpec(memory_space=pltpu.VMEM)),
        input_output_aliases={0: 2, 1: 3},
        compiler_params=pltpu.CompilerParams(has_side_effects=EFFECT),
    )(pltpu.with_memory_space_constraint(x, pltpu.HBM),
      pltpu.with_memory_space_constraint(land, pltpu.HBM))
    return dict(send1=send1, recv1=recv1, x=x_thru, land=land_thru), token


def _gather_relay(hnd, after, name):
    def body(land_ref, recv1, after_ref, send2, recv2, land_thru, token):
        ax, ay, ac, me = _mesh_pos()
        sibling, _ = _peer(ax, ay, ac, 1)
        for i, d in enumerate(_HOPS[1:]):
            peer, pidx = _peer(ax, ay, ac, d)
            came = _blk(land_ref, pidx)
            pltpu.make_async_remote_copy(
                src_ref=came, dst_ref=came, send_sem=send2.at[i], recv_sem=recv1.at[i + 1],
                device_id=peer, device_id_type=MESH).wait_recv()
            pltpu.make_async_remote_copy(
                src_ref=came, dst_ref=came, send_sem=send2.at[i], recv_sem=recv2.at[i],
                device_id=sibling, device_id_type=MESH).start()
        token[...] = jnp.zeros_like(token)

    land = hnd["land"]
    send2, recv2, land_thru, token = pl.pallas_call(
        body, name=name,
        out_shape=(pltpu.SemaphoreType.DMA((3,)), pltpu.SemaphoreType.DMA((3,)),
                   pltpu.HBM(land.shape, land.dtype), jax.ShapeDtypeStruct((8, 128), F32)),
        in_specs=(HBM, SEM, ANYSPEC), out_specs=(SEM, SEM, HBM, pl.BlockSpec(memory_space=pltpu.VMEM)),
        input_output_aliases={0: 2},
        compiler_params=pltpu.CompilerParams(has_side_effects=EFFECT),
    )(land, hnd["recv1"], after)
    return dict(hnd, send2=send2, recv2=recv2, land=land_thru), token


def _gather_wait(hnd, after, name):
    def body(x_ref, land_ref, send1, recv1, send2, recv2, after_ref, x_dead, got_ref):
        ax, ay, ac, me = _mesh_pos()
        sibling, sidx = _peer(ax, ay, ac, 1)
        for i, d in enumerate(_HOPS):
            peer, pidx = _peer(ax, ay, ac, d)
            cp = pltpu.make_async_remote_copy(
                src_ref=x_ref, dst_ref=_blk(land_ref, pidx), send_sem=send1.at[i], recv_sem=recv1.at[i],
                device_id=peer, device_id_type=MESH)
            cp.wait_send()
            if i == 0:
                cp.wait_recv()
        for i, d in enumerate(_HOPS[1:]):
            _, pidx = _peer(ax, ay, ac, d)
            _, fidx = _peer(ax, ay, ac, d ^ 1)
            cp = pltpu.make_async_remote_copy(
                src_ref=_blk(land_ref, pidx), dst_ref=_blk(land_ref, fidx), send_sem=send2.at[i],
                recv_sem=recv2.at[i], device_id=sibling, device_id_type=MESH)
            cp.wait_send()
            cp.wait_recv()

    x, land = hnd["x"], hnd["land"]
    return pl.pallas_call(
        body, name=name,
        out_shape=(pltpu.HBM(x.shape, x.dtype), pltpu.HBM(land.shape, land.dtype)),
        in_specs=(HBM, HBM, SEM, SEM, SEM, SEM, ANYSPEC), out_specs=(HBM, HBM),
        input_output_aliases={0: 0, 1: 1},
        compiler_params=pltpu.CompilerParams(has_side_effects=EFFECT),
    )(x, land, hnd["send1"], hnd["recv1"], hnd["send2"], hnd["recv2"], after)[1]


def _mm(a, b, *, mode, grid, a_spec, b_spec, o_spec, o_shape, acc_shape, out_dtype, name):
    nred = grid[2]

    def body(a_ref, b_ref, o_ref, *scratch):
        p = lax.dot_general(a_ref[...], b_ref[...], _DIMS[mode], preferred_element_type=F32)
        if nred == 1:
            o_ref[...] = p.astype(o_ref.dtype)
            return
        acc = scratch[0]
        r = pl.program_id(2)

        @pl.when(r == 0)
        def _():
            acc[...] = p

        @pl.when(r > 0)
        def _():
            acc[...] += p

        @pl.when(r == nred - 1)
        def _():
            o_ref[...] = acc[...].astype(o_ref.dtype)

    return pl.pallas_call(
        body, name=name, out_shape=jax.ShapeDtypeStruct(o_shape, out_dtype), grid=grid,
        in_specs=[a_spec, b_spec], out_specs=o_spec,
        scratch_shapes=[] if nred == 1 else [pltpu.VMEM(acc_shape, F32)],
        compiler_params=_cparams(("parallel", "parallel", "arbitrary"), VMEM_BIG),
    )(a, b)


def _tile(n, cands):
    for c in cands:
        if n % c == 0:
            return c
    return n


def _mm_nn(a, b, out_dtype, name):
    m, k = a.shape
    n = b.shape[1]
    tm, tn = _tile(m, (512, 256, 128)), _tile(n, (1024, 640, 512, 256, 128))
    return _mm(a, b, mode="nn", grid=(m // tm, n // tn, 1),
               a_spec=pl.BlockSpec((tm, k), lambda i, j, r: (i, 0)),
               b_spec=pl.BlockSpec((k, tn), lambda i, j, r: (0, j)),
               o_spec=pl.BlockSpec((tm, tn), lambda i, j, r: (i, j)),
               o_shape=(m, n), acc_shape=None, out_dtype=out_dtype, name=name)


def _mm_nt(a, b, out_dtype, name):
    m, n = a.shape
    k = b.shape[0]
    tm, tn = _tile(m, (512, 256, 128)), _tile(n, (1024, 640, 512, 256, 128))
    return _mm(a, b, mode="nt", grid=(m // tm, 1, n // tn),
               a_spec=pl.BlockSpec((tm, tn), lambda i, j, r: (i, r)),
               b_spec=pl.BlockSpec((k, tn), lambda i, j, r: (0, r)),
               o_spec=pl.BlockSpec((tm, k), lambda i, j, r: (i, 0)),
               o_shape=(m, k), acc_shape=(tm, k), out_dtype=out_dtype, name=name)


def _mm_tn(a, b, out_dtype, name):
    s, k = a.shape
    n = b.shape[1]
    ts, tn = _tile(s, (2048, 1024, 512, 256, 128)), _tile(n, (640, 512, 256, 128))
    return _mm(a, b, mode="tn", grid=(1, n // tn, s // ts),
               a_spec=pl.BlockSpec((ts, k), lambda i, j, r: (r, 0)),
               b_spec=pl.BlockSpec((ts, tn), lambda i, j, r: (r, j)),
               o_spec=pl.BlockSpec((k, tn), lambda i, j, r: (0, j)),
               o_shape=(k, n), acc_shape=(k, tn), out_dtype=out_dtype, name=name)


def _gu_spec(tm, idx):
    return pl.BlockSpec((2, None, tm, FFB), idx)


def _ffn_up(h, wi, name):
    s = h.shape[0]
    tm = _tile(s, (1024, 512, 256, 128))

    def body(h_ref, wg_ref, wu_ref, gu_ref, act_ref):
        hv = h_ref[...]
        g = jnp.dot(hv, wg_ref[...], preferred_element_type=F32)
        u = jnp.dot(hv, wu_ref[...], preferred_element_type=F32)
        gu_ref[0] = g.astype(BF16)
        gu_ref[1] = u.astype(BF16)
        act_ref[...] = (g * _sigmoid(g) * u).astype(BF16)

    wspec = lambda off: pl.BlockSpec((None, D, FFB), lambda j, i: (j + off, 0, 0))
    return pl.pallas_call(
        body, name=name,
        out_shape=(jax.ShapeDtypeStruct((2, 4, s, FFB), BF16), jax.ShapeDtypeStruct((4, s, FFB), BF16)),
        grid=(4, s // tm),
        in_specs=[pl.BlockSpec((tm, D), lambda j, i: (i, 0)), wspec(0), wspec(4)],
        out_specs=(_gu_spec(tm, lambda j, i: (0, j, i, 0)), pl.BlockSpec((None, tm, FFB), lambda j, i: (j, i, 0))),
        compiler_params=_cparams(("parallel", "parallel"), VMEM_BIG),
    )(h, wi, wi)


def _ffn_down(act, wo, name):
    s = act.shape[1]
    tm = _tile(s, (1024, 512, 256, 128))
    return _mm(act, wo, mode="nn", grid=(s // tm, 1, 4),
               a_spec=pl.BlockSpec((None, tm, FFB), lambda i, j, r: (r, i, 0)),
               b_spec=pl.BlockSpec((None, FFB, D), lambda i, j, r: (r, 0, 0)),
               o_spec=pl.BlockSpec((tm, D), lambda i, j, r: (i, 0)),
               o_shape=(s, D), acc_shape=(tm, D), out_dtype=F32, name=name)


def _ffn_down_bwd(dy, gu, wo, name):
    s = dy.shape[0]
    tm = _tile(s, (512, 256, 128))
    ns = s // tm

    def body(dy_ref, gu_ref, wo_ref, da_ref, dwo_ref, acc):
        i = pl.program_id(1)
        dyv = dy_ref[...]
        dact = lax.dot_general(dyv, wo_ref[...], _NT, preferred_element_type=F32)
        gv = gu_ref[0].astype(F32)
        uv = gu_ref[1].astype(F32)
        sg = _sigmoid(gv)
        silu = gv * sg
        da_ref[0] = (dact * uv * (sg * (1.0 + gv * (1.0 - sg)))).astype(BF16)
        da_ref[1] = (dact * silu).astype(BF16)
        p = lax.dot_general((silu * uv).astype(BF16), dyv, _TN, preferred_element_type=F32)

        @pl.when(i == 0)
        def _():
            acc[...] = p

        @pl.when(i > 0)
        def _():
            acc[...] += p

        @pl.when(i == ns - 1)
        def _():
            dwo_ref[...] = acc[...].astype(BF16)

    aspec = _gu_spec(tm, lambda j, i: (0, j, i, 0))
    return pl.pallas_call(
        body, name=name,
        out_shape=(jax.ShapeDtypeStruct((2, 4, s, FFB), BF16), jax.ShapeDtypeStruct((4, FFB, D), BF16)),
        grid=(4, ns),
        in_specs=[pl.BlockSpec((tm, D), lambda j, i: (i, 0)), aspec,
                  pl.BlockSpec((None, FFB, D), lambda j, i: (j, 0, 0))],
        out_specs=(aspec, pl.BlockSpec((None, FFB, D), lambda j, i: (j, 0, 0))),
        scratch_shapes=[pltpu.VMEM((FFB, D), F32)],
        compiler_params=_cparams(("parallel", "arbitrary"), VMEM_BIG),
    )(dy, gu, wo)


def _ffn_dh(da, wi, name):
    s = da.shape[2]
    tm = _tile(s, (1024, 512, 256, 128))

    def body(da_ref, wg_ref, wu_ref, o_ref, acc):
        r = pl.program_id(1)
        p = (lax.dot_general(da_ref[0], wg_ref[...], _NT, preferred_element_type=F32)
             + lax.dot_general(da_ref[1], wu_ref[...], _NT, preferred_element_type=F32))

        @pl.when(r == 0)
        def _():
            acc[...] = p

        @pl.when(r > 0)
        def _():
            acc[...] += p

        @pl.when(r == 3)
        def _():
            o_ref[...] = acc[...]

    wspec = lambda off: pl.BlockSpec((None, D, FFB), lambda i, r: (r + off, 0, 0))
    return pl.pallas_call(
        body, name=name, out_shape=jax.ShapeDtypeStruct((s, D), F32), grid=(s // tm, 4),
        in_specs=[_gu_spec(tm, lambda i, r: (0, r, i, 0)), wspec(0), wspec(4)],
        out_specs=pl.BlockSpec((tm, D), lambda i, r: (i, 0)),
        scratch_shapes=[pltpu.VMEM((tm, D), F32)],
        compiler_params=_cparams(("parallel", "arbitrary"), VMEM_BIG),
    )(da, wi, wi)


def _ffn_dwi(h, da, name):
    s = h.shape[0]
    ts = _tile(s, (2048, 1024, 512, 256, 128))
    return _mm(h, da, mode="tn", grid=(NDEV, 1, s // ts),
               a_spec=pl.BlockSpec((ts, D), lambda k, j, r: (r, 0)),
               b_spec=pl.BlockSpec((None, ts, FFB), lambda k, j, r: (k, r, 0)),
               o_spec=pl.BlockSpec((None, D, FFB), lambda k, j, r: (k, 0, 0)),
               o_shape=(NDEV, D, FFB), acc_shape=(D, FFB), out_dtype=BF16, name=name)


TR = 256


def _rows_spec(s):
    tr = _tile(s, (TR, 128))
    return tr, pl.BlockSpec((tr, D), lambda i: (i, 0))


def _pspec(l, sub):
    return pl.BlockSpec((None, None, 8, D), lambda i: (l, sub, 0, 0))


def _pre_math(x, p_ref):
    r = lax.rsqrt(jnp.mean(x * x, axis=1, keepdims=True) + RMS_EPS)
    return (x * r) * p_ref[0:1, :] * (1.0 + p_ref[1:2, :]) + p_ref[2:3, :]


ANYSPEC = pl.BlockSpec(memory_space=pl.ANY)


def _prenorm(x, ptab, l, sub, name, toks=()):
    s = x.shape[0]
    tr, spec = _rows_spec(s)

    def body(x_ref, p_ref, *rest):
        rest[-1][...] = _pre_math(x_ref[...], p_ref).astype(BF16)

    return pl.pallas_call(
        body, name=name, out_shape=jax.ShapeDtypeStruct((s, D), BF16), grid=(s // tr,),
        in_specs=[spec, _pspec(l, sub)] + [ANYSPEC] * len(toks), out_specs=spec,
        compiler_params=_cparams(("parallel",)),
    )(x, ptab, *toks)


def _resid(x, y, ptab, l, sub, coef, nxt, name, toks=()):
    s = x.shape[0]
    tr, spec = _rows_spec(s)

    def body(x_ref, y_ref, p_ref, *rest):
        yv = y_ref[...]
        r = lax.rsqrt(jnp.mean(yv * yv, axis=1, keepdims=True) + RMS_EPS)
        xn = x_ref[...] + (coef * p_ref[4:5, :]) * ((yv * r) * p_ref[3:4, :])
        if nxt is None:
            rest[-1][...] = xn
        else:
            rest[-2][...] = xn
            rest[-1][...] = _pre_math(xn, rest[0]).astype(BF16)

    tspecs = [ANYSPEC] * len(toks)
    if nxt is None:
        return pl.pallas_call(
            body, name=name, out_shape=jax.ShapeDtypeStruct((s, D), F32), grid=(s // tr,),
            in_specs=[spec, spec, _pspec(l, sub)] + tspecs, out_specs=spec,
            compiler_params=_cparams(("parallel",)),
        )(x, y, ptab, *toks), None
    return pl.pallas_call(
        body, name=name,
        out_shape=(jax.ShapeDtypeStruct((s, D), F32), jax.ShapeDtypeStruct((s, D), BF16)),
        grid=(s // tr,),
        in_specs=[spec, spec, _pspec(l, sub), _pspec(*nxt)] + tspecs, out_specs=(spec, spec),
        compiler_params=_cparams(("parallel",)),
    )(x, y, ptab, ptab, *toks)


def _loss_grad(x, tgt, name):
    s = x.shape[0]
    tr, spec = _rows_spec(s)

    def body(x_ref, t_ref, dx_ref, l_ref):
        @pl.when(pl.program_id(0) == 0)
        def _():
            l_ref[...] = jnp.zeros_like(l_ref)

        e = x_ref[...] - t_ref[...]
        dx_ref[...] = e * (1.0 / D)
        l_ref[0:1, :] += jnp.sum(e * e, axis=0, keepdims=True) * (0.5 / D)

    return pl.pallas_call(
        body, name=name,
        out_shape=(jax.ShapeDtypeStruct((s, D), F32), jax.ShapeDtypeStruct((8, D), F32)),
        grid=(s // tr,), in_specs=[spec, spec],
        out_specs=(spec, pl.BlockSpec((8, D), lambda i: (0, 0))),
        compiler_params=_cparams(("arbitrary",)),
    )(x, tgt)


def _post_bwd(dx, y, ptab, l, sub, coef, name, toks=()):
    s = dx.shape[0]
    tr, spec = _rows_spec(s)

    def body(dx_ref, y_ref, p_ref, *rest):
        dy_ref, red_ref = rest[-2:]

        @pl.when(pl.program_id(0) == 0)
        def _():
            red_ref[...] = jnp.zeros_like(red_ref)

        dxv, yv = dx_ref[...], y_ref[...]
        gpost, gate = p_ref[3:4, :], p_ref[4:5, :]
        r = lax.rsqrt(jnp.mean(yv * yv, axis=1, keepdims=True) + RMS_EPS)
        yhat = yv * r
        red_ref[0:1, :] += jnp.sum(dxv * yhat * gpost, axis=0, keepdims=True) * coef
        dn = dxv * (coef * gate)
        red_ref[1:2, :] += jnp.sum(dn * yhat, axis=0, keepdims=True)
        dyh = dn * gpost
        dy_ref[...] = (r * (dyh - yhat * jnp.mean(dyh * yhat, axis=1, keepdims=True))).astype(BF16)

    return pl.pallas_call(
        body, name=name,
        out_shape=(jax.ShapeDtypeStruct((s, D), BF16), jax.ShapeDtypeStruct((8, D), F32)),
        grid=(s // tr,), in_specs=[spec, spec, _pspec(l, sub)] + [ANYSPEC] * len(toks),
        out_specs=(spec, pl.BlockSpec((8, D), lambda i: (0, 0))),
        compiler_params=_cparams(("arbitrary",)),
    )(dx, y, ptab, *toks)


def _pre_bwd(dx, dh, x, ptab, l, sub, name, dh2=None):
    s = dx.shape[0]
    tr, spec = _rows_spec(s)
    extra = [] if dh2 is None else [dh2]

    def body(dx_ref, dh_ref, x_ref, p_ref, *rest):
        o_ref, red_ref = rest[-2:]

        @pl.when(pl.program_id(0) == 0)
        def _():
            red_ref[...] = jnp.zeros_like(red_ref)

        dhv, xv = dh_ref[...], x_ref[...]
        if extra:
            dhv = dhv + rest[0][...]
        gpre, scale = p_ref[0:1, :], p_ref[1:2, :]
        r = lax.rsqrt(jnp.mean(xv * xv, axis=1, keepdims=True) + RMS_EPS)
        xhat = xv * r
        red_ref[0:1, :] += jnp.sum(dhv, axis=0, keepdims=True)
        red_ref[1:2, :] += jnp.sum(dhv * xhat * gpre, axis=0, keepdims=True)
        red_ref[2:3, :] += jnp.sum(dhv * xhat * (1.0 + scale), axis=0, keepdims=True)
        dxh = dhv * (gpre * (1.0 + scale))
        o_ref[...] = dx_ref[...] + r * (dxh - xhat * jnp.mean(dxh * xhat, axis=1, keepdims=True))

    return pl.pallas_call(
        body, name=name,
        out_shape=(jax.ShapeDtypeStruct((s, D), F32), jax.ShapeDtypeStruct((8, D), F32)),
        grid=(s // tr,), in_specs=[spec, spec, spec, _pspec(l, sub)] + [spec] * len(extra),
        out_specs=(spec, pl.BlockSpec((8, D), lambda i: (0, 0))),
        compiler_params=_cparams(("arbitrary",)),
    )(dx, dh, x, ptab, *extra)


def _cond_fwd(c_pad, wc, bc, name):
    w = wc.shape[2]

    def body(c_ref, w_ref, b_ref, o_ref, ca_ref):
        cv = c_ref[...]
        ca = (cv * _sigmoid(cv)).astype(BF16)
        ca_ref[...] = ca
        o_ref[...] = jnp.dot(ca, w_ref[...].astype(BF16), preferred_element_type=F32) + b_ref[...]

    return pl.pallas_call(
        body, name=name,
        out_shape=(jax.ShapeDtypeStruct((DEPTH, 128, w), F32), jax.ShapeDtypeStruct((128, D), BF16)),
        grid=(DEPTH,),
        in_specs=[pl.BlockSpec((128, D), lambda i: (0, 0)),
                  pl.BlockSpec((None, D, w), lambda i: (i, 0, 0)),
                  pl.BlockSpec((None, 1, w), lambda i: (i, 0, 0))],
        out_specs=(pl.BlockSpec((None, 128, w), lambda i: (i, 0, 0)),
                   pl.BlockSpec((128, D), lambda i: (0, 0))),
        compiler_params=_cparams(("arbitrary",), VMEM_BIG),
    )(c_pad, wc, bc)


def _cond_bwd(ca_pad, dmod, name):
    w = dmod.shape[2]
    return _mm(ca_pad, dmod, mode="tn", grid=(DEPTH, 1, 1),
               a_spec=pl.BlockSpec((128, D), lambda i, j, r: (0, 0)),
               b_spec=pl.BlockSpec((None, 128, w), lambda i, j, r: (i, 0, 0)),
               o_spec=pl.BlockSpec((None, D, w), lambda i, j, r: (i, 0, 0)),
               o_shape=(DEPTH, D, w), acc_shape=None, out_dtype=F32, name=name)


def _split3(x):
    hi = x.astype(BF16)
    r1 = x - hi.astype(F32)
    mid = r1.astype(BF16)
    lo = (r1 - mid.astype(F32)).astype(BF16)
    return hi, mid, lo


def _tri_dot(t, x):
    hi, mid, lo = _split3(x)
    return (jnp.dot(t, hi, preferred_element_type=F32) + jnp.dot(t, mid, preferred_element_type=F32)
            + jnp.dot(t, lo, preferred_element_type=F32))


def _fox_cum(fl, bf, name):
    s = fl.shape[0]
    tb = _tile(s, (256, 128))

    def body(fl_ref, b_ref, cum_ref):
        row = lax.broadcasted_iota(jnp.int32, (tb, tb), 0)
        col = lax.broadcasted_iota(jnp.int32, (tb, tb), 1)
        tri = (col <= row).astype(BF16)
        carry = jnp.zeros((1, 128), F32)
        for blk in range(s // tb):
            z = fl_ref[blk * tb:(blk + 1) * tb, :] + b_ref[0:1, :]
            lf = jnp.minimum(z, 0.0) - jnp.log(1.0 + jnp.exp(-jnp.abs(z)))
            cum_ref[blk * tb:(blk + 1) * tb, :] = _tri_dot(tri, lf) + carry
            carry = carry + jnp.sum(lf, axis=0, keepdims=True)

    return pl.pallas_call(
        body, name=name, out_shape=jax.ShapeDtypeStruct((s, 128), F32),
    )(fl, bf)


def _fox_cum_bwd(dcum, fl, bf, name):
    s = fl.shape[0]
    tb = _tile(s, (256, 128))

    def body(dc_ref, fl_ref, b_ref, dfl_ref, db_ref):
        row = lax.broadcasted_iota(jnp.int32, (tb, tb), 0)
        col = lax.broadcasted_iota(jnp.int32, (tb, tb), 1)
        tri = (col >= row).astype(BF16)
        carry = jnp.zeros((1, 128), F32)
        dbs = jnp.zeros((1, 128), F32)
        for blk in reversed(range(s // tb)):
            dc = dc_ref[blk * tb:(blk + 1) * tb, :]
            dl = _tri_dot(tri, dc) + carry
            carry = carry + jnp.sum(dc, axis=0, keepdims=True)
            z = fl_ref[blk * tb:(blk + 1) * tb, :] + b_ref[0:1, :]
            dz = dl * _sigmoid(-z)
            dfl_ref[blk * tb:(blk + 1) * tb, :] = dz
            dbs = dbs + jnp.sum(dz, axis=0, keepdims=True)
        db_ref[...] = jnp.broadcast_to(dbs, (8, 128))

    return pl.pallas_call(
        body, name=name,
        out_shape=(jax.ShapeDtypeStruct((s, 128), F32), jax.ShapeDtypeStruct((8, 128), F32)),
    )(dcum, fl, bf)


def _fox_qkv(h, w_qkv, name):
    s = h.shape[0]
    tm = _tile(s, (512, 256, 128))
    return _mm(h, w_qkv, mode="nn", grid=(s // tm, 3, 1),
               a_spec=pl.BlockSpec((tm, D), lambda i, j, r: (i, 0)),
               b_spec=pl.BlockSpec((D, D), lambda i, j, r: (0, j)),
               o_spec=pl.BlockSpec((None, tm, D), lambda i, j, r: (j, i, 0)),
               o_shape=(3, s, D), acc_shape=None, out_dtype=BF16, name=name)


def _fox_qkv_dx(dqkv, w_qkv, name):
    s = dqkv.shape[1]
    tm = _tile(s, (512, 256, 128))
    return _mm(dqkv, w_qkv, mode="nt", grid=(s // tm, 1, 3),
               a_spec=pl.BlockSpec((None, tm, D), lambda i, j, r: (r, i, 0)),
               b_spec=pl.BlockSpec((D, D), lambda i, j, r: (0, r)),
               o_spec=pl.BlockSpec((tm, D), lambda i, j, r: (i, 0)),
               o_shape=(s, D), acc_shape=(tm, D), out_dtype=F32, name=name)


def _fox_qkv_dw(h, dqkv, name):
    s = h.shape[0]
    ts = _tile(s, (2048, 1024, 512, 256, 128))
    return _mm(h, dqkv, mode="tn", grid=(3, 1, s // ts),
               a_spec=pl.BlockSpec((ts, D), lambda i, j, r: (r, 0)),
               b_spec=pl.BlockSpec((None, ts, D), lambda i, j, r: (i, r, 0)),
               o_spec=pl.BlockSpec((None, D, D), lambda i, j, r: (i, 0, 0)),
               o_shape=(3, D, D), acc_shape=(D, D), out_dtype=BF16, name=name)


HPAIRS = HEADS // 2


def _first_head():
    return lax.broadcasted_iota(jnp.int32, (1, 2 * HDIM), 1) < HDIM


def _one_head(x, sel):
    return jnp.where(sel, x, jnp.zeros_like(x))


def _fox_scores(qm, k_ref, cq, ck, qi, tq, n):
    sc = lax.dot_general(qm, k_ref[0:n, :], _NT, preferred_element_type=F32) * (HDIM ** -0.5)
    sc = sc + cq - ck
    row = lax.broadcasted_iota(jnp.int32, (tq, n), 0) + qi * tq
    col = lax.broadcasted_iota(jnp.int32, (tq, n), 1)
    return sc, col <= row


def _fox_specs(s):
    sect = lambda i: pl.BlockSpec((None, s, 2 * HDIM), lambda p: (i, 0, p))
    ospec = pl.BlockSpec((s, 2 * HDIM), lambda p: (0, p))
    cspec = pl.BlockSpec((None, 2, s, 1), lambda p: (p, 0, 0, 0))
    rspec = pl.BlockSpec((None, 2, 1, s), lambda p: (p, 0, 0, 0))
    return sect, ospec, cspec, rspec


def _fox_attn_fwd(qkv, cq, ck, name):
    s = qkv.shape[1]
    tq = _tile(s, (256, 128))

    def body(q_ref, k_ref, v_ref, cq_ref, ck_ref, o_ref, lse_ref):
        first = _first_head()
        for qi in range(s // tq):
            n = (qi + 1) * tq
            rows = slice(qi * tq, n)
            q2 = q_ref[rows, :]
            outs = []
            for hh in range(2):
                sel = first if hh == 0 else jnp.logical_not(first)
                sc, keep = _fox_scores(_one_head(q2, sel), k_ref, cq_ref[hh, rows, :], ck_ref[hh, :, 0:n], qi, tq, n)
                sc = jnp.where(keep, sc, -1e30)
                m = jnp.max(sc, axis=1, keepdims=True)
                p = jnp.exp(sc - m)
                lsum = jnp.sum(p, axis=1, keepdims=True)
                outs.append(jnp.dot(p.astype(BF16), v_ref[0:n, :], preferred_element_type=F32) / lsum)
                lse_ref[hh, rows, :] = m + jnp.log(lsum)
            o_ref[rows, :] = jnp.where(first, outs[0], outs[1]).astype(BF16)

    sect, ospec, cspec, rspec = _fox_specs(s)
    return pl.pallas_call(
        body, name=name,
        out_shape=(jax.ShapeDtypeStruct((s, D), BF16), jax.ShapeDtypeStruct((HPAIRS, 2, s, 1), F32)),
        grid=(HPAIRS,), in_specs=[sect(0), sect(1), sect(2), cspec, rspec], out_specs=(ospec, cspec),
        compiler_params=_cparams(("parallel",), VMEM_BIG),
    )(qkv, qkv, qkv, cq, ck)


def _fox_attn_bwd(qkv, do, lse, cq, ck, name):
    s = qkv.shape[1]
    tq = _tile(s, (256, 128))
    scale = HDIM ** -0.5

    def body(q_ref, k_ref, v_ref, do_ref, lse_ref, cq_ref, ck_ref,
             dqkv_ref, dcq_ref, dck_ref, dk_acc, dv_acc, dck_acc):
        first = _first_head()
        dk_acc[...] = jnp.zeros_like(dk_acc)
        dv_acc[...] = jnp.zeros_like(dv_acc)
        dck_acc[...] = jnp.zeros_like(dck_acc)
        for qi in range(s // tq):
            n = (qi + 1) * tq
            rows = slice(qi * tq, n)
            q2, do2 = q_ref[rows, :], do_ref[rows, :]
            dq, dk, dv = [], [], []
            for hh in range(2):
                sel = first if hh == 0 else jnp.logical_not(first)
                sc, keep = _fox_scores(_one_head(q2, sel), k_ref, cq_ref[hh, rows, :], ck_ref[hh, :, 0:n], qi, tq, n)
                p = jnp.where(keep, jnp.exp(sc - lse_ref[hh, rows, :]), 0.0)
                dp = lax.dot_general(_one_head(do2, sel), v_ref[0:n, :], _NT, preferred_element_type=F32)
                ds = p * (dp - jnp.sum(p * dp, axis=1, keepdims=True))
                dsb = ds.astype(BF16)
                dq.append(jnp.dot(dsb, k_ref[0:n, :], preferred_element_type=F32))
                dk.append(lax.dot_general(dsb, q2, _TN, preferred_element_type=F32))
                dv.append(lax.dot_general(p.astype(BF16), do2, _TN, preferred_element_type=F32))
                dcq_ref[hh, rows, :] = jnp.sum(ds, axis=1, keepdims=True)
                dck_acc[hh, :, 0:n] -= jnp.sum(ds, axis=0, keepdims=True)
            dqkv_ref[0, rows, :] = (jnp.where(first, dq[0], dq[1]) * scale).astype(BF16)
            dk_acc[0:n, :] += jnp.where(first, dk[0], dk[1]) * scale
            dv_acc[0:n, :] += jnp.where(first, dv[0], dv[1])
        dqkv_ref[1] = dk_acc[...].astype(BF16)
        dqkv_ref[2] = dv_acc[...].astype(BF16)
        dck_ref[...] = dck_acc[...]

    sect, ospec, cspec, rspec = _fox_specs(s)
    return pl.pallas_call(
        body, name=name,
        out_shape=(jax.ShapeDtypeStruct((3, s, D), BF16), jax.ShapeDtypeStruct((HPAIRS, 2, s, 1), F32),
                   jax.ShapeDtypeStruct((HPAIRS, 2, 1, s), F32)),
        grid=(HPAIRS,), in_specs=[sect(0), sect(1), sect(2), ospec, cspec, cspec, rspec],
        out_specs=(pl.BlockSpec((3, s, 2 * HDIM), lambda p: (0, 0, p)), cspec, rspec),
        scratch_shapes=[pltpu.VMEM((s, 2 * HDIM), F32), pltpu.VMEM((s, 2 * HDIM), F32), pltpu.VMEM((2, 1, s), F32)],
        compiler_params=_cparams(("parallel",), VMEM_BIG),
    )(qkv, qkv, qkv, do, lse, cq, ck)


TC = 256
HALO = 8


def _chunk_specs(s):
    tc = _tile(s, (TC, 128))
    per = tc // HALO
    nblk = s // HALO
    cur = pl.BlockSpec((tc, 8, 128), lambda i: (i, 0, 0))
    past = pl.BlockSpec((HALO, 8, 128), lambda i: (jnp.maximum(i * per - 1, 0), 0, 0))
    future = pl.BlockSpec((HALO, 8, 128), lambda i: (jnp.minimum((i + 1) * per, nblk - 1), 0, 0))
    return tc, cur, past, future


def _vec_spec(n):
    return pl.BlockSpec((n, 8, 128), lambda i: (0, 0, 0))


def _conv_past(buf, w_ref, kw, tc):
    out = w_ref[kw - 1] * buf[HALO:HALO + tc]
    for k in range(kw - 1):
        off = HALO - (kw - 1) + k
        out = out + w_ref[k] * buf[off:off + tc]
    return out


def _sconv_fwd(bg, cg, xv, w, name):
    s = bg.shape[0]
    tc, cur, past, _ = _chunk_specs(s)

    def body(bg_ref, cg_ref, xv_ref, cgp_ref, xvp_ref, w_ref, y_ref, zbuf):
        first = pl.program_id(0) == 0
        zbuf[0:HALO] = jnp.where(first, 0.0, cgp_ref[...] * xvp_ref[...])
        zbuf[HALO:HALO + tc] = cg_ref[...] * xv_ref[...]
        y_ref[...] = bg_ref[...] * _conv_past(zbuf, w_ref, 3, tc)

    return pl.pallas_call(
        body, name=name, out_shape=jax.ShapeDtypeStruct((s, 8, 128), F32), grid=(s // tc,),
        in_specs=[cur, cur, cur, past, past, _vec_spec(3)], out_specs=cur,
        scratch_shapes=[pltpu.VMEM((tc + HALO, 8, 128), F32)],
        compiler_params=_cparams(("parallel",)),
    )(bg, cg, xv, cg, xv, w)


def _sconv_bwd(dy, bg, cg, xv, w, name):
    s = dy.shape[0]
    tc, cur, past, future = _chunk_specs(s)
    nch = s // tc

    def body(dy_ref, bg_ref, cg_ref, xv_ref, cgp_ref, xvp_ref, dyf_ref, bgf_ref, w_ref,
             dbg_ref, dcg_ref, dxv_ref, dw_ref, zbuf, dbuf):
        i = pl.program_id(0)

        @pl.when(i == 0)
        def _():
            dw_ref[...] = jnp.zeros_like(dw_ref)

        z = cg_ref[...] * xv_ref[...]
        zbuf[0:HALO] = jnp.where(i == 0, 0.0, cgp_ref[...] * xvp_ref[...])
        zbuf[HALO:HALO + tc] = z
        dyv = dy_ref[...]
        dbg_ref[...] = dyv * _conv_past(zbuf, w_ref, 3, tc)
        dbuf[0:tc] = dyv * bg_ref[...]
        dbuf[tc:tc + HALO] = jnp.where(i == nch - 1, 0.0, dyf_ref[...] * bgf_ref[...])
        dz = jnp.zeros((tc, 8, 128), F32)
        for k in range(3):
            sh = dbuf[2 - k:2 - k + tc]
            dz = dz + w_ref[k] * sh
            dw_ref[k] += jnp.sum(z * sh, axis=0)
        dcg_ref[...] = dz * xv_ref[...]
        dxv_ref[...] = dz * cg_ref[...]

    shp = jax.ShapeDtypeStruct((s, 8, 128), F32)
    return pl.pallas_call(
        body, name=name, out_shape=(shp, shp, shp, jax.ShapeDtypeStruct((3, 8, 128), F32)),
        grid=(nch,),
        in_specs=[cur, cur, cur, cur, past, past, future, future, _vec_spec(3)],
        out_specs=(cur, cur, cur, _vec_spec(3)),
        scratch_shapes=[pltpu.VMEM((tc + HALO, 8, 128), F32), pltpu.VMEM((tc + HALO, 8, 128), F32)],
        compiler_params=_cparams(("arbitrary",)),
    )(dy, bg, cg, xv, cg, xv, dy, bg, w)


def _lru_conv_fwd(xp, wb, name):
    s = xp.shape[0]
    tc, cur, past, _ = _chunk_specs(s)

    def body(x_ref, xp_ref, w_ref, o_ref, buf):
        buf[0:HALO] = jnp.where(pl.program_id(0) == 0, 0.0, xp_ref[...])
        buf[HALO:HALO + tc] = x_ref[...]
        o_ref[...] = _conv_past(buf, w_ref, 4, tc) + w_ref[4]

    return pl.pallas_call(
        body, name=name, out_shape=jax.ShapeDtypeStruct((s, 8, 128), F32), grid=(s // tc,),
        in_specs=[cur, past, _vec_spec(8)], out_specs=cur,
        scratch_shapes=[pltpu.VMEM((tc + HALO, 8, 128), F32)],
        compiler_params=_cparams(("parallel",)),
    )(xp, xp, wb)


def _lru_conv_bwd(dxb, xp, wb, name):
    s = dxb.shape[0]
    tc, cur, _, future = _chunk_specs(s)
    nch = s // tc

    def body(d_ref, df_ref, x_ref, w_ref, o_ref, red_ref, dbuf):
        i = pl.program_id(0)

        @pl.when(i == 0)
        def _():
            red_ref[...] = jnp.zeros_like(red_ref)

        dv = d_ref[...]
        dbuf[0:tc] = dv
        dbuf[tc:tc + HALO] = jnp.where(i == nch - 1, 0.0, df_ref[...])
        xv = x_ref[...]
        dx = jnp.zeros((tc, 8, 128), F32)
        for k in range(4):
            sh = dbuf[3 - k:3 - k + tc]
            dx = dx + w_ref[k] * sh
            red_ref[k] += jnp.sum(xv * sh, axis=0)
        red_ref[4] += jnp.sum(dv, axis=0)
        o_ref[...] = dx

    return pl.pallas_call(
        body, name=name,
        out_shape=(jax.ShapeDtypeStruct((s, 8, 128), F32), jax.ShapeDtypeStruct((8, 8, 128), F32)),
        grid=(nch,), in_specs=[cur, future, cur, _vec_spec(8)], out_specs=(cur, _vec_spec(8)),
        scratch_shapes=[pltpu.VMEM((tc + HALO, 8, 128), F32)],
        compiler_params=_cparams(("arbitrary",)),
    )(dxb, dxb, xp, wb)


def _lru_gates(ra, ia, pv_ref):
    sp = _softplus(-pv_ref[2])
    r = _sigmoid(ra + pv_ref[0])
    ig = _sigmoid(ia + pv_ref[1])
    log_a = (-LRU_C) * r * sp
    a = jnp.exp(log_a)
    mult = jnp.sqrt(-jnp.tanh(log_a) * (a * a + 1.0))
    return r, ig, a, mult, sp


def _lru_scan_fwd(ra, ia, xb, gate, pv, name):
    s = ra.shape[0]
    tc, cur, _, _ = _chunk_specs(s)

    def body(ra_ref, ia_ref, xb_ref, g_ref, pv_ref, hs_ref, hp_ref, y_ref, abuf, bbuf, hcar):
        @pl.when(pl.program_id(0) == 0)
        def _():
            hcar[...] = jnp.zeros_like(hcar)

        xbv = xb_ref[...]
        _, ig, a, mult, _ = _lru_gates(ra_ref[...], ia_ref[...], pv_ref)
        abuf[...] = a
        bbuf[...] = mult * (ig * xbv)

        def step(t, h):
            hp_ref[t] = h
            h = abuf[t] * h + bbuf[t]
            hs_ref[t] = h
            return h

        hcar[...] = lax.fori_loop(0, tc, step, hcar[...], unroll=8)
        y_ref[...] = hs_ref[...] * _gelu_parts(g_ref[...])[0]

    shp = jax.ShapeDtypeStruct((s, 8, 128), F32)
    return pl.pallas_call(
        body, name=name, out_shape=(shp, shp, shp), grid=(s // tc,),
        in_specs=[cur, cur, cur, cur, _vec_spec(8)], out_specs=(cur, cur, cur),
        scratch_shapes=[pltpu.VMEM((tc, 8, 128), F32), pltpu.VMEM((tc, 8, 128), F32),
                        pltpu.VMEM((8, 128), F32)],
        compiler_params=_cparams(("arbitrary",)),
    )(ra, ia, xb, gate, pv)


def _lru_scan_bwd(dy, ra, ia, xb, gate, hs, hp, pv, name):
    s = dy.shape[0]
    tc = _tile(s, (TC, 128))
    nch = s // tc
    rev = pl.BlockSpec((tc, 8, 128), lambda i: (nch - 1 - i, 0, 0))

    def body(dy_ref, ra_ref, ia_ref, xb_ref, g_ref, hs_ref, hp_ref, pv_ref,
             dg_ref, dra_ref, dia_ref, dxb_ref, red_ref, abuf, dbuf, gbuf, car):
        @pl.when(pl.program_id(0) == 0)
        def _():
            red_ref[...] = jnp.zeros_like(red_ref)
            car[...] = jnp.zeros_like(car)

        xbv = xb_ref[...]
        r, ig, a, mult, sp = _lru_gates(ra_ref[...], ia_ref[...], pv_ref)
        ge, dge = _gelu_parts(g_ref[...])
        dyv = dy_ref[...]
        dg_ref[...] = dyv * hs_ref[...] * dge
        abuf[...] = a
        dbuf[...] = dyv * ge

        def step(k, c):
            t = tc - 1 - k
            g = dbuf[t] + c
            gbuf[t] = g
            return abuf[t] * g

        car[...] = lax.fori_loop(0, tc, step, car[...], unroll=8)
        g = gbuf[...]
        d_a = g * hp_ref[...]
        d_m = g * (ig * xbv)
        d_loga = d_a * a - d_m * (a * a / mult)
        dxb_ref[...] = g * mult * ig
        dra = d_loga * ((-LRU_C) * sp) * r * (1.0 - r)
        dia = g * mult * xbv * ig * (1.0 - ig)
        dra_ref[...] = dra
        dia_ref[...] = dia
        red_ref[0] += jnp.sum(dra, axis=0)
        red_ref[1] += jnp.sum(dia, axis=0)
        red_ref[2] += jnp.sum(d_loga * r, axis=0) * (LRU_C * _sigmoid(-pv_ref[2]))

    shp = jax.ShapeDtypeStruct((s, 8, 128), F32)
    return pl.pallas_call(
        body, name=name, out_shape=(shp, shp, shp, shp, jax.ShapeDtypeStruct((8, 8, 128), F32)),
        grid=(nch,), in_specs=[rev] * 7 + [_vec_spec(8)],
        out_specs=(rev, rev, rev, rev, _vec_spec(8)),
        scratch_shapes=[pltpu.VMEM((tc, 8, 128), F32), pltpu.VMEM((tc, 8, 128), F32),
                        pltpu.VMEM((tc, 8, 128), F32), pltpu.VMEM((8, 128), F32)],
        compiler_params=_cparams(("arbitrary",)),
    )(dy, ra, ia, xb, gate, hs, hp, pv)


def _adamw(src, w, m, v, name):
    nl, n, rr, cc = src.shape
    tr = rr
    for cand in sorted((d for d in range(16, rr + 1, 16) if rr % d == 0), reverse=True):
        if cand * cc <= 192 * 1024:
            tr = cand
            break
    c1 = 1.0 - ADAM_B1 ** ADAM_STEP
    c2 = 1.0 - ADAM_B2 ** ADAM_STEP

    def body(s_ref, w_ref, m_ref, v_ref, g_out, d_out, m_out, v_out):
        g = s_ref[0].astype(F32)
        for k in range(1, n):
            g = g + s_ref[k].astype(F32)
        mn = ADAM_B1 * m_ref[...] + (1.0 - ADAM_B1) * g
        vn = ADAM_B2 * v_ref[...] + (1.0 - ADAM_B2) * (g * g)
        g_out[...] = g
        m_out[...] = mn
        v_out[...] = vn
        d_out[...] = (-ADAM_LR) * ((mn / c1) / (jnp.sqrt(vn / c2) + ADAM_EPS) + ADAM_WD * w_ref[...])

    pspec = pl.BlockSpec((None, tr, cc), lambda l, i: (l, i, 0))
    shp = jax.ShapeDtypeStruct((nl, rr, cc), F32)
    return pl.pallas_call(
        body, name=name, out_shape=(shp, shp, shp, shp), grid=(nl, rr // tr),
        in_specs=[pl.BlockSpec((None, n, tr, cc), lambda l, i: (l, 0, i, 0)), pspec, pspec, pspec],
        out_specs=(pspec, pspec, pspec, pspec),
        compiler_params=_cparams(("parallel", "parallel"), VMEM_BIG),
    )(src, w, m, v)


def _adamw_slice(src, w, m, v, bufs, idx, name):
    n, rr, cc = src.shape
    nl = w.shape[0]
    tr = rr
    for cand in sorted((d for d in range(16, rr + 1, 16) if rr % d == 0), reverse=True):
        if cand * cc <= 192 * 1024:
            tr = cand
            break
    c1 = 1.0 - ADAM_B1 ** ADAM_STEP
    c2 = 1.0 - ADAM_B2 ** ADAM_STEP
    if bufs is None:
        bufs = tuple(lax.empty((nl, rr, cc), F32) for _ in range(4))

    def body(s_ref, w_ref, m_ref, v_ref, b0, b1, b2, b3, g_out, d_out, m_out, v_out):
        g = s_ref[0].astype(F32)
        for k in range(1, n):
            g = g + s_ref[k].astype(F32)
        mn = ADAM_B1 * m_ref[...] + (1.0 - ADAM_B1) * g
        vn = ADAM_B2 * v_ref[...] + (1.0 - ADAM_B2) * (g * g)
        g_out[...] = g
        m_out[...] = mn
        v_out[...] = vn
        d_out[...] = (-ADAM_LR) * ((mn / c1) / (jnp.sqrt(vn / c2) + ADAM_EPS) + ADAM_WD * w_ref[...])

    pspec = pl.BlockSpec((None, tr, cc), lambda i: (idx, i, 0))
    anyspec = pl.BlockSpec(memory_space=pl.ANY)
    shp = jax.ShapeDtypeStruct((nl, rr, cc), F32)
    return pl.pallas_call(
        body, name=name, out_shape=(shp, shp, shp, shp), grid=(rr // tr,),
        in_specs=[pl.BlockSpec((n, tr, cc), lambda i: (0, i, 0)), pspec, pspec, pspec,
                  anyspec, anyspec, anyspec, anyspec],
        out_specs=(pspec, pspec, pspec, pspec),
        input_output_aliases={4: 0, 5: 1, 6: 2, 7: 3},
        compiler_params=_cparams(("parallel",), VMEM_BIG),
    )(src, w, m, v, *bufs)


def _sum_devices(x, name):
    _, rr, cc = x.shape

    def body(x_ref, o_ref):
        acc = x_ref[0]
        for k in range(1, NDEV):
            acc = acc + x_ref[k]
        o_ref[...] = acc

    return pl.pallas_call(
        body, name=name, out_shape=jax.ShapeDtypeStruct((rr, cc), F32), grid=(rr // 8,),
        in_specs=[pl.BlockSpec((NDEV, 8, cc), lambda i: (0, i, 0))],
        out_specs=pl.BlockSpec((8, cc), lambda i: (i, 0)),
        compiler_params=_cparams(("parallel",)),
    )(x)


def _to3(x):
    return x.reshape(x.shape[0], 8, 128)


def _heads(x):
    s = x.shape[0]
    return x.reshape(s, HEADS, HDIM).transpose(1, 0, 2).astype(BF16)


def _unheads(x):
    s = x.shape[1]
    return x.transpose(1, 0, 2).reshape(s, D)


def _block_diag(w):
    eye = jnp.eye(HEADS, dtype=w.dtype)
    return (w[:, :, None, :] * eye[:, None, :, None]).reshape(D, D)


def _diag_blocks(x):
    return jnp.diagonal(x.reshape(HEADS, HDIM, HEADS, HDIM), axis1=0, axis2=2).transpose(2, 0, 1)


def _col_blocks(dw, n):
    return dw.reshape(dw.shape[0], NDEV, n).transpose(1, 0, 2)


def kernel(x, c, w_cond, b_cond, norm_pre, norm_post, w_ffn_in, w_ffn_out, fox_w_in, fox_b_f, fox_w_out, sconv_w_in, sconv_conv_w, sconv_w_out, lru_w_in, lru_conv_w, lru_conv_b, lru_w_a, lru_b_a, lru_w_x, lru_b_x, lru_lambda, lru_w_out, loss_target, m_w_cond, m_b_cond, m_norm_pre, m_norm_post, m_w_ffn_in, m_w_ffn_out, m_fox_w_in, m_fox_b_f, m_fox_w_out, m_sconv_w_in, m_sconv_conv_w, m_sconv_w_out, m_lru_w_in, m_lru_conv_w, m_lru_conv_b, m_lru_w_a, m_lru_b_a, m_lru_w_x, m_lru_b_x, m_lru_lambda, m_lru_w_out, v_w_cond, v_b_cond, v_norm_pre, v_norm_post, v_w_ffn_in, v_w_ffn_out, v_fox_w_in, v_fox_b_f, v_fox_w_out, v_sconv_w_in, v_sconv_conv_w, v_sconv_w_out, v_lru_w_in, v_lru_conv_w, v_lru_conv_b, v_lru_w_a, v_lru_b_a, v_lru_w_x, v_lru_b_x, v_lru_lambda, v_lru_w_out):
    me = 4 * lax.axis_index("x") + 2 * lax.axis_index("y") + lax.axis_index("c")
    s = x.shape[1]
    x0 = x[0]
    tgt = loss_target[0]
    wcw = w_cond.shape[2]
    dsh = norm_pre.shape[2]

    small_parts = [c.reshape(-1), norm_pre.reshape(-1), norm_post.reshape(-1), sconv_conv_w.reshape(-1),
                   lru_conv_w.reshape(-1), lru_conv_b.reshape(-1), lru_lambda.reshape(-1)]
    sizes = [p.shape[0] for p in small_parts]
    flat = jnp.concatenate(small_parts)
    padn = (-flat.shape[0]) % 1024
    flat = jnp.pad(flat, (0, padn)).reshape(-1, 1024)
    sm = _all_gather(flat, 0, "ag_small").reshape(NDEV, -1)
    offs = [0]
    for n_ in sizes:
        offs.append(offs[-1] + n_)
    piece = lambda i: sm[:, offs[i]:offs[i + 1]]
    c_all = piece(0)
    unshard = lambda p, lead: p.reshape((NDEV,) + lead + (dsh,)).transpose(
        tuple(range(1, len(lead) + 1)) + (0, len(lead) + 1)).reshape(lead + (D,))
    npre = unshard(piece(1), (DEPTH, 3))
    npost = unshard(piece(2), (DEPTH, 3))
    scw = unshard(piece(3), (3,))
    lcw = unshard(piece(4), (4,))
    lcb = unshard(piece(5), ())
    llam = unshard(piece(6), ())

    c_pad = jnp.pad(c_all, ((0, 128 - NDEV), (0, 0)))
    bc_mine = lax.dynamic_slice(b_cond, (0, me * wcw), (DEPTH, wcw)).reshape(DEPTH, 1, wcw)
    modc, ca_pad = _cond_fwd(c_pad, w_cond, bc_mine, "cond_fwd")
    modg = _all_gather(modc[:, :NDEV, :], 0, "ag_mod")
    mod = lax.dynamic_index_in_dim(modg, me, axis=2, keepdims=False)
    mod = mod.transpose(1, 0, 2).reshape(DEPTH, 3, 3, D)
    ptab = jnp.stack([npre, mod[:, :, 1], mod[:, :, 0], npost, mod[:, :, 2],
                      jnp.zeros_like(npre), jnp.zeros_like(npre), jnp.zeros_like(npre)], axis=2)

    need = []
    for l in range(DEPTH):
        need += [(f"wi{l}0", w_ffn_in[l, 0]), (f"wo{l}0", w_ffn_out[l, 0])]
        if l % 3 == 0:
            need += [(f"fwi{l // 3}", fox_w_in[l // 3]), (f"fwo{l // 3}", fox_w_out[l // 3])]
        elif l % 3 == 1:
            need += [("swi", sconv_w_in[0]), ("swo", sconv_w_out[0])]
        else:
            need += [("lwi", lru_w_in[0]), ("lwo", lru_w_out[0])]
        need += [(f"wi{l}1", w_ffn_in[l, 1]), (f"wo{l}1", w_ffn_out[l, 1])]
    ag_keys = [k_ for k_, _ in need]
    lands = _alloc_many([(NDEV,) + w_.shape for _, w_ in need], BF16, "alloc_gather")
    pend = {}
    chain = [jnp.zeros((8, 128), F32)]
    for (key, w_), land in zip(need, lands):
        pend[key], chain[0] = _gather_start((w_ + chain[0][0, 0]).astype(BF16), land, "ags_" + key)
    relayed = [0]
    relay_toks = []

    def ag_relay(upto, after):
        while relayed[0] <= min(upto, len(ag_keys) - 1):
            key = ag_keys[relayed[0]]
            pend[key], tok = _gather_relay(pend[key], after, "agr_" + key)
            relay_toks.append(tok)
            relayed[0] += 1

    def ag_wait(key, after):
        return _gather_wait(pend.pop(key), after, "agw_" + key)

    ag_relay(1, chain[0])
    cols = lambda wg: wg.transpose(1, 0, 2).reshape(D, -1)
    wgate = jnp.concatenate([_block_diag(lru_w_a[0]), _block_diag(lru_w_x[0])], axis=1).astype(BF16)
    bf_pad = jnp.pad(fox_b_f, ((0, 0), (0, 128 - HEADS))).reshape(2, 1, 128)
    scw3 = scw.reshape(3, 8, 128)
    lcwb = jnp.concatenate([lcw, lcb[None], jnp.zeros((3, D), F32)], axis=0).reshape(8, 8, 128)
    lpv = jnp.concatenate([lru_b_a.reshape(1, D), lru_b_x.reshape(1, D), llam[None],
                           jnp.zeros((5, D), F32)], axis=0).reshape(8, 8, 128)

    subs = [(l, sub) for l in range(DEPTH) for sub in range(3)]
    coef = lambda sub: 1.0 if sub == 1 else 0.5

    saved = {}
    xcur = x0
    h = _prenorm(xcur, ptab, 0, 0, "prenorm", toks=(chain[0],))
    for idx, (l, sub) in enumerate(subs):
        tag = f"{l}{sub}"
        sv = {"x": xcur, "h": h}
        if sub != 1:
            f = 0 if sub == 0 else 1
            wi = ag_wait(f"wi{l}{f}", h)
            wo = ag_wait(f"wo{l}{f}", h).reshape(4, FFB, D)
            gu, act = _ffn_up(h, wi, "ffn_up_" + tag)
            y = _ffn_down(act, wo, "ffn_down_" + tag)
            sv.update(gu=gu, wi=wi, wo=wo)
        elif l % 3 == 0:
            j = l // 3
            fwi_j = cols(ag_wait(f"fwi{j}", h))
            w_qkv = fwi_j[:, :3 * D]
            w_f = jnp.pad(fwi_j[:, 3 * D:], ((0, 0), (0, 128 - HEADS)))
            fwo_j = ag_wait(f"fwo{j}", h).reshape(D, D)
            qkv = _fox_qkv(h, w_qkv, "fox_qkv_" + tag)
            fl = _mm_nn(h, w_f, F32, "fox_f_" + tag)
            cum = _fox_cum(fl, bf_pad[j], "fox_cum_" + tag)
            cumt = cum[:, :HEADS].T.reshape(HPAIRS, 2, s)
            cq, ck = cumt[:, :, :, None], cumt[:, :, None, :]
            o2, lse = _fox_attn_fwd(qkv, cq, ck, "fox_attn_" + tag)
            y = _mm_nn(o2, fwo_j, F32, "fox_out_" + tag)
            sv.update(w_qkv=w_qkv, w_f=w_f, fwo=fwo_j, qkv=qkv, fl=fl, cq=cq, ck=ck, lse=lse, o2=o2)
        elif l % 3 == 1:
            swi = cols(ag_wait("swi", h))
            swo = ag_wait("swo", h).reshape(D, D)
            proj = _mm_nn(h, swi, F32, "sconv_in_" + tag)
            bg, cg, xv = (_to3(proj[:, i * D:(i + 1) * D]) for i in range(3))
            y3 = _sconv_fwd(bg, cg, xv, scw3, "sconv_mix_" + tag)
            y2 = y3.reshape(s, D).astype(BF16)
            y = _mm_nn(y2, swo, F32, "sconv_out_" + tag)
            sv.update(bg=bg, cg=cg, xv=xv, y2=y2)
        else:
            lwi = cols(ag_wait("lwi", h))
            lwo = ag_wait("lwo", h).reshape(D, D)
            proj = _mm_nn(h, lwi, F32, "lru_in_" + tag)
            gate3, xp3 = _to3(proj[:, :D]), _to3(proj[:, D:])
            xb3 = _lru_conv_fwd(xp3, lcwb, "lru_conv_" + tag)
            xb2 = xb3.reshape(s, D).astype(BF16)
            gpre = _mm_nn(xb2, wgate, F32, "lru_gate_" + tag)
            ra3, ia3 = _to3(gpre[:, :D]), _to3(gpre[:, D:])
            hs3, hp3, y3 = _lru_scan_fwd(ra3, ia3, xb3, gate3, lpv, "lru_scan_" + tag)
            y2 = y3.reshape(s, D).astype(BF16)
            y = _mm_nn(y2, lwo, F32, "lru_out_" + tag)
            sv.update(gate3=gate3, xp3=xp3, xb3=xb3, xb2=xb2, ra3=ra3, ia3=ia3, hs3=hs3, hp3=hp3, y2=y2)
        sv["y"] = y
        saved[(l, sub)] = sv
        nxt = subs[idx + 1] if idx + 1 < len(subs) else None
        ag_relay(2 * idx + 3, y)
        xcur, h = _resid(xcur, y, ptab, l, sub, coef(sub), nxt, "resid_" + tag, toks=tuple(relay_toks))
        del relay_toks[:]

    dx, loss_cols = _loss_grad(xcur, tgt, "loss_grad")

    d_mod = [[[None] * 3 for _ in range(3)] for _ in range(DEPTH)]
    d_npre = [[None] * 3 for _ in range(DEPTH)]
    d_npost = [[None] * 3 for _ in range(DEPTH)]
    d_fbf = [None, None]
    small_g = {}
    rs_pend, rs_order = {}, []
    toks = []
    dh2 = None
    rs_lands = dict(zip(ag_keys, _alloc_many([(NDEV,) + w_.shape for _, w_ in need], BF16, "alloc_scatter")))

    def rs_start(key, blocks):
        hnd = _exchange_start(blocks, "rss_" + key, False, land=rs_lands[key])
        rs_pend[key] = hnd[:4]
        rs_order.append(key)
        toks.append(hnd[4])

    for (l, sub) in reversed(subs):
        tag = f"{l}{sub}"
        sv = saved[(l, sub)]
        dy, red = _post_bwd(dx, sv["y"], ptab, l, sub, coef(sub), "post_bwd_" + tag, toks=tuple(toks))
        del toks[:]
        d_mod[l][sub][2] = red[0]
        d_npost[l][sub] = red[1]
        hb = sv["h"]
        if sub != 1:
            f = 0 if sub == 0 else 1
            da, dwo = _ffn_down_bwd(dy, sv["gu"], sv["wo"], "ffn_down_bwd_" + tag)
            dh = _ffn_dh(da, sv["wi"], "ffn_dh_" + tag)
            rs_start(f"wo{l}{f}", dwo.reshape(NDEV, FFB // 2, D))
            dwi = lambda hb=hb, da=da, tag=tag: _ffn_dwi(hb, da.reshape(NDEV, s, FFB), "ffn_dwi_" + tag)
            if (l, sub) != subs[0]:
                rs_start(f"wi{l}{f}", dwi())
        elif l % 3 == 0:
            j = l // 3
            do2 = _mm_nt(dy, sv["fwo"], BF16, "fox_out_dx_" + tag)
            rs_start(f"fwo{j}", _mm_tn(sv["o2"], dy, BF16, "fox_out_dw_" + tag).reshape(NDEV, dsh, D))
            dqkv, dcq, dck = _fox_attn_bwd(sv["qkv"], do2, sv["lse"], sv["cq"], sv["ck"], "fox_attn_bwd_" + tag)
            dcum = (dcq[..., 0] + dck[:, :, 0, :]).reshape(HEADS, s).T
            dcum = jnp.pad(dcum, ((0, 0), (0, 128 - HEADS)))
            dfl, dbf = _fox_cum_bwd(dcum, sv["fl"], bf_pad[j], "fox_cum_bwd_" + tag)
            d_fbf[j] = dbf[0, :HEADS]
            dflb = dfl.astype(BF16)
            dh = _fox_qkv_dx(dqkv, sv["w_qkv"], "fox_qkv_dx_" + tag)
            dh2 = _mm_nt(dflb, sv["w_f"], F32, "fox_f_dx_" + tag)
            dw3 = _fox_qkv_dw(hb, dqkv, "fox_qkv_dw_" + tag)
            dwf = _mm_tn(hb, dflb, BF16, "fox_f_dw_" + tag)
            dwfox = jnp.concatenate([dw3[0], dw3[1], dw3[2], dwf[:, :HEADS]], axis=1)
            rs_start(f"fwi{j}", _col_blocks(dwfox, fox_w_in.shape[2]))
        elif l % 3 == 1:
            dy3 = _to3(_mm_nt(dy, swo, F32, "sconv_out_dx_" + tag))
            rs_start("swo", _mm_tn(sv["y2"], dy, BF16, "sconv_out_dw_" + tag).reshape(NDEV, dsh, D))
            dbg, dcg, dxv, dscw = _sconv_bwd(dy3, sv["bg"], sv["cg"], sv["xv"], scw3, "sconv_mix_bwd_" + tag)
            small_g["sconv_conv_w"] = dscw.reshape(3, D)
            dproj = jnp.concatenate([t.reshape(s, D) for t in (dbg, dcg, dxv)], axis=1).astype(BF16)
            dh = _mm_nt(dproj, swi, F32, "sconv_in_dx_" + tag)
            rs_start("swi", _col_blocks(_mm_tn(hb, dproj, BF16, "sconv_in_dw_" + tag), sconv_w_in.shape[2]))
        else:
            dy3 = _to3(_mm_nt(dy, lwo, F32, "lru_out_dx_" + tag))
            rs_start("lwo", _mm_tn(sv["y2"], dy, BF16, "lru_out_dw_" + tag).reshape(NDEV, dsh, D))
            dgate3, dra3, dia3, dxb3, lred = _lru_scan_bwd(
                dy3, sv["ra3"], sv["ia3"], sv["xb3"], sv["gate3"], sv["hs3"], sv["hp3"], lpv, "lru_scan_bwd_" + tag)
            dgp = jnp.concatenate([dra3.reshape(s, D), dia3.reshape(s, D)], axis=1).astype(BF16)
            dxb3 = dxb3 + _to3(_mm_nt(dgp, wgate, F32, "lru_gate_dx_" + tag))
            dwgate = _mm_tn(sv["xb2"], dgp, F32, "lru_gate_dw_" + tag)
            dxp3, cred = _lru_conv_bwd(dxb3, sv["xp3"], lcwb, "lru_conv_bwd_" + tag)
            lred, cred = lred.reshape(8, D), cred.reshape(8, D)
            small_g.update(lru_w_a=_diag_blocks(dwgate[:, :D]), lru_w_x=_diag_blocks(dwgate[:, D:]),
                           lru_b_a=lred[0], lru_b_x=lred[1], lru_lambda=lred[2],
                           lru_conv_w=cred[:4], lru_conv_b=cred[4])
            dproj = jnp.concatenate([dgate3.reshape(s, D), dxp3.reshape(s, D)], axis=1).astype(BF16)
            dh = _mm_nt(dproj, lwi, F32, "lru_in_dx_" + tag)
            rs_start("lwi", _col_blocks(_mm_tn(hb, dproj, BF16, "lru_in_dw_" + tag), lru_w_in.shape[2]))
        dx, red = _pre_bwd(dx, dh, sv["x"], ptab, l, sub, "pre_bwd_" + tag, dh2=dh2)
        dh2 = None
        d_mod[l][sub][0] = red[0]
        d_mod[l][sub][1] = red[1]
        d_npre[l][sub] = red[2]
    grad_x = dx[None]

    dmod_mine = jnp.stack([jnp.stack([jnp.stack(d_mod[l][sub]) for sub in range(3)]) for l in range(DEPTH)])
    gparts = [loss_cols[0], dmod_mine.reshape(-1),
              jnp.stack([jnp.stack(r_) for r_ in d_npre]).reshape(-1),
              jnp.stack([jnp.stack(r_) for r_ in d_npost]).reshape(-1),
              jnp.stack(d_fbf).reshape(-1), small_g["sconv_conv_w"].reshape(-1),
              small_g["lru_conv_w"].reshape(-1), small_g["lru_conv_b"].reshape(-1),
              small_g["lru_w_a"].reshape(-1), small_g["lru_b_a"].reshape(-1),
              small_g["lru_w_x"].reshape(-1), small_g["lru_b_x"].reshape(-1),
              small_g["lru_lambda"].reshape(-1)]
    gsizes = [p.shape[0] for p in gparts]
    gflat = jnp.concatenate(gparts)
    gflat = jnp.pad(gflat, (0, (-gflat.shape[0]) % (8 * 1024))).reshape(-1, 1024)
    small_hnd = _exchange_start(gflat, "ags_smallgrads", True)
    rs_start("wi00", dwi())

    out = {}
    after = [small_hnd[4]]

    def update(keys, w, m, v, name):
        got = {}
        for k_ in sorted(keys, key=rs_order.index):
            got[k_] = after[0] = _exchange_wait(rs_pend.pop(k_), after[0], "rsw_" + k_, False)
        src = jnp.stack([got[k_] for k_ in keys])
        shp = w.shape
        w3, m3, v3 = (t.reshape((src.shape[0],) + src.shape[2:]) for t in (w, m, v))
        res = tuple(t.reshape(shp) for t in _adamw(src, w3, m3, v3, "adamw_" + name))
        after[0] = res[1]
        return res

    lf = [f"{l}{f}" for l in range(DEPTH) for f in range(2)]
    out["fox_w_in"] = update(["fwi0", "fwi1"], fox_w_in, m_fox_w_in, v_fox_w_in, "fwi")
    out["fox_w_out"] = update(["fwo0", "fwo1"], fox_w_out, m_fox_w_out, v_fox_w_out, "fwo")
    out["lru_w_out"] = update(["lwo"], lru_w_out, m_lru_w_out, v_lru_w_out, "lwo")
    out["lru_w_in"] = update(["lwi"], lru_w_in, m_lru_w_in, v_lru_w_in, "lwi")
    out["sconv_w_out"] = update(["swo"], sconv_w_out, m_sconv_w_out, v_sconv_w_out, "swo")
    out["sconv_w_in"] = update(["swi"], sconv_w_in, m_sconv_w_in, v_sconv_w_in, "swi")

    stacked = {"wi": [t.reshape((-1,) + w_ffn_in.shape[2:]) for t in (w_ffn_in, m_w_ffn_in, v_w_ffn_in)],
               "wo": [t.reshape((-1,) + w_ffn_out.shape[2:]) for t in (w_ffn_out, m_w_ffn_out, v_w_ffn_out)]}
    bufs = {"wi": None, "wo": None}

    def update_slice(k_):
        kind = k_[:2]
        src = after[0] = _exchange_wait(rs_pend.pop(k_), after[0], "rsw_" + k_, False)
        bufs[kind] = _adamw_slice(src, *stacked[kind], bufs[kind], lf.index(k_[2:]), "adamw_" + k_)
        after[0] = bufs[kind][1]

    for k_ in rs_order:
        if k_[:2] in ("wi", "wo") and k_ != "wi00":
            update_slice(k_)

    gall = _exchange_wait(small_hnd[:4], after[0], "agw_smallgrads", True)
    gsum = _sum_devices(gall, "sum_smallgrads").reshape(-1)
    goffs = [0]
    for n_ in gsizes:
        goffs.append(goffs[-1] + n_)
    gpiece = lambda i: gsum[goffs[i]:goffs[i + 1]]
    loss = jnp.sum(gpiece(0))
    my_ch = lambda g, lead: lax.dynamic_slice_in_dim(g.reshape(lead + (D,)), me * dsh, dsh, axis=len(lead))

    dmod_all = gall.reshape(NDEV, -1)[:, goffs[1]:goffs[2]].reshape(NDEV, DEPTH, 3 * 3 * D)
    dmod_cols = lax.dynamic_slice_in_dim(dmod_all, me * wcw, wcw, axis=2).transpose(1, 0, 2)
    dmod_cols = jnp.pad(dmod_cols, ((0, 0), (0, 128 - NDEV), (0, 0))).astype(BF16)
    g_wcond = _cond_bwd(ca_pad, dmod_cols, "cond_bwd")

    out["w_cond"] = _adamw(g_wcond[:, None], w_cond, m_w_cond, v_w_cond, "adamw_wcond")

    small = [
        ("b_cond", b_cond, m_b_cond, v_b_cond, gpiece(1)),
        ("norm_pre", norm_pre, m_norm_pre, v_norm_pre, my_ch(gpiece(2), (DEPTH, 3))),
        ("norm_post", norm_post, m_norm_post, v_norm_post, my_ch(gpiece(3), (DEPTH, 3))),
        ("fox_b_f", fox_b_f, m_fox_b_f, v_fox_b_f, gpiece(4)),
        ("sconv_conv_w", sconv_conv_w, m_sconv_conv_w, v_sconv_conv_w, my_ch(gpiece(5), (1, 3))),
        ("lru_conv_w", lru_conv_w, m_lru_conv_w, v_lru_conv_w, my_ch(gpiece(6), (1, 4))),
        ("lru_conv_b", lru_conv_b, m_lru_conv_b, v_lru_conv_b, my_ch(gpiece(7), (1,))),
        ("lru_w_a", lru_w_a, m_lru_w_a, v_lru_w_a, gpiece(8)),
        ("lru_b_a", lru_b_a, m_lru_b_a, v_lru_b_a, gpiece(9)),
        ("lru_w_x", lru_w_x, m_lru_w_x, v_lru_w_x, gpiece(10)),
        ("lru_b_x", lru_b_x, m_lru_b_x, v_lru_b_x, gpiece(11)),
        ("lru_lambda", lru_lambda, m_lru_lambda, v_lru_lambda, my_ch(gpiece(12), (1,))),
    ]
    pack = lambda ts: jnp.concatenate([t.reshape(-1) for t in ts])
    ssz = [w_.size for _, w_, _, _, _ in small]
    tot = sum(ssz)
    padr = (-tot) % (16 * 1024)
    pk = lambda ts: jnp.pad(pack(ts), (0, padr)).reshape(1, -1, 1024)
    sg, sd, smm, svv = _adamw(pk([t[4] for t in small])[:, None], pk([t[1] for t in small]),
                              pk([t[2] for t in small]), pk([t[3] for t in small]), "adamw_small")
    soff = 0
    for (name, w_, _, _, _), n_ in zip(small, ssz):
        out[name] = tuple(t.reshape(-1)[soff:soff + n_].reshape(w_.shape) for t in (sg, sd, smm, svv))
        soff += n_

    after[0] = sd
    update_slice("wi00")
    out["w_ffn_in"] = tuple(t.reshape(w_ffn_in.shape) for t in bufs["wi"])
    out["w_ffn_out"] = tuple(t.reshape(w_ffn_out.shape) for t in bufs["wo"])

    names =["w_cond", "b_cond", "norm_pre", "norm_post", "w_ffn_in", "w_ffn_out", "fox_w_in", "fox_b_f",
             "fox_w_out", "sconv_w_in", "sconv_conv_w", "sconv_w_out", "lru_w_in", "lru_conv_w", "lru_conv_b",
             "lru_w_a", "lru_b_a", "lru_w_x", "lru_b_x", "lru_lambda", "lru_w_out"]
    return (loss, grad_x, *[out[n_][0] for n_ in names], *[out[n_][1] for n_ in names],
            *[out[n_][2] for n_ in names], *[out[n_][3] for n_ in names])
```

```python
import functools
import math

import jax
import jax.numpy as jnp
from jax import lax
from jax.experimental import pallas as pl
from jax.experimental.pallas import tpu as pltpu

F32 = jnp.float32
BF16 = jnp.bfloat16
NDEV = 8
D = 1024
DFF = 2816
FFB = 704
HEADS = 16
HDIM = 64
DEPTH = 4
RMS_EPS = 1e-6
LRU_C = 8.0
ADAM_LR, ADAM_B1, ADAM_B2, ADAM_EPS, ADAM_WD, ADAM_STEP = 0.001, 0.9, 0.999, 1e-08, 0.01, 10
FOXP = 3200
MESH = pl.DeviceIdType.MESH
HBM = pl.BlockSpec(memory_space=pltpu.HBM)
VMEM_BIG = 48 * 1024 * 1024

_NN = (((1,), (0,)), ((), ()))
_NT = (((1,), (1,)), ((), ()))
_TN = (((0,), (0,)), ((), ()))
_DIMS = {"nn": _NN, "nt": _NT, "tn": _TN}


def _cparams(sem=None, vmem=None):
    kw = {}
    if sem is not None:
        kw["dimension_semantics"] = sem
    if vmem is not None:
        kw["vmem_limit_bytes"] = vmem
    return pltpu.CompilerParams(**kw)


def _sigmoid(x):
    return 1.0 / (1.0 + jnp.exp(-x))


def _softplus(x):
    return jnp.maximum(x, 0.0) + jnp.log(1.0 + jnp.exp(-jnp.abs(x)))


_GELU_C = math.sqrt(2.0 / math.pi)


def _gelu_parts(x):
    u = _GELU_C * (x + 0.044715 * x * x * x)
    t = jnp.tanh(u)
    g = 0.5 * x * (1.0 + t)
    dg = 0.5 * (1.0 + t) + 0.5 * x * (1.0 - t * t) * _GELU_C * (1.0 + 3.0 * 0.044715 * x * x)
    return g, dg


def _mesh_pos():
    ax, ay, ac = lax.axis_index("x"), lax.axis_index("y"), lax.axis_index("c")
    return ax, ay, ac, 4 * ax + 2 * ay + ac


def _peer(ax, ay, ac, d):
    px = 1 - ax if (d >> 2) & 1 else ax
    py = 1 - ay if (d >> 1) & 1 else ay
    pc = 1 - ac if d & 1 else ac
    return (px, py, pc), 4 * px + 2 * py + pc


def _exchange(x, axis, name, gather):
    if gather:
        x = jnp.expand_dims(x, axis)
        oshape = x.shape[:axis] + (NDEV,) + x.shape[axis + 1:]
    else:
        oshape = x.shape
    lead = (slice(None),) * axis

    def blk(ref, k):
        return ref.at[lead + (pl.ds(k, 1),)]

    def body(x_ref, o_ref, send_sems, recv_sems, local_sem):
        ax, ay, ac, me = _mesh_pos()
        src = (lambda k: x_ref) if gather else (lambda k: blk(x_ref, k))
        mine = pltpu.make_async_copy(src(me), blk(o_ref, me), local_sem)
        mine.start()
        sends = []
        for d in range(1, NDEV):
            peer, pidx = _peer(ax, ay, ac, d)
            cp = pltpu.make_async_remote_copy(
                src_ref=src(pidx), dst_ref=blk(o_ref, me), send_sem=send_sems.at[d - 1],
                recv_sem=recv_sems.at[d - 1], device_id=peer, device_id_type=MESH)
            cp.start()
            sends.append(cp)
        for d in range(1, NDEV):
            peer, pidx = _peer(ax, ay, ac, d)
            pltpu.make_async_remote_copy(
                src_ref=src(pidx), dst_ref=blk(o_ref, pidx), send_sem=send_sems.at[d - 1],
                recv_sem=recv_sems.at[d - 1], device_id=peer, device_id_type=MESH).wait_recv()
        for cp in sends:
            cp.wait_send()
        mine.wait()

    return pl.pallas_call(
        body, name=name, out_shape=jax.ShapeDtypeStruct(oshape, x.dtype),
        in_specs=[HBM], out_specs=HBM,
        scratch_shapes=[pltpu.SemaphoreType.DMA((NDEV - 1,)), pltpu.SemaphoreType.DMA((NDEV - 1,)),
                        pltpu.SemaphoreType.DMA(())],
    )(x)


def _all_gather(x, axis, name):
    return _exchange(x, axis, name, True)


SEM = pl.BlockSpec(memory_space=pltpu.SEMAPHORE)
EFFECT = pltpu.SideEffectType.DATAFLOW_SIDE_EFFECTING


def _exchange_start(x, name, gather, land=None):
    me = 4 * lax.axis_index("x") + 2 * lax.axis_index("y") + lax.axis_index("c")
    if gather:
        x = x[None]
        oshape = (NDEV,) + x.shape[1:]
        own = x
    else:
        oshape = x.shape
        own = lax.dynamic_index_in_dim(x, me, 0, keepdims=True)
    land = lax.empty(oshape, x.dtype) if land is None else land
    land = lax.dynamic_update_index_in_dim(land, own, me, 0)

    def blk(ref, k):
        return ref.at[pl.ds(k, 1)]

    def body(x_ref, land_ref, send_sems, recv_sems, x_thru, land_thru, token):
        ax, ay, ac, me = _mesh_pos()
        src = (lambda k: x_ref) if gather else (lambda k: blk(x_ref, k))
        for d in range(1, NDEV):
            peer, pidx = _peer(ax, ay, ac, d)
            pltpu.make_async_remote_copy(
                src_ref=src(pidx), dst_ref=blk(land_ref, me), send_sem=send_sems.at[d - 1],
                recv_sem=recv_sems.at[d - 1], device_id=peer, device_id_type=MESH).start()
        token[...] = jnp.zeros_like(token)

    return pl.pallas_call(
        body, name=name,
        out_shape=(pltpu.SemaphoreType.DMA((NDEV - 1,)), pltpu.SemaphoreType.DMA((NDEV - 1,)),
                   pltpu.HBM(x.shape, x.dtype), pltpu.HBM(oshape, x.dtype),
                   jax.ShapeDtypeStruct((8, 128), F32)),
        in_specs=(HBM, HBM), out_specs=(SEM, SEM, HBM, HBM, pl.BlockSpec(memory_space=pltpu.VMEM)),
        input_output_aliases={0: 2, 1: 3},
        compiler_params=pltpu.CompilerParams(has_side_effects=EFFECT),
    )(pltpu.with_memory_space_constraint(x, pltpu.HBM),
      pltpu.with_memory_space_constraint(land, pltpu.HBM))


def _exchange_wait(handle, after, name, gather):
    send_sems, recv_sems, x_thru, land_thru = handle

    def blk(ref, k):
        return ref.at[pl.ds(k, 1)]

    def body(x_ref, land_ref, send_sems, recv_sems, after_ref, x_dead, got_ref):
        ax, ay, ac, me = _mesh_pos()
        src = (lambda k: x_ref) if gather else (lambda k: blk(x_ref, k))
        for d in range(1, NDEV):
            peer, pidx = _peer(ax, ay, ac, d)
            cp = pltpu.make_async_remote_copy(
                src_ref=src(pidx), dst_ref=blk(land_ref, pidx), send_sem=send_sems.at[d - 1],
                recv_sem=recv_sems.at[d - 1], device_id=peer, device_id_type=MESH)
            cp.wait_send()
            cp.wait_recv()

    return pl.pallas_call(
        body, name=name,
        out_shape=(pltpu.HBM(x_thru.shape, x_thru.dtype), pltpu.HBM(land_thru.shape, land_thru.dtype)),
        in_specs=(HBM, HBM, SEM, SEM, pl.BlockSpec(memory_space=pl.ANY)), out_specs=(HBM, HBM),
        input_output_aliases={0: 0, 1: 1},
        compiler_params=pltpu.CompilerParams(has_side_effects=EFFECT),
    )(x_thru, land_thru, send_sems, recv_sems, after)[1]


_HOPS = (1, 2, 4, 6)


def _blk(ref, k):
    return ref.at[pl.ds(k, 1)]


def _alloc_many(shapes, dtype, name):
    def body(*refs):
        pass

    return pl.pallas_call(
        body, name=name, out_shape=tuple(pltpu.HBM(s_, dtype) for s_ in shapes),
        out_specs=tuple(HBM for _ in shapes),
    )()


def _gather_start(x, land, name):
    me = 4 * lax.axis_index("x") + 2 * lax.axis_index("y") + lax.axis_index("c")
    x = x[None]
    oshape = (NDEV,) + x.shape[1:]
    land = lax.dynamic_update_index_in_dim(land, x, me, 0)

    def body(x_ref, land_ref, send_sems, recv_sems, x_thru, land_thru, token):
        ax, ay, ac, me = _mesh_pos()
        for i, d in enumerate(_HOPS):
            peer, _ = _peer(ax, ay, ac, d)
            pltpu.make_async_remote_copy(
                src_ref=x_ref, dst_ref=_blk(land_ref, me), send_sem=send_sems.at[i],
                recv_sem=recv_sems.at[i], device_id=peer, device_id_type=MESH).start()
        token[...] = jnp.zeros_like(token)

    send1, recv1, x_thru, land_thru, token = pl.pallas_call(
        body, name=name,
        out_shape=(pltpu.SemaphoreType.DMA((4,)), pltpu.SemaphoreType.DMA((4,)),
                   pltpu.HBM(x.shape, x.dtype), pltpu.HBM(oshape, x.dtype),
                   jax.ShapeDtypeStruct((8, 128), F32)),
        in_specs=(HBM, HBM), out_specs=(SEM, SEM, HBM, HBM, pl.BlockSpec(memory_space=pltpu.VMEM)),
        input_output_aliases={0: 2, 1: 3},
        compiler_params=pltpu.CompilerParams(has_side_effects=EFFECT),
    )(pltpu.with_memory_space_constraint(x, pltpu.HBM),
      pltpu.with_memory_space_constraint(land, pltpu.HBM))
    return dict(send1=send1, recv1=recv1, x=x_thru, land=land_thru), token


def _gather_relay(hnd, after, name):
    def body(land_ref, recv1, after_ref, send2, recv2, land_thru, token):
        ax, ay, ac, me = _mesh_pos()
        sibling, _ = _peer(ax, ay, ac, 1)
        for i, d in enumerate(_HOPS[1:]):
            peer, pidx = _peer(ax, ay, ac, d)
            came = _blk(land_ref, pidx)
            pltpu.make_async_remote_copy(
                src_ref=came, dst_ref=came, send_sem=send2.at[i], recv_sem=recv1.at[i + 1],
                device_id=peer, device_id_type=MESH).wait_recv()
            pltpu.make_async_remote_copy(
                src_ref=came, dst_ref=came, send_sem=send2.at[i], recv_sem=recv2.at[i],
                device_id=sibling, device_id_type=MESH).start()
        token[...] = jnp.zeros_like(token)

    land = hnd["land"]
    send2, recv2, land_thru, token = pl.pallas_call(
        body, name=name,
        out_shape=(pltpu.SemaphoreType.DMA((3,)), pltpu.SemaphoreType.DMA((3,)),
                   pltpu.HBM(land.shape, land.dtype), jax.ShapeDtypeStruct((8, 128), F32)),
        in_specs=(HBM, SEM, ANYSPEC), out_specs=(SEM, SEM, HBM, pl.BlockSpec(memory_space=pltpu.VMEM)),
        input_output_aliases={0: 2},
        compiler_params=pltpu.CompilerParams(has_side_effects=EFFECT),
    )(land, hnd["recv1"], after)
    return dict(hnd, send2=send2, recv2=recv2, land=land_thru), token


def _gather_wait(hnd, after, name):
    def body(x_ref, land_ref, send1, recv1, send2, recv2, after_ref, x_dead, got_ref):
        ax, ay, ac, me = _mesh_pos()
        sibling, sidx = _peer(ax, ay, ac, 1)
        for i, d in enumerate(_HOPS):
            peer, pidx = _peer(ax, ay, ac, d)
            cp = pltpu.make_async_remote_copy(
                src_ref=x_ref, dst_ref=_blk(land_ref, pidx), send_sem=send1.at[i], recv_sem=recv1.at[i],
                device_id=peer, device_id_type=MESH)
            cp.wait_send()
            if i == 0:
                cp.wait_recv()
        for i, d in enumerate(_HOPS[1:]):
            _, pidx = _peer(ax, ay, ac, d)
            _, fidx = _peer(ax, ay, ac, d ^ 1)
            cp = pltpu.make_async_remote_copy(
                src_ref=_blk(land_ref, pidx), dst_ref=_blk(land_ref, fidx), send_sem=send2.at[i],
                recv_sem=recv2.at[i], device_id=sibling, device_id_type=MESH)
            cp.wait_send()
            cp.wait_recv()

    x, land = hnd["x"], hnd["land"]
    return pl.pallas_call(
        body, name=name,
        out_shape=(pltpu.HBM(x.shape, x.dtype), pltpu.HBM(land.shape, land.dtype)),
        in_specs=(HBM, HBM, SEM, SEM, SEM, SEM, ANYSPEC), out_specs=(HBM, HBM),
        input_output_aliases={0: 0, 1: 1},
        compiler_params=pltpu.CompilerParams(has_side_effects=EFFECT),
    )(x, land, hnd["send1"], hnd["recv1"], hnd["send2"], hnd["recv2"], after)[1]


def _mm(a, b, *, mode, grid, a_spec, b_spec, o_spec, o_shape, acc_shape, out_dtype, name):
    nred = grid[2]

    def body(a_ref, b_ref, o_ref, *scratch):
        p = lax.dot_general(a_ref[...], b_ref[...], _DIMS[mode], preferred_element_type=F32)
        if nred == 1:
            o_ref[...] = p.astype(o_ref.dtype)
            return
        acc = scratch[0]
        r = pl.program_id(2)

        @pl.when(r == 0)
        def _():
            acc[...] = p

        @pl.when(r > 0)
        def _():
            acc[...] += p

        @pl.when(r == nred - 1)
        def _():
            o_ref[...] = acc[...].astype(o_ref.dtype)

    return pl.pallas_call(
        body, name=name, out_shape=jax.ShapeDtypeStruct(o_shape, out_dtype), grid=grid,
        in_specs=[a_spec, b_spec], out_specs=o_spec,
        scratch_shapes=[] if nred == 1 else [pltpu.VMEM(acc_shape, F32)],
        compiler_params=_cparams(("parallel", "parallel", "arbitrary"), VMEM_BIG),
    )(a, b)


def _tile(n, cands):
    for c in cands:
        if n % c == 0:
            return c
    return n


def _mm_nn(a, b, out_dtype, name):
    m, k = a.shape
    n = b.shape[1]
    tm, tn = _tile(m, (512, 256, 128)), _tile(n, (1024, 640, 512, 256, 128))
    return _mm(a, b, mode="nn", grid=(m // tm, n // tn, 1),
               a_spec=pl.BlockSpec((tm, k), lambda i, j, r: (i, 0)),
               b_spec=pl.BlockSpec((k, tn), lambda i, j, r: (0, j)),
               o_spec=pl.BlockSpec((tm, tn), lambda i, j, r: (i, j)),
               o_shape=(m, n), acc_shape=None, out_dtype=out_dtype, name=name)


def _mm_nt(a, b, out_dtype, name):
    m, n = a.shape
    k = b.shape[0]
    tm, tn = _tile(m, (512, 256, 128)), _tile(n, (1024, 640, 512, 256, 128))
    return _mm(a, b, mode="nt", grid=(m // tm, 1, n // tn),
               a_spec=pl.BlockSpec((tm, tn), lambda i, j, r: (i, r)),
               b_spec=pl.BlockSpec((k, tn), lambda i, j, r: (0, r)),
               o_spec=pl.BlockSpec((tm, k), lambda i, j, r: (i, 0)),
               o_shape=(m, k), acc_shape=(tm, k), out_dtype=out_dtype, name=name)


def _mm_tn(a, b, out_dtype, name):
    s, k = a.shape
    n = b.shape[1]
    ts, tn = _tile(s, (2048, 1024, 512, 256, 128)), _tile(n, (640, 512, 256, 128))
    return _mm(a, b, mode="tn", grid=(1, n // tn, s // ts),
               a_spec=pl.BlockSpec((ts, k), lambda i, j, r: (r, 0)),
               b_spec=pl.BlockSpec((ts, tn), lambda i, j, r: (r, j)),
               o_spec=pl.BlockSpec((k, tn), lambda i, j, r: (0, j)),
               o_shape=(k, n), acc_shape=(k, tn), out_dtype=out_dtype, name=name)


def _gu_spec(tm, idx):
    return pl.BlockSpec((2, None, tm, FFB), idx)


def _ffn_up(h, wi, name):
    s = h.shape[0]
    tm = _tile(s, (1024, 512, 256, 128))

    def body(h_ref, wg_ref, wu_ref, gu_ref, act_ref):
        hv = h_ref[...]
        g = jnp.dot(hv, wg_ref[...], preferred_element_type=F32)
        u = jnp.dot(hv, wu_ref[...], preferred_element_type=F32)
        gu_ref[0] = g.astype(BF16)
        gu_ref[1] = u.astype(BF16)
        act_ref[...] = (g * _sigmoid(g) * u).astype(BF16)

    wspec = lambda off: pl.BlockSpec((None, D, FFB), lambda j, i: (j + off, 0, 0))
    return pl.pallas_call(
        body, name=name,
        out_shape=(jax.ShapeDtypeStruct((2, 4, s, FFB), BF16), jax.ShapeDtypeStruct((4, s, FFB), BF16)),
        grid=(4, s // tm),
        in_specs=[pl.BlockSpec((tm, D), lambda j, i: (i, 0)), wspec(0), wspec(4)],
        out_specs=(_gu_spec(tm, lambda j, i: (0, j, i, 0)), pl.BlockSpec((None, tm, FFB), lambda j, i: (j, i, 0))),
        compiler_params=_cparams(("parallel", "parallel"), VMEM_BIG),
    )(h, wi, wi)


def _ffn_down(act, wo, name):
    s = act.shape[1]
    tm = _tile(s, (1024, 512, 256, 128))
    return _mm(act, wo, mode="nn", grid=(s // tm, 1, 4),
               a_spec=pl.BlockSpec((None, tm, FFB), lambda i, j, r: (r, i, 0)),
               b_spec=pl.BlockSpec((None, FFB, D), lambda i, j, r: (r, 0, 0)),
               o_spec=pl.BlockSpec((tm, D), lambda i, j, r: (i, 0)),
               o_shape=(s, D), acc_shape=(tm, D), out_dtype=F32, name=name)


def _ffn_down_bwd(dy, gu, wo, name):
    s = dy.shape[0]
    tm = _tile(s, (512, 256, 128))
    ns = s // tm

    def body(dy_ref, gu_ref, wo_ref, da_ref, dwo_ref, acc):
        i = pl.program_id(1)
        dyv = dy_ref[...]
        dact = lax.dot_general(dyv, wo_ref[...], _NT, preferred_element_type=F32)
        gv = gu_ref[0].astype(F32)
        uv = gu_ref[1].astype(F32)
        sg = _sigmoid(gv)
        silu = gv * sg
        da_ref[0] = (dact * uv * (sg * (1.0 + gv * (1.0 - sg)))).astype(BF16)
        da_ref[1] = (dact * silu).astype(BF16)
        p = lax.dot_general((silu * uv).astype(BF16), dyv, _TN, preferred_element_type=F32)

        @pl.when(i == 0)
        def _():
            acc[...] = p

        @pl.when(i > 0)
        def _():
            acc[...] += p

        @pl.when(i == ns - 1)
        def _():
            dwo_ref[...] = acc[...].astype(BF16)

    aspec = _gu_spec(tm, lambda j, i: (0, j, i, 0))
    return pl.pallas_call(
        body, name=name,
        out_shape=(jax.ShapeDtypeStruct((2, 4, s, FFB), BF16), jax.ShapeDtypeStruct((4, FFB, D), BF16)),
        grid=(4, ns),
        in_specs=[pl.BlockSpec((tm, D), lambda j, i: (i, 0)), aspec,
                  pl.BlockSpec((None, FFB, D), lambda j, i: (j, 0, 0))],
        out_specs=(aspec, pl.BlockSpec((None, FFB, D), lambda j, i: (j, 0, 0))),
        scratch_shapes=[pltpu.VMEM((FFB, D), F32)],
        compiler_params=_cparams(("parallel", "arbitrary"), VMEM_BIG),
    )(dy, gu, wo)


def _ffn_dh(da, wi, name):
    s = da.shape[2]
    tm = _tile(s, (1024, 512, 256, 128))

    def body(da_ref, wg_ref, wu_ref, o_ref, acc):
        r = pl.program_id(1)
        p = (lax.dot_general(da_ref[0], wg_ref[...], _NT, preferred_element_type=F32)
             + lax.dot_general(da_ref[1], wu_ref[...], _NT, preferred_element_type=F32))

        @pl.when(r == 0)
        def _():
            acc[...] = p

        @pl.when(r > 0)
        def _():
            acc[...] += p

        @pl.when(r == 3)
        def _():
            o_ref[...] = acc[...]

    wspec = lambda off: pl.BlockSpec((None, D, FFB), lambda i, r: (r + off, 0, 0))
    return pl.pallas_call(
        body, name=name, out_shape=jax.ShapeDtypeStruct((s, D), F32), grid=(s // tm, 4),
        in_specs=[_gu_spec(tm, lambda i, r: (0, r, i, 0)), wspec(0), wspec(4)],
        out_specs=pl.BlockSpec((tm, D), lambda i, r: (i, 0)),
        scratch_shapes=[pltpu.VMEM((tm, D), F32)],
        compiler_params=_cparams(("parallel", "arbitrary"), VMEM_BIG),
    )(da, wi, wi)


def _ffn_dwi(h, da, name):
    s = h.shape[0]
    ts = _tile(s, (2048, 1024, 512, 256, 128))
    return _mm(h, da, mode="tn", grid=(NDEV, 1, s // ts),
               a_spec=pl.BlockSpec((ts, D), lambda k, j, r: (r, 0)),
               b_spec=pl.BlockSpec((None, ts, FFB), lambda k, j, r: (k, r, 0)),
               o_spec=pl.BlockSpec((None, D, FFB), lambda k, j, r: (k, 0, 0)),
               o_shape=(NDEV, D, FFB), acc_shape=(D, FFB), out_dtype=BF16, name=name)


TR = 256


def _rows_spec(s):
    tr = _tile(s, (TR, 128))
    return tr, pl.BlockSpec((tr, D), lambda i: (i, 0))


def _pspec(l, sub):
    return pl.BlockSpec((None, None, 8, D), lambda i: (l, sub, 0, 0))


def _pre_math(x, p_ref):
    r = lax.rsqrt(jnp.mean(x * x, axis=1, keepdims=True) + RMS_EPS)
    return (x * r) * p_ref[0:1, :] * (1.0 + p_ref[1:2, :]) + p_ref[2:3, :]


ANYSPEC = pl.BlockSpec(memory_space=pl.ANY)


def _prenorm(x, ptab, l, sub, name, toks=()):
    s = x.shape[0]
    tr, spec = _rows_spec(s)

    def body(x_ref, p_ref, *rest):
        rest[-1][...] = _pre_math(x_ref[...], p_ref).astype(BF16)

    return pl.pallas_call(
        body, name=name, out_shape=jax.ShapeDtypeStruct((s, D), BF16), grid=(s // tr,),
        in_specs=[spec, _pspec(l, sub)] + [ANYSPEC] * len(toks), out_specs=spec,
        compiler_params=_cparams(("parallel",)),
    )(x, ptab, *toks)


def _resid(x, y, ptab, l, sub, coef, nxt, name, toks=()):
    s = x.shape[0]
    tr, spec = _rows_spec(s)

    def body(x_ref, y_ref, p_ref, *rest):
        yv = y_ref[...]
        r = lax.rsqrt(jnp.mean(yv * yv, axis=1, keepdims=True) + RMS_EPS)
        xn = x_ref[...] + (coef * p_ref[4:5, :]) * ((yv * r) * p_ref[3:4, :])
        if nxt is None:
            rest[-1][...] = xn
        else:
            rest[-2][...] = xn
            rest[-1][...] = _pre_math(xn, rest[0]).astype(BF16)

    tspecs = [ANYSPEC] * len(toks)
    if nxt is None:
        return pl.pallas_call(
            body, name=name, out_shape=jax.ShapeDtypeStruct((s, D), F32), grid=(s // tr,),
            in_specs=[spec, spec, _pspec(l, sub)] + tspecs, out_specs=spec,
            compiler_params=_cparams(("parallel",)),
        )(x, y, ptab, *toks), None
    return pl.pallas_call(
        body, name=name,
        out_shape=(jax.ShapeDtypeStruct((s, D), F32), jax.ShapeDtypeStruct((s, D), BF16)),
        grid=(s // tr,),
        in_specs=[spec, spec, _pspec(l, sub), _pspec(*nxt)] + tspecs, out_specs=(spec, spec),
        compiler_params=_cparams(("parallel",)),
    )(x, y, ptab, ptab, *toks)


def _loss_grad(x, tgt, name):
    s = x.shape[0]
    tr, spec = _rows_spec(s)

    def body(x_ref, t_ref, dx_ref, l_ref):
        @pl.when(pl.program_id(0) == 0)
        def _():
            l_ref[...] = jnp.zeros_like(l_ref)

        e = x_ref[...] - t_ref[...]
        dx_ref[...] = e * (1.0 / D)
        l_ref[0:1, :] += jnp.sum(e * e, axis=0, keepdims=True) * (0.5 / D)

    return pl.pallas_call(
        body, name=name,
        out_shape=(jax.ShapeDtypeStruct((s, D), F32), jax.ShapeDtypeStruct((8, D), F32)),
        grid=(s // tr,), in_specs=[spec, spec],
        out_specs=(spec, pl.BlockSpec((8, D), lambda i: (0, 0))),
        compiler_params=_cparams(("arbitrary",)),
    )(x, tgt)


def _post_bwd(dx, y, ptab, l, sub, coef, name, toks=()):
    s = dx.shape[0]
    tr, spec = _rows_spec(s)

    def body(dx_ref, y_ref, p_ref, *rest):
        dy_ref, red_ref = rest[-2:]

        @pl.when(pl.program_id(0) == 0)
        def _():
            red_ref[...] = jnp.zeros_like(red_ref)

        dxv, yv = dx_ref[...], y_ref[...]
        gpost, gate = p_ref[3:4, :], p_ref[4:5, :]
        r = lax.rsqrt(jnp.mean(yv * yv, axis=1, keepdims=True) + RMS_EPS)
        yhat = yv * r
        red_ref[0:1, :] += jnp.sum(dxv * yhat * gpost, axis=0, keepdims=True) * coef
        dn = dxv * (coef * gate)
        red_ref[1:2, :] += jnp.sum(dn * yhat, axis=0, keepdims=True)
        dyh = dn * gpost
        dy_ref[...] = (r * (dyh - yhat * jnp.mean(dyh * yhat, axis=1, keepdims=True))).astype(BF16)

    return pl.pallas_call(
        body, name=name,
        out_shape=(jax.ShapeDtypeStruct((s, D), BF16), jax.ShapeDtypeStruct((8, D), F32)),
        grid=(s // tr,), in_specs=[spec, spec, _pspec(l, sub)] + [ANYSPEC] * len(toks),
        out_specs=(spec, pl.BlockSpec((8, D), lambda i: (0, 0))),
        compiler_params=_cparams(("arbitrary",)),
    )(dx, y, ptab, *toks)


def _pre_bwd(dx, dh, x, ptab, l, sub, name, dh2=None, toks=()):
    s = dx.shape[0]
    tr, spec = _rows_spec(s)
    extra = [] if dh2 is None else [dh2]

    def body(dx_ref, dh_ref, x_ref, p_ref, *rest):
        o_ref, red_ref = rest[-2:]

        @pl.when(pl.program_id(0) == 0)
        def _():
            red_ref[...] = jnp.zeros_like(red_ref)

        dhv, xv = dh_ref[...], x_ref[...]
        if extra:
            dhv = dhv + rest[0][...]
        gpre, scale = p_ref[0:1, :], p_ref[1:2, :]
        r = lax.rsqrt(jnp.mean(xv * xv, axis=1, keepdims=True) + RMS_EPS)
        xhat = xv * r
        red_ref[0:1, :] += jnp.sum(dhv, axis=0, keepdims=True)
        red_ref[1:2, :] += jnp.sum(dhv * xhat * gpre, axis=0, keepdims=True)
        red_ref[2:3, :] += jnp.sum(dhv * xhat * (1.0 + scale), axis=0, keepdims=True)
        dxh = dhv * (gpre * (1.0 + scale))
        o_ref[...] = dx_ref[...] + r * (dxh - xhat * jnp.mean(dxh * xhat, axis=1, keepdims=True))

    return pl.pallas_call(
        body, name=name,
        out_shape=(jax.ShapeDtypeStruct((s, D), F32), jax.ShapeDtypeStruct((8, D), F32)),
        grid=(s // tr,), in_specs=[spec, spec, spec, _pspec(l, sub)] + [spec] * len(extra) + [ANYSPEC] * len(toks),
        out_specs=(spec, pl.BlockSpec((8, D), lambda i: (0, 0))),
        compiler_params=_cparams(("arbitrary",)),
    )(dx, dh, x, ptab, *extra, *toks)


def _cond_fwd(c_pad, wc, bc, name):
    w = wc.shape[2]

    def body(c_ref, w_ref, b_ref, o_ref, ca_ref):
        cv = c_ref[...]
        ca = (cv * _sigmoid(cv)).astype(BF16)
        ca_ref[...] = ca
        o_ref[...] = jnp.dot(ca, w_ref[...].astype(BF16), preferred_element_type=F32) + b_ref[...]

    return pl.pallas_call(
        body, name=name,
        out_shape=(jax.ShapeDtypeStruct((DEPTH, 128, w), F32), jax.ShapeDtypeStruct((128, D), BF16)),
        grid=(DEPTH,),
        in_specs=[pl.BlockSpec((128, D), lambda i: (0, 0)),
                  pl.BlockSpec((None, D, w), lambda i: (i, 0, 0)),
                  pl.BlockSpec((None, 1, w), lambda i: (i, 0, 0))],
        out_specs=(pl.BlockSpec((None, 128, w), lambda i: (i, 0, 0)),
                   pl.BlockSpec((128, D), lambda i: (0, 0))),
        compiler_params=_cparams(("arbitrary",), VMEM_BIG),
    )(c_pad, wc, bc)


def _cond_bwd(ca_pad, dmod, name):
    w = dmod.shape[2]
    return _mm(ca_pad, dmod, mode="tn", grid=(DEPTH, 1, 1),
               a_spec=pl.BlockSpec((128, D), lambda i, j, r: (0, 0)),
               b_spec=pl.BlockSpec((None, 128, w), lambda i, j, r: (i, 0, 0)),
               o_spec=pl.BlockSpec((None, D, w), lambda i, j, r: (i, 0, 0)),
               o_shape=(DEPTH, D, w), acc_shape=None, out_dtype=F32, name=name)


def _split3(x):
    hi = x.astype(BF16)
    r1 = x - hi.astype(F32)
    mid = r1.astype(BF16)
    lo = (r1 - mid.astype(F32)).astype(BF16)
    return hi, mid, lo


def _tri_dot(t, x):
    hi, mid, lo = _split3(x)
    return (jnp.dot(t, hi, preferred_element_type=F32) + jnp.dot(t, mid, preferred_element_type=F32)
            + jnp.dot(t, lo, preferred_element_type=F32))


def _fox_cum(fl, bf, name):
    s = fl.shape[0]
    tb = _tile(s, (256, 128))

    def body(fl_ref, b_ref, cum_ref):
        row = lax.broadcasted_iota(jnp.int32, (tb, tb), 0)
        col = lax.broadcasted_iota(jnp.int32, (tb, tb), 1)
        tri = (col <= row).astype(BF16)
        carry = jnp.zeros((1, 128), F32)
        for blk in range(s // tb):
            z = fl_ref[blk * tb:(blk + 1) * tb, :] + b_ref[0:1, :]
            lf = jnp.minimum(z, 0.0) - jnp.log(1.0 + jnp.exp(-jnp.abs(z)))
            cum_ref[blk * tb:(blk + 1) * tb, :] = _tri_dot(tri, lf) + carry
            carry = carry + jnp.sum(lf, axis=0, keepdims=True)

    return pl.pallas_call(
        body, name=name, out_shape=jax.ShapeDtypeStruct((s, 128), F32),
    )(fl, bf)


def _fox_cum_bwd(dcum, fl, bf, name):
    s = fl.shape[0]
    tb = _tile(s, (256, 128))

    def body(dc_ref, fl_ref, b_ref, dfl_ref, db_ref):
        row = lax.broadcasted_iota(jnp.int32, (tb, tb), 0)
        col = lax.broadcasted_iota(jnp.int32, (tb, tb), 1)
        tri = (col >= row).astype(BF16)
        carry = jnp.zeros((1, 128), F32)
        dbs = jnp.zeros((1, 128), F32)
        for blk in reversed(range(s // tb)):
            dc = dc_ref[blk * tb:(blk + 1) * tb, :]
            dl = _tri_dot(tri, dc) + carry
            carry = carry + jnp.sum(dc, axis=0, keepdims=True)
            z = fl_ref[blk * tb:(blk + 1) * tb, :] + b_ref[0:1, :]
            dz = dl * _sigmoid(-z)
            dfl_ref[blk * tb:(blk + 1) * tb, :] = dz
            dbs = dbs + jnp.sum(dz, axis=0, keepdims=True)
        db_ref[...] = jnp.broadcast_to(dbs, (8, 128))

    return pl.pallas_call(
        body, name=name,
        out_shape=(jax.ShapeDtypeStruct((s, 128), F32), jax.ShapeDtypeStruct((8, 128), F32)),
    )(dcum, fl, bf)


def _fox_qkv(h, w_qkv, name):
    s = h.shape[0]
    tm = _tile(s, (512, 256, 128))
    return _mm(h, w_qkv, mode="nn", grid=(s // tm, 3, 1),
               a_spec=pl.BlockSpec((tm, D), lambda i, j, r: (i, 0)),
               b_spec=pl.BlockSpec((D, D), lambda i, j, r: (0, j)),
               o_spec=pl.BlockSpec((None, tm, D), lambda i, j, r: (j, i, 0)),
               o_shape=(3, s, D), acc_shape=None, out_dtype=BF16, name=name)


def _fox_qkv_dx(dqkv, w_qkv, name):
    s = dqkv.shape[1]
    tm = _tile(s, (512, 256, 128))
    return _mm(dqkv, w_qkv, mode="nt", grid=(s // tm, 1, 3),
               a_spec=pl.BlockSpec((None, tm, D), lambda i, j, r: (r, i, 0)),
               b_spec=pl.BlockSpec((D, D), lambda i, j, r: (0, r)),
               o_spec=pl.BlockSpec((tm, D), lambda i, j, r: (i, 0)),
               o_shape=(s, D), acc_shape=(tm, D), out_dtype=F32, name=name)


def _fox_qkv_dw(h, dqkv, name):
    s = h.shape[0]
    ts = _tile(s, (2048, 1024, 512, 256, 128))
    return _mm(h, dqkv, mode="tn", grid=(3, 1, s // ts),
               a_spec=pl.BlockSpec((ts, D), lambda i, j, r: (r, 0)),
               b_spec=pl.BlockSpec((None, ts, D), lambda i, j, r: (i, r, 0)),
               o_spec=pl.BlockSpec((None, D, D), lambda i, j, r: (i, 0, 0)),
               o_shape=(3, D, D), acc_shape=(D, D), out_dtype=BF16, name=name)


HPAIRS = HEADS // 2


def _first_head():
    return lax.broadcasted_iota(jnp.int32, (1, 2 * HDIM), 1) < HDIM


def _one_head(x, sel):
    return jnp.where(sel, x, jnp.zeros_like(x))


def _fox_scores(qm, k_ref, cq, ck, qi, tq, n):
    sc = lax.dot_general(qm, k_ref[0:n, :], _NT, preferred_element_type=F32) * (HDIM ** -0.5)
    sc = sc + cq - ck
    row = lax.broadcasted_iota(jnp.int32, (tq, n), 0) + qi * tq
    col = lax.broadcasted_iota(jnp.int32, (tq, n), 1)
    return sc, col <= row


def _fox_specs(s):
    sect = lambda i: pl.BlockSpec((None, s, 2 * HDIM), lambda p: (i, 0, p))
    ospec = pl.BlockSpec((s, 2 * HDIM), lambda p: (0, p))
    cspec = pl.BlockSpec((s, 128), lambda p: (0, 0))
    rspec = pl.BlockSpec((None, 2, 1, s), lambda p: (p, 0, 0, 0))
    return sect, ospec, cspec, rspec


def _head_lane(hh):
    return lax.broadcasted_iota(jnp.int32, (1, 128), 1) == 2 * pl.program_id(0) + hh


def _pick(tile, hsel):
    return jnp.sum(jnp.where(hsel, tile, 0.0), axis=1, keepdims=True)


def _fox_attn_fwd(qkv, cum, ck, name):
    s = qkv.shape[1]
    tq = _tile(s, (256, 128))

    def body(q_ref, k_ref, v_ref, cum_ref, ck_ref, o_ref, lse_ref):
        first = _first_head()

        @pl.when(pl.program_id(0) == 0)
        def _():
            lse_ref[...] = jnp.zeros_like(lse_ref)

        for qi in range(s // tq):
            n = (qi + 1) * tq
            rows = slice(qi * tq, n)
            q2 = q_ref[rows, :]
            cum_t, lse_t = cum_ref[rows, :], lse_ref[rows, :]
            outs = []
            for hh in range(2):
                sel = first if hh == 0 else jnp.logical_not(first)
                hsel = _head_lane(hh)
                sc, keep = _fox_scores(_one_head(q2, sel), k_ref, _pick(cum_t, hsel), ck_ref[hh, :, 0:n], qi, tq, n)
                sc = jnp.where(keep, sc, -1e30)
                m = jnp.max(sc, axis=1, keepdims=True)
                p = jnp.exp(sc - m)
                lsum = jnp.sum(p, axis=1, keepdims=True)
                outs.append(jnp.dot(p.astype(BF16), v_ref[0:n, :], preferred_element_type=F32) / lsum)
                lse_t = jnp.where(hsel, m + jnp.log(lsum), lse_t)
            lse_ref[rows, :] = lse_t
            o_ref[rows, :] = jnp.where(first, outs[0], outs[1]).astype(BF16)

    sect, ospec, cspec, rspec = _fox_specs(s)
    return pl.pallas_call(
        body, name=name,
        out_shape=(jax.ShapeDtypeStruct((s, D), BF16), jax.ShapeDtypeStruct((s, 128), F32)),
        grid=(HPAIRS,), in_specs=[sect(0), sect(1), sect(2), cspec, rspec], out_specs=(ospec, cspec),
        compiler_params=_cparams(("arbitrary",), VMEM_BIG),
    )(qkv, qkv, qkv, cum, ck)


def _fox_attn_bwd(qkv, do, lse, cum, ck, name):
    s = qkv.shape[1]
    tq = _tile(s, (256, 128))
    scale = HDIM ** -0.5

    def body(q_ref, k_ref, v_ref, do_ref, lse_ref, cum_ref, ck_ref,
             dqkv_ref, dcq_ref, dck_ref, dk_acc, dv_acc, dck_acc):
        first = _first_head()

        @pl.when(pl.program_id(0) == 0)
        def _():
            dcq_ref[...] = jnp.zeros_like(dcq_ref)

        dk_acc[...] = jnp.zeros_like(dk_acc)
        dv_acc[...] = jnp.zeros_like(dv_acc)
        dck_acc[...] = jnp.zeros_like(dck_acc)
        for qi in range(s // tq):
            n = (qi + 1) * tq
            rows = slice(qi * tq, n)
            q2, do2 = q_ref[rows, :], do_ref[rows, :]
            cum_t, lse_t, dcq_t = cum_ref[rows, :], lse_ref[rows, :], dcq_ref[rows, :]
            dq, dk, dv = [], [], []
            for hh in range(2):
                sel = first if hh == 0 else jnp.logical_not(first)
                hsel = _head_lane(hh)
                sc, keep = _fox_scores(_one_head(q2, sel), k_ref, _pick(cum_t, hsel), ck_ref[hh, :, 0:n], qi, tq, n)
                p = jnp.where(keep, jnp.exp(sc - _pick(lse_t, hsel)), 0.0)
                dp = lax.dot_general(_one_head(do2, sel), v_ref[0:n, :], _NT, preferred_element_type=F32)
                ds = p * (dp - jnp.sum(p * dp, axis=1, keepdims=True))
                dsb = ds.astype(BF16)
                dq.append(jnp.dot(dsb, k_ref[0:n, :], preferred_element_type=F32))
                dk.append(lax.dot_general(dsb, q2, _TN, preferred_element_type=F32))
                dv.append(lax.dot_general(p.astype(BF16), do2, _TN, preferred_element_type=F32))
                dcq_t = jnp.where(hsel, jnp.sum(ds, axis=1, keepdims=True), dcq_t)
                dck_acc[hh, :, 0:n] -= jnp.sum(ds, axis=0, keepdims=True)
            dcq_ref[rows, :] = dcq_t
            dqkv_ref[0, rows, :] = (jnp.where(first, dq[0], dq[1]) * scale).astype(BF16)
            dk_acc[0:n, :] += jnp.where(first, dk[0], dk[1]) * scale
            dv_acc[0:n, :] += jnp.where(first, dv[0], dv[1])
        dqkv_ref[1] = dk_acc[...].astype(BF16)
        dqkv_ref[2] = dv_acc[...].astype(BF16)
        dck_ref[...] = dck_acc[...]

    sect, ospec, cspec, rspec = _fox_specs(s)
    return pl.pallas_call(
        body, name=name,
        out_shape=(jax.ShapeDtypeStruct((3, s, D), BF16), jax.ShapeDtypeStruct((s, 128), F32),
                   jax.ShapeDtypeStruct((HPAIRS, 2, 1, s), F32)),
        grid=(HPAIRS,), in_specs=[sect(0), sect(1), sect(2), ospec, cspec, cspec, rspec],
        out_specs=(pl.BlockSpec((3, s, 2 * HDIM), lambda p: (0, 0, p)), cspec, rspec),
        scratch_shapes=[pltpu.VMEM((s, 2 * HDIM), F32), pltpu.VMEM((s, 2 * HDIM), F32), pltpu.VMEM((2, 1, s), F32)],
        compiler_params=_cparams(("arbitrary",), VMEM_BIG),
    )(qkv, qkv, qkv, do, lse, cum, ck)


TC = 256
HALO = 8


def _chunk_specs(s):
    tc = _tile(s, (TC, 128))
    per = tc // HALO
    nblk = s // HALO
    cur = pl.BlockSpec((tc, 8, 128), lambda i: (i, 0, 0))
    past = pl.BlockSpec((HALO, 8, 128), lambda i: (jnp.maximum(i * per - 1, 0), 0, 0))
    future = pl.BlockSpec((HALO, 8, 128), lambda i: (jnp.minimum((i + 1) * per, nblk - 1), 0, 0))
    return tc, cur, past, future


def _vec_spec(n):
    return pl.BlockSpec((n, 8, 128), lambda i: (0, 0, 0))


def _conv_past(buf, w_ref, kw, tc):
    out = w_ref[kw - 1] * buf[HALO:HALO + tc]
    for k in range(kw - 1):
        off = HALO - (kw - 1) + k
        out = out + w_ref[k] * buf[off:off + tc]
    return out


def _sconv_fwd(bg, cg, xv, w, name):
    s = bg.shape[0]
    tc, cur, past, _ = _chunk_specs(s)

    def body(bg_ref, cg_ref, xv_ref, cgp_ref, xvp_ref, w_ref, y_ref, zbuf):
        first = pl.program_id(0) == 0
        zbuf[0:HALO] = jnp.where(first, 0.0, cgp_ref[...] * xvp_ref[...])
        zbuf[HALO:HALO + tc] = cg_ref[...] * xv_ref[...]
        y_ref[...] = bg_ref[...] * _conv_past(zbuf, w_ref, 3, tc)

    return pl.pallas_call(
        body, name=name, out_shape=jax.ShapeDtypeStruct((s, 8, 128), F32), grid=(s // tc,),
        in_specs=[cur, cur, cur, past, past, _vec_spec(3)], out_specs=cur,
        scratch_shapes=[pltpu.VMEM((tc + HALO, 8, 128), F32)],
        compiler_params=_cparams(("parallel",)),
    )(bg, cg, xv, cg, xv, w)


def _sconv_bwd(dy, bg, cg, xv, w, name):
    s = dy.shape[0]
    tc, cur, past, future = _chunk_specs(s)
    nch = s // tc

    def body(dy_ref, bg_ref, cg_ref, xv_ref, cgp_ref, xvp_ref, dyf_ref, bgf_ref, w_ref,
             dbg_ref, dcg_ref, dxv_ref, dw_ref, zbuf, dbuf):
        i = pl.program_id(0)

        @pl.when(i == 0)
        def _():
            dw_ref[...] = jnp.zeros_like(dw_ref)

        z = cg_ref[...] * xv_ref[...]
        zbuf[0:HALO] = jnp.where(i == 0, 0.0, cgp_ref[...] * xvp_ref[...])
        zbuf[HALO:HALO + tc] = z
        dyv = dy_ref[...]
        dbg_ref[...] = dyv * _conv_past(zbuf, w_ref, 3, tc)
        dbuf[0:tc] = dyv * bg_ref[...]
        dbuf[tc:tc + HALO] = jnp.where(i == nch - 1, 0.0, dyf_ref[...] * bgf_ref[...])
        dz = jnp.zeros((tc, 8, 128), F32)
        for k in range(3):
            sh = dbuf[2 - k:2 - k + tc]
            dz = dz + w_ref[k] * sh
            dw_ref[k] += jnp.sum(z * sh, axis=0)
        dcg_ref[...] = dz * xv_ref[...]
        dxv_ref[...] = dz * cg_ref[...]

    shp = jax.ShapeDtypeStruct((s, 8, 128), F32)
    return pl.pallas_call(
        body, name=name, out_shape=(shp, shp, shp, jax.ShapeDtypeStruct((3, 8, 128), F32)),
        grid=(nch,),
        in_specs=[cur, cur, cur, cur, past, past, future, future, _vec_spec(3)],
        out_specs=(cur, cur, cur, _vec_spec(3)),
        scratch_shapes=[pltpu.VMEM((tc + HALO, 8, 128), F32), pltpu.VMEM((tc + HALO, 8, 128), F32)],
        compiler_params=_cparams(("arbitrary",)),
    )(dy, bg, cg, xv, cg, xv, dy, bg, w)


def _lru_conv_fwd(xp, wb, name):
    s = xp.shape[0]
    tc, cur, past, _ = _chunk_specs(s)

    def body(x_ref, xp_ref, w_ref, o_ref, buf):
        buf[0:HALO] = jnp.where(pl.program_id(0) == 0, 0.0, xp_ref[...])
        buf[HALO:HALO + tc] = x_ref[...]
        o_ref[...] = _conv_past(buf, w_ref, 4, tc) + w_ref[4]

    return pl.pallas_call(
        body, name=name, out_shape=jax.ShapeDtypeStruct((s, 8, 128), F32), grid=(s // tc,),
        in_specs=[cur, past, _vec_spec(8)], out_specs=cur,
        scratch_shapes=[pltpu.VMEM((tc + HALO, 8, 128), F32)],
        compiler_params=_cparams(("parallel",)),
    )(xp, xp, wb)


def _lru_conv_bwd(dxb, xp, wb, name):
    s = dxb.shape[0]
    tc, cur, _, future = _chunk_specs(s)
    nch = s // tc

    def body(d_ref, df_ref, x_ref, w_ref, o_ref, red_ref, dbuf):
        i = pl.program_id(0)

        @pl.when(i == 0)
        def _():
            red_ref[...] = jnp.zeros_like(red_ref)

        dv = d_ref[...]
        dbuf[0:tc] = dv
        dbuf[tc:tc + HALO] = jnp.where(i == nch - 1, 0.0, df_ref[...])
        xv = x_ref[...]
        dx = jnp.zeros((tc, 8, 128), F32)
        for k in range(4):
            sh = dbuf[3 - k:3 - k + tc]
            dx = dx + w_ref[k] * sh
            red_ref[k] += jnp.sum(xv * sh, axis=0)
        red_ref[4] += jnp.sum(dv, axis=0)
        o_ref[...] = dx

    return pl.pallas_call(
        body, name=name,
        out_shape=(jax.ShapeDtypeStruct((s, 8, 128), F32), jax.ShapeDtypeStruct((8, 8, 128), F32)),
        grid=(nch,), in_specs=[cur, future, cur, _vec_spec(8)], out_specs=(cur, _vec_spec(8)),
        scratch_shapes=[pltpu.VMEM((tc + HALO, 8, 128), F32)],
        compiler_params=_cparams(("arbitrary",)),
    )(dxb, dxb, xp, wb)


def _lru_gates(ra, ia, pv_ref):
    sp = _softplus(-pv_ref[2])
    r = _sigmoid(ra + pv_ref[0])
    ig = _sigmoid(ia + pv_ref[1])
    log_a = (-LRU_C) * r * sp
    a = jnp.exp(log_a)
    mult = jnp.sqrt(-jnp.tanh(log_a) * (a * a + 1.0))
    return r, ig, a, mult, sp


def _lru_scan_fwd(ra, ia, xb, gate, pv, name):
    s = ra.shape[0]
    tc, cur, _, _ = _chunk_specs(s)

    def body(ra_ref, ia_ref, xb_ref, g_ref, pv_ref, hs_ref, hp_ref, y_ref, abuf, bbuf, hcar):
        @pl.when(pl.program_id(0) == 0)
        def _():
            hcar[...] = jnp.zeros_like(hcar)

        xbv = xb_ref[...]
        _, ig, a, mult, _ = _lru_gates(ra_ref[...], ia_ref[...], pv_ref)
        abuf[...] = a
        bbuf[...] = mult * (ig * xbv)

        def step(t, h):
            hp_ref[t] = h
            h = abuf[t] * h + bbuf[t]
            hs_ref[t] = h
            return h

        hcar[...] = lax.fori_loop(0, tc, step, hcar[...], unroll=8)
        y_ref[...] = hs_ref[...] * _gelu_parts(g_ref[...])[0]

    shp = jax.ShapeDtypeStruct((s, 8, 128), F32)
    return pl.pallas_call(
        body, name=name, out_shape=(shp, shp, shp), grid=(s // tc,),
        in_specs=[cur, cur, cur, cur, _vec_spec(8)], out_specs=(cur, cur, cur),
        scratch_shapes=[pltpu.VMEM((tc, 8, 128), F32), pltpu.VMEM((tc, 8, 128), F32),
                        pltpu.VMEM((8, 128), F32)],
        compiler_params=_cparams(("arbitrary",)),
    )(ra, ia, xb, gate, pv)


def _lru_scan_bwd(dy, ra, ia, xb, gate, hs, hp, pv, name):
    s = dy.shape[0]
    tc = _tile(s, (TC, 128))
    nch = s // tc
    rev = pl.BlockSpec((tc, 8, 128), lambda i: (nch - 1 - i, 0, 0))

    def body(dy_ref, ra_ref, ia_ref, xb_ref, g_ref, hs_ref, hp_ref, pv_ref,
             dg_ref, dra_ref, dia_ref, dxb_ref, red_ref, abuf, dbuf, gbuf, car):
        @pl.when(pl.program_id(0) == 0)
        def _():
            red_ref[...] = jnp.zeros_like(red_ref)
            car[...] = jnp.zeros_like(car)

        xbv = xb_ref[...]
        r, ig, a, mult, sp = _lru_gates(ra_ref[...], ia_ref[...], pv_ref)
        ge, dge = _gelu_parts(g_ref[...])
        dyv = dy_ref[...]
        dg_ref[...] = dyv * hs_ref[...] * dge
        abuf[...] = a
        dbuf[...] = dyv * ge

        def step(k, c):
            t = tc - 1 - k
            g = dbuf[t] + c
            gbuf[t] = g
            return abuf[t] * g

        car[...] = lax.fori_loop(0, tc, step, car[...], unroll=8)
        g = gbuf[...]
        d_a = g * hp_ref[...]
        d_m = g * (ig * xbv)
        d_loga = d_a * a - d_m * (a * a / mult)
        dxb_ref[...] = g * mult * ig
        dra = d_loga * ((-LRU_C) * sp) * r * (1.0 - r)
        dia = g * mult * xbv * ig * (1.0 - ig)
        dra_ref[...] = dra
        dia_ref[...] = dia
        red_ref[0] += jnp.sum(dra, axis=0)
        red_ref[1] += jnp.sum(dia, axis=0)
        red_ref[2] += jnp.sum(d_loga * r, axis=0) * (LRU_C * _sigmoid(-pv_ref[2]))

    shp = jax.ShapeDtypeStruct((s, 8, 128), F32)
    return pl.pallas_call(
        body, name=name, out_shape=(shp, shp, shp, shp, jax.ShapeDtypeStruct((8, 8, 128), F32)),
        grid=(nch,), in_specs=[rev] * 7 + [_vec_spec(8)],
        out_specs=(rev, rev, rev, rev, _vec_spec(8)),
        scratch_shapes=[pltpu.VMEM((tc, 8, 128), F32), pltpu.VMEM((tc, 8, 128), F32),
                        pltpu.VMEM((tc, 8, 128), F32), pltpu.VMEM((8, 128), F32)],
        compiler_params=_cparams(("arbitrary",)),
    )(dy, ra, ia, xb, gate, hs, hp, pv)


def _adamw(src, w, m, v, name):
    nl, n, rr, cc = src.shape
    tr = rr
    for cand in sorted((d for d in range(16, rr + 1, 16) if rr % d == 0), reverse=True):
        if cand * cc <= 192 * 1024:
            tr = cand
            break
    c1 = 1.0 - ADAM_B1 ** ADAM_STEP
    c2 = 1.0 - ADAM_B2 ** ADAM_STEP

    def body(s_ref, w_ref, m_ref, v_ref, g_out, d_out, m_out, v_out):
        g = s_ref[0].astype(F32)
        for k in range(1, n):
            g = g + s_ref[k].astype(F32)
        mn = ADAM_B1 * m_ref[...] + (1.0 - ADAM_B1) * g
        vn = ADAM_B2 * v_ref[...] + (1.0 - ADAM_B2) * (g * g)
        g_out[...] = g
        m_out[...] = mn
        v_out[...] = vn
        d_out[...] = (-ADAM_LR) * ((mn / c1) / (jnp.sqrt(vn / c2) + ADAM_EPS) + ADAM_WD * w_ref[...])

    pspec = pl.BlockSpec((None, tr, cc), lambda l, i: (l, i, 0))
    shp = jax.ShapeDtypeStruct((nl, rr, cc), F32)
    return pl.pallas_call(
        body, name=name, out_shape=(shp, shp, shp, shp), grid=(nl, rr // tr),
        in_specs=[pl.BlockSpec((None, n, tr, cc), lambda l, i: (l, 0, i, 0)), pspec, pspec, pspec],
        out_specs=(pspec, pspec, pspec, pspec),
        compiler_params=_cparams(("parallel", "parallel"), VMEM_BIG),
    )(src, w, m, v)


def _adamw_slice(src, w, m, v, bufs, idx, name):
    n, rr, cc = src.shape
    nl = w.shape[0]
    tr = rr
    for cand in sorted((d for d in range(16, rr + 1, 16) if rr % d == 0), reverse=True):
        if cand * cc <= 192 * 1024:
            tr = cand
            break
    c1 = 1.0 - ADAM_B1 ** ADAM_STEP
    c2 = 1.0 - ADAM_B2 ** ADAM_STEP
    if bufs is None:
        bufs = tuple(lax.empty((nl, rr, cc), F32) for _ in range(4))

    def body(s_ref, w_ref, m_ref, v_ref, b0, b1, b2, b3, g_out, d_out, m_out, v_out):
        g = s_ref[0].astype(F32)
        for k in range(1, n):
            g = g + s_ref[k].astype(F32)
        mn = ADAM_B1 * m_ref[...] + (1.0 - ADAM_B1) * g
        vn = ADAM_B2 * v_ref[...] + (1.0 - ADAM_B2) * (g * g)
        g_out[...] = g
        m_out[...] = mn
        v_out[...] = vn
        d_out[...] = (-ADAM_LR) * ((mn / c1) / (jnp.sqrt(vn / c2) + ADAM_EPS) + ADAM_WD * w_ref[...])

    pspec = pl.BlockSpec((None, tr, cc), lambda i: (idx, i, 0))
    anyspec = pl.BlockSpec(memory_space=pl.ANY)
    shp = jax.ShapeDtypeStruct((nl, rr, cc), F32)
    return pl.pallas_call(
        body, name=name, out_shape=(shp, shp, shp, shp), grid=(rr // tr,),
        in_specs=[pl.BlockSpec((n, tr, cc), lambda i: (0, i, 0)), pspec, pspec, pspec,
                  anyspec, anyspec, anyspec, anyspec],
        out_specs=(pspec, pspec, pspec, pspec),
        input_output_aliases={4: 0, 5: 1, 6: 2, 7: 3},
        compiler_params=_cparams(("parallel",), VMEM_BIG),
    )(src, w, m, v, *bufs)


def _sum_devices(x, name):
    _, rr, cc = x.shape

    def body(x_ref, o_ref):
        acc = x_ref[0]
        for k in range(1, NDEV):
            acc = acc + x_ref[k]
        o_ref[...] = acc

    return pl.pallas_call(
        body, name=name, out_shape=jax.ShapeDtypeStruct((rr, cc), F32), grid=(rr // 8,),
        in_specs=[pl.BlockSpec((NDEV, 8, cc), lambda i: (0, i, 0))],
        out_specs=pl.BlockSpec((8, cc), lambda i: (i, 0)),
        compiler_params=_cparams(("parallel",)),
    )(x)


def _to3(x):
    return x.reshape(x.shape[0], 8, 128)


def _heads(x):
    s = x.shape[0]
    return x.reshape(s, HEADS, HDIM).transpose(1, 0, 2).astype(BF16)


def _unheads(x):
    s = x.shape[1]
    return x.transpose(1, 0, 2).reshape(s, D)


def _block_diag(w):
    eye = jnp.eye(HEADS, dtype=w.dtype)
    return (w[:, :, None, :] * eye[:, None, :, None]).reshape(D, D)


def _diag_blocks(x):
    return jnp.diagonal(x.reshape(HEADS, HDIM, HEADS, HDIM), axis1=0, axis2=2).transpose(2, 0, 1)


def _col_blocks(dw, n):
    return dw.reshape(dw.shape[0], NDEV, n).transpose(1, 0, 2)


def kernel(x, c, w_cond, b_cond, norm_pre, norm_post, w_ffn_in, w_ffn_out, fox_w_in, fox_b_f, fox_w_out, sconv_w_in, sconv_conv_w, sconv_w_out, lru_w_in, lru_conv_w, lru_conv_b, lru_w_a, lru_b_a, lru_w_x, lru_b_x, lru_lambda, lru_w_out, loss_target, m_w_cond, m_b_cond, m_norm_pre, m_norm_post, m_w_ffn_in, m_w_ffn_out, m_fox_w_in, m_fox_b_f, m_fox_w_out, m_sconv_w_in, m_sconv_conv_w, m_sconv_w_out, m_lru_w_in, m_lru_conv_w, m_lru_conv_b, m_lru_w_a, m_lru_b_a, m_lru_w_x, m_lru_b_x, m_lru_lambda, m_lru_w_out, v_w_cond, v_b_cond, v_norm_pre, v_norm_post, v_w_ffn_in, v_w_ffn_out, v_fox_w_in, v_fox_b_f, v_fox_w_out, v_sconv_w_in, v_sconv_conv_w, v_sconv_w_out, v_lru_w_in, v_lru_conv_w, v_lru_conv_b, v_lru_w_a, v_lru_b_a, v_lru_w_x, v_lru_b_x, v_lru_lambda, v_lru_w_out):
    me = 4 * lax.axis_index("x") + 2 * lax.axis_index("y") + lax.axis_index("c")
    s = x.shape[1]
    x0 = x[0]
    tgt = loss_target[0]
    wcw = w_cond.shape[2]
    dsh = norm_pre.shape[2]

    small_parts = [c.reshape(-1), norm_pre.reshape(-1), norm_post.reshape(-1), sconv_conv_w.reshape(-1),
                   lru_conv_w.reshape(-1), lru_conv_b.reshape(-1), lru_lambda.reshape(-1)]
    sizes = [p.shape[0] for p in small_parts]
    flat = jnp.concatenate(small_parts)
    padn = (-flat.shape[0]) % 1024
    flat = jnp.pad(flat, (0, padn)).reshape(-1, 1024)
    sm = _all_gather(flat, 0, "ag_small").reshape(NDEV, -1)
    offs = [0]
    for n_ in sizes:
        offs.append(offs[-1] + n_)
    piece = lambda i: sm[:, offs[i]:offs[i + 1]]
    c_all = piece(0)
    unshard = lambda p, lead: p.reshape((NDEV,) + lead + (dsh,)).transpose(
        tuple(range(1, len(lead) + 1)) + (0, len(lead) + 1)).reshape(lead + (D,))
    npre = unshard(piece(1), (DEPTH, 3))
    npost = unshard(piece(2), (DEPTH, 3))
    scw = unshard(piece(3), (3,))
    lcw = unshard(piece(4), (4,))
    lcb = unshard(piece(5), ())
    llam = unshard(piece(6), ())

    c_pad = jnp.pad(c_all, ((0, 128 - NDEV), (0, 0)))
    bc_mine = lax.dynamic_slice(b_cond, (0, me * wcw), (DEPTH, wcw)).reshape(DEPTH, 1, wcw)
    modc, ca_pad = _cond_fwd(c_pad, w_cond, bc_mine, "cond_fwd")
    modg = _all_gather(modc[:, :NDEV, :], 0, "ag_mod")
    mod = lax.dynamic_index_in_dim(modg, me, axis=2, keepdims=False)
    mod = mod.transpose(1, 0, 2).reshape(DEPTH, 3, 3, D)
    ptab = jnp.stack([npre, mod[:, :, 1], mod[:, :, 0], npost, mod[:, :, 2],
                      jnp.zeros_like(npre), jnp.zeros_like(npre), jnp.zeros_like(npre)], axis=2)

    need = []
    for l in range(DEPTH):
        need += [(f"wi{l}0", w_ffn_in[l, 0]), (f"wo{l}0", w_ffn_out[l, 0])]
        if l % 3 == 0:
            need += [(f"fwi{l // 3}", fox_w_in[l // 3]), (f"fwo{l // 3}", fox_w_out[l // 3])]
        elif l % 3 == 1:
            need += [("swi", sconv_w_in[0]), ("swo", sconv_w_out[0])]
        else:
            need += [("lwi", lru_w_in[0]), ("lwo", lru_w_out[0])]
        need += [(f"wi{l}1", w_ffn_in[l, 1]), (f"wo{l}1", w_ffn_out[l, 1])]
    ag_keys = [k_ for k_, _ in need]
    lands = _alloc_many([(NDEV,) + w_.shape for _, w_ in need], BF16, "alloc_gather")
    pend = {}
    chain = [jnp.zeros((8, 128), F32)]
    for (key, w_), land in zip(need, lands):
        pend[key], chain[0] = _gather_start((w_ + chain[0][0, 0]).astype(BF16), land, "ags_" + key)
    relayed = [0]
    relay_toks = []

    def ag_relay(upto, after):
        while relayed[0] <= min(upto, len(ag_keys) - 1):
            key = ag_keys[relayed[0]]
            pend[key], tok = _gather_relay(pend[key], after, "agr_" + key)
            relay_toks.append(tok)
            relayed[0] += 1

    def ag_wait(key, after):
        return _gather_wait(pend.pop(key), after, "agw_" + key)

    ag_relay(1, chain[0])
    cols = lambda wg: wg.transpose(1, 0, 2).reshape(D, -1)
    wgate = jnp.concatenate([_block_diag(lru_w_a[0]), _block_diag(lru_w_x[0])], axis=1).astype(BF16)
    bf_pad = jnp.pad(fox_b_f, ((0, 0), (0, 128 - HEADS))).reshape(2, 1, 128)
    scw3 = scw.reshape(3, 8, 128)
    lcwb = jnp.concatenate([lcw, lcb[None], jnp.zeros((3, D), F32)], axis=0).reshape(8, 8, 128)
    lpv = jnp.concatenate([lru_b_a.reshape(1, D), lru_b_x.reshape(1, D), llam[None],
                           jnp.zeros((5, D), F32)], axis=0).reshape(8, 8, 128)

    subs = [(l, sub) for l in range(DEPTH) for sub in range(3)]
    coef = lambda sub: 1.0 if sub == 1 else 0.5

    saved = {}
    xcur = x0
    h = _prenorm(xcur, ptab, 0, 0, "prenorm", toks=(chain[0],))
    for idx, (l, sub) in enumerate(subs):
        tag = f"{l}{sub}"
        sv = {"x": xcur, "h": h}
        if sub != 1:
            f = 0 if sub == 0 else 1
            wi = ag_wait(f"wi{l}{f}", h)
            wo = ag_wait(f"wo{l}{f}", h).reshape(4, FFB, D)
            gu, act = _ffn_up(h, wi, "ffn_up_" + tag)
            y = _ffn_down(act, wo, "ffn_down_" + tag)
            sv.update(gu=gu, wi=wi, wo=wo)
        elif l % 3 == 0:
            j = l // 3
            fwi_j = cols(ag_wait(f"fwi{j}", h))
            w_qkv = fwi_j[:, :3 * D]
            w_f = jnp.pad(fwi_j[:, 3 * D:], ((0, 0), (0, 128 - HEADS)))
            fwo_j = ag_wait(f"fwo{j}", h).reshape(D, D)
            qkv = _fox_qkv(h, w_qkv, "fox_qkv_" + tag)
            fl = _mm_nn(h, w_f, F32, "fox_f_" + tag)
            cum = _fox_cum(fl, bf_pad[j], "fox_cum_" + tag)
            ck = cum[:, :HEADS].T.reshape(HPAIRS, 2, 1, s)
            o2, lse = _fox_attn_fwd(qkv, cum, ck, "fox_attn_" + tag)
            y = _mm_nn(o2, fwo_j, F32, "fox_out_" + tag)
            sv.update(w_qkv=w_qkv, w_f=w_f, fwo=fwo_j, qkv=qkv, fl=fl, cum=cum, ck=ck, lse=lse, o2=o2)
        elif l % 3 == 1:
            swi = cols(ag_wait("swi", h))
            swo = ag_wait("swo", h).reshape(D, D)
            proj = _mm_nn(h, swi, F32, "sconv_in_" + tag)
            bg, cg, xv = (_to3(proj[:, i * D:(i + 1) * D]) for i in range(3))
            y3 = _sconv_fwd(bg, cg, xv, scw3, "sconv_mix_" + tag)
            y2 = y3.reshape(s, D).astype(BF16)
            y = _mm_nn(y2, swo, F32, "sconv_out_" + tag)
            sv.update(bg=bg, cg=cg, xv=xv, y2=y2)
        else:
            lwi = cols(ag_wait("lwi", h))
            lwo = ag_wait("lwo", h).reshape(D, D)
            proj = _mm_nn(h, lwi, F32, "lru_in_" + tag)
            gate3, xp3 = _to3(proj[:, :D]), _to3(proj[:, D:])
            xb3 = _lru_conv_fwd(xp3, lcwb, "lru_conv_" + tag)
            xb2 = xb3.reshape(s, D).astype(BF16)
            gpre = _mm_nn(xb2, wgate, F32, "lru_gate_" + tag)
            ra3, ia3 = _to3(gpre[:, :D]), _to3(gpre[:, D:])
            hs3, hp3, y3 = _lru_scan_fwd(ra3, ia3, xb3, gate3, lpv, "lru_scan_" + tag)
            y2 = y3.reshape(s, D).astype(BF16)
            y = _mm_nn(y2, lwo, F32, "lru_out_" + tag)
            sv.update(gate3=gate3, xp3=xp3, xb3=xb3, xb2=xb2, ra3=ra3, ia3=ia3, hs3=hs3, hp3=hp3, y2=y2)
        sv["y"] = y
        saved[(l, sub)] = sv
        nxt = subs[idx + 1] if idx + 1 < len(subs) else None
        ag_relay(2 * idx + 3, y)
        xcur, h = _resid(xcur, y, ptab, l, sub, coef(sub), nxt, "resid_" + tag, toks=tuple(relay_toks))
        del relay_toks[:]

    dx, loss_cols = _loss_grad(xcur, tgt, "loss_grad")

    d_mod = [[[None] * 3 for _ in range(3)] for _ in range(DEPTH)]
    d_npre = [[None] * 3 for _ in range(DEPTH)]
    d_npost = [[None] * 3 for _ in range(DEPTH)]
    d_fbf = [None, None]
    small_g = {}
    rs_pend, rs_order = {}, []
    toks = []
    dh2 = None
    rs_lands = dict(zip(ag_keys, _alloc_many([(NDEV,) + w_.shape for _, w_ in need], BF16, "alloc_scatter")))

    def rs_start(key, blocks):
        hnd = _exchange_start(blocks, "rss_" + key, False, land=rs_lands[key])
        rs_pend[key] = hnd[:4]
        rs_order.append(key)
        toks.append(hnd[4])

    for (l, sub) in reversed(subs):
        tag = f"{l}{sub}"
        sv = saved[(l, sub)]
        dy, red = _post_bwd(dx, sv["y"], ptab, l, sub, coef(sub), "post_bwd_" + tag, toks=tuple(toks))
        del toks[:]
        d_mod[l][sub][2] = red[0]
        d_npost[l][sub] = red[1]
        hb = sv["h"]
        if sub != 1:
            f = 0 if sub == 0 else 1
            da, dwo = _ffn_down_bwd(dy, sv["gu"], sv["wo"], "ffn_down_bwd_" + tag)
            rs_start(f"wo{l}{f}", dwo.reshape(NDEV, FFB // 2, D))
            rs_start(f"wi{l}{f}", _ffn_dwi(hb, da.reshape(NDEV, s, FFB), "ffn_dwi_" + tag))
            dh = _ffn_dh(da, sv["wi"], "ffn_dh_" + tag)
        elif l % 3 == 0:
            j = l // 3
            do2 = _mm_nt(dy, sv["fwo"], BF16, "fox_out_dx_" + tag)
            rs_start(f"fwo{j}", _mm_tn(sv["o2"], dy, BF16, "fox_out_dw_" + tag).reshape(NDEV, dsh, D))
            dqkv, dcq, dck = _fox_attn_bwd(sv["qkv"], do2, sv["lse"], sv["cum"], sv["ck"], "fox_attn_bwd_" + tag)
            dcum = dcq + jnp.pad(dck.reshape(HEADS, s).T, ((0, 0), (0, 128 - HEADS)))
            dfl, dbf = _fox_cum_bwd(dcum, sv["fl"], bf_pad[j], "fox_cum_bwd_" + tag)
            d_fbf[j] = dbf[0, :HEADS]
            dflb = dfl.astype(BF16)
            dh = _fox_qkv_dx(dqkv, sv["w_qkv"], "fox_qkv_dx_" + tag)
            dh2 = _mm_nt(dflb, sv["w_f"], F32, "fox_f_dx_" + tag)
            dw3 = _fox_qkv_dw(hb, dqkv, "fox_qkv_dw_" + tag)
            dwf = _mm_tn(hb, dflb, BF16, "fox_f_dw_" + tag)
            dwfox = jnp.concatenate([dw3[0], dw3[1], dw3[2], dwf[:, :HEADS]], axis=1)
            rs_start(f"fwi{j}", _col_blocks(dwfox, fox_w_in.shape[2]))
        elif l % 3 == 1:
            dy3 = _to3(_mm_nt(dy, swo, F32, "sconv_out_dx_" + tag))
            rs_start("swo", _mm_tn(sv["y2"], dy, BF16, "sconv_out_dw_" + tag).reshape(NDEV, dsh, D))
            dbg, dcg, dxv, dscw = _sconv_bwd(dy3, sv["bg"], sv["cg"], sv["xv"], scw3, "sconv_mix_bwd_" + tag)
            small_g["sconv_conv_w"] = dscw.reshape(3, D)
            dproj = jnp.concatenate([t.reshape(s, D) for t in (dbg, dcg, dxv)], axis=1).astype(BF16)
            dh = _mm_nt(dproj, swi, F32, "sconv_in_dx_" + tag)
            rs_start("swi", _col_blocks(_mm_tn(hb, dproj, BF16, "sconv_in_dw_" + tag), sconv_w_in.shape[2]))
        else:
            dy3 = _to3(_mm_nt(dy, lwo, F32, "lru_out_dx_" + tag))
            rs_start("lwo", _mm_tn(sv["y2"], dy, BF16, "lru_out_dw_" + tag).reshape(NDEV, dsh, D))
            dgate3, dra3, dia3, dxb3, lred = _lru_scan_bwd(
                dy3, sv["ra3"], sv["ia3"], sv["xb3"], sv["gate3"], sv["hs3"], sv["hp3"], lpv, "lru_scan_bwd_" + tag)
            dgp = jnp.concatenate([dra3.reshape(s, D), dia3.reshape(s, D)], axis=1).astype(BF16)
            dxb3 = dxb3 + _to3(_mm_nt(dgp, wgate, F32, "lru_gate_dx_" + tag))
            dwgate = _mm_tn(sv["xb2"], dgp, F32, "lru_gate_dw_" + tag)
            dxp3, cred = _lru_conv_bwd(dxb3, sv["xp3"], lcwb, "lru_conv_bwd_" + tag)
            lred, cred = lred.reshape(8, D), cred.reshape(8, D)
            small_g.update(lru_w_a=_diag_blocks(dwgate[:, :D]), lru_w_x=_diag_blocks(dwgate[:, D:]),
                           lru_b_a=lred[0], lru_b_x=lred[1], lru_lambda=lred[2],
                           lru_conv_w=cred[:4], lru_conv_b=cred[4])
            dproj = jnp.concatenate([dgate3.reshape(s, D), dxp3.reshape(s, D)], axis=1).astype(BF16)
            dh = _mm_nt(dproj, lwi, F32, "lru_in_dx_" + tag)
            rs_start("lwi", _col_blocks(_mm_tn(hb, dproj, BF16, "lru_in_dw_" + tag), lru_w_in.shape[2]))
        dx, red = _pre_bwd(dx, dh, sv["x"], ptab, l, sub, "pre_bwd_" + tag, dh2=dh2, toks=tuple(toks))
        dh2 = None
        del toks[:]
        d_mod[l][sub][0] = red[0]
        d_mod[l][sub][1] = red[1]
        d_npre[l][sub] = red[2]
    grad_x = dx[None]

    dmod_mine = jnp.stack([jnp.stack([jnp.stack(d_mod[l][sub]) for sub in range(3)]) for l in range(DEPTH)])
    gparts = [loss_cols[0], dmod_mine.reshape(-1),
              jnp.stack([jnp.stack(r_) for r_ in d_npre]).reshape(-1),
              jnp.stack([jnp.stack(r_) for r_ in d_npost]).reshape(-1),
              jnp.stack(d_fbf).reshape(-1), small_g["sconv_conv_w"].reshape(-1),
              small_g["lru_conv_w"].reshape(-1), small_g["lru_conv_b"].reshape(-1),
              small_g["lru_w_a"].reshape(-1), small_g["lru_b_a"].reshape(-1),
              small_g["lru_w_x"].reshape(-1), small_g["lru_b_x"].reshape(-1),
              small_g["lru_lambda"].reshape(-1)]
    gsizes = [p.shape[0] for p in gparts]
    gflat = jnp.concatenate(gparts)
    gflat = jnp.pad(gflat, (0, (-gflat.shape[0]) % (8 * 1024))).reshape(-1, 1024)
    small_hnd = _exchange_start(gflat, "ags_smallgrads", True)

    out = {}
    after = [small_hnd[4]]

    def update(keys, w, m, v, name):
        got = {}
        for k_ in sorted(keys, key=rs_order.index):
            got[k_] = after[0] = _exchange_wait(rs_pend.pop(k_), after[0], "rsw_" + k_, False)
        src = jnp.stack([got[k_] for k_ in keys])
        shp = w.shape
        w3, m3, v3 = (t.reshape((src.shape[0],) + src.shape[2:]) for t in (w, m, v))
        res = tuple(t.reshape(shp) for t in _adamw(src, w3, m3, v3, "adamw_" + name))
        after[0] = res[1]
        return res

    lf = [f"{l}{f}" for l in range(DEPTH) for f in range(2)]
    out["fox_w_in"] = update(["fwi0", "fwi1"], fox_w_in, m_fox_w_in, v_fox_w_in, "fwi")
    out["fox_w_out"] = update(["fwo0", "fwo1"], fox_w_out, m_fox_w_out, v_fox_w_out, "fwo")
    out["lru_w_out"] = update(["lwo"], lru_w_out, m_lru_w_out, v_lru_w_out, "lwo")
    out["lru_w_in"] = update(["lwi"], lru_w_in, m_lru_w_in, v_lru_w_in, "lwi")
    out["sconv_w_out"] = update(["swo"], sconv_w_out, m_sconv_w_out, v_sconv_w_out, "swo")
    out["sconv_w_in"] = update(["swi"], sconv_w_in, m_sconv_w_in, v_sconv_w_in, "swi")

    stacked = {"wi": [t.reshape((-1,) + w_ffn_in.shape[2:]) for t in (w_ffn_in, m_w_ffn_in, v_w_ffn_in)],
               "wo": [t.reshape((-1,) + w_ffn_out.shape[2:]) for t in (w_ffn_out, m_w_ffn_out, v_w_ffn_out)]}
    bufs = {"wi": None, "wo": None}

    def update_slice(k_):
        kind = k_[:2]
        src = after[0] = _exchange_wait(rs_pend.pop(k_), after[0], "rsw_" + k_, False)
        bufs[kind] = _adamw_slice(src, *stacked[kind], bufs[kind], lf.index(k_[2:]), "adamw_" + k_)
        after[0] = bufs[kind][1]

    for k_ in rs_order:
        if k_[:2] in ("wi", "wo") and k_ != "wi00":
            update_slice(k_)

    gall = _exchange_wait(small_hnd[:4], after[0], "agw_smallgrads", True)
    gsum = _sum_devices(gall, "sum_smallgrads").reshape(-1)
    goffs = [0]
    for n_ in gsizes:
        goffs.append(goffs[-1] + n_)
    gpiece = lambda i: gsum[goffs[i]:goffs[i + 1]]
    loss = jnp.sum(gpiece(0))
    my_ch = lambda g, lead: lax.dynamic_slice_in_dim(g.reshape(lead + (D,)), me * dsh, dsh, axis=len(lead))

    dmod_all = gall.reshape(NDEV, -1)[:, goffs[1]:goffs[2]].reshape(NDEV, DEPTH, 3 * 3 * D)
    dmod_cols = lax.dynamic_slice_in_dim(dmod_all, me * wcw, wcw, axis=2).transpose(1, 0, 2)
    dmod_cols = jnp.pad(dmod_cols, ((0, 0), (0, 128 - NDEV), (0, 0))).astype(BF16)
    g_wcond = _cond_bwd(ca_pad, dmod_cols, "cond_bwd")

    out["w_cond"] = _adamw(g_wcond[:, None], w_cond, m_w_cond, v_w_cond, "adamw_wcond")

    small = [
        ("b_cond", b_cond, m_b_cond, v_b_cond, gpiece(1)),
        ("norm_pre", norm_pre, m_norm_pre, v_norm_pre, my_ch(gpiece(2), (DEPTH, 3))),
        ("norm_post", norm_post, m_norm_post, v_norm_post, my_ch(gpiece(3), (DEPTH, 3))),
        ("fox_b_f", fox_b_f, m_fox_b_f, v_fox_b_f, gpiece(4)),
        ("sconv_conv_w", sconv_conv_w, m_sconv_conv_w, v_sconv_conv_w, my_ch(gpiece(5), (1, 3))),
        ("lru_conv_w", lru_conv_w, m_lru_conv_w, v_lru_conv_w, my_ch(gpiece(6), (1, 4))),
        ("lru_conv_b", lru_conv_b, m_lru_conv_b, v_lru_conv_b, my_ch(gpiece(7), (1,))),
        ("lru_w_a", lru_w_a, m_lru_w_a, v_lru_w_a, gpiece(8)),
        ("lru_b_a", lru_b_a, m_lru_b_a, v_lru_b_a, gpiece(9)),
        ("lru_w_x", lru_w_x, m_lru_w_x, v_lru_w_x, gpiece(10)),
        ("lru_b_x", lru_b_x, m_lru_b_x, v_lru_b_x, gpiece(11)),
        ("lru_lambda", lru_lambda, m_lru_lambda, v_lru_lambda, my_ch(gpiece(12), (1,))),
    ]
    pack = lambda ts: jnp.concatenate([t.reshape(-1) for t in ts])
    ssz = [w_.size for _, w_, _, _, _ in small]
    tot = sum(ssz)
    padr = (-tot) % (16 * 1024)
    pk = lambda ts: jnp.pad(pack(ts), (0, padr)).reshape(1, -1, 1024)
    sg, sd, smm, svv = _adamw(pk([t[4] for t in small])[:, None], pk([t[1] for t in small]),
                              pk([t[2] for t in small]), pk([t[3] for t in small]), "adamw_small")
    soff = 0
    for (name, w_, _, _, _), n_ in zip(small, ssz):
        out[name] = tuple(t.reshape(-1)[soff:soff + n_].reshape(w_.shape) for t in (sg, sd, smm, svv))
        soff += n_

    after[0] = sd
    update_slice("wi00")
    out["w_ffn_in"] = tuple(t.reshape(w_ffn_in.shape) for t in bufs["wi"])
    out["w_ffn_out"] = tuple(t.reshape(w_ffn_out.shape) for t in bufs["wo"])

    names =["w_cond", "b_cond", "norm_pre", "norm_post", "w_ffn_in", "w_ffn_out", "fox_w_in", "fox_b_f",
             "fox_w_out", "sconv_w_in", "sconv_conv_w", "sconv_w_out", "lru_w_in", "lru_conv_w", "lru_conv_b",
             "lru_w_a", "lru_b_a", "lru_w_x", "lru_b_x", "lru_lambda", "lru_w_out"]
    return (loss, grad_x, *[out[n_][0] for n_ in names], *[out[n_][1] for n_ in names],
            *[out[n_][2] for n_ in names], *[out[n_][3] for n_ in names])
```

```python
import functools
import math

import jax
import jax.numpy as jnp
from jax import lax
from jax.experimental import pallas as pl
from jax.experimental.pallas import tpu as pltpu

F32 = jnp.float32
BF16 = jnp.bfloat16
NDEV = 8
D = 1024
DFF = 2816
FFB = 704
HEADS = 16
HDIM = 64
DEPTH = 4
RMS_EPS = 1e-6
LRU_C = 8.0
ADAM_LR, ADAM_B1, ADAM_B2, ADAM_EPS, ADAM_WD, ADAM_STEP = 0.001, 0.9, 0.999, 1e-08, 0.01, 10
MESH = pl.DeviceIdType.MESH
HBM = pl.BlockSpec(memory_space=pltpu.HBM)
VMEM_BIG = 48 * 1024 * 1024

_NN = (((1,), (0,)), ((), ()))
_NT = (((1,), (1,)), ((), ()))
_TN = (((0,), (0,)), ((), ()))
_DIMS = {"nn": _NN, "nt": _NT, "tn": _TN}


def _cparams(sem=None, vmem=None):
    kw = {}
    if sem is not None:
        kw["dimension_semantics"] = sem
    if vmem is not None:
        kw["vmem_limit_bytes"] = vmem
    return pltpu.CompilerParams(**kw)


def _sigmoid(x):
    return 1.0 / (1.0 + jnp.exp(-x))


def _softplus(x):
    return jnp.maximum(x, 0.0) + jnp.log(1.0 + jnp.exp(-jnp.abs(x)))


_GELU_C = math.sqrt(2.0 / math.pi)


def _gelu_parts(x):
    u = _GELU_C * (x + 0.044715 * x * x * x)
    t = jnp.tanh(u)
    g = 0.5 * x * (1.0 + t)
    dg = 0.5 * (1.0 + t) + 0.5 * x * (1.0 - t * t) * _GELU_C * (1.0 + 3.0 * 0.044715 * x * x)
    return g, dg


def _mesh_pos():
    ax, ay, ac = lax.axis_index("x"), lax.axis_index("y"), lax.axis_index("c")
    return ax, ay, ac, 4 * ax + 2 * ay + ac


def _peer(ax, ay, ac, d):
    px = 1 - ax if (d >> 2) & 1 else ax
    py = 1 - ay if (d >> 1) & 1 else ay
    pc = 1 - ac if d & 1 else ac
    return (px, py, pc), 4 * px + 2 * py + pc


def _exchange(x, axis, name, gather):
    if gather:
        x = jnp.expand_dims(x, axis)
        oshape = x.shape[:axis] + (NDEV,) + x.shape[axis + 1:]
    else:
        oshape = x.shape
    lead = (slice(None),) * axis

    def blk(ref, k):
        return ref.at[lead + (pl.ds(k, 1),)]

    def body(x_ref, o_ref, send_sems, recv_sems, local_sem):
        ax, ay, ac, me = _mesh_pos()
        src = (lambda k: x_ref) if gather else (lambda k: blk(x_ref, k))
        mine = pltpu.make_async_copy(src(me), blk(o_ref, me), local_sem)
        mine.start()
        sends = []
        for d in range(1, NDEV):
            peer, pidx = _peer(ax, ay, ac, d)
            cp = pltpu.make_async_remote_copy(
                src_ref=src(pidx), dst_ref=blk(o_ref, me), send_sem=send_sems.at[d - 1],
                recv_sem=recv_sems.at[d - 1], device_id=peer, device_id_type=MESH)
            cp.start()
            sends.append(cp)
        for d in range(1, NDEV):
            peer, pidx = _peer(ax, ay, ac, d)
            pltpu.make_async_remote_copy(
                src_ref=src(pidx), dst_ref=blk(o_ref, pidx), send_sem=send_sems.at[d - 1],
                recv_sem=recv_sems.at[d - 1], device_id=peer, device_id_type=MESH).wait_recv()
        for cp in sends:
            cp.wait_send()
        mine.wait()

    return pl.pallas_call(
        body, name=name, out_shape=jax.ShapeDtypeStruct(oshape, x.dtype),
        in_specs=[HBM], out_specs=HBM,
        scratch_shapes=[pltpu.SemaphoreType.DMA((NDEV - 1,)), pltpu.SemaphoreType.DMA((NDEV - 1,)),
                        pltpu.SemaphoreType.DMA(())],
    )(x)


def _all_gather(x, axis, name):
    return _exchange(x, axis, name, True)


SEM = pl.BlockSpec(memory_space=pltpu.SEMAPHORE)
EFFECT = pltpu.SideEffectType.DATAFLOW_SIDE_EFFECTING


def _exchange_start(x, name, gather, land=None):
    me = 4 * lax.axis_index("x") + 2 * lax.axis_index("y") + lax.axis_index("c")
    if gather:
        x = x[None]
        oshape = (NDEV,) + x.shape[1:]
        own = x
    else:
        oshape = x.shape
        own = lax.dynamic_index_in_dim(x, me, 0, keepdims=True)
    land = lax.empty(oshape, x.dtype) if land is None else land
    land = lax.dynamic_update_index_in_dim(land, own, me, 0)

    def blk(ref, k):
        return ref.at[pl.ds(k, 1)]

    def body(x_ref, land_ref, send_sems, recv_sems, x_thru, land_thru, token):
        ax, ay, ac, me = _mesh_pos()
        src = (lambda k: x_ref) if gather else (lambda k: blk(x_ref, k))
        for d in range(1, NDEV):
            peer, pidx = _peer(ax, ay, ac, d)
            pltpu.make_async_remote_copy(
                src_ref=src(pidx), dst_ref=blk(land_ref, me), send_sem=send_sems.at[d - 1],
                recv_sem=recv_sems.at[d - 1], device_id=peer, device_id_type=MESH).start()
        token[...] = jnp.zeros_like(token)

    return pl.pallas_call(
        body, name=name,
        out_shape=(pltpu.SemaphoreType.DMA((NDEV - 1,)), pltpu.SemaphoreType.DMA((NDEV - 1,)),
                   pltpu.HBM(x.shape, x.dtype), pltpu.HBM(oshape, x.dtype),
                   jax.ShapeDtypeStruct((8, 128), F32)),
        in_specs=(HBM, HBM), out_specs=(SEM, SEM, HBM, HBM, pl.BlockSpec(memory_space=pltpu.VMEM)),
        input_output_aliases={0: 2, 1: 3},
        compiler_params=pltpu.CompilerParams(has_side_effects=EFFECT),
    )(pltpu.with_memory_space_constraint(x, pltpu.HBM),
      pltpu.with_memory_space_constraint(land, pltpu.HBM))


def _exchange_wait(handle, after, name, gather):
    send_sems, recv_sems, x_thru, land_thru = handle

    def blk(ref, k):
        return ref.at[pl.ds(k, 1)]

    def body(x_ref, land_ref, send_sems, recv_sems, after_ref, x_dead, got_ref):
        ax, ay, ac, me = _mesh_pos()
        src = (lambda k: x_ref) if gather else (lambda k: blk(x_ref, k))
        for d in range(1, NDEV):
            peer, pidx = _peer(ax, ay, ac, d)
            cp = pltpu.make_async_remote_copy(
                src_ref=src(pidx), dst_ref=blk(land_ref, pidx), send_sem=send_sems.at[d - 1],
                recv_sem=recv_sems.at[d - 1], device_id=peer, device_id_type=MESH)
            cp.wait_send()
            cp.wait_recv()

    return pl.pallas_call(
        body, name=name,
        out_shape=(pltpu.HBM(x_thru.shape, x_thru.dtype), pltpu.HBM(land_thru.shape, land_thru.dtype)),
        in_specs=(HBM, HBM, SEM, SEM, pl.BlockSpec(memory_space=pl.ANY)), out_specs=(HBM, HBM),
        input_output_aliases={0: 0, 1: 1},
        compiler_params=pltpu.CompilerParams(has_side_effects=EFFECT),
    )(x_thru, land_thru, send_sems, recv_sems, after)[1]


_HOPS = (1, 2, 4, 6)


def _blk(ref, k):
    return ref.at[pl.ds(k, 1)]


def _alloc_many(shapes, dtype, name):
    def body(*refs):
        pass

    return pl.pallas_call(
        body, name=name, out_shape=tuple(pltpu.HBM(s_, dtype) for s_ in shapes),
        out_specs=tuple(HBM for _ in shapes),
    )()


def _gather_start(x, land, name):
    me = 4 * lax.axis_index("x") + 2 * lax.axis_index("y") + lax.axis_index("c")
    x = x[None]
    oshape = (NDEV,) + x.shape[1:]
    land = lax.dynamic_update_index_in_dim(land, x, me, 0)

    def body(x_ref, land_ref, send_sems, recv_sems, x_thru, land_thru, token):
        ax, ay, ac, me = _mesh_pos()
        for i, d in enumerate(_HOPS):
            peer, _ = _peer(ax, ay, ac, d)
            pltpu.make_async_remote_copy(
                src_ref=x_ref, dst_ref=_blk(land_ref, me), send_sem=send_sems.at[i],
                recv_sem=recv_sems.at[i], device_id=peer, device_id_type=MESH).start()
        token[...] = jnp.zeros_like(token)

    send1, recv1, x_thru, land_thru, token = pl.pallas_call(
        body, name=name,
        out_shape=(pltpu.SemaphoreType.DMA((4,)), pltpu.SemaphoreType.DMA((4,)),
                   pltpu.HBM(x.shape, x.dtype), pltpu.HBM(oshape, x.dtype),
                   jax.ShapeDtypeStruct((8, 128), F32)),
        in_specs=(HBM, HBM), out_specs=(SEM, SEM, HBM, HBM, pl.BlockSpec(memory_space=pltpu.VMEM)),
        input_output_aliases={0: 2, 1: 3},
        compiler_params=pltpu.CompilerParams(has_side_effects=EFFECT),
    )(pltpu.with_memory_space_constraint(x, pltpu.HBM),
      pltpu.with_memory_space_constraint(land, pltpu.HBM))
    return dict(send1=send1, recv1=recv1, x=x_thru, land=land_thru), token


def _gather_relay(hnd, after, name):
    def body(land_ref, recv1, after_ref, send2, recv2, land_thru, token):
        ax, ay, ac, me = _mesh_pos()
        sibling, _ = _peer(ax, ay, ac, 1)
        for i, d in enumerate(_HOPS[1:]):
            peer, pidx = _peer(ax, ay, ac, d)
            came = _blk(land_ref, pidx)
            pltpu.make_async_remote_copy(
                src_ref=came, dst_ref=came, send_sem=send2.at[i], recv_sem=recv1.at[i + 1],
                device_id=peer, device_id_type=MESH).wait_recv()
            pltpu.make_async_remote_copy(
                src_ref=came, dst_ref=came, send_sem=send2.at[i], recv_sem=recv2.at[i],
                device_id=sibling, device_id_type=MESH).start()
        token[...] = jnp.zeros_like(token)

    land = hnd["land"]
    send2, recv2, land_thru, token = pl.pallas_call(
        body, name=name,
        out_shape=(pltpu.SemaphoreType.DMA((3,)), pltpu.SemaphoreType.DMA((3,)),
                   pltpu.HBM(land.shape, land.dtype), jax.ShapeDtypeStruct((8, 128), F32)),
        in_specs=(HBM, SEM, ANYSPEC), out_specs=(SEM, SEM, HBM, pl.BlockSpec(memory_space=pltpu.VMEM)),
        input_output_aliases={0: 2},
        compiler_params=pltpu.CompilerParams(has_side_effects=EFFECT),
    )(land, hnd["recv1"], after)
    return dict(hnd, send2=send2, recv2=recv2, land=land_thru), token


def _gather_wait(hnd, after, name):
    def body(x_ref, land_ref, send1, recv1, send2, recv2, after_ref, x_dead, got_ref):
        ax, ay, ac, me = _mesh_pos()
        sibling, sidx = _peer(ax, ay, ac, 1)
        for i, d in enumerate(_HOPS):
            peer, pidx = _peer(ax, ay, ac, d)
            cp = pltpu.make_async_remote_copy(
                src_ref=x_ref, dst_ref=_blk(land_ref, pidx), send_sem=send1.at[i], recv_sem=recv1.at[i],
                device_id=peer, device_id_type=MESH)
            cp.wait_send()
            if i == 0:
                cp.wait_recv()
        for i, d in enumerate(_HOPS[1:]):
            _, pidx = _peer(ax, ay, ac, d)
            _, fidx = _peer(ax, ay, ac, d ^ 1)
            cp = pltpu.make_async_remote_copy(
                src_ref=_blk(land_ref, pidx), dst_ref=_blk(land_ref, fidx), send_sem=send2.at[i],
                recv_sem=recv2.at[i], device_id=sibling, device_id_type=MESH)
            cp.wait_send()
            cp.wait_recv()

    x, land = hnd["x"], hnd["land"]
    return pl.pallas_call(
        body, name=name,
        out_shape=(pltpu.HBM(x.shape, x.dtype), pltpu.HBM(land.shape, land.dtype)),
        in_specs=(HBM, HBM, SEM, SEM, SEM, SEM, ANYSPEC), out_specs=(HBM, HBM),
        input_output_aliases={0: 0, 1: 1},
        compiler_params=pltpu.CompilerParams(has_side_effects=EFFECT),
    )(x, land, hnd["send1"], hnd["recv1"], hnd["send2"], hnd["recv2"], after)[1]


def _mm(a, b, *, mode, grid, a_spec, b_spec, o_spec, o_shape, acc_shape, out_dtype, name):
    nred = grid[2]

    def body(a_ref, b_ref, o_ref, *scratch):
        p = lax.dot_general(a_ref[...], b_ref[...], _DIMS[mode], preferred_element_type=F32)
        if nred == 1:
            o_ref[...] = p.astype(o_ref.dtype)
            return
        acc = scratch[0]
        r = pl.program_id(2)

        @pl.when(r == 0)
        def _():
            acc[...] = p

        @pl.when(r > 0)
        def _():
            acc[...] += p

        @pl.when(r == nred - 1)
        def _():
            o_ref[...] = acc[...].astype(o_ref.dtype)

    return pl.pallas_call(
        body, name=name, out_shape=jax.ShapeDtypeStruct(o_shape, out_dtype), grid=grid,
        in_specs=[a_spec, b_spec], out_specs=o_spec,
        scratch_shapes=[] if nred == 1 else [pltpu.VMEM(acc_shape, F32)],
        compiler_params=_cparams(("parallel", "parallel", "arbitrary"), VMEM_BIG),
    )(a, b)


def _tile(n, cands):
    for c in cands:
        if n % c == 0:
            return c
    return n


def _mm_nn(a, b, out_dtype, name):
    m, k = a.shape
    n = b.shape[1]
    tm, tn = _tile(m, (512, 256, 128)), _tile(n, (1024, 640, 512, 256, 128))
    return _mm(a, b, mode="nn", grid=(m // tm, n // tn, 1),
               a_spec=pl.BlockSpec((tm, k), lambda i, j, r: (i, 0)),
               b_spec=pl.BlockSpec((k, tn), lambda i, j, r: (0, j)),
               o_spec=pl.BlockSpec((tm, tn), lambda i, j, r: (i, j)),
               o_shape=(m, n), acc_shape=None, out_dtype=out_dtype, name=name)


def _mm_nt(a, b, out_dtype, name):
    m, n = a.shape
    k = b.shape[0]
    tm, tn = _tile(m, (512, 256, 128)), _tile(n, (1024, 640, 512, 256, 128))
    return _mm(a, b, mode="nt", grid=(m // tm, 1, n // tn),
               a_spec=pl.BlockSpec((tm, tn), lambda i, j, r: (i, r)),
               b_spec=pl.BlockSpec((k, tn), lambda i, j, r: (0, r)),
               o_spec=pl.BlockSpec((tm, k), lambda i, j, r: (i, 0)),
               o_shape=(m, k), acc_shape=(tm, k), out_dtype=out_dtype, name=name)


def _mm_tn(a, b, out_dtype, name):
    s, k = a.shape
    n = b.shape[1]
    ts, tn = _tile(s, (2048, 1024, 512, 256, 128)), _tile(n, (640, 512, 256, 128))
    return _mm(a, b, mode="tn", grid=(1, n // tn, s // ts),
               a_spec=pl.BlockSpec((ts, k), lambda i, j, r: (r, 0)),
               b_spec=pl.BlockSpec((ts, tn), lambda i, j, r: (r, j)),
               o_spec=pl.BlockSpec((k, tn), lambda i, j, r: (0, j)),
               o_shape=(k, n), acc_shape=(k, tn), out_dtype=out_dtype, name=name)


def _gu_spec(tm, idx):
    return pl.BlockSpec((2, None, tm, FFB), idx)


def _ffn_up(h, wi, name):
    s = h.shape[0]
    tm = _tile(s, (1024, 512, 256, 128))

    def body(h_ref, wg_ref, wu_ref, gu_ref, act_ref):
        hv = h_ref[...]
        g = jnp.dot(hv, wg_ref[...], preferred_element_type=F32)
        u = jnp.dot(hv, wu_ref[...], preferred_element_type=F32)
        gu_ref[0] = g.astype(BF16)
        gu_ref[1] = u.astype(BF16)
        act_ref[...] = (g * _sigmoid(g) * u).astype(BF16)

    wspec = lambda off: pl.BlockSpec((None, D, FFB), lambda j, i: (j + off, 0, 0))
    return pl.pallas_call(
        body, name=name,
        out_shape=(jax.ShapeDtypeStruct((2, 4, s, FFB), BF16), jax.ShapeDtypeStruct((4, s, FFB), BF16)),
        grid=(4, s // tm),
        in_specs=[pl.BlockSpec((tm, D), lambda j, i: (i, 0)), wspec(0), wspec(4)],
        out_specs=(_gu_spec(tm, lambda j, i: (0, j, i, 0)), pl.BlockSpec((None, tm, FFB), lambda j, i: (j, i, 0))),
        compiler_params=_cparams(("parallel", "parallel"), VMEM_BIG),
    )(h, wi, wi)


def _ffn_down(act, wo, name):
    s = act.shape[1]
    tm = _tile(s, (1024, 512, 256, 128))
    return _mm(act, wo, mode="nn", grid=(s // tm, 1, 4),
               a_spec=pl.BlockSpec((None, tm, FFB), lambda i, j, r: (r, i, 0)),
               b_spec=pl.BlockSpec((None, FFB, D), lambda i, j, r: (r, 0, 0)),
               o_spec=pl.BlockSpec((tm, D), lambda i, j, r: (i, 0)),
               o_shape=(s, D), acc_shape=(tm, D), out_dtype=F32, name=name)


def _ffn_down_bwd(dy, gu, wo, name):
    s = dy.shape[0]
    tm = _tile(s, (512, 256, 128))
    ns = s // tm

    def body(dy_ref, gu_ref, wo_ref, da_ref, dwo_ref, acc):
        i = pl.program_id(1)
        dyv = dy_ref[...]
        dact = lax.dot_general(dyv, wo_ref[...], _NT, preferred_element_type=F32)
        gv = gu_ref[0].astype(F32)
        uv = gu_ref[1].astype(F32)
        sg = _sigmoid(gv)
        silu = gv * sg
        da_ref[0] = (dact * uv * (sg * (1.0 + gv * (1.0 - sg)))).astype(BF16)
        da_ref[1] = (dact * silu).astype(BF16)
        p = lax.dot_general((silu * uv).astype(BF16), dyv, _TN, preferred_element_type=F32)

        @pl.when(i == 0)
        def _():
            acc[...] = p

        @pl.when(i > 0)
        def _():
            acc[...] += p

        @pl.when(i == ns - 1)
        def _():
            dwo_ref[...] = acc[...].astype(BF16)

    aspec = _gu_spec(tm, lambda j, i: (0, j, i, 0))
    return pl.pallas_call(
        body, name=name,
        out_shape=(jax.ShapeDtypeStruct((2, 4, s, FFB), BF16), jax.ShapeDtypeStruct((4, FFB, D), BF16)),
        grid=(4, ns),
        in_specs=[pl.BlockSpec((tm, D), lambda j, i: (i, 0)), aspec,
                  pl.BlockSpec((None, FFB, D), lambda j, i: (j, 0, 0))],
        out_specs=(aspec, pl.BlockSpec((None, FFB, D), lambda j, i: (j, 0, 0))),
        scratch_shapes=[pltpu.VMEM((FFB, D), F32)],
        compiler_params=_cparams(("parallel", "arbitrary"), VMEM_BIG),
    )(dy, gu, wo)


def _ffn_dh(da, wi, name):
    s = da.shape[2]
    tm = _tile(s, (1024, 512, 256, 128))

    def body(da_ref, wg_ref, wu_ref, o_ref, acc):
        r = pl.program_id(1)
        p = (lax.dot_general(da_ref[0], wg_ref[...], _NT, preferred_element_type=F32)
             + lax.dot_general(da_ref[1], wu_ref[...], _NT, preferred_element_type=F32))

        @pl.when(r == 0)
        def _():
            acc[...] = p

        @pl.when(r > 0)
        def _():
            acc[...] += p

        @pl.when(r == 3)
        def _():
            o_ref[...] = acc[...]

    wspec = lambda off: pl.BlockSpec((None, D, FFB), lambda i, r: (r + off, 0, 0))
    return pl.pallas_call(
        body, name=name, out_shape=jax.ShapeDtypeStruct((s, D), F32), grid=(s // tm, 4),
        in_specs=[_gu_spec(tm, lambda i, r: (0, r, i, 0)), wspec(0), wspec(4)],
        out_specs=pl.BlockSpec((tm, D), lambda i, r: (i, 0)),
        scratch_shapes=[pltpu.VMEM((tm, D), F32)],
        compiler_params=_cparams(("parallel", "arbitrary"), VMEM_BIG),
    )(da, wi, wi)


def _ffn_dwi(h, da, name):
    s = h.shape[0]
    ts = _tile(s, (2048, 1024, 512, 256, 128))
    return _mm(h, da, mode="tn", grid=(NDEV, 1, s // ts),
               a_spec=pl.BlockSpec((ts, D), lambda k, j, r: (r, 0)),
               b_spec=pl.BlockSpec((None, ts, FFB), lambda k, j, r: (k, r, 0)),
               o_spec=pl.BlockSpec((None, D, FFB), lambda k, j, r: (k, 0, 0)),
               o_shape=(NDEV, D, FFB), acc_shape=(D, FFB), out_dtype=BF16, name=name)


TR = 256


def _rows_spec(s):
    tr = _tile(s, (TR, 128))
    return tr, pl.BlockSpec((tr, D), lambda i: (i, 0))


def _pspec(l, sub):
    return pl.BlockSpec((None, None, 8, D), lambda i: (l, sub, 0, 0))


def _pre_math(x, p_ref):
    r = lax.rsqrt(jnp.mean(x * x, axis=1, keepdims=True) + RMS_EPS)
    return (x * r) * p_ref[0:1, :] * (1.0 + p_ref[1:2, :]) + p_ref[2:3, :]


ANYSPEC = pl.BlockSpec(memory_space=pl.ANY)


def _prenorm(x, ptab, l, sub, name, toks=()):
    s = x.shape[0]
    tr, spec = _rows_spec(s)

    def body(x_ref, p_ref, *rest):
        rest[-1][...] = _pre_math(x_ref[...], p_ref).astype(BF16)

    return pl.pallas_call(
        body, name=name, out_shape=jax.ShapeDtypeStruct((s, D), BF16), grid=(s // tr,),
        in_specs=[spec, _pspec(l, sub)] + [ANYSPEC] * len(toks), out_specs=spec,
        compiler_params=_cparams(("parallel",)),
    )(x, ptab, *toks)


def _resid(x, y, ptab, l, sub, coef, nxt, name, toks=()):
    s = x.shape[0]
    tr, spec = _rows_spec(s)

    def body(x_ref, y_ref, p_ref, *rest):
        yv = y_ref[...]
        r = lax.rsqrt(jnp.mean(yv * yv, axis=1, keepdims=True) + RMS_EPS)
        xn = x_ref[...] + (coef * p_ref[4:5, :]) * ((yv * r) * p_ref[3:4, :])
        if nxt is None:
            rest[-1][...] = xn
        else:
            rest[-2][...] = xn
            rest[-1][...] = _pre_math(xn, rest[0]).astype(BF16)

    tspecs = [ANYSPEC] * len(toks)
    if nxt is None:
        return pl.pallas_call(
            body, name=name, out_shape=jax.ShapeDtypeStruct((s, D), F32), grid=(s // tr,),
            in_specs=[spec, spec, _pspec(l, sub)] + tspecs, out_specs=spec,
            compiler_params=_cparams(("parallel",)),
        )(x, y, ptab, *toks), None
    return pl.pallas_call(
        body, name=name,
        out_shape=(jax.ShapeDtypeStruct((s, D), F32), jax.ShapeDtypeStruct((s, D), BF16)),
        grid=(s // tr,),
        in_specs=[spec, spec, _pspec(l, sub), _pspec(*nxt)] + tspecs, out_specs=(spec, spec),
        compiler_params=_cparams(("parallel",)),
    )(x, y, ptab, ptab, *toks)


def _loss_grad(x, tgt, name):
    s = x.shape[0]
    tr, spec = _rows_spec(s)

    def body(x_ref, t_ref, dx_ref, l_ref):
        @pl.when(pl.program_id(0) == 0)
        def _():
            l_ref[...] = jnp.zeros_like(l_ref)

        e = x_ref[...] - t_ref[...]
        dx_ref[...] = e * (1.0 / D)
        l_ref[0:1, :] += jnp.sum(e * e, axis=0, keepdims=True) * (0.5 / D)

    return pl.pallas_call(
        body, name=name,
        out_shape=(jax.ShapeDtypeStruct((s, D), F32), jax.ShapeDtypeStruct((8, D), F32)),
        grid=(s // tr,), in_specs=[spec, spec],
        out_specs=(spec, pl.BlockSpec((8, D), lambda i: (0, 0))),
        compiler_params=_cparams(("arbitrary",)),
    )(x, tgt)


def _post_bwd(dx, y, ptab, l, sub, coef, name, toks=()):
    s = dx.shape[0]
    tr, spec = _rows_spec(s)

    def body(dx_ref, y_ref, p_ref, *rest):
        dy_ref, red_ref = rest[-2:]

        @pl.when(pl.program_id(0) == 0)
        def _():
            red_ref[...] = jnp.zeros_like(red_ref)

        dxv, yv = dx_ref[...], y_ref[...]
        gpost, gate = p_ref[3:4, :], p_ref[4:5, :]
        r = lax.rsqrt(jnp.mean(yv * yv, axis=1, keepdims=True) + RMS_EPS)
        yhat = yv * r
        red_ref[0:1, :] += jnp.sum(dxv * yhat * gpost, axis=0, keepdims=True) * coef
        dn = dxv * (coef * gate)
        red_ref[1:2, :] += jnp.sum(dn * yhat, axis=0, keepdims=True)
        dyh = dn * gpost
        dy_ref[...] = (r * (dyh - yhat * jnp.mean(dyh * yhat, axis=1, keepdims=True))).astype(BF16)

    return pl.pallas_call(
        body, name=name,
        out_shape=(jax.ShapeDtypeStruct((s, D), BF16), jax.ShapeDtypeStruct((8, D), F32)),
        grid=(s // tr,), in_specs=[spec, spec, _pspec(l, sub)] + [ANYSPEC] * len(toks),
        out_specs=(spec, pl.BlockSpec((8, D), lambda i: (0, 0))),
        compiler_params=_cparams(("arbitrary",)),
    )(dx, y, ptab, *toks)


def _pre_bwd(dx, dh, x, ptab, l, sub, name, dh2=None, toks=()):
    s = dx.shape[0]
    tr, spec = _rows_spec(s)
    extra = [] if dh2 is None else [dh2]

    def body(dx_ref, dh_ref, x_ref, p_ref, *rest):
        o_ref, red_ref = rest[-2:]

        @pl.when(pl.program_id(0) == 0)
        def _():
            red_ref[...] = jnp.zeros_like(red_ref)

        dhv, xv = dh_ref[...], x_ref[...]
        if extra:
            dhv = dhv + rest[0][...]
        gpre, scale = p_ref[0:1, :], p_ref[1:2, :]
        r = lax.rsqrt(jnp.mean(xv * xv, axis=1, keepdims=True) + RMS_EPS)
        xhat = xv * r
        red_ref[0:1, :] += jnp.sum(dhv, axis=0, keepdims=True)
        red_ref[1:2, :] += jnp.sum(dhv * xhat * gpre, axis=0, keepdims=True)
        red_ref[2:3, :] += jnp.sum(dhv * xhat * (1.0 + scale), axis=0, keepdims=True)
        dxh = dhv * (gpre * (1.0 + scale))
        o_ref[...] = dx_ref[...] + r * (dxh - xhat * jnp.mean(dxh * xhat, axis=1, keepdims=True))

    return pl.pallas_call(
        body, name=name,
        out_shape=(jax.ShapeDtypeStruct((s, D), F32), jax.ShapeDtypeStruct((8, D), F32)),
        grid=(s // tr,), in_specs=[spec, spec, spec, _pspec(l, sub)] + [spec] * len(extra) + [ANYSPEC] * len(toks),
        out_specs=(spec, pl.BlockSpec((8, D), lambda i: (0, 0))),
        compiler_params=_cparams(("arbitrary",)),
    )(dx, dh, x, ptab, *extra, *toks)


def _cond_fwd(c_pad, wc, bc, name):
    w = wc.shape[2]

    def body(c_ref, w_ref, b_ref, o_ref, ca_ref):
        cv = c_ref[...]
        ca = (cv * _sigmoid(cv)).astype(BF16)
        ca_ref[...] = ca
        o_ref[...] = jnp.dot(ca, w_ref[...].astype(BF16), preferred_element_type=F32) + b_ref[...]

    return pl.pallas_call(
        body, name=name,
        out_shape=(jax.ShapeDtypeStruct((DEPTH, 128, w), F32), jax.ShapeDtypeStruct((128, D), BF16)),
        grid=(DEPTH,),
        in_specs=[pl.BlockSpec((128, D), lambda i: (0, 0)),
                  pl.BlockSpec((None, D, w), lambda i: (i, 0, 0)),
                  pl.BlockSpec((None, 1, w), lambda i: (i, 0, 0))],
        out_specs=(pl.BlockSpec((None, 128, w), lambda i: (i, 0, 0)),
                   pl.BlockSpec((128, D), lambda i: (0, 0))),
        compiler_params=_cparams(("arbitrary",), VMEM_BIG),
    )(c_pad, wc, bc)


def _cond_bwd(ca_pad, dmod, name):
    w = dmod.shape[2]
    return _mm(ca_pad, dmod, mode="tn", grid=(DEPTH, 1, 1),
               a_spec=pl.BlockSpec((128, D), lambda i, j, r: (0, 0)),
               b_spec=pl.BlockSpec((None, 128, w), lambda i, j, r: (i, 0, 0)),
               o_spec=pl.BlockSpec((None, D, w), lambda i, j, r: (i, 0, 0)),
               o_shape=(DEPTH, D, w), acc_shape=None, out_dtype=F32, name=name)


def _split3(x):
    hi = x.astype(BF16)
    r1 = x - hi.astype(F32)
    mid = r1.astype(BF16)
    lo = (r1 - mid.astype(F32)).astype(BF16)
    return hi, mid, lo


def _tri_dot(t, x):
    hi, mid, lo = _split3(x)
    return (jnp.dot(t, hi, preferred_element_type=F32) + jnp.dot(t, mid, preferred_element_type=F32)
            + jnp.dot(t, lo, preferred_element_type=F32))


def _fox_cum(fl, bf, name):
    s = fl.shape[0]
    tb = _tile(s, (256, 128))

    def body(fl_ref, b_ref, cum_ref):
        row = lax.broadcasted_iota(jnp.int32, (tb, tb), 0)
        col = lax.broadcasted_iota(jnp.int32, (tb, tb), 1)
        tri = (col <= row).astype(BF16)
        carry = jnp.zeros((1, 128), F32)
        for blk in range(s // tb):
            z = fl_ref[blk * tb:(blk + 1) * tb, :] + b_ref[0:1, :]
            lf = jnp.minimum(z, 0.0) - jnp.log(1.0 + jnp.exp(-jnp.abs(z)))
            cum_ref[blk * tb:(blk + 1) * tb, :] = _tri_dot(tri, lf) + carry
            carry = carry + jnp.sum(lf, axis=0, keepdims=True)

    return pl.pallas_call(
        body, name=name, out_shape=jax.ShapeDtypeStruct((s, 128), F32),
    )(fl, bf)


def _fox_cum_bwd(dcum, fl, bf, name):
    s = fl.shape[0]
    tb = _tile(s, (256, 128))

    def body(dc_ref, fl_ref, b_ref, dfl_ref, db_ref):
        row = lax.broadcasted_iota(jnp.int32, (tb, tb), 0)
        col = lax.broadcasted_iota(jnp.int32, (tb, tb), 1)
        tri = (col >= row).astype(BF16)
        carry = jnp.zeros((1, 128), F32)
        dbs = jnp.zeros((1, 128), F32)
        for blk in reversed(range(s // tb)):
            dc = dc_ref[blk * tb:(blk + 1) * tb, :]
            dl = _tri_dot(tri, dc) + carry
            carry = carry + jnp.sum(dc, axis=0, keepdims=True)
            z = fl_ref[blk * tb:(blk + 1) * tb, :] + b_ref[0:1, :]
            dz = dl * _sigmoid(-z)
            dfl_ref[blk * tb:(blk + 1) * tb, :] = dz
            dbs = dbs + jnp.sum(dz, axis=0, keepdims=True)
        db_ref[...] = jnp.broadcast_to(dbs, (8, 128))

    return pl.pallas_call(
        body, name=name,
        out_shape=(jax.ShapeDtypeStruct((s, 128), F32), jax.ShapeDtypeStruct((8, 128), F32)),
    )(dcum, fl, bf)


def _mm_sections(h, w, nsec, out_dtype, name):
    s = h.shape[0]
    tm = _tile(s, (512, 256, 128))
    return _mm(h, w, mode="nn", grid=(s // tm, nsec, 1),
               a_spec=pl.BlockSpec((tm, D), lambda i, j, r: (i, 0)),
               b_spec=pl.BlockSpec((D, D), lambda i, j, r: (0, j)),
               o_spec=pl.BlockSpec((None, tm, D), lambda i, j, r: (j, i, 0)),
               o_shape=(nsec, s, D), acc_shape=None, out_dtype=out_dtype, name=name)


def _fox_qkv(h, w_qkv, name):
    return _mm_sections(h, w_qkv, 3, BF16, name)


def _fox_qkv_dx(dqkv, w_qkv, name):
    s = dqkv.shape[1]
    tm = _tile(s, (512, 256, 128))
    return _mm(dqkv, w_qkv, mode="nt", grid=(s // tm, 1, 3),
               a_spec=pl.BlockSpec((None, tm, D), lambda i, j, r: (r, i, 0)),
               b_spec=pl.BlockSpec((D, D), lambda i, j, r: (0, r)),
               o_spec=pl.BlockSpec((tm, D), lambda i, j, r: (i, 0)),
               o_shape=(s, D), acc_shape=(tm, D), out_dtype=F32, name=name)


def _fox_qkv_dw(h, dqkv, name):
    s = h.shape[0]
    ts = _tile(s, (2048, 1024, 512, 256, 128))
    return _mm(h, dqkv, mode="tn", grid=(3, 1, s // ts),
               a_spec=pl.BlockSpec((ts, D), lambda i, j, r: (r, 0)),
               b_spec=pl.BlockSpec((None, ts, D), lambda i, j, r: (i, r, 0)),
               o_spec=pl.BlockSpec((None, D, D), lambda i, j, r: (i, 0, 0)),
               o_shape=(3, D, D), acc_shape=(D, D), out_dtype=BF16, name=name)


HPAIRS = HEADS // 2


def _first_head():
    return lax.broadcasted_iota(jnp.int32, (1, 2 * HDIM), 1) < HDIM


def _one_head(x, sel):
    return jnp.where(sel, x, jnp.zeros_like(x))


def _fox_scores(qm, k_ref, cq, ck, qi, tq, n):
    sc = lax.dot_general(qm, k_ref[0:n, :], _NT, preferred_element_type=F32) * (HDIM ** -0.5)
    sc = sc + cq - ck
    row = lax.broadcasted_iota(jnp.int32, (tq, n), 0) + qi * tq
    col = lax.broadcasted_iota(jnp.int32, (tq, n), 1)
    return sc, col <= row


def _fox_specs(s):
    sect = lambda i: pl.BlockSpec((None, s, 2 * HDIM), lambda p: (i, 0, p))
    ospec = pl.BlockSpec((s, 2 * HDIM), lambda p: (0, p))
    cspec = pl.BlockSpec((s, 128), lambda p: (0, 0))
    rspec = pl.BlockSpec((None, 2, 1, s), lambda p: (p, 0, 0, 0))
    return sect, ospec, cspec, rspec


def _head_lane(hh):
    return lax.broadcasted_iota(jnp.int32, (1, 128), 1) == 2 * pl.program_id(0) + hh


def _pick(tile, hsel):
    return jnp.sum(jnp.where(hsel, tile, 0.0), axis=1, keepdims=True)


def _fox_attn_fwd(qkv, cum, ck, name):
    s = qkv.shape[1]
    tq = _tile(s, (256, 128))

    def body(q_ref, k_ref, v_ref, cum_ref, ck_ref, o_ref, lse_ref):
        first = _first_head()

        @pl.when(pl.program_id(0) == 0)
        def _():
            lse_ref[...] = jnp.zeros_like(lse_ref)

        for qi in range(s // tq):
            n = (qi + 1) * tq
            rows = slice(qi * tq, n)
            q2 = q_ref[rows, :]
            cum_t, lse_t = cum_ref[rows, :], lse_ref[rows, :]
            outs = []
            for hh in range(2):
                sel = first if hh == 0 else jnp.logical_not(first)
                hsel = _head_lane(hh)
                sc, keep = _fox_scores(_one_head(q2, sel), k_ref, _pick(cum_t, hsel), ck_ref[hh, :, 0:n], qi, tq, n)
                sc = jnp.where(keep, sc, -1e30)
                m = jnp.max(sc, axis=1, keepdims=True)
                p = jnp.exp(sc - m)
                lsum = jnp.sum(p, axis=1, keepdims=True)
                outs.append(jnp.dot(p.astype(BF16), v_ref[0:n, :], preferred_element_type=F32) / lsum)
                lse_t = jnp.where(hsel, m + jnp.log(lsum), lse_t)
            lse_ref[rows, :] = lse_t
            o_ref[rows, :] = jnp.where(first, outs[0], outs[1]).astype(BF16)

    sect, ospec, cspec, rspec = _fox_specs(s)
    return pl.pallas_call(
        body, name=name,
        out_shape=(jax.ShapeDtypeStruct((s, D), BF16), jax.ShapeDtypeStruct((s, 128), F32)),
        grid=(HPAIRS,), in_specs=[sect(0), sect(1), sect(2), cspec, rspec], out_specs=(ospec, cspec),
        compiler_params=_cparams(("arbitrary",), VMEM_BIG),
    )(qkv, qkv, qkv, cum, ck)


def _fox_attn_bwd(qkv, do, lse, cum, ck, name):
    s = qkv.shape[1]
    tq = _tile(s, (256, 128))
    scale = HDIM ** -0.5

    def body(q_ref, k_ref, v_ref, do_ref, lse_ref, cum_ref, ck_ref,
             dqkv_ref, dcq_ref, dck_ref, dk_acc, dv_acc, dck_acc):
        first = _first_head()

        @pl.when(pl.program_id(0) == 0)
        def _():
            dcq_ref[...] = jnp.zeros_like(dcq_ref)

        dk_acc[...] = jnp.zeros_like(dk_acc)
        dv_acc[...] = jnp.zeros_like(dv_acc)
        dck_acc[...] = jnp.zeros_like(dck_acc)
        for qi in range(s // tq):
            n = (qi + 1) * tq
            rows = slice(qi * tq, n)
            q2, do2 = q_ref[rows, :], do_ref[rows, :]
            cum_t, lse_t, dcq_t = cum_ref[rows, :], lse_ref[rows, :], dcq_ref[rows, :]
            dq, dk, dv = [], [], []
            for hh in range(2):
                sel = first if hh == 0 else jnp.logical_not(first)
                hsel = _head_lane(hh)
                sc, keep = _fox_scores(_one_head(q2, sel), k_ref, _pick(cum_t, hsel), ck_ref[hh, :, 0:n], qi, tq, n)
                p = jnp.where(keep, jnp.exp(sc - _pick(lse_t, hsel)), 0.0)
                dp = lax.dot_general(_one_head(do2, sel), v_ref[0:n, :], _NT, preferred_element_type=F32)
                ds = p * (dp - jnp.sum(p * dp, axis=1, keepdims=True))
                dsb = ds.astype(BF16)
                dq.append(jnp.dot(dsb, k_ref[0:n, :], preferred_element_type=F32))
                dk.append(lax.dot_general(dsb, q2, _TN, preferred_element_type=F32))
                dv.append(lax.dot_general(p.astype(BF16), do2, _TN, preferred_element_type=F32))
                dcq_t = jnp.where(hsel, jnp.sum(ds, axis=1, keepdims=True), dcq_t)
                dck_acc[hh, :, 0:n] -= jnp.sum(ds, axis=0, keepdims=True)
            dcq_ref[rows, :] = dcq_t
            dqkv_ref[0, rows, :] = (jnp.where(first, dq[0], dq[1]) * scale).astype(BF16)
            dk_acc[0:n, :] += jnp.where(first, dk[0], dk[1]) * scale
            dv_acc[0:n, :] += jnp.where(first, dv[0], dv[1])
        dqkv_ref[1] = dk_acc[...].astype(BF16)
        dqkv_ref[2] = dv_acc[...].astype(BF16)
        dck_ref[...] = dck_acc[...]

    sect, ospec, cspec, rspec = _fox_specs(s)
    return pl.pallas_call(
        body, name=name,
        out_shape=(jax.ShapeDtypeStruct((3, s, D), BF16), jax.ShapeDtypeStruct((s, 128), F32),
                   jax.ShapeDtypeStruct((HPAIRS, 2, 1, s), F32)),
        grid=(HPAIRS,), in_specs=[sect(0), sect(1), sect(2), ospec, cspec, cspec, rspec],
        out_specs=(pl.BlockSpec((3, s, 2 * HDIM), lambda p: (0, 0, p)), cspec, rspec),
        scratch_shapes=[pltpu.VMEM((s, 2 * HDIM), F32), pltpu.VMEM((s, 2 * HDIM), F32), pltpu.VMEM((2, 1, s), F32)],
        compiler_params=_cparams(("arbitrary",), VMEM_BIG),
    )(qkv, qkv, qkv, do, lse, cum, ck)


TC = 256
HALO = 8


def _chunk_specs(s):
    tc = _tile(s, (TC, 128))
    per = tc // HALO
    nblk = s // HALO
    cur = pl.BlockSpec((tc, 8, 128), lambda i: (i, 0, 0))
    past = pl.BlockSpec((HALO, 8, 128), lambda i: (jnp.maximum(i * per - 1, 0), 0, 0))
    future = pl.BlockSpec((HALO, 8, 128), lambda i: (jnp.minimum((i + 1) * per, nblk - 1), 0, 0))
    return tc, cur, past, future


def _vec_spec(n):
    return pl.BlockSpec((n, 8, 128), lambda i: (0, 0, 0))


def _conv_past(buf, w_ref, kw, tc):
    out = w_ref[kw - 1] * buf[HALO:HALO + tc]
    for k in range(kw - 1):
        off = HALO - (kw - 1) + k
        out = out + w_ref[k] * buf[off:off + tc]
    return out


def _sconv_fwd(bg, cg, xv, w, name):
    s = bg.shape[0]
    tc, cur, past, _ = _chunk_specs(s)

    def body(bg_ref, cg_ref, xv_ref, cgp_ref, xvp_ref, w_ref, y_ref, zbuf):
        first = pl.program_id(0) == 0
        zbuf[0:HALO] = jnp.where(first, 0.0, cgp_ref[...] * xvp_ref[...])
        zbuf[HALO:HALO + tc] = cg_ref[...] * xv_ref[...]
        y_ref[...] = bg_ref[...] * _conv_past(zbuf, w_ref, 3, tc)

    return pl.pallas_call(
        body, name=name, out_shape=jax.ShapeDtypeStruct((s, 8, 128), F32), grid=(s // tc,),
        in_specs=[cur, cur, cur, past, past, _vec_spec(3)], out_specs=cur,
        scratch_shapes=[pltpu.VMEM((tc + HALO, 8, 128), F32)],
        compiler_params=_cparams(("parallel",)),
    )(bg, cg, xv, cg, xv, w)


def _sconv_bwd(dy, bg, cg, xv, w, name):
    s = dy.shape[0]
    tc, cur, past, future = _chunk_specs(s)
    nch = s // tc

    def body(dy_ref, bg_ref, cg_ref, xv_ref, cgp_ref, xvp_ref, dyf_ref, bgf_ref, w_ref,
             dbg_ref, dcg_ref, dxv_ref, dw_ref, zbuf, dbuf):
        i = pl.program_id(0)

        @pl.when(i == 0)
        def _():
            dw_ref[...] = jnp.zeros_like(dw_ref)

        z = cg_ref[...] * xv_ref[...]
        zbuf[0:HALO] = jnp.where(i == 0, 0.0, cgp_ref[...] * xvp_ref[...])
        zbuf[HALO:HALO + tc] = z
        dyv = dy_ref[...]
        dbg_ref[...] = dyv * _conv_past(zbuf, w_ref, 3, tc)
        dbuf[0:tc] = dyv * bg_ref[...]
        dbuf[tc:tc + HALO] = jnp.where(i == nch - 1, 0.0, dyf_ref[...] * bgf_ref[...])
        dz = jnp.zeros((tc, 8, 128), F32)
        for k in range(3):
            sh = dbuf[2 - k:2 - k + tc]
            dz = dz + w_ref[k] * sh
            dw_ref[k] += jnp.sum(z * sh, axis=0)
        dcg_ref[...] = dz * xv_ref[...]
        dxv_ref[...] = dz * cg_ref[...]

    shp = jax.ShapeDtypeStruct((s, 8, 128), F32)
    return pl.pallas_call(
        body, name=name, out_shape=(shp, shp, shp, jax.ShapeDtypeStruct((3, 8, 128), F32)),
        grid=(nch,),
        in_specs=[cur, cur, cur, cur, past, past, future, future, _vec_spec(3)],
        out_specs=(cur, cur, cur, _vec_spec(3)),
        scratch_shapes=[pltpu.VMEM((tc + HALO, 8, 128), F32), pltpu.VMEM((tc + HALO, 8, 128), F32)],
        compiler_params=_cparams(("arbitrary",)),
    )(dy, bg, cg, xv, cg, xv, dy, bg, w)


def _lru_conv_fwd(xp, wb, name):
    s = xp.shape[0]
    tc, cur, past, _ = _chunk_specs(s)

    def body(x_ref, xp_ref, w_ref, o_ref, buf):
        buf[0:HALO] = jnp.where(pl.program_id(0) == 0, 0.0, xp_ref[...])
        buf[HALO:HALO + tc] = x_ref[...]
        o_ref[...] = _conv_past(buf, w_ref, 4, tc) + w_ref[4]

    return pl.pallas_call(
        body, name=name, out_shape=jax.ShapeDtypeStruct((s, 8, 128), F32), grid=(s // tc,),
        in_specs=[cur, past, _vec_spec(8)], out_specs=cur,
        scratch_shapes=[pltpu.VMEM((tc + HALO, 8, 128), F32)],
        compiler_params=_cparams(("parallel",)),
    )(xp, xp, wb)


def _lru_conv_bwd(dxb, xp, wb, name):
    s = dxb.shape[0]
    tc, cur, _, future = _chunk_specs(s)
    nch = s // tc

    def body(d_ref, df_ref, x_ref, w_ref, o_ref, red_ref, dbuf):
        i = pl.program_id(0)

        @pl.when(i == 0)
        def _():
            red_ref[...] = jnp.zeros_like(red_ref)

        dv = d_ref[...]
        dbuf[0:tc] = dv
        dbuf[tc:tc + HALO] = jnp.where(i == nch - 1, 0.0, df_ref[...])
        xv = x_ref[...]
        dx = jnp.zeros((tc, 8, 128), F32)
        for k in range(4):
            sh = dbuf[3 - k:3 - k + tc]
            dx = dx + w_ref[k] * sh
            red_ref[k] += jnp.sum(xv * sh, axis=0)
        red_ref[4] += jnp.sum(dv, axis=0)
        o_ref[...] = dx

    return pl.pallas_call(
        body, name=name,
        out_shape=(jax.ShapeDtypeStruct((s, 8, 128), F32), jax.ShapeDtypeStruct((8, 8, 128), F32)),
        grid=(nch,), in_specs=[cur, future, cur, _vec_spec(8)], out_specs=(cur, _vec_spec(8)),
        scratch_shapes=[pltpu.VMEM((tc + HALO, 8, 128), F32)],
        compiler_params=_cparams(("arbitrary",)),
    )(dxb, dxb, xp, wb)


def _lru_gates(ra, ia, pv_ref):
    sp = _softplus(-pv_ref[2])
    r = _sigmoid(ra + pv_ref[0])
    ig = _sigmoid(ia + pv_ref[1])
    log_a = (-LRU_C) * r * sp
    a = jnp.exp(log_a)
    mult = jnp.sqrt(-jnp.tanh(log_a) * (a * a + 1.0))
    return r, ig, a, mult, sp


def _lru_scan_fwd(ra, ia, xb, gate, pv, name):
    s = ra.shape[0]
    tc, cur, _, _ = _chunk_specs(s)

    def body(ra_ref, ia_ref, xb_ref, g_ref, pv_ref, hs_ref, hp_ref, y_ref, abuf, bbuf, hcar):
        @pl.when(pl.program_id(0) == 0)
        def _():
            hcar[...] = jnp.zeros_like(hcar)

        xbv = xb_ref[...]
        _, ig, a, mult, _ = _lru_gates(ra_ref[...], ia_ref[...], pv_ref)
        abuf[...] = a
        bbuf[...] = mult * (ig * xbv)

        def step(t, h):
            hp_ref[t] = h
            h = abuf[t] * h + bbuf[t]
            hs_ref[t] = h
            return h

        hcar[...] = lax.fori_loop(0, tc, step, hcar[...], unroll=8)
        y_ref[...] = hs_ref[...] * _gelu_parts(g_ref[...])[0]

    shp = jax.ShapeDtypeStruct((s, 8, 128), F32)
    return pl.pallas_call(
        body, name=name, out_shape=(shp, shp, shp), grid=(s // tc,),
        in_specs=[cur, cur, cur, cur, _vec_spec(8)], out_specs=(cur, cur, cur),
        scratch_shapes=[pltpu.VMEM((tc, 8, 128), F32), pltpu.VMEM((tc, 8, 128), F32),
                        pltpu.VMEM((8, 128), F32)],
        compiler_params=_cparams(("arbitrary",)),
    )(ra, ia, xb, gate, pv)


def _lru_scan_bwd(dy, ra, ia, xb, gate, hs, hp, pv, name):
    s = dy.shape[0]
    tc = _tile(s, (TC, 128))
    nch = s // tc
    rev = pl.BlockSpec((tc, 8, 128), lambda i: (nch - 1 - i, 0, 0))

    def body(dy_ref, ra_ref, ia_ref, xb_ref, g_ref, hs_ref, hp_ref, pv_ref,
             dg_ref, dra_ref, dia_ref, dxb_ref, red_ref, abuf, dbuf, gbuf, car):
        @pl.when(pl.program_id(0) == 0)
        def _():
            red_ref[...] = jnp.zeros_like(red_ref)
            car[...] = jnp.zeros_like(car)

        xbv = xb_ref[...]
        r, ig, a, mult, sp = _lru_gates(ra_ref[...], ia_ref[...], pv_ref)
        ge, dge = _gelu_parts(g_ref[...])
        dyv = dy_ref[...]
        dg_ref[...] = dyv * hs_ref[...] * dge
        abuf[...] = a
        dbuf[...] = dyv * ge

        def step(k, c):
            t = tc - 1 - k
            g = dbuf[t] + c
            gbuf[t] = g
            return abuf[t] * g

        car[...] = lax.fori_loop(0, tc, step, car[...], unroll=8)
        g = gbuf[...]
        d_a = g * hp_ref[...]
        d_m = g * (ig * xbv)
        d_loga = d_a * a - d_m * (a * a / mult)
        dxb_ref[...] = g * mult * ig
        dra = d_loga * ((-LRU_C) * sp) * r * (1.0 - r)
        dia = g * mult * xbv * ig * (1.0 - ig)
        dra_ref[...] = dra
        dia_ref[...] = dia
        red_ref[0] += jnp.sum(dra, axis=0)
        red_ref[1] += jnp.sum(dia, axis=0)
        red_ref[2] += jnp.sum(d_loga * r, axis=0) * (LRU_C * _sigmoid(-pv_ref[2]))

    shp = jax.ShapeDtypeStruct((s, 8, 128), F32)
    return pl.pallas_call(
        body, name=name, out_shape=(shp, shp, shp, shp, jax.ShapeDtypeStruct((8, 8, 128), F32)),
        grid=(nch,), in_specs=[rev] * 7 + [_vec_spec(8)],
        out_specs=(rev, rev, rev, rev, _vec_spec(8)),
        scratch_shapes=[pltpu.VMEM((tc, 8, 128), F32), pltpu.VMEM((tc, 8, 128), F32),
                        pltpu.VMEM((tc, 8, 128), F32), pltpu.VMEM((8, 128), F32)],
        compiler_params=_cparams(("arbitrary",)),
    )(dy, ra, ia, xb, gate, hs, hp, pv)


def _adamw(src, w, m, v, name):
    nl, n, rr, cc = src.shape
    tr = rr
    for cand in sorted((d for d in range(16, rr + 1, 16) if rr % d == 0), reverse=True):
        if cand * cc <= 192 * 1024:
            tr = cand
            break
    c1 = 1.0 - ADAM_B1 ** ADAM_STEP
    c2 = 1.0 - ADAM_B2 ** ADAM_STEP

    def body(s_ref, w_ref, m_ref, v_ref, g_out, d_out, m_out, v_out):
        g = s_ref[0].astype(F32)
        for k in range(1, n):
            g = g + s_ref[k].astype(F32)
        mn = ADAM_B1 * m_ref[...] + (1.0 - ADAM_B1) * g
        vn = ADAM_B2 * v_ref[...] + (1.0 - ADAM_B2) * (g * g)
        g_out[...] = g
        m_out[...] = mn
        v_out[...] = vn
        d_out[...] = (-ADAM_LR) * ((mn / c1) / (jnp.sqrt(vn / c2) + ADAM_EPS) + ADAM_WD * w_ref[...])

    pspec = pl.BlockSpec((None, tr, cc), lambda l, i: (l, i, 0))
    shp = jax.ShapeDtypeStruct((nl, rr, cc), F32)
    return pl.pallas_call(
        body, name=name, out_shape=(shp, shp, shp, shp), grid=(nl, rr // tr),
        in_specs=[pl.BlockSpec((None, n, tr, cc), lambda l, i: (l, 0, i, 0)), pspec, pspec, pspec],
        out_specs=(pspec, pspec, pspec, pspec),
        compiler_params=_cparams(("parallel", "parallel"), VMEM_BIG),
    )(src, w, m, v)


def _adamw_slice(src, w, m, v, bufs, idx, name):
    n, rr, cc = src.shape
    nl = w.shape[0]
    tr = rr
    for cand in sorted((d for d in range(16, rr + 1, 16) if rr % d == 0), reverse=True):
        if cand * cc <= 192 * 1024:
            tr = cand
            break
    c1 = 1.0 - ADAM_B1 ** ADAM_STEP
    c2 = 1.0 - ADAM_B2 ** ADAM_STEP
    if bufs is None:
        bufs = tuple(lax.empty((nl, rr, cc), F32) for _ in range(4))

    def body(s_ref, w_ref, m_ref, v_ref, b0, b1, b2, b3, g_out, d_out, m_out, v_out):
        g = s_ref[0].astype(F32)
        for k in range(1, n):
            g = g + s_ref[k].astype(F32)
        mn = ADAM_B1 * m_ref[...] + (1.0 - ADAM_B1) * g
        vn = ADAM_B2 * v_ref[...] + (1.0 - ADAM_B2) * (g * g)
        g_out[...] = g
        m_out[...] = mn
        v_out[...] = vn
        d_out[...] = (-ADAM_LR) * ((mn / c1) / (jnp.sqrt(vn / c2) + ADAM_EPS) + ADAM_WD * w_ref[...])

    pspec = pl.BlockSpec((None, tr, cc), lambda i: (idx, i, 0))
    anyspec = pl.BlockSpec(memory_space=pl.ANY)
    shp = jax.ShapeDtypeStruct((nl, rr, cc), F32)
    return pl.pallas_call(
        body, name=name, out_shape=(shp, shp, shp, shp), grid=(rr // tr,),
        in_specs=[pl.BlockSpec((n, tr, cc), lambda i: (0, i, 0)), pspec, pspec, pspec,
                  anyspec, anyspec, anyspec, anyspec],
        out_specs=(pspec, pspec, pspec, pspec),
        input_output_aliases={4: 0, 5: 1, 6: 2, 7: 3},
        compiler_params=_cparams(("parallel",), VMEM_BIG),
    )(src, w, m, v, *bufs)


def _sum_devices(x, name):
    _, rr, cc = x.shape

    def body(x_ref, o_ref):
        acc = x_ref[0]
        for k in range(1, NDEV):
            acc = acc + x_ref[k]
        o_ref[...] = acc

    return pl.pallas_call(
        body, name=name, out_shape=jax.ShapeDtypeStruct((rr, cc), F32), grid=(rr // 8,),
        in_specs=[pl.BlockSpec((NDEV, 8, cc), lambda i: (0, i, 0))],
        out_specs=pl.BlockSpec((8, cc), lambda i: (i, 0)),
        compiler_params=_cparams(("parallel",)),
    )(x)


def _to3(x):
    return x.reshape(x.shape[0], 8, 128)


def _block_diag(w):
    eye = jnp.eye(HEADS, dtype=w.dtype)
    return (w[:, :, None, :] * eye[:, None, :, None]).reshape(D, D)


def _diag_blocks(x):
    return jnp.diagonal(x.reshape(HEADS, HDIM, HEADS, HDIM), axis1=0, axis2=2).transpose(2, 0, 1)


def _col_blocks(dw, n):
    return dw.reshape(dw.shape[0], NDEV, n).transpose(1, 0, 2)


def kernel(x, c, w_cond, b_cond, norm_pre, norm_post, w_ffn_in, w_ffn_out, fox_w_in, fox_b_f, fox_w_out, sconv_w_in, sconv_conv_w, sconv_w_out, lru_w_in, lru_conv_w, lru_conv_b, lru_w_a, lru_b_a, lru_w_x, lru_b_x, lru_lambda, lru_w_out, loss_target, m_w_cond, m_b_cond, m_norm_pre, m_norm_post, m_w_ffn_in, m_w_ffn_out, m_fox_w_in, m_fox_b_f, m_fox_w_out, m_sconv_w_in, m_sconv_conv_w, m_sconv_w_out, m_lru_w_in, m_lru_conv_w, m_lru_conv_b, m_lru_w_a, m_lru_b_a, m_lru_w_x, m_lru_b_x, m_lru_lambda, m_lru_w_out, v_w_cond, v_b_cond, v_norm_pre, v_norm_post, v_w_ffn_in, v_w_ffn_out, v_fox_w_in, v_fox_b_f, v_fox_w_out, v_sconv_w_in, v_sconv_conv_w, v_sconv_w_out, v_lru_w_in, v_lru_conv_w, v_lru_conv_b, v_lru_w_a, v_lru_b_a, v_lru_w_x, v_lru_b_x, v_lru_lambda, v_lru_w_out):
    me = 4 * lax.axis_index("x") + 2 * lax.axis_index("y") + lax.axis_index("c")
    s = x.shape[1]
    x0 = x[0]
    tgt = loss_target[0]
    wcw = w_cond.shape[2]
    dsh = norm_pre.shape[2]

    small_parts = [c.reshape(-1), norm_pre.reshape(-1), norm_post.reshape(-1), sconv_conv_w.reshape(-1),
                   lru_conv_w.reshape(-1), lru_conv_b.reshape(-1), lru_lambda.reshape(-1)]
    sizes = [p.shape[0] for p in small_parts]
    flat = jnp.concatenate(small_parts)
    padn = (-flat.shape[0]) % 1024
    flat = jnp.pad(flat, (0, padn)).reshape(-1, 1024)
    sm = _all_gather(flat, 0, "ag_small").reshape(NDEV, -1)
    offs = [0]
    for n_ in sizes:
        offs.append(offs[-1] + n_)
    piece = lambda i: sm[:, offs[i]:offs[i + 1]]
    c_all = piece(0)
    unshard = lambda p, lead: p.reshape((NDEV,) + lead + (dsh,)).transpose(
        tuple(range(1, len(lead) + 1)) + (0, len(lead) + 1)).reshape(lead + (D,))
    npre = unshard(piece(1), (DEPTH, 3))
    npost = unshard(piece(2), (DEPTH, 3))
    scw = unshard(piece(3), (3,))
    lcw = unshard(piece(4), (4,))
    lcb = unshard(piece(5), ())
    llam = unshard(piece(6), ())

    c_pad = jnp.pad(c_all, ((0, 128 - NDEV), (0, 0)))
    bc_mine = lax.dynamic_slice(b_cond, (0, me * wcw), (DEPTH, wcw)).reshape(DEPTH, 1, wcw)
    modc, ca_pad = _cond_fwd(c_pad, w_cond, bc_mine, "cond_fwd")
    modg = _all_gather(modc[:, :NDEV, :], 0, "ag_mod")
    mod = lax.dynamic_index_in_dim(modg, me, axis=2, keepdims=False)
    mod = mod.transpose(1, 0, 2).reshape(DEPTH, 3, 3, D)
    ptab = jnp.stack([npre, mod[:, :, 1], mod[:, :, 0], npost, mod[:, :, 2],
                      jnp.zeros_like(npre), jnp.zeros_like(npre), jnp.zeros_like(npre)], axis=2)

    need = []
    for l in range(DEPTH):
        need += [(f"wi{l}0", w_ffn_in[l, 0]), (f"wo{l}0", w_ffn_out[l, 0])]
        if l % 3 == 0:
            need += [(f"fwi{l // 3}", fox_w_in[l // 3]), (f"fwo{l // 3}", fox_w_out[l // 3])]
        elif l % 3 == 1:
            need += [("swi", sconv_w_in[0]), ("swo", sconv_w_out[0])]
        else:
            need += [("lwi", lru_w_in[0]), ("lwo", lru_w_out[0])]
        need += [(f"wi{l}1", w_ffn_in[l, 1]), (f"wo{l}1", w_ffn_out[l, 1])]
    ag_keys = [k_ for k_, _ in need]
    lands = _alloc_many([(NDEV,) + w_.shape for _, w_ in need], BF16, "alloc_gather")
    pend = {}
    chain = [jnp.zeros((8, 128), F32)]
    for (key, w_), land in zip(need, lands):
        pend[key], chain[0] = _gather_start((w_ + chain[0][0, 0]).astype(BF16), land, "ags_" + key)
    relayed = [0]
    relay_toks = []

    def ag_relay(upto, after):
        while relayed[0] <= min(upto, len(ag_keys) - 1):
            key = ag_keys[relayed[0]]
            pend[key], tok = _gather_relay(pend[key], after, "agr_" + key)
            relay_toks.append(tok)
            relayed[0] += 1

    def ag_wait(key, after):
        return _gather_wait(pend.pop(key), after, "agw_" + key)

    ag_relay(1, chain[0])
    cols = lambda wg: wg.transpose(1, 0, 2).reshape(D, -1)
    wgate = jnp.concatenate([_block_diag(lru_w_a[0]), _block_diag(lru_w_x[0])], axis=1).astype(BF16)
    bf_pad = jnp.pad(fox_b_f, ((0, 0), (0, 128 - HEADS))).reshape(2, 1, 128)
    scw3 = scw.reshape(3, 8, 128)
    lcwb = jnp.concatenate([lcw, lcb[None], jnp.zeros((3, D), F32)], axis=0).reshape(8, 8, 128)
    lpv = jnp.concatenate([lru_b_a.reshape(1, D), lru_b_x.reshape(1, D), llam[None],
                           jnp.zeros((5, D), F32)], axis=0).reshape(8, 8, 128)

    subs = [(l, sub) for l in range(DEPTH) for sub in range(3)]
    coef = lambda sub: 1.0 if sub == 1 else 0.5

    saved = {}
    xcur = x0
    h = _prenorm(xcur, ptab, 0, 0, "prenorm", toks=(chain[0],))
    for idx, (l, sub) in enumerate(subs):
        tag = f"{l}{sub}"
        sv = {"x": xcur, "h": h}
        if sub != 1:
            f = 0 if sub == 0 else 1
            wi = ag_wait(f"wi{l}{f}", h)
            wo = ag_wait(f"wo{l}{f}", h).reshape(4, FFB, D)
            gu, act = _ffn_up(h, wi, "ffn_up_" + tag)
            y = _ffn_down(act, wo, "ffn_down_" + tag)
            sv.update(gu=gu, wi=wi, wo=wo)
        elif l % 3 == 0:
            j = l // 3
            fwi_j = cols(ag_wait(f"fwi{j}", h))
            w_qkv = fwi_j
            w_f = jnp.pad(fwi_j[:, 3 * D:], ((0, 0), (0, 128 - HEADS)))
            fwo_j = ag_wait(f"fwo{j}", h).reshape(D, D)
            qkv = _fox_qkv(h, w_qkv, "fox_qkv_" + tag)
            fl = _mm_nn(h, w_f, F32, "fox_f_" + tag)
            cum = _fox_cum(fl, bf_pad[j], "fox_cum_" + tag)
            ck = cum[:, :HEADS].T.reshape(HPAIRS, 2, 1, s)
            o2, lse = _fox_attn_fwd(qkv, cum, ck, "fox_attn_" + tag)
            y = _mm_nn(o2, fwo_j, F32, "fox_out_" + tag)
            sv.update(w_qkv=w_qkv, w_f=w_f, fwo=fwo_j, qkv=qkv, fl=fl, cum=cum, ck=ck, lse=lse, o2=o2)
        elif l % 3 == 1:
            swi = cols(ag_wait("swi", h))
            swo = ag_wait("swo", h).reshape(D, D)
            proj = _mm_sections(h, swi, 3, F32, "sconv_in_" + tag)
            bg, cg, xv = (_to3(proj[i]) for i in range(3))
            y3 = _sconv_fwd(bg, cg, xv, scw3, "sconv_mix_" + tag)
            y2 = y3.reshape(s, D).astype(BF16)
            y = _mm_nn(y2, swo, F32, "sconv_out_" + tag)
            sv.update(bg=bg, cg=cg, xv=xv, y2=y2)
        else:
            lwi = cols(ag_wait("lwi", h))
            lwo = ag_wait("lwo", h).reshape(D, D)
            proj = _mm_sections(h, lwi, 2, F32, "lru_in_" + tag)
            gate3, xp3 = _to3(proj[0]), _to3(proj[1])
            xb3 = _lru_conv_fwd(xp3, lcwb, "lru_conv_" + tag)
            xb2 = xb3.reshape(s, D).astype(BF16)
            gpre = _mm_sections(xb2, wgate, 2, F32, "lru_gate_" + tag)
            ra3, ia3 = _to3(gpre[0]), _to3(gpre[1])
            hs3, hp3, y3 = _lru_scan_fwd(ra3, ia3, xb3, gate3, lpv, "lru_scan_" + tag)
            y2 = y3.reshape(s, D).astype(BF16)
            y = _mm_nn(y2, lwo, F32, "lru_out_" + tag)
            sv.update(gate3=gate3, xp3=xp3, xb3=xb3, xb2=xb2, ra3=ra3, ia3=ia3, hs3=hs3, hp3=hp3, y2=y2)
        sv["y"] = y
        saved[(l, sub)] = sv
        nxt = subs[idx + 1] if idx + 1 < len(subs) else None
        ag_relay(2 * idx + 3, y)
        xcur, h = _resid(xcur, y, ptab, l, sub, coef(sub), nxt, "resid_" + tag, toks=tuple(relay_toks))
        del relay_toks[:]

    dx, loss_cols = _loss_grad(xcur, tgt, "loss_grad")

    d_mod = [[[None] * 3 for _ in range(3)] for _ in range(DEPTH)]
    d_npre = [[None] * 3 for _ in range(DEPTH)]
    d_npost = [[None] * 3 for _ in range(DEPTH)]
    d_fbf = [None, None]
    small_g = {}
    rs_pend, rs_order = {}, []
    toks = []
    dh2 = None
    rs_lands = dict(zip(ag_keys, _alloc_many([(NDEV,) + w_.shape for _, w_ in need], BF16, "alloc_scatter")))

    def rs_start(key, blocks):
        hnd = _exchange_start(blocks, "rss_" + key, False, land=rs_lands[key])
        rs_pend[key] = hnd[:4]
        rs_order.append(key)
        toks.append(hnd[4])

    for (l, sub) in reversed(subs):
        tag = f"{l}{sub}"
        sv = saved[(l, sub)]
        dy, red = _post_bwd(dx, sv["y"], ptab, l, sub, coef(sub), "post_bwd_" + tag, toks=tuple(toks))
        del toks[:]
        d_mod[l][sub][2] = red[0]
        d_npost[l][sub] = red[1]
        hb = sv["h"]
        if sub != 1:
            f = 0 if sub == 0 else 1
            da, dwo = _ffn_down_bwd(dy, sv["gu"], sv["wo"], "ffn_down_bwd_" + tag)
            rs_start(f"wo{l}{f}", dwo.reshape(NDEV, FFB // 2, D))
            rs_start(f"wi{l}{f}", _ffn_dwi(hb, da.reshape(NDEV, s, FFB), "ffn_dwi_" + tag))
            dh = _ffn_dh(da, sv["wi"], "ffn_dh_" + tag)
        elif l % 3 == 0:
            j = l // 3
            do2 = _mm_nt(dy, sv["fwo"], BF16, "fox_out_dx_" + tag)
            rs_start(f"fwo{j}", _mm_tn(sv["o2"], dy, BF16, "fox_out_dw_" + tag).reshape(NDEV, dsh, D))
            dqkv, dcq, dck = _fox_attn_bwd(sv["qkv"], do2, sv["lse"], sv["cum"], sv["ck"], "fox_attn_bwd_" + tag)
            dcum = dcq + jnp.pad(dck.reshape(HEADS, s).T, ((0, 0), (0, 128 - HEADS)))
            dfl, dbf = _fox_cum_bwd(dcum, sv["fl"], bf_pad[j], "fox_cum_bwd_" + tag)
            d_fbf[j] = dbf[0, :HEADS]
            dflb = dfl.astype(BF16)
            dh = _fox_qkv_dx(dqkv, sv["w_qkv"], "fox_qkv_dx_" + tag)
            dh2 = _mm_nt(dflb, sv["w_f"], F32, "fox_f_dx_" + tag)
            dw3 = _fox_qkv_dw(hb, dqkv, "fox_qkv_dw_" + tag)
            dwf = _mm_tn(hb, dflb, BF16, "fox_f_dw_" + tag)
            dwfox = jnp.concatenate([dw3[0], dw3[1], dw3[2], dwf[:, :HEADS]], axis=1)
            rs_start(f"fwi{j}", _col_blocks(dwfox, fox_w_in.shape[2]))
        elif l % 3 == 1:
            dy3 = _to3(_mm_nt(dy, swo, F32, "sconv_out_dx_" + tag))
            rs_start("swo", _mm_tn(sv["y2"], dy, BF16, "sconv_out_dw_" + tag).reshape(NDEV, dsh, D))
            dbg, dcg, dxv, dscw = _sconv_bwd(dy3, sv["bg"], sv["cg"], sv["xv"], scw3, "sconv_mix_bwd_" + tag)
            small_g["sconv_conv_w"] = dscw.reshape(3, D)
            dproj = jnp.concatenate([t.reshape(s, D) for t in (dbg, dcg, dxv)], axis=1).astype(BF16)
            dh = _mm_nt(dproj, swi, F32, "sconv_in_dx_" + tag)
            rs_start("swi", _col_blocks(_mm_tn(hb, dproj, BF16, "sconv_in_dw_" + tag), sconv_w_in.shape[2]))
        else:
            dy3 = _to3(_mm_nt(dy, lwo, F32, "lru_out_dx_" + tag))
            rs_start("lwo", _mm_tn(sv["y2"], dy, BF16, "lru_out_dw_" + tag).reshape(NDEV, dsh, D))
            dgate3, dra3, dia3, dxb3, lred = _lru_scan_bwd(
                dy3, sv["ra3"], sv["ia3"], sv["xb3"], sv["gate3"], sv["hs3"], sv["hp3"], lpv, "lru_scan_bwd_" + tag)
            dgp = jnp.concatenate([dra3.reshape(s, D), dia3.reshape(s, D)], axis=1).astype(BF16)
            dxb3 = dxb3 + _to3(_mm_nt(dgp, wgate, F32, "lru_gate_dx_" + tag))
            dwgate = _mm_tn(sv["xb2"], dgp, F32, "lru_gate_dw_" + tag)
            dxp3, cred = _lru_conv_bwd(dxb3, sv["xp3"], lcwb, "lru_conv_bwd_" + tag)
            lred, cred = lred.reshape(8, D), cred.reshape(8, D)
            small_g.update(lru_w_a=_diag_blocks(dwgate[:, :D]), lru_w_x=_diag_blocks(dwgate[:, D:]),
                           lru_b_a=lred[0], lru_b_x=lred[1], lru_lambda=lred[2],
                           lru_conv_w=cred[:4], lru_conv_b=cred[4])
            dproj = jnp.concatenate([dgate3.reshape(s, D), dxp3.reshape(s, D)], axis=1).astype(BF16)
            dh = _mm_nt(dproj, lwi, F32, "lru_in_dx_" + tag)
            rs_start("lwi", _col_blocks(_mm_tn(hb, dproj, BF16, "lru_in_dw_" + tag), lru_w_in.shape[2]))
        dx, red = _pre_bwd(dx, dh, sv["x"], ptab, l, sub, "pre_bwd_" + tag, dh2=dh2, toks=tuple(toks))
        dh2 = None
        del toks[:]
        d_mod[l][sub][0] = red[0]
        d_mod[l][sub][1] = red[1]
        d_npre[l][sub] = red[2]
    grad_x = dx[None]

    dmod_mine = jnp.stack([jnp.stack([jnp.stack(d_mod[l][sub]) for sub in range(3)]) for l in range(DEPTH)])
    gparts = [loss_cols[0], dmod_mine.reshape(-1),
              jnp.stack([jnp.stack(r_) for r_ in d_npre]).reshape(-1),
              jnp.stack([jnp.stack(r_) for r_ in d_npost]).reshape(-1),
              jnp.stack(d_fbf).reshape(-1), small_g["sconv_conv_w"].reshape(-1),
              small_g["lru_conv_w"].reshape(-1), small_g["lru_conv_b"].reshape(-1),
              small_g["lru_w_a"].reshape(-1), small_g["lru_b_a"].reshape(-1),
              small_g["lru_w_x"].reshape(-1), small_g["lru_b_x"].reshape(-1),
              small_g["lru_lambda"].reshape(-1)]
    gsizes = [p.shape[0] for p in gparts]
    gflat = jnp.concatenate(gparts)
    gflat = jnp.pad(gflat, (0, (-gflat.shape[0]) % (8 * 1024))).reshape(-1, 1024)
    small_hnd = _exchange_start(gflat, "ags_smallgrads", True)

    out = {}
    after = [small_hnd[4]]

    def update(keys, w, m, v, name):
        got = {}
        for k_ in sorted(keys, key=rs_order.index):
            got[k_] = after[0] = _exchange_wait(rs_pend.pop(k_), after[0], "rsw_" + k_, False)
        src = jnp.stack([got[k_] for k_ in keys])
        shp = w.shape
        w3, m3, v3 = (t.reshape((src.shape[0],) + src.shape[2:]) for t in (w, m, v))
        res = tuple(t.reshape(shp) for t in _adamw(src, w3, m3, v3, "adamw_" + name))
        after[0] = res[1]
        return res

    lf = [f"{l}{f}" for l in range(DEPTH) for f in range(2)]
    out["fox_w_in"] = update(["fwi0", "fwi1"], fox_w_in, m_fox_w_in, v_fox_w_in, "fwi")
    out["fox_w_out"] = update(["fwo0", "fwo1"], fox_w_out, m_fox_w_out, v_fox_w_out, "fwo")
    out["lru_w_out"] = update(["lwo"], lru_w_out, m_lru_w_out, v_lru_w_out, "lwo")
    out["lru_w_in"] = update(["lwi"], lru_w_in, m_lru_w_in, v_lru_w_in, "lwi")
    out["sconv_w_out"] = update(["swo"], sconv_w_out, m_sconv_w_out, v_sconv_w_out, "swo")
    out["sconv_w_in"] = update(["swi"], sconv_w_in, m_sconv_w_in, v_sconv_w_in, "swi")

    stacked = {"wi": [t.reshape((-1,) + w_ffn_in.shape[2:]) for t in (w_ffn_in, m_w_ffn_in, v_w_ffn_in)],
               "wo": [t.reshape((-1,) + w_ffn_out.shape[2:]) for t in (w_ffn_out, m_w_ffn_out, v_w_ffn_out)]}
    bufs = {"wi": None, "wo": None}

    def update_slice(k_):
        kind = k_[:2]
        src = after[0] = _exchange_wait(rs_pend.pop(k_), after[0], "rsw_" + k_, False)
        bufs[kind] = _adamw_slice(src, *stacked[kind], bufs[kind], lf.index(k_[2:]), "adamw_" + k_)
        after[0] = bufs[kind][1]

    for k_ in rs_order:
        if k_[:2] in ("wi", "wo") and k_ != "wi00":
            update_slice(k_)

    gall = _exchange_wait(small_hnd[:4], after[0], "agw_smallgrads", True)
    gsum = _sum_devices(gall, "sum_smallgrads").reshape(-1)
    goffs = [0]
    for n_ in gsizes:
        goffs.append(goffs[-1] + n_)
    gpiece = lambda i: gsum[goffs[i]:goffs[i + 1]]
    loss = jnp.sum(gpiece(0))
    my_ch = lambda g, lead: lax.dynamic_slice_in_dim(g.reshape(lead + (D,)), me * dsh, dsh, axis=len(lead))

    dmod_all = gall.reshape(NDEV, -1)[:, goffs[1]:goffs[2]].reshape(NDEV, DEPTH, 3 * 3 * D)
    dmod_cols = lax.dynamic_slice_in_dim(dmod_all, me * wcw, wcw, axis=2).transpose(1, 0, 2)
    dmod_cols = jnp.pad(dmod_cols, ((0, 0), (0, 128 - NDEV), (0, 0))).astype(BF16)
    g_wcond = _cond_bwd(ca_pad, dmod_cols, "cond_bwd")

    out["w_cond"] = _adamw(g_wcond[:, None], w_cond, m_w_cond, v_w_cond, "adamw_wcond")

    small = [
        ("b_cond", b_cond, m_b_cond, v_b_cond, gpiece(1)),
        ("norm_pre", norm_pre, m_norm_pre, v_norm_pre, my_ch(gpiece(2), (DEPTH, 3))),
        ("norm_post", norm_post, m_norm_post, v_norm_post, my_ch(gpiece(3), (DEPTH, 3))),
        ("fox_b_f", fox_b_f, m_fox_b_f, v_fox_b_f, gpiece(4)),
        ("sconv_conv_w", sconv_conv_w, m_sconv_conv_w, v_sconv_conv_w, my_ch(gpiece(5), (1, 3))),
        ("lru_conv_w", lru_conv_w, m_lru_conv_w, v_lru_conv_w, my_ch(gpiece(6), (1, 4))),
        ("lru_conv_b", lru_conv_b, m_lru_conv_b, v_lru_conv_b, my_ch(gpiece(7), (1,))),
        ("lru_w_a", lru_w_a, m_lru_w_a, v_lru_w_a, gpiece(8)),
        ("lru_b_a", lru_b_a, m_lru_b_a, v_lru_b_a, gpiece(9)),
        ("lru_w_x", lru_w_x, m_lru_w_x, v_lru_w_x, gpiece(10)),
        ("lru_b_x", lru_b_x, m_lru_b_x, v_lru_b_x, gpiece(11)),
        ("lru_lambda", lru_lambda, m_lru_lambda, v_lru_lambda, my_ch(gpiece(12), (1,))),
    ]
    pack = lambda ts: jnp.concatenate([t.reshape(-1) for t in ts])
    ssz = [w_.size for _, w_, _, _, _ in small]
    tot = sum(ssz)
    padr = (-tot) % (16 * 1024)
    pk = lambda ts: jnp.pad(pack(ts), (0, padr)).reshape(1, -1, 1024)
    sg, sd, smm, svv = _adamw(pk([t[4] for t in small])[:, None], pk([t[1] for t in small]),
                              pk([t[2] for t in small]), pk([t[3] for t in small]), "adamw_small")
    soff = 0
    for (name, w_, _, _, _), n_ in zip(small, ssz):
        out[name] = tuple(t.reshape(-1)[soff:soff + n_].reshape(w_.shape) for t in (sg, sd, smm, svv))
        soff += n_

    after[0] = sd
    update_slice("wi00")
    out["w_ffn_in"] = tuple(t.reshape(w_ffn_in.shape) for t in bufs["wi"])
    out["w_ffn_out"] = tuple(t.reshape(w_ffn_out.shape) for t in bufs["wo"])

    names =["w_cond", "b_cond", "norm_pre", "norm_post", "w_ffn_in", "w_ffn_out", "fox_w_in", "fox_b_f",
             "fox_w_out", "sconv_w_in", "sconv_conv_w", "sconv_w_out", "lru_w_in", "lru_conv_w", "lru_conv_b",
             "lru_w_a", "lru_b_a", "lru_w_x", "lru_b_x", "lru_lambda", "lru_w_out"]
    return (loss, grad_x, *[out[n_][0] for n_ in names], *[out[n_][1] for n_ in names],
            *[out[n_][2] for n_ in names], *[out[n_][3] for n_ in names])
```

```python
import functools
import math

import jax
import jax.numpy as jnp
from jax import lax
from jax.experimental import pallas as pl
from jax.experimental.pallas import tpu as pltpu

F32 = jnp.float32
BF16 = jnp.bfloat16
NDEV = 8
D = 1024
DFF = 2816
FFB = 704
HEADS = 16
HDIM = 64
DEPTH = 4
RMS_EPS = 1e-6
LRU_C = 8.0
ADAM_LR, ADAM_B1, ADAM_B2, ADAM_EPS, ADAM_WD, ADAM_STEP = 0.001, 0.9, 0.999, 1e-08, 0.01, 10
MESH = pl.DeviceIdType.MESH
HBM = pl.BlockSpec(memory_space=pltpu.HBM)
VMEM_BIG = 48 * 1024 * 1024

_NN = (((1,), (0,)), ((), ()))
_NT = (((1,), (1,)), ((), ()))
_TN = (((0,), (0,)), ((), ()))
_DIMS = {"nn": _NN, "nt": _NT, "tn": _TN}


def _cparams(sem=None, vmem=None):
    kw = {}
    if sem is not None:
        kw["dimension_semantics"] = sem
    if vmem is not None:
        kw["vmem_limit_bytes"] = vmem
    return pltpu.CompilerParams(**kw)


def _sigmoid(x):
    return 1.0 / (1.0 + jnp.exp(-x))


def _softplus(x):
    return jnp.maximum(x, 0.0) + jnp.log(1.0 + jnp.exp(-jnp.abs(x)))


_GELU_C = math.sqrt(2.0 / math.pi)


def _gelu_parts(x):
    u = _GELU_C * (x + 0.044715 * x * x * x)
    t = jnp.tanh(u)
    g = 0.5 * x * (1.0 + t)
    dg = 0.5 * (1.0 + t) + 0.5 * x * (1.0 - t * t) * _GELU_C * (1.0 + 3.0 * 0.044715 * x * x)
    return g, dg


def _mesh_pos():
    ax, ay, ac = lax.axis_index("x"), lax.axis_index("y"), lax.axis_index("c")
    return ax, ay, ac, 4 * ax + 2 * ay + ac


def _peer(ax, ay, ac, d):
    px = 1 - ax if (d >> 2) & 1 else ax
    py = 1 - ay if (d >> 1) & 1 else ay
    pc = 1 - ac if d & 1 else ac
    return (px, py, pc), 4 * px + 2 * py + pc


def _exchange(x, axis, name, gather):
    if gather:
        x = jnp.expand_dims(x, axis)
        oshape = x.shape[:axis] + (NDEV,) + x.shape[axis + 1:]
    else:
        oshape = x.shape
    lead = (slice(None),) * axis

    def blk(ref, k):
        return ref.at[lead + (pl.ds(k, 1),)]

    def body(x_ref, o_ref, send_sems, recv_sems, local_sem):
        ax, ay, ac, me = _mesh_pos()
        src = (lambda k: x_ref) if gather else (lambda k: blk(x_ref, k))
        mine = pltpu.make_async_copy(src(me), blk(o_ref, me), local_sem)
        mine.start()
        sends = []
        for d in range(1, NDEV):
            peer, pidx = _peer(ax, ay, ac, d)
            cp = pltpu.make_async_remote_copy(
                src_ref=src(pidx), dst_ref=blk(o_ref, me), send_sem=send_sems.at[d - 1],
                recv_sem=recv_sems.at[d - 1], device_id=peer, device_id_type=MESH)
            cp.start()
            sends.append(cp)
        for d in range(1, NDEV):
            peer, pidx = _peer(ax, ay, ac, d)
            pltpu.make_async_remote_copy(
                src_ref=src(pidx), dst_ref=blk(o_ref, pidx), send_sem=send_sems.at[d - 1],
                recv_sem=recv_sems.at[d - 1], device_id=peer, device_id_type=MESH).wait_recv()
        for cp in sends:
            cp.wait_send()
        mine.wait()

    return pl.pallas_call(
        body, name=name, out_shape=jax.ShapeDtypeStruct(oshape, x.dtype),
        in_specs=[HBM], out_specs=HBM,
        scratch_shapes=[pltpu.SemaphoreType.DMA((NDEV - 1,)), pltpu.SemaphoreType.DMA((NDEV - 1,)),
                        pltpu.SemaphoreType.DMA(())],
    )(x)


def _all_gather(x, axis, name):
    return _exchange(x, axis, name, True)


SEM = pl.BlockSpec(memory_space=pltpu.SEMAPHORE)
EFFECT = pltpu.SideEffectType.DATAFLOW_SIDE_EFFECTING


def _exchange_start(x, name, gather, land=None):
    me = 4 * lax.axis_index("x") + 2 * lax.axis_index("y") + lax.axis_index("c")
    if gather:
        x = x[None]
        oshape = (NDEV,) + x.shape[1:]
        own = x
    else:
        oshape = x.shape
        own = lax.dynamic_index_in_dim(x, me, 0, keepdims=True)
    land = lax.empty(oshape, x.dtype) if land is None else land
    land = lax.dynamic_update_index_in_dim(land, own, me, 0)

    def blk(ref, k):
        return ref.at[pl.ds(k, 1)]

    def body(x_ref, land_ref, send_sems, recv_sems, x_thru, land_thru, token):
        ax, ay, ac, me = _mesh_pos()
        src = (lambda k: x_ref) if gather else (lambda k: blk(x_ref, k))
        for d in range(1, NDEV):
            peer, pidx = _peer(ax, ay, ac, d)
            pltpu.make_async_remote_copy(
                src_ref=src(pidx), dst_ref=blk(land_ref, me), send_sem=send_sems.at[d - 1],
                recv_sem=recv_sems.at[d - 1], device_id=peer, device_id_type=MESH).start()
        token[...] = jnp.zeros_like(token)

    return pl.pallas_call(
        body, name=name,
        out_shape=(pltpu.SemaphoreType.DMA((NDEV - 1,)), pltpu.SemaphoreType.DMA((NDEV - 1,)),
                   pltpu.HBM(x.shape, x.dtype), pltpu.HBM(oshape, x.dtype),
                   jax.ShapeDtypeStruct((8, 128), F32)),
        in_specs=(HBM, HBM), out_specs=(SEM, SEM, HBM, HBM, pl.BlockSpec(memory_space=pltpu.VMEM)),
        input_output_aliases={0: 2, 1: 3},
        compiler_params=pltpu.CompilerParams(has_side_effects=EFFECT),
    )(pltpu.with_memory_space_constraint(x, pltpu.HBM),
      pltpu.with_memory_space_constraint(land, pltpu.HBM))


def _exchange_wait(handle, after, name, gather):
    send_sems, recv_sems, x_thru, land_thru = handle

    def blk(ref, k):
        return ref.at[pl.ds(k, 1)]

    def body(x_ref, land_ref, send_sems, recv_sems, after_ref, x_dead, got_ref):
        ax, ay, ac, me = _mesh_pos()
        src = (lambda k: x_ref) if gather else (lambda k: blk(x_ref, k))
        for d in range(1, NDEV):
            peer, pidx = _peer(ax, ay, ac, d)
            cp = pltpu.make_async_remote_copy(
                src_ref=src(pidx), dst_ref=blk(land_ref, pidx), send_sem=send_sems.at[d - 1],
                recv_sem=recv_sems.at[d - 1], device_id=peer, device_id_type=MESH)
            cp.wait_send()
            cp.wait_recv()

    return pl.pallas_call(
        body, name=name,
        out_shape=(pltpu.HBM(x_thru.shape, x_thru.dtype), pltpu.HBM(land_thru.shape, land_thru.dtype)),
        in_specs=(HBM, HBM, SEM, SEM, pl.BlockSpec(memory_space=pl.ANY)), out_specs=(HBM, HBM),
        input_output_aliases={0: 0, 1: 1},
        compiler_params=pltpu.CompilerParams(has_side_effects=EFFECT),
    )(x_thru, land_thru, send_sems, recv_sems, after)[1]


_HOPS = (1, 2, 4, 6)


def _blk(ref, k):
    return ref.at[pl.ds(k, 1)]


def _alloc_many(shapes, dtype, name):
    def body(*refs):
        pass

    return pl.pallas_call(
        body, name=name, out_shape=tuple(pltpu.HBM(s_, dtype) for s_ in shapes),
        out_specs=tuple(HBM for _ in shapes),
    )()


def _gather_start(x, land, name):
    me = 4 * lax.axis_index("x") + 2 * lax.axis_index("y") + lax.axis_index("c")
    x = x[None]
    oshape = (NDEV,) + x.shape[1:]
    land = lax.dynamic_update_index_in_dim(land, x, me, 0)

    def body(x_ref, land_ref, send_sems, recv_sems, x_thru, land_thru, token):
        ax, ay, ac, me = _mesh_pos()
        for i, d in enumerate(_HOPS):
            peer, _ = _peer(ax, ay, ac, d)
            pltpu.make_async_remote_copy(
                src_ref=x_ref, dst_ref=_blk(land_ref, me), send_sem=send_sems.at[i],
                recv_sem=recv_sems.at[i], device_id=peer, device_id_type=MESH).start()
        token[...] = jnp.zeros_like(token)

    send1, recv1, x_thru, land_thru, token = pl.pallas_call(
        body, name=name,
        out_shape=(pltpu.SemaphoreType.DMA((4,)), pltpu.SemaphoreType.DMA((4,)),
                   pltpu.HBM(x.shape, x.dtype), pltpu.HBM(oshape, x.dtype),
                   jax.ShapeDtypeStruct((8, 128), F32)),
        in_specs=(HBM, HBM), out_specs=(SEM, SEM, HBM, HBM, pl.BlockSpec(memory_space=pltpu.VMEM)),
        input_output_aliases={0: 2, 1: 3},
        compiler_params=pltpu.CompilerParams(has_side_effects=EFFECT),
    )(pltpu.with_memory_space_constraint(x, pltpu.HBM),
      pltpu.with_memory_space_constraint(land, pltpu.HBM))
    return dict(send1=send1, recv1=recv1, x=x_thru, land=land_thru), token


def _gather_relay(hnd, after, name):
    def body(land_ref, recv1, after_ref, send2, recv2, land_thru, token):
        ax, ay, ac, me = _mesh_pos()
        sibling, _ = _peer(ax, ay, ac, 1)
        for i, d in enumerate(_HOPS[1:]):
            peer, pidx = _peer(ax, ay, ac, d)
            came = _blk(land_ref, pidx)
            pltpu.make_async_remote_copy(
                src_ref=came, dst_ref=came, send_sem=send2.at[i], recv_sem=recv1.at[i + 1],
                device_id=peer, device_id_type=MESH).wait_recv()
            pltpu.make_async_remote_copy(
                src_ref=came, dst_ref=came, send_sem=send2.at[i], recv_sem=recv2.at[i],
                device_id=sibling, device_id_type=MESH).start()
        token[...] = jnp.zeros_like(token)

    land = hnd["land"]
    send2, recv2, land_thru, token = pl.pallas_call(
        body, name=name,
        out_shape=(pltpu.SemaphoreType.DMA((3,)), pltpu.SemaphoreType.DMA((3,)),
                   pltpu.HBM(land.shape, land.dtype), jax.ShapeDtypeStruct((8, 128), F32)),
        in_specs=(HBM, SEM, ANYSPEC), out_specs=(SEM, SEM, HBM, pl.BlockSpec(memory_space=pltpu.VMEM)),
        input_output_aliases={0: 2},
        compiler_params=pltpu.CompilerParams(has_side_effects=EFFECT),
    )(land, hnd["recv1"], after)
    return dict(hnd, send2=send2, recv2=recv2, land=land_thru), token


def _gather_wait(hnd, after, name):
    def body(x_ref, land_ref, send1, recv1, send2, recv2, after_ref, x_dead, got_ref):
        ax, ay, ac, me = _mesh_pos()
        sibling, sidx = _peer(ax, ay, ac, 1)
        for i, d in enumerate(_HOPS):
            peer, pidx = _peer(ax, ay, ac, d)
            cp = pltpu.make_async_remote_copy(
                src_ref=x_ref, dst_ref=_blk(land_ref, pidx), send_sem=send1.at[i], recv_sem=recv1.at[i],
                device_id=peer, device_id_type=MESH)
            cp.wait_send()
            if i == 0:
                cp.wait_recv()
        for i, d in enumerate(_HOPS[1:]):
            _, pidx = _peer(ax, ay, ac, d)
            _, fidx = _peer(ax, ay, ac, d ^ 1)
            cp = pltpu.make_async_remote_copy(
                src_ref=_blk(land_ref, pidx), dst_ref=_blk(land_ref, fidx), send_sem=send2.at[i],
                recv_sem=recv2.at[i], device_id=sibling, device_id_type=MESH)
            cp.wait_send()
            cp.wait_recv()

    x, land = hnd["x"], hnd["land"]
    return pl.pallas_call(
        body, name=name,
        out_shape=(pltpu.HBM(x.shape, x.dtype), pltpu.HBM(land.shape, land.dtype)),
        in_specs=(HBM, HBM, SEM, SEM, SEM, SEM, ANYSPEC), out_specs=(HBM, HBM),
        input_output_aliases={0: 0, 1: 1},
        compiler_params=pltpu.CompilerParams(has_side_effects=EFFECT),
    )(x, land, hnd["send1"], hnd["recv1"], hnd["send2"], hnd["recv2"], after)[1]


def _mm(a, b, *, mode, grid, a_spec, b_spec, o_spec, o_shape, acc_shape, out_dtype, name):
    nred = grid[2]

    def body(a_ref, b_ref, o_ref, *scratch):
        p = lax.dot_general(a_ref[...], b_ref[...], _DIMS[mode], preferred_element_type=F32)
        if nred == 1:
            o_ref[...] = p.astype(o_ref.dtype)
            return
        acc = scratch[0]
        r = pl.program_id(2)

        @pl.when(r == 0)
        def _():
            acc[...] = p

        @pl.when(r > 0)
        def _():
            acc[...] += p

        @pl.when(r == nred - 1)
        def _():
            o_ref[...] = acc[...].astype(o_ref.dtype)

    return pl.pallas_call(
        body, name=name, out_shape=jax.ShapeDtypeStruct(o_shape, out_dtype), grid=grid,
        in_specs=[a_spec, b_spec], out_specs=o_spec,
        scratch_shapes=[] if nred == 1 else [pltpu.VMEM(acc_shape, F32)],
        compiler_params=_cparams(("parallel", "parallel", "arbitrary"), VMEM_BIG),
    )(a, b)


def _tile(n, cands):
    for c in cands:
        if n % c == 0:
            return c
    return n


def _mm_nn(a, b, out_dtype, name):
    m, k = a.shape
    n = b.shape[1]
    tm, tn = _tile(m, (512, 256, 128)), _tile(n, (1024, 640, 512, 256, 128))
    return _mm(a, b, mode="nn", grid=(m // tm, n // tn, 1),
               a_spec=pl.BlockSpec((tm, k), lambda i, j, r: (i, 0)),
               b_spec=pl.BlockSpec((k, tn), lambda i, j, r: (0, j)),
               o_spec=pl.BlockSpec((tm, tn), lambda i, j, r: (i, j)),
               o_shape=(m, n), acc_shape=None, out_dtype=out_dtype, name=name)


def _mm_nt(a, b, out_dtype, name):
    m, n = a.shape
    k = b.shape[0]
    tm, tn = _tile(m, (512, 256, 128)), _tile(n, (1024, 640, 512, 256, 128))
    return _mm(a, b, mode="nt", grid=(m // tm, 1, n // tn),
               a_spec=pl.BlockSpec((tm, tn), lambda i, j, r: (i, r)),
               b_spec=pl.BlockSpec((k, tn), lambda i, j, r: (0, r)),
               o_spec=pl.BlockSpec((tm, k), lambda i, j, r: (i, 0)),
               o_shape=(m, k), acc_shape=(tm, k), out_dtype=out_dtype, name=name)


def _mm_tn(a, b, out_dtype, name):
    s, k = a.shape
    n = b.shape[1]
    ts, tn = _tile(s, (2048, 1024, 512, 256, 128)), _tile(n, (640, 512, 256, 128))
    return _mm(a, b, mode="tn", grid=(1, n // tn, s // ts),
               a_spec=pl.BlockSpec((ts, k), lambda i, j, r: (r, 0)),
               b_spec=pl.BlockSpec((ts, tn), lambda i, j, r: (r, j)),
               o_spec=pl.BlockSpec((k, tn), lambda i, j, r: (0, j)),
               o_shape=(k, n), acc_shape=(k, tn), out_dtype=out_dtype, name=name)


def _gu_spec(tm, idx):
    return pl.BlockSpec((2, None, tm, FFB), idx)


def _ffn_up(h, wi, name):
    s = h.shape[0]
    tm = _tile(s, (1024, 512, 256, 128))

    def body(h_ref, wg_ref, wu_ref, gu_ref, act_ref):
        hv = h_ref[...]
        g = jnp.dot(hv, wg_ref[...], preferred_element_type=F32)
        u = jnp.dot(hv, wu_ref[...], preferred_element_type=F32)
        gu_ref[0] = g.astype(BF16)
        gu_ref[1] = u.astype(BF16)
        act_ref[...] = (g * _sigmoid(g) * u).astype(BF16)

    wspec = lambda off: pl.BlockSpec((None, D, FFB), lambda j, i: (j + off, 0, 0))
    return pl.pallas_call(
        body, name=name,
        out_shape=(jax.ShapeDtypeStruct((2, 4, s, FFB), BF16), jax.ShapeDtypeStruct((4, s, FFB), BF16)),
        grid=(4, s // tm),
        in_specs=[pl.BlockSpec((tm, D), lambda j, i: (i, 0)), wspec(0), wspec(4)],
        out_specs=(_gu_spec(tm, lambda j, i: (0, j, i, 0)), pl.BlockSpec((None, tm, FFB), lambda j, i: (j, i, 0))),
        compiler_params=_cparams(("parallel", "parallel"), VMEM_BIG),
    )(h, wi, wi)


def _ffn_down(act, wo, name):
    s = act.shape[1]
    tm = _tile(s, (1024, 512, 256, 128))
    return _mm(act, wo, mode="nn", grid=(s // tm, 1, 4),
               a_spec=pl.BlockSpec((None, tm, FFB), lambda i, j, r: (r, i, 0)),
               b_spec=pl.BlockSpec((None, FFB, D), lambda i, j, r: (r, 0, 0)),
               o_spec=pl.BlockSpec((tm, D), lambda i, j, r: (i, 0)),
               o_shape=(s, D), acc_shape=(tm, D), out_dtype=F32, name=name)


def _ffn_down_bwd(dy, gu, wo, name):
    s = dy.shape[0]
    tm = _tile(s, (512, 256, 128))
    ns = s // tm

    def body(dy_ref, gu_ref, wo_ref, da_ref, dwo_ref, acc):
        i = pl.program_id(1)
        dyv = dy_ref[...]
        dact = lax.dot_general(dyv, wo_ref[...], _NT, preferred_element_type=F32)
        gv = gu_ref[0].astype(F32)
        uv = gu_ref[1].astype(F32)
        sg = _sigmoid(gv)
        silu = gv * sg
        da_ref[0] = (dact * uv * (sg * (1.0 + gv * (1.0 - sg)))).astype(BF16)
        da_ref[1] = (dact * silu).astype(BF16)
        p = lax.dot_general((silu * uv).astype(BF16), dyv, _TN, preferred_element_type=F32)

        @pl.when(i == 0)
        def _():
            acc[...] = p

        @pl.when(i > 0)
        def _():
            acc[...] += p

        @pl.when(i == ns - 1)
        def _():
            dwo_ref[...] = acc[...].astype(BF16)

    aspec = _gu_spec(tm, lambda j, i: (0, j, i, 0))
    return pl.pallas_call(
        body, name=name,
        out_shape=(jax.ShapeDtypeStruct((2, 4, s, FFB), BF16), jax.ShapeDtypeStruct((4, FFB, D), BF16)),
        grid=(4, ns),
        in_specs=[pl.BlockSpec((tm, D), lambda j, i: (i, 0)), aspec,
                  pl.BlockSpec((None, FFB, D), lambda j, i: (j, 0, 0))],
        out_specs=(aspec, pl.BlockSpec((None, FFB, D), lambda j, i: (j, 0, 0))),
        scratch_shapes=[pltpu.VMEM((FFB, D), F32)],
        compiler_params=_cparams(("parallel", "arbitrary"), VMEM_BIG),
    )(dy, gu, wo)


def _ffn_dh(da, wi, name):
    s = da.shape[2]
    tm = _tile(s, (1024, 512, 256, 128))

    def body(da_ref, wg_ref, wu_ref, o_ref, acc):
        r = pl.program_id(1)
        p = (lax.dot_general(da_ref[0], wg_ref[...], _NT, preferred_element_type=F32)
             + lax.dot_general(da_ref[1], wu_ref[...], _NT, preferred_element_type=F32))

        @pl.when(r == 0)
        def _():
            acc[...] = p

        @pl.when(r > 0)
        def _():
            acc[...] += p

        @pl.when(r == 3)
        def _():
            o_ref[...] = acc[...].astype(BF16)

    wspec = lambda off: pl.BlockSpec((None, D, FFB), lambda i, r: (r + off, 0, 0))
    return pl.pallas_call(
        body, name=name, out_shape=jax.ShapeDtypeStruct((s, D), BF16), grid=(s // tm, 4),
        in_specs=[_gu_spec(tm, lambda i, r: (0, r, i, 0)), wspec(0), wspec(4)],
        out_specs=pl.BlockSpec((tm, D), lambda i, r: (i, 0)),
        scratch_shapes=[pltpu.VMEM((tm, D), F32)],
        compiler_params=_cparams(("parallel", "arbitrary"), VMEM_BIG),
    )(da, wi, wi)


def _ffn_dwi(h, da, name):
    s = h.shape[0]
    ts = _tile(s, (2048, 1024, 512, 256, 128))
    return _mm(h, da, mode="tn", grid=(NDEV, 1, s // ts),
               a_spec=pl.BlockSpec((ts, D), lambda k, j, r: (r, 0)),
               b_spec=pl.BlockSpec((None, ts, FFB), lambda k, j, r: (k, r, 0)),
               o_spec=pl.BlockSpec((None, D, FFB), lambda k, j, r: (k, 0, 0)),
               o_shape=(NDEV, D, FFB), acc_shape=(D, FFB), out_dtype=BF16, name=name)


TR = 256


def _rows_spec(s):
    tr = _tile(s, (TR, 128))
    return tr, pl.BlockSpec((tr, D), lambda i: (i, 0))


def _pspec(l, sub):
    return pl.BlockSpec((None, None, 8, D), lambda i: (l, sub, 0, 0))


def _pre_math(x, p_ref):
    r = lax.rsqrt(jnp.mean(x * x, axis=1, keepdims=True) + RMS_EPS)
    return (x * r) * p_ref[0:1, :] * (1.0 + p_ref[1:2, :]) + p_ref[2:3, :]


ANYSPEC = pl.BlockSpec(memory_space=pl.ANY)


def _prenorm(x, ptab, l, sub, name, toks=()):
    s = x.shape[0]
    tr, spec = _rows_spec(s)

    def body(x_ref, p_ref, *rest):
        rest[-1][...] = _pre_math(x_ref[...], p_ref).astype(BF16)

    return pl.pallas_call(
        body, name=name, out_shape=jax.ShapeDtypeStruct((s, D), BF16), grid=(s // tr,),
        in_specs=[spec, _pspec(l, sub)] + [ANYSPEC] * len(toks), out_specs=spec,
        compiler_params=_cparams(("parallel",)),
    )(x, ptab, *toks)


def _resid(x, y, ptab, l, sub, coef, nxt, name, toks=()):
    s = x.shape[0]
    tr, spec = _rows_spec(s)

    def body(x_ref, y_ref, p_ref, *rest):
        yv = y_ref[...]
        r = lax.rsqrt(jnp.mean(yv * yv, axis=1, keepdims=True) + RMS_EPS)
        xn = x_ref[...] + (coef * p_ref[4:5, :]) * ((yv * r) * p_ref[3:4, :])
        if nxt is None:
            rest[-1][...] = xn
        else:
            rest[-2][...] = xn
            rest[-1][...] = _pre_math(xn, rest[0]).astype(BF16)

    tspecs = [ANYSPEC] * len(toks)
    if nxt is None:
        return pl.pallas_call(
            body, name=name, out_shape=jax.ShapeDtypeStruct((s, D), F32), grid=(s // tr,),
            in_specs=[spec, spec, _pspec(l, sub)] + tspecs, out_specs=spec,
            compiler_params=_cparams(("parallel",)),
        )(x, y, ptab, *toks), None
    return pl.pallas_call(
        body, name=name,
        out_shape=(jax.ShapeDtypeStruct((s, D), F32), jax.ShapeDtypeStruct((s, D), BF16)),
        grid=(s // tr,),
        in_specs=[spec, spec, _pspec(l, sub), _pspec(*nxt)] + tspecs, out_specs=(spec, spec),
        compiler_params=_cparams(("parallel",)),
    )(x, y, ptab, ptab, *toks)


def _loss_grad(x, tgt, name):
    s = x.shape[0]
    tr, spec = _rows_spec(s)

    def body(x_ref, t_ref, dx_ref, l_ref):
        @pl.when(pl.program_id(0) == 0)
        def _():
            l_ref[...] = jnp.zeros_like(l_ref)

        e = x_ref[...] - t_ref[...]
        dx_ref[...] = e * (1.0 / D)
        l_ref[0:1, :] += jnp.sum(e * e, axis=0, keepdims=True) * (0.5 / D)

    return pl.pallas_call(
        body, name=name,
        out_shape=(jax.ShapeDtypeStruct((s, D), F32), jax.ShapeDtypeStruct((8, D), F32)),
        grid=(s // tr,), in_specs=[spec, spec],
        out_specs=(spec, pl.BlockSpec((8, D), lambda i: (0, 0))),
        compiler_params=_cparams(("arbitrary",)),
    )(x, tgt)


def _post_bwd(dx, y, ptab, l, sub, coef, name, toks=()):
    s = dx.shape[0]
    tr, spec = _rows_spec(s)

    def body(dx_ref, y_ref, p_ref, *rest):
        dy_ref, red_ref = rest[-2:]

        @pl.when(pl.program_id(0) == 0)
        def _():
            red_ref[...] = jnp.zeros_like(red_ref)

        dxv, yv = dx_ref[...], y_ref[...]
        gpost, gate = p_ref[3:4, :], p_ref[4:5, :]
        r = lax.rsqrt(jnp.mean(yv * yv, axis=1, keepdims=True) + RMS_EPS)
        yhat = yv * r
        red_ref[0:1, :] += jnp.sum(dxv * yhat * gpost, axis=0, keepdims=True) * coef
        dn = dxv * (coef * gate)
        red_ref[1:2, :] += jnp.sum(dn * yhat, axis=0, keepdims=True)
        dyh = dn * gpost
        dy_ref[...] = (r * (dyh - yhat * jnp.mean(dyh * yhat, axis=1, keepdims=True))).astype(BF16)

    return pl.pallas_call(
        body, name=name,
        out_shape=(jax.ShapeDtypeStruct((s, D), BF16), jax.ShapeDtypeStruct((8, D), F32)),
        grid=(s // tr,), in_specs=[spec, spec, _pspec(l, sub)] + [ANYSPEC] * len(toks),
        out_specs=(spec, pl.BlockSpec((8, D), lambda i: (0, 0))),
        compiler_params=_cparams(("arbitrary",)),
    )(dx, y, ptab, *toks)


def _pre_bwd(dx, dh, x, ptab, l, sub, name, dh2=None, toks=()):
    s = dx.shape[0]
    tr, spec = _rows_spec(s)
    extra = [] if dh2 is None else [dh2]

    def body(dx_ref, dh_ref, x_ref, p_ref, *rest):
        o_ref, red_ref = rest[-2:]

        @pl.when(pl.program_id(0) == 0)
        def _():
            red_ref[...] = jnp.zeros_like(red_ref)

        dhv, xv = dh_ref[...].astype(F32), x_ref[...]
        if extra:
            dhv = dhv + rest[0][...]
        gpre, scale = p_ref[0:1, :], p_ref[1:2, :]
        r = lax.rsqrt(jnp.mean(xv * xv, axis=1, keepdims=True) + RMS_EPS)
        xhat = xv * r
        red_ref[0:1, :] += jnp.sum(dhv, axis=0, keepdims=True)
        red_ref[1:2, :] += jnp.sum(dhv * xhat * gpre, axis=0, keepdims=True)
        red_ref[2:3, :] += jnp.sum(dhv * xhat * (1.0 + scale), axis=0, keepdims=True)
        dxh = dhv * (gpre * (1.0 + scale))
        o_ref[...] = dx_ref[...] + r * (dxh - xhat * jnp.mean(dxh * xhat, axis=1, keepdims=True))

    return pl.pallas_call(
        body, name=name,
        out_shape=(jax.ShapeDtypeStruct((s, D), F32), jax.ShapeDtypeStruct((8, D), F32)),
        grid=(s // tr,), in_specs=[spec, spec, spec, _pspec(l, sub)] + [spec] * len(extra) + [ANYSPEC] * len(toks),
        out_specs=(spec, pl.BlockSpec((8, D), lambda i: (0, 0))),
        compiler_params=_cparams(("arbitrary",)),
    )(dx, dh, x, ptab, *extra, *toks)


def _cond_fwd(c_pad, wc, bc, name):
    w = wc.shape[2]

    def body(c_ref, w_ref, b_ref, o_ref, ca_ref):
        cv = c_ref[...]
        ca = (cv * _sigmoid(cv)).astype(BF16)
        ca_ref[...] = ca
        o_ref[...] = jnp.dot(ca, w_ref[...].astype(BF16), preferred_element_type=F32) + b_ref[...]

    return pl.pallas_call(
        body, name=name,
        out_shape=(jax.ShapeDtypeStruct((DEPTH, 128, w), F32), jax.ShapeDtypeStruct((128, D), BF16)),
        grid=(DEPTH,),
        in_specs=[pl.BlockSpec((128, D), lambda i: (0, 0)),
                  pl.BlockSpec((None, D, w), lambda i: (i, 0, 0)),
                  pl.BlockSpec((None, 1, w), lambda i: (i, 0, 0))],
        out_specs=(pl.BlockSpec((None, 128, w), lambda i: (i, 0, 0)),
                   pl.BlockSpec((128, D), lambda i: (0, 0))),
        compiler_params=_cparams(("arbitrary",), VMEM_BIG),
    )(c_pad, wc, bc)


def _cond_bwd(ca_pad, dmod, name):
    w = dmod.shape[2]
    return _mm(ca_pad, dmod, mode="tn", grid=(DEPTH, 1, 1),
               a_spec=pl.BlockSpec((128, D), lambda i, j, r: (0, 0)),
               b_spec=pl.BlockSpec((None, 128, w), lambda i, j, r: (i, 0, 0)),
               o_spec=pl.BlockSpec((None, D, w), lambda i, j, r: (i, 0, 0)),
               o_shape=(DEPTH, D, w), acc_shape=None, out_dtype=F32, name=name)


def _split3(x):
    hi = x.astype(BF16)
    r1 = x - hi.astype(F32)
    mid = r1.astype(BF16)
    lo = (r1 - mid.astype(F32)).astype(BF16)
    return hi, mid, lo


def _tri_dot(t, x):
    hi, mid, lo = _split3(x)
    return (jnp.dot(t, hi, preferred_element_type=F32) + jnp.dot(t, mid, preferred_element_type=F32)
            + jnp.dot(t, lo, preferred_element_type=F32))


def _fox_cum(fl, bf, name):
    s = fl.shape[0]
    tb = _tile(s, (256, 128))

    def body(fl_ref, b_ref, cum_ref):
        row = lax.broadcasted_iota(jnp.int32, (tb, tb), 0)
        col = lax.broadcasted_iota(jnp.int32, (tb, tb), 1)
        tri = (col <= row).astype(BF16)
        carry = jnp.zeros((1, 128), F32)
        for blk in range(s // tb):
            z = fl_ref[blk * tb:(blk + 1) * tb, :] + b_ref[0:1, :]
            lf = jnp.minimum(z, 0.0) - jnp.log(1.0 + jnp.exp(-jnp.abs(z)))
            cum_ref[blk * tb:(blk + 1) * tb, :] = _tri_dot(tri, lf) + carry
            carry = carry + jnp.sum(lf, axis=0, keepdims=True)

    return pl.pallas_call(
        body, name=name, out_shape=jax.ShapeDtypeStruct((s, 128), F32),
    )(fl, bf)


def _fox_cum_bwd(dcum, fl, bf, name):
    s = fl.shape[0]
    tb = _tile(s, (256, 128))

    def body(dc_ref, fl_ref, b_ref, dfl_ref, db_ref):
        row = lax.broadcasted_iota(jnp.int32, (tb, tb), 0)
        col = lax.broadcasted_iota(jnp.int32, (tb, tb), 1)
        tri = (col >= row).astype(BF16)
        carry = jnp.zeros((1, 128), F32)
        dbs = jnp.zeros((1, 128), F32)
        for blk in reversed(range(s // tb)):
            dc = dc_ref[blk * tb:(blk + 1) * tb, :]
            dl = _tri_dot(tri, dc) + carry
            carry = carry + jnp.sum(dc, axis=0, keepdims=True)
            z = fl_ref[blk * tb:(blk + 1) * tb, :] + b_ref[0:1, :]
            dz = dl * _sigmoid(-z)
            dfl_ref[blk * tb:(blk + 1) * tb, :] = dz
            dbs = dbs + jnp.sum(dz, axis=0, keepdims=True)
        db_ref[...] = jnp.broadcast_to(dbs, (8, 128))

    return pl.pallas_call(
        body, name=name,
        out_shape=(jax.ShapeDtypeStruct((s, 128), F32), jax.ShapeDtypeStruct((8, 128), F32)),
    )(dcum, fl, bf)


def _mm_sections(h, w, nsec, out_dtype, name):
    s = h.shape[0]
    tm = _tile(s, (512, 256, 128))
    return _mm(h, w, mode="nn", grid=(s // tm, nsec, 1),
               a_spec=pl.BlockSpec((tm, D), lambda i, j, r: (i, 0)),
               b_spec=pl.BlockSpec((D, D), lambda i, j, r: (0, j)),
               o_spec=pl.BlockSpec((None, tm, D), lambda i, j, r: (j, i, 0)),
               o_shape=(nsec, s, D), acc_shape=None, out_dtype=out_dtype, name=name)


def _fox_qkv(h, w_qkv, name):
    return _mm_sections(h, w_qkv, 3, BF16, name)


def _fox_qkv_dx(dqkv, w_qkv, name):
    s = dqkv.shape[1]
    tm = _tile(s, (512, 256, 128))
    return _mm(dqkv, w_qkv, mode="nt", grid=(s // tm, 1, 3),
               a_spec=pl.BlockSpec((None, tm, D), lambda i, j, r: (r, i, 0)),
               b_spec=pl.BlockSpec((D, D), lambda i, j, r: (0, r)),
               o_spec=pl.BlockSpec((tm, D), lambda i, j, r: (i, 0)),
               o_shape=(s, D), acc_shape=(tm, D), out_dtype=F32, name=name)


def _fox_qkv_dw(h, dqkv, name):
    s = h.shape[0]
    ts = _tile(s, (2048, 1024, 512, 256, 128))
    return _mm(h, dqkv, mode="tn", grid=(3, 1, s // ts),
               a_spec=pl.BlockSpec((ts, D), lambda i, j, r: (r, 0)),
               b_spec=pl.BlockSpec((None, ts, D), lambda i, j, r: (i, r, 0)),
               o_spec=pl.BlockSpec((None, D, D), lambda i, j, r: (i, 0, 0)),
               o_shape=(3, D, D), acc_shape=(D, D), out_dtype=BF16, name=name)


HPAIRS = HEADS // 2


def _first_head():
    return lax.broadcasted_iota(jnp.int32, (1, 2 * HDIM), 1) < HDIM


def _one_head(x, sel):
    return jnp.where(sel, x, jnp.zeros_like(x))


def _fox_scores(qm, k_ref, cq, ck, qi, tq, n):
    sc = lax.dot_general(qm, k_ref[0:n, :], _NT, preferred_element_type=F32) * (HDIM ** -0.5)
    sc = sc + cq - ck
    row = lax.broadcasted_iota(jnp.int32, (tq, n), 0) + qi * tq
    col = lax.broadcasted_iota(jnp.int32, (tq, n), 1)
    return sc, col <= row


def _fox_specs(s):
    sect = lambda i: pl.BlockSpec((None, s, 2 * HDIM), lambda p: (i, 0, p))
    ospec = pl.BlockSpec((s, 2 * HDIM), lambda p: (0, p))
    cspec = pl.BlockSpec((s, 128), lambda p: (0, 0))
    rspec = pl.BlockSpec((None, 2, 1, s), lambda p: (p, 0, 0, 0))
    return sect, ospec, cspec, rspec


def _head_lane(hh):
    return lax.broadcasted_iota(jnp.int32, (1, 128), 1) == 2 * pl.program_id(0) + hh


def _pick(tile, hsel):
    return jnp.sum(jnp.where(hsel, tile, 0.0), axis=1, keepdims=True)


def _fox_attn_fwd(qkv, cum, ck, name):
    s = qkv.shape[1]
    tq = _tile(s, (256, 128))

    def body(q_ref, k_ref, v_ref, cum_ref, ck_ref, o_ref, lse_ref):
        first = _first_head()

        @pl.when(pl.program_id(0) == 0)
        def _():
            lse_ref[...] = jnp.zeros_like(lse_ref)

        for qi in range(s // tq):
            n = (qi + 1) * tq
            rows = slice(qi * tq, n)
            q2 = q_ref[rows, :]
            cum_t, lse_t = cum_ref[rows, :], lse_ref[rows, :]
            outs = []
            for hh in range(2):
                sel = first if hh == 0 else jnp.logical_not(first)
                hsel = _head_lane(hh)
                sc, keep = _fox_scores(_one_head(q2, sel), k_ref, _pick(cum_t, hsel), ck_ref[hh, :, 0:n], qi, tq, n)
                sc = jnp.where(keep, sc, -1e30)
                m = jnp.max(sc, axis=1, keepdims=True)
                p = jnp.exp(sc - m)
                lsum = jnp.sum(p, axis=1, keepdims=True)
                outs.append(jnp.dot(p.astype(BF16), v_ref[0:n, :], preferred_element_type=F32) / lsum)
                lse_t = jnp.where(hsel, m + jnp.log(lsum), lse_t)
            lse_ref[rows, :] = lse_t
            o_ref[rows, :] = jnp.where(first, outs[0], outs[1]).astype(BF16)

    sect, ospec, cspec, rspec = _fox_specs(s)
    return pl.pallas_call(
        body, name=name,
        out_shape=(jax.ShapeDtypeStruct((s, D), BF16), jax.ShapeDtypeStruct((s, 128), F32)),
        grid=(HPAIRS,), in_specs=[sect(0), sect(1), sect(2), cspec, rspec], out_specs=(ospec, cspec),
        compiler_params=_cparams(("arbitrary",), VMEM_BIG),
    )(qkv, qkv, qkv, cum, ck)


def _fox_attn_bwd(qkv, do, lse, cum, ck, name):
    s = qkv.shape[1]
    tq = _tile(s, (256, 128))
    scale = HDIM ** -0.5

    def body(q_ref, k_ref, v_ref, do_ref, lse_ref, cum_ref, ck_ref,
             dqkv_ref, dcq_ref, dck_ref, dk_acc, dv_acc, dck_acc):
        first = _first_head()

        @pl.when(pl.program_id(0) == 0)
        def _():
            dcq_ref[...] = jnp.zeros_like(dcq_ref)

        dk_acc[...] = jnp.zeros_like(dk_acc)
        dv_acc[...] = jnp.zeros_like(dv_acc)
        dck_acc[...] = jnp.zeros_like(dck_acc)
        for qi in range(s // tq):
            n = (qi + 1) * tq
            rows = slice(qi * tq, n)
            q2, do2 = q_ref[rows, :], do_ref[rows, :]
            cum_t, lse_t, dcq_t = cum_ref[rows, :], lse_ref[rows, :], dcq_ref[rows, :]
            dq, dk, dv = [], [], []
            for hh in range(2):
                sel = first if hh == 0 else jnp.logical_not(first)
                hsel = _head_lane(hh)
                sc, keep = _fox_scores(_one_head(q2, sel), k_ref, _pick(cum_t, hsel), ck_ref[hh, :, 0:n], qi, tq, n)
                p = jnp.where(keep, jnp.exp(sc - _pick(lse_t, hsel)), 0.0)
                dp = lax.dot_general(_one_head(do2, sel), v_ref[0:n, :], _NT, preferred_element_type=F32)
                ds = p * (dp - jnp.sum(p * dp, axis=1, keepdims=True))
                dsb = ds.astype(BF16)
                dq.append(jnp.dot(dsb, k_ref[0:n, :], preferred_element_type=F32))
                dk.append(lax.dot_general(dsb, q2, _TN, preferred_element_type=F32))
                dv.append(lax.dot_general(p.astype(BF16), do2, _TN, preferred_element_type=F32))
                dcq_t = jnp.where(hsel, jnp.sum(ds, axis=1, keepdims=True), dcq_t)
                dck_acc[hh, :, 0:n] -= jnp.sum(ds, axis=0, keepdims=True)
            dcq_ref[rows, :] = dcq_t
            dqkv_ref[0, rows, :] = (jnp.where(first, dq[0], dq[1]) * scale).astype(BF16)
            dk_acc[0:n, :] += jnp.where(first, dk[0], dk[1]) * scale
            dv_acc[0:n, :] += jnp.where(first, dv[0], dv[1])
        dqkv_ref[1] = dk_acc[...].astype(BF16)
        dqkv_ref[2] = dv_acc[...].astype(BF16)
        dck_ref[...] = dck_acc[...]

    sect, ospec, cspec, rspec = _fox_specs(s)
    return pl.pallas_call(
        body, name=name,
        out_shape=(jax.ShapeDtypeStruct((3, s, D), BF16), jax.ShapeDtypeStruct((s, 128), F32),
                   jax.ShapeDtypeStruct((HPAIRS, 2, 1, s), F32)),
        grid=(HPAIRS,), in_specs=[sect(0), sect(1), sect(2), ospec, cspec, cspec, rspec],
        out_specs=(pl.BlockSpec((3, s, 2 * HDIM), lambda p: (0, 0, p)), cspec, rspec),
        scratch_shapes=[pltpu.VMEM((s, 2 * HDIM), F32), pltpu.VMEM((s, 2 * HDIM), F32), pltpu.VMEM((2, 1, s), F32)],
        compiler_params=_cparams(("arbitrary",), VMEM_BIG),
    )(qkv, qkv, qkv, do, lse, cum, ck)


TC = 256
HALO = 8


def _chunk_specs(s):
    tc = _tile(s, (TC, 128))
    per = tc // HALO
    nblk = s // HALO
    cur = pl.BlockSpec((tc, 8, 128), lambda i: (i, 0, 0))
    past = pl.BlockSpec((HALO, 8, 128), lambda i: (jnp.maximum(i * per - 1, 0), 0, 0))
    future = pl.BlockSpec((HALO, 8, 128), lambda i: (jnp.minimum((i + 1) * per, nblk - 1), 0, 0))
    return tc, cur, past, future


def _vec_spec(n):
    return pl.BlockSpec((n, 8, 128), lambda i: (0, 0, 0))


def _conv_past(buf, w_ref, kw, tc):
    out = w_ref[kw - 1] * buf[HALO:HALO + tc]
    for k in range(kw - 1):
        off = HALO - (kw - 1) + k
        out = out + w_ref[k] * buf[off:off + tc]
    return out


def _sconv_fwd(bg, cg, xv, w, name):
    s = bg.shape[0]
    tc, cur, past, _ = _chunk_specs(s)

    def body(bg_ref, cg_ref, xv_ref, cgp_ref, xvp_ref, w_ref, y_ref, zbuf):
        first = pl.program_id(0) == 0
        zbuf[0:HALO] = jnp.where(first, 0.0, cgp_ref[...] * xvp_ref[...])
        zbuf[HALO:HALO + tc] = cg_ref[...] * xv_ref[...]
        y_ref[...] = bg_ref[...] * _conv_past(zbuf, w_ref, 3, tc)

    return pl.pallas_call(
        body, name=name, out_shape=jax.ShapeDtypeStruct((s, 8, 128), F32), grid=(s // tc,),
        in_specs=[cur, cur, cur, past, past, _vec_spec(3)], out_specs=cur,
        scratch_shapes=[pltpu.VMEM((tc + HALO, 8, 128), F32)],
        compiler_params=_cparams(("parallel",)),
    )(bg, cg, xv, cg, xv, w)


def _sconv_bwd(dy, bg, cg, xv, w, name):
    s = dy.shape[0]
    tc, cur, past, future = _chunk_specs(s)
    nch = s // tc

    def body(dy_ref, bg_ref, cg_ref, xv_ref, cgp_ref, xvp_ref, dyf_ref, bgf_ref, w_ref,
             dbg_ref, dcg_ref, dxv_ref, dw_ref, zbuf, dbuf):
        i = pl.program_id(0)

        @pl.when(i == 0)
        def _():
            dw_ref[...] = jnp.zeros_like(dw_ref)

        z = cg_ref[...] * xv_ref[...]
        zbuf[0:HALO] = jnp.where(i == 0, 0.0, cgp_ref[...] * xvp_ref[...])
        zbuf[HALO:HALO + tc] = z
        dyv = dy_ref[...]
        dbg_ref[...] = dyv * _conv_past(zbuf, w_ref, 3, tc)
        dbuf[0:tc] = dyv * bg_ref[...]
        dbuf[tc:tc + HALO] = jnp.where(i == nch - 1, 0.0, dyf_ref[...] * bgf_ref[...])
        dz = jnp.zeros((tc, 8, 128), F32)
        for k in range(3):
            sh = dbuf[2 - k:2 - k + tc]
            dz = dz + w_ref[k] * sh
            dw_ref[k] += jnp.sum(z * sh, axis=0)
        dcg_ref[...] = dz * xv_ref[...]
        dxv_ref[...] = dz * cg_ref[...]

    shp = jax.ShapeDtypeStruct((s, 8, 128), F32)
    return pl.pallas_call(
        body, name=name, out_shape=(shp, shp, shp, jax.ShapeDtypeStruct((3, 8, 128), F32)),
        grid=(nch,),
        in_specs=[cur, cur, cur, cur, past, past, future, future, _vec_spec(3)],
        out_specs=(cur, cur, cur, _vec_spec(3)),
        scratch_shapes=[pltpu.VMEM((tc + HALO, 8, 128), F32), pltpu.VMEM((tc + HALO, 8, 128), F32)],
        compiler_params=_cparams(("arbitrary",)),
    )(dy, bg, cg, xv, cg, xv, dy, bg, w)


def _lru_conv_fwd(xp, wb, name):
    s = xp.shape[0]
    tc, cur, past, _ = _chunk_specs(s)

    def body(x_ref, xp_ref, w_ref, o_ref, buf):
        buf[0:HALO] = jnp.where(pl.program_id(0) == 0, 0.0, xp_ref[...])
        buf[HALO:HALO + tc] = x_ref[...]
        o_ref[...] = _conv_past(buf, w_ref, 4, tc) + w_ref[4]

    return pl.pallas_call(
        body, name=name, out_shape=jax.ShapeDtypeStruct((s, 8, 128), F32), grid=(s // tc,),
        in_specs=[cur, past, _vec_spec(8)], out_specs=cur,
        scratch_shapes=[pltpu.VMEM((tc + HALO, 8, 128), F32)],
        compiler_params=_cparams(("parallel",)),
    )(xp, xp, wb)


def _lru_conv_bwd(dxb, xp, wb, name):
    s = dxb.shape[0]
    tc, cur, _, future = _chunk_specs(s)
    nch = s // tc

    def body(d_ref, df_ref, x_ref, w_ref, o_ref, red_ref, dbuf):
        i = pl.program_id(0)

        @pl.when(i == 0)
        def _():
            red_ref[...] = jnp.zeros_like(red_ref)

        dv = d_ref[...]
        dbuf[0:tc] = dv
        dbuf[tc:tc + HALO] = jnp.where(i == nch - 1, 0.0, df_ref[...])
        xv = x_ref[...]
        dx = jnp.zeros((tc, 8, 128), F32)
        for k in range(4):
            sh = dbuf[3 - k:3 - k + tc]
            dx = dx + w_ref[k] * sh
            red_ref[k] += jnp.sum(xv * sh, axis=0)
        red_ref[4] += jnp.sum(dv, axis=0)
        o_ref[...] = dx

    return pl.pallas_call(
        body, name=name,
        out_shape=(jax.ShapeDtypeStruct((s, 8, 128), F32), jax.ShapeDtypeStruct((8, 8, 128), F32)),
        grid=(nch,), in_specs=[cur, future, cur, _vec_spec(8)], out_specs=(cur, _vec_spec(8)),
        scratch_shapes=[pltpu.VMEM((tc + HALO, 8, 128), F32)],
        compiler_params=_cparams(("arbitrary",)),
    )(dxb, dxb, xp, wb)


def _lru_gates(ra, ia, pv_ref):
    sp = _softplus(-pv_ref[2])
    r = _sigmoid(ra + pv_ref[0])
    ig = _sigmoid(ia + pv_ref[1])
    log_a = (-LRU_C) * r * sp
    a = jnp.exp(log_a)
    mult = jnp.sqrt(-jnp.tanh(log_a) * (a * a + 1.0))
    return r, ig, a, mult, sp


def _lru_scan_fwd(ra, ia, xb, gate, pv, name):
    s = ra.shape[0]
    tc, cur, _, _ = _chunk_specs(s)

    def body(ra_ref, ia_ref, xb_ref, g_ref, pv_ref, hs_ref, hp_ref, y_ref, abuf, bbuf, hcar):
        @pl.when(pl.program_id(0) == 0)
        def _():
            hcar[...] = jnp.zeros_like(hcar)

        xbv = xb_ref[...]
        _, ig, a, mult, _ = _lru_gates(ra_ref[...], ia_ref[...], pv_ref)
        abuf[...] = a
        bbuf[...] = mult * (ig * xbv)

        def step(t, h):
            hp_ref[t] = h
            h = abuf[t] * h + bbuf[t]
            hs_ref[t] = h
            return h

        hcar[...] = lax.fori_loop(0, tc, step, hcar[...], unroll=8)
        y_ref[...] = hs_ref[...] * _gelu_parts(g_ref[...])[0]

    shp = jax.ShapeDtypeStruct((s, 8, 128), F32)
    return pl.pallas_call(
        body, name=name, out_shape=(shp, shp, shp), grid=(s // tc,),
        in_specs=[cur, cur, cur, cur, _vec_spec(8)], out_specs=(cur, cur, cur),
        scratch_shapes=[pltpu.VMEM((tc, 8, 128), F32), pltpu.VMEM((tc, 8, 128), F32),
                        pltpu.VMEM((8, 128), F32)],
        compiler_params=_cparams(("arbitrary",)),
    )(ra, ia, xb, gate, pv)


def _lru_scan_bwd(dy, ra, ia, xb, gate, hs, hp, pv, name):
    s = dy.shape[0]
    tc = _tile(s, (TC, 128))
    nch = s // tc
    rev = pl.BlockSpec((tc, 8, 128), lambda i: (nch - 1 - i, 0, 0))

    def body(dy_ref, ra_ref, ia_ref, xb_ref, g_ref, hs_ref, hp_ref, pv_ref,
             dg_ref, dra_ref, dia_ref, dxb_ref, red_ref, abuf, dbuf, gbuf, car):
        @pl.when(pl.program_id(0) == 0)
        def _():
            red_ref[...] = jnp.zeros_like(red_ref)
            car[...] = jnp.zeros_like(car)

        xbv = xb_ref[...]
        r, ig, a, mult, sp = _lru_gates(ra_ref[...], ia_ref[...], pv_ref)
        ge, dge = _gelu_parts(g_ref[...])
        dyv = dy_ref[...]
        dg_ref[...] = dyv * hs_ref[...] * dge
        abuf[...] = a
        dbuf[...] = dyv * ge

        def step(k, c):
            t = tc - 1 - k
            g = dbuf[t] + c
            gbuf[t] = g
            return abuf[t] * g

        car[...] = lax.fori_loop(0, tc, step, car[...], unroll=8)
        g = gbuf[...]
        d_a = g * hp_ref[...]
        d_m = g * (ig * xbv)
        d_loga = d_a * a - d_m * (a * a / mult)
        dxb_ref[...] = g * mult * ig
        dra = d_loga * ((-LRU_C) * sp) * r * (1.0 - r)
        dia = g * mult * xbv * ig * (1.0 - ig)
        dra_ref[...] = dra
        dia_ref[...] = dia
        red_ref[0] += jnp.sum(dra, axis=0)
        red_ref[1] += jnp.sum(dia, axis=0)
        red_ref[2] += jnp.sum(d_loga * r, axis=0) * (LRU_C * _sigmoid(-pv_ref[2]))

    shp = jax.ShapeDtypeStruct((s, 8, 128), F32)
    return pl.pallas_call(
        body, name=name, out_shape=(shp, shp, shp, shp, jax.ShapeDtypeStruct((8, 8, 128), F32)),
        grid=(nch,), in_specs=[rev] * 7 + [_vec_spec(8)],
        out_specs=(rev, rev, rev, rev, _vec_spec(8)),
        scratch_shapes=[pltpu.VMEM((tc, 8, 128), F32), pltpu.VMEM((tc, 8, 128), F32),
                        pltpu.VMEM((tc, 8, 128), F32), pltpu.VMEM((8, 128), F32)],
        compiler_params=_cparams(("arbitrary",)),
    )(dy, ra, ia, xb, gate, hs, hp, pv)


def _adamw(src, w, m, v, name):
    nl, n, rr, cc = src.shape
    tr = rr
    for cand in sorted((d for d in range(16, rr + 1, 16) if rr % d == 0), reverse=True):
        if cand * cc <= 192 * 1024:
            tr = cand
            break
    c1 = 1.0 - ADAM_B1 ** ADAM_STEP
    c2 = 1.0 - ADAM_B2 ** ADAM_STEP

    def body(s_ref, w_ref, m_ref, v_ref, g_out, d_out, m_out, v_out):
        g = s_ref[0].astype(F32)
        for k in range(1, n):
            g = g + s_ref[k].astype(F32)
        mn = ADAM_B1 * m_ref[...] + (1.0 - ADAM_B1) * g
        vn = ADAM_B2 * v_ref[...] + (1.0 - ADAM_B2) * (g * g)
        g_out[...] = g
        m_out[...] = mn
        v_out[...] = vn
        d_out[...] = (-ADAM_LR) * ((mn / c1) / (jnp.sqrt(vn / c2) + ADAM_EPS) + ADAM_WD * w_ref[...])

    pspec = pl.BlockSpec((None, tr, cc), lambda l, i: (l, i, 0))
    shp = jax.ShapeDtypeStruct((nl, rr, cc), F32)
    return pl.pallas_call(
        body, name=name, out_shape=(shp, shp, shp, shp), grid=(nl, rr // tr),
        in_specs=[pl.BlockSpec((None, n, tr, cc), lambda l, i: (l, 0, i, 0)), pspec, pspec, pspec],
        out_specs=(pspec, pspec, pspec, pspec),
        compiler_params=_cparams(("parallel", "parallel"), VMEM_BIG),
    )(src, w, m, v)


def _adamw_slice(src, w, m, v, bufs, idx, name):
    n, rr, cc = src.shape
    nl = w.shape[0]
    tr = rr
    for cand in sorted((d for d in range(16, rr + 1, 16) if rr % d == 0), reverse=True):
        if cand * cc <= 192 * 1024:
            tr = cand
            break
    c1 = 1.0 - ADAM_B1 ** ADAM_STEP
    c2 = 1.0 - ADAM_B2 ** ADAM_STEP
    if bufs is None:
        bufs = tuple(lax.empty((nl, rr, cc), F32) for _ in range(4))

    def body(s_ref, w_ref, m_ref, v_ref, b0, b1, b2, b3, g_out, d_out, m_out, v_out):
        g = s_ref[0].astype(F32)
        for k in range(1, n):
            g = g + s_ref[k].astype(F32)
        mn = ADAM_B1 * m_ref[...] + (1.0 - ADAM_B1) * g
        vn = ADAM_B2 * v_ref[...] + (1.0 - ADAM_B2) * (g * g)
        g_out[...] = g
        m_out[...] = mn
        v_out[...] = vn
        d_out[...] = (-ADAM_LR) * ((mn / c1) / (jnp.sqrt(vn / c2) + ADAM_EPS) + ADAM_WD * w_ref[...])

    pspec = pl.BlockSpec((None, tr, cc), lambda i: (idx, i, 0))
    anyspec = pl.BlockSpec(memory_space=pl.ANY)
    shp = jax.ShapeDtypeStruct((nl, rr, cc), F32)
    return pl.pallas_call(
        body, name=name, out_shape=(shp, shp, shp, shp), grid=(rr // tr,),
        in_specs=[pl.BlockSpec((n, tr, cc), lambda i: (0, i, 0)), pspec, pspec, pspec,
                  anyspec, anyspec, anyspec, anyspec],
        out_specs=(pspec, pspec, pspec, pspec),
        input_output_aliases={4: 0, 5: 1, 6: 2, 7: 3},
        compiler_params=_cparams(("parallel",), VMEM_BIG),
    )(src, w, m, v, *bufs)


def _sum_devices(x, name):
    _, rr, cc = x.shape

    def body(x_ref, o_ref):
        acc = x_ref[0]
        for k in range(1, NDEV):
            acc = acc + x_ref[k]
        o_ref[...] = acc

    return pl.pallas_call(
        body, name=name, out_shape=jax.ShapeDtypeStruct((rr, cc), F32), grid=(rr // 8,),
        in_specs=[pl.BlockSpec((NDEV, 8, cc), lambda i: (0, i, 0))],
        out_specs=pl.BlockSpec((8, cc), lambda i: (i, 0)),
        compiler_params=_cparams(("parallel",)),
    )(x)


def _to3(x):
    return x.reshape(x.shape[0], 8, 128)


def _block_diag(w):
    eye = jnp.eye(HEADS, dtype=w.dtype)
    return (w[:, :, None, :] * eye[:, None, :, None]).reshape(D, D)


def _diag_blocks(x):
    return jnp.diagonal(x.reshape(HEADS, HDIM, HEADS, HDIM), axis1=0, axis2=2).transpose(2, 0, 1)


def _col_blocks(dw, n):
    return dw.reshape(dw.shape[0], NDEV, n).transpose(1, 0, 2)


def kernel(x, c, w_cond, b_cond, norm_pre, norm_post, w_ffn_in, w_ffn_out, fox_w_in, fox_b_f, fox_w_out, sconv_w_in, sconv_conv_w, sconv_w_out, lru_w_in, lru_conv_w, lru_conv_b, lru_w_a, lru_b_a, lru_w_x, lru_b_x, lru_lambda, lru_w_out, loss_target, m_w_cond, m_b_cond, m_norm_pre, m_norm_post, m_w_ffn_in, m_w_ffn_out, m_fox_w_in, m_fox_b_f, m_fox_w_out, m_sconv_w_in, m_sconv_conv_w, m_sconv_w_out, m_lru_w_in, m_lru_conv_w, m_lru_conv_b, m_lru_w_a, m_lru_b_a, m_lru_w_x, m_lru_b_x, m_lru_lambda, m_lru_w_out, v_w_cond, v_b_cond, v_norm_pre, v_norm_post, v_w_ffn_in, v_w_ffn_out, v_fox_w_in, v_fox_b_f, v_fox_w_out, v_sconv_w_in, v_sconv_conv_w, v_sconv_w_out, v_lru_w_in, v_lru_conv_w, v_lru_conv_b, v_lru_w_a, v_lru_b_a, v_lru_w_x, v_lru_b_x, v_lru_lambda, v_lru_w_out):
    me = 4 * lax.axis_index("x") + 2 * lax.axis_index("y") + lax.axis_index("c")
    s = x.shape[1]
    x0 = x[0]
    tgt = loss_target[0]
    wcw = w_cond.shape[2]
    dsh = norm_pre.shape[2]

    small_parts = [c.reshape(-1), norm_pre.reshape(-1), norm_post.reshape(-1), sconv_conv_w.reshape(-1),
                   lru_conv_w.reshape(-1), lru_conv_b.reshape(-1), lru_lambda.reshape(-1)]
    sizes = [p.shape[0] for p in small_parts]
    flat = jnp.concatenate(small_parts)
    padn = (-flat.shape[0]) % 1024
    flat = jnp.pad(flat, (0, padn)).reshape(-1, 1024)
    sm = _all_gather(flat, 0, "ag_small").reshape(NDEV, -1)
    offs = [0]
    for n_ in sizes:
        offs.append(offs[-1] + n_)
    piece = lambda i: sm[:, offs[i]:offs[i + 1]]
    c_all = piece(0)
    unshard = lambda p, lead: p.reshape((NDEV,) + lead + (dsh,)).transpose(
        tuple(range(1, len(lead) + 1)) + (0, len(lead) + 1)).reshape(lead + (D,))
    npre = unshard(piece(1), (DEPTH, 3))
    npost = unshard(piece(2), (DEPTH, 3))
    scw = unshard(piece(3), (3,))
    lcw = unshard(piece(4), (4,))
    lcb = unshard(piece(5), ())
    llam = unshard(piece(6), ())

    c_pad = jnp.pad(c_all, ((0, 128 - NDEV), (0, 0)))
    bc_mine = lax.dynamic_slice(b_cond, (0, me * wcw), (DEPTH, wcw)).reshape(DEPTH, 1, wcw)
    modc, ca_pad = _cond_fwd(c_pad, w_cond, bc_mine, "cond_fwd")
    modg = _all_gather(modc[:, :NDEV, :], 0, "ag_mod")
    mod = lax.dynamic_index_in_dim(modg, me, axis=2, keepdims=False)
    mod = mod.transpose(1, 0, 2).reshape(DEPTH, 3, 3, D)
    ptab = jnp.stack([npre, mod[:, :, 1], mod[:, :, 0], npost, mod[:, :, 2],
                      jnp.zeros_like(npre), jnp.zeros_like(npre), jnp.zeros_like(npre)], axis=2)

    need = []
    for l in range(DEPTH):
        need += [(f"wi{l}0", w_ffn_in[l, 0]), (f"wo{l}0", w_ffn_out[l, 0])]
        if l % 3 == 0:
            need += [(f"fwi{l // 3}", fox_w_in[l // 3]), (f"fwo{l // 3}", fox_w_out[l // 3])]
        elif l % 3 == 1:
            need += [("swi", sconv_w_in[0]), ("swo", sconv_w_out[0])]
        else:
            need += [("lwi", lru_w_in[0]), ("lwo", lru_w_out[0])]
        need += [(f"wi{l}1", w_ffn_in[l, 1]), (f"wo{l}1", w_ffn_out[l, 1])]
    ag_keys = [k_ for k_, _ in need]
    lands = _alloc_many([(NDEV,) + w_.shape for _, w_ in need], BF16, "alloc_gather")
    pend = {}
    chain = [jnp.zeros((8, 128), F32)]
    for (key, w_), land in zip(need, lands):
        pend[key], chain[0] = _gather_start((w_ + chain[0][0, 0]).astype(BF16), land, "ags_" + key)
    relayed = [0]
    relay_toks = []

    def ag_relay(upto, after):
        while relayed[0] <= min(upto, len(ag_keys) - 1):
            key = ag_keys[relayed[0]]
            pend[key], tok = _gather_relay(pend[key], after, "agr_" + key)
            relay_toks.append(tok)
            relayed[0] += 1

    def ag_wait(key, after):
        return _gather_wait(pend.pop(key), after, "agw_" + key)

    ag_relay(1, chain[0])
    cols = lambda wg: wg.transpose(1, 0, 2).reshape(D, -1)
    wgate = jnp.concatenate([_block_diag(lru_w_a[0]), _block_diag(lru_w_x[0])], axis=1).astype(BF16)
    bf_pad = jnp.pad(fox_b_f, ((0, 0), (0, 128 - HEADS))).reshape(2, 1, 128)
    scw3 = scw.reshape(3, 8, 128)
    lcwb = jnp.concatenate([lcw, lcb[None], jnp.zeros((3, D), F32)], axis=0).reshape(8, 8, 128)
    lpv = jnp.concatenate([lru_b_a.reshape(1, D), lru_b_x.reshape(1, D), llam[None],
                           jnp.zeros((5, D), F32)], axis=0).reshape(8, 8, 128)

    subs = [(l, sub) for l in range(DEPTH) for sub in range(3)]
    coef = lambda sub: 1.0 if sub == 1 else 0.5

    saved = {}
    xcur = x0
    h = _prenorm(xcur, ptab, 0, 0, "prenorm", toks=(chain[0],))
    for idx, (l, sub) in enumerate(subs):
        tag = f"{l}{sub}"
        sv = {"x": xcur, "h": h}
        if sub != 1:
            f = 0 if sub == 0 else 1
            wi = ag_wait(f"wi{l}{f}", h)
            wo = ag_wait(f"wo{l}{f}", h).reshape(4, FFB, D)
            gu, act = _ffn_up(h, wi, "ffn_up_" + tag)
            y = _ffn_down(act, wo, "ffn_down_" + tag)
            sv.update(gu=gu, wi=wi, wo=wo)
        elif l % 3 == 0:
            j = l // 3
            fwi_j = cols(ag_wait(f"fwi{j}", h))
            w_qkv = fwi_j
            w_f = jnp.pad(fwi_j[:, 3 * D:], ((0, 0), (0, 128 - HEADS)))
            fwo_j = ag_wait(f"fwo{j}", h).reshape(D, D)
            qkv = _fox_qkv(h, w_qkv, "fox_qkv_" + tag)
            fl = _mm_nn(h, w_f, F32, "fox_f_" + tag)
            cum = _fox_cum(fl, bf_pad[j], "fox_cum_" + tag)
            ck = cum[:, :HEADS].T.reshape(HPAIRS, 2, 1, s)
            o2, lse = _fox_attn_fwd(qkv, cum, ck, "fox_attn_" + tag)
            y = _mm_nn(o2, fwo_j, F32, "fox_out_" + tag)
            sv.update(w_qkv=w_qkv, w_f=w_f, fwo=fwo_j, qkv=qkv, fl=fl, cum=cum, ck=ck, lse=lse, o2=o2)
        elif l % 3 == 1:
            swi = cols(ag_wait("swi", h))
            swo = ag_wait("swo", h).reshape(D, D)
            proj = _mm_sections(h, swi, 3, F32, "sconv_in_" + tag)
            bg, cg, xv = (_to3(proj[i]) for i in range(3))
            y3 = _sconv_fwd(bg, cg, xv, scw3, "sconv_mix_" + tag)
            y2 = y3.reshape(s, D).astype(BF16)
            y = _mm_nn(y2, swo, F32, "sconv_out_" + tag)
            sv.update(bg=bg, cg=cg, xv=xv, y2=y2)
        else:
            lwi = cols(ag_wait("lwi", h))
            lwo = ag_wait("lwo", h).reshape(D, D)
            proj = _mm_sections(h, lwi, 2, F32, "lru_in_" + tag)
            gate3, xp3 = _to3(proj[0]), _to3(proj[1])
            xb3 = _lru_conv_fwd(xp3, lcwb, "lru_conv_" + tag)
            xb2 = xb3.reshape(s, D).astype(BF16)
            gpre = _mm_sections(xb2, wgate, 2, F32, "lru_gate_" + tag)
            ra3, ia3 = _to3(gpre[0]), _to3(gpre[1])
            hs3, hp3, y3 = _lru_scan_fwd(ra3, ia3, xb3, gate3, lpv, "lru_scan_" + tag)
            y2 = y3.reshape(s, D).astype(BF16)
            y = _mm_nn(y2, lwo, F32, "lru_out_" + tag)
            sv.update(gate3=gate3, xp3=xp3, xb3=xb3, xb2=xb2, ra3=ra3, ia3=ia3, hs3=hs3, hp3=hp3, y2=y2)
        sv["y"] = y
        saved[(l, sub)] = sv
        nxt = subs[idx + 1] if idx + 1 < len(subs) else None
        ag_relay(2 * idx + 3, y)
        xcur, h = _resid(xcur, y, ptab, l, sub, coef(sub), nxt, "resid_" + tag, toks=tuple(relay_toks))
        del relay_toks[:]

    dx, loss_cols = _loss_grad(xcur, tgt, "loss_grad")

    d_mod = [[[None] * 3 for _ in range(3)] for _ in range(DEPTH)]
    d_npre = [[None] * 3 for _ in range(DEPTH)]
    d_npost = [[None] * 3 for _ in range(DEPTH)]
    d_fbf = [None, None]
    small_g = {}
    rs_pend, rs_order = {}, []
    toks = []
    dh2 = None
    rs_lands = dict(zip(ag_keys, _alloc_many([(NDEV,) + w_.shape for _, w_ in need], BF16, "alloc_scatter")))

    def rs_start(key, blocks):
        hnd = _exchange_start(blocks, "rss_" + key, False, land=rs_lands[key])
        rs_pend[key] = hnd[:4]
        rs_order.append(key)
        toks.append(hnd[4])

    for (l, sub) in reversed(subs):
        tag = f"{l}{sub}"
        sv = saved[(l, sub)]
        dy, red = _post_bwd(dx, sv["y"], ptab, l, sub, coef(sub), "post_bwd_" + tag, toks=tuple(toks))
        del toks[:]
        d_mod[l][sub][2] = red[0]
        d_npost[l][sub] = red[1]
        hb = sv["h"]
        if sub != 1:
            f = 0 if sub == 0 else 1
            da, dwo = _ffn_down_bwd(dy, sv["gu"], sv["wo"], "ffn_down_bwd_" + tag)
            rs_start(f"wo{l}{f}", dwo.reshape(NDEV, FFB // 2, D))
            rs_start(f"wi{l}{f}", _ffn_dwi(hb, da.reshape(NDEV, s, FFB), "ffn_dwi_" + tag))
            dh = _ffn_dh(da, sv["wi"], "ffn_dh_" + tag)
        elif l % 3 == 0:
            j = l // 3
            do2 = _mm_nt(dy, sv["fwo"], BF16, "fox_out_dx_" + tag)
            rs_start(f"fwo{j}", _mm_tn(sv["o2"], dy, BF16, "fox_out_dw_" + tag).reshape(NDEV, dsh, D))
            dqkv, dcq, dck = _fox_attn_bwd(sv["qkv"], do2, sv["lse"], sv["cum"], sv["ck"], "fox_attn_bwd_" + tag)
            dcum = dcq + jnp.pad(dck.reshape(HEADS, s).T, ((0, 0), (0, 128 - HEADS)))
            dfl, dbf = _fox_cum_bwd(dcum, sv["fl"], bf_pad[j], "fox_cum_bwd_" + tag)
            d_fbf[j] = dbf[0, :HEADS]
            dflb = dfl.astype(BF16)
            dh = _fox_qkv_dx(dqkv, sv["w_qkv"], "fox_qkv_dx_" + tag)
            dh2 = _mm_nt(dflb, sv["w_f"], F32, "fox_f_dx_" + tag)
            dw3 = _fox_qkv_dw(hb, dqkv, "fox_qkv_dw_" + tag)
            dwf = _mm_tn(hb, dflb, BF16, "fox_f_dw_" + tag)
            dwfox = jnp.concatenate([dw3[0], dw3[1], dw3[2], dwf[:, :HEADS]], axis=1)
            rs_start(f"fwi{j}", _col_blocks(dwfox, fox_w_in.shape[2]))
        elif l % 3 == 1:
            dy3 = _to3(_mm_nt(dy, swo, F32, "sconv_out_dx_" + tag))
            rs_start("swo", _mm_tn(sv["y2"], dy, BF16, "sconv_out_dw_" + tag).reshape(NDEV, dsh, D))
            dbg, dcg, dxv, dscw = _sconv_bwd(dy3, sv["bg"], sv["cg"], sv["xv"], scw3, "sconv_mix_bwd_" + tag)
            small_g["sconv_conv_w"] = dscw.reshape(3, D)
            dproj = jnp.concatenate([t.reshape(s, D) for t in (dbg, dcg, dxv)], axis=1).astype(BF16)
            dh = _mm_nt(dproj, swi, F32, "sconv_in_dx_" + tag)
            rs_start("swi", _col_blocks(_mm_tn(hb, dproj, BF16, "sconv_in_dw_" + tag), sconv_w_in.shape[2]))
        else:
            dy3 = _to3(_mm_nt(dy, lwo, F32, "lru_out_dx_" + tag))
            rs_start("lwo", _mm_tn(sv["y2"], dy, BF16, "lru_out_dw_" + tag).reshape(NDEV, dsh, D))
            dgate3, dra3, dia3, dxb3, lred = _lru_scan_bwd(
                dy3, sv["ra3"], sv["ia3"], sv["xb3"], sv["gate3"], sv["hs3"], sv["hp3"], lpv, "lru_scan_bwd_" + tag)
            dgp = jnp.concatenate([dra3.reshape(s, D), dia3.reshape(s, D)], axis=1).astype(BF16)
            dxb3 = dxb3 + _to3(_mm_nt(dgp, wgate, F32, "lru_gate_dx_" + tag))
            dwgate = _mm_tn(sv["xb2"], dgp, F32, "lru_gate_dw_" + tag)
            dxp3, cred = _lru_conv_bwd(dxb3, sv["xp3"], lcwb, "lru_conv_bwd_" + tag)
            lred, cred = lred.reshape(8, D), cred.reshape(8, D)
            small_g.update(lru_w_a=_diag_blocks(dwgate[:, :D]), lru_w_x=_diag_blocks(dwgate[:, D:]),
                           lru_b_a=lred[0], lru_b_x=lred[1], lru_lambda=lred[2],
                           lru_conv_w=cred[:4], lru_conv_b=cred[4])
            dproj = jnp.concatenate([dgate3.reshape(s, D), dxp3.reshape(s, D)], axis=1).astype(BF16)
            dh = _mm_nt(dproj, lwi, F32, "lru_in_dx_" + tag)
            rs_start("lwi", _col_blocks(_mm_tn(hb, dproj, BF16, "lru_in_dw_" + tag), lru_w_in.shape[2]))
        dx, red = _pre_bwd(dx, dh, sv["x"], ptab, l, sub, "pre_bwd_" + tag, dh2=dh2, toks=tuple(toks))
        dh2 = None
        del toks[:]
        d_mod[l][sub][0] = red[0]
        d_mod[l][sub][1] = red[1]
        d_npre[l][sub] = red[2]
    grad_x = dx[None]

    dmod_mine = jnp.stack([jnp.stack([jnp.stack(d_mod[l][sub]) for sub in range(3)]) for l in range(DEPTH)])
    gparts = [loss_cols[0], dmod_mine.reshape(-1),
              jnp.stack([jnp.stack(r_) for r_ in d_npre]).reshape(-1),
              jnp.stack([jnp.stack(r_) for r_ in d_npost]).reshape(-1),
              jnp.stack(d_fbf).reshape(-1), small_g["sconv_conv_w"].reshape(-1),
              small_g["lru_conv_w"].reshape(-1), small_g["lru_conv_b"].reshape(-1),
              small_g["lru_w_a"].reshape(-1), small_g["lru_b_a"].reshape(-1),
              small_g["lru_w_x"].reshape(-1), small_g["lru_b_x"].reshape(-1),
              small_g["lru_lambda"].reshape(-1)]
    gsizes = [p.shape[0] for p in gparts]
    gflat = jnp.concatenate(gparts)
    gflat = jnp.pad(gflat, (0, (-gflat.shape[0]) % (8 * 1024))).reshape(-1, 1024)
    small_hnd = _exchange_start(gflat, "ags_smallgrads", True)

    out = {}
    after = [small_hnd[4]]

    def update(keys, w, m, v, name):
        got = {}
        for k_ in sorted(keys, key=rs_order.index):
            got[k_] = after[0] = _exchange_wait(rs_pend.pop(k_), after[0], "rsw_" + k_, False)
        src = jnp.stack([got[k_] for k_ in keys])
        shp = w.shape
        w3, m3, v3 = (t.reshape((src.shape[0],) + src.shape[2:]) for t in (w, m, v))
        res = tuple(t.reshape(shp) for t in _adamw(src, w3, m3, v3, "adamw_" + name))
        after[0] = res[1]
        return res

    lf = [f"{l}{f}" for l in range(DEPTH) for f in range(2)]
    out["fox_w_in"] = update(["fwi0", "fwi1"], fox_w_in, m_fox_w_in, v_fox_w_in, "fwi")
    out["fox_w_out"] = update(["fwo0", "fwo1"], fox_w_out, m_fox_w_out, v_fox_w_out, "fwo")
    out["lru_w_out"] = update(["lwo"], lru_w_out, m_lru_w_out, v_lru_w_out, "lwo")
    out["lru_w_in"] = update(["lwi"], lru_w_in, m_lru_w_in, v_lru_w_in, "lwi")
    out["sconv_w_out"] = update(["swo"], sconv_w_out, m_sconv_w_out, v_sconv_w_out, "swo")
    out["sconv_w_in"] = update(["swi"], sconv_w_in, m_sconv_w_in, v_sconv_w_in, "swi")

    stacked = {"wi": [t.reshape((-1,) + w_ffn_in.shape[2:]) for t in (w_ffn_in, m_w_ffn_in, v_w_ffn_in)],
               "wo": [t.reshape((-1,) + w_ffn_out.shape[2:]) for t in (w_ffn_out, m_w_ffn_out, v_w_ffn_out)]}
    bufs = {"wi": None, "wo": None}

    def update_slice(k_):
        kind = k_[:2]
        src = after[0] = _exchange_wait(rs_pend.pop(k_), after[0], "rsw_" + k_, False)
        bufs[kind] = _adamw_slice(src, *stacked[kind], bufs[kind], lf.index(k_[2:]), "adamw_" + k_)
        after[0] = bufs[kind][1]

    for k_ in rs_order:
        if k_[:2] in ("wi", "wo") and k_ != "wi00":
            update_slice(k_)

    gall = _exchange_wait(small_hnd[:4], after[0], "agw_smallgrads", True)
    gsum = _sum_devices(gall, "sum_smallgrads").reshape(-1)
    goffs = [0]
    for n_ in gsizes:
        goffs.append(goffs[-1] + n_)
    gpiece = lambda i: gsum[goffs[i]:goffs[i + 1]]
    loss = jnp.sum(gpiece(0))
    my_ch = lambda g, lead: lax.dynamic_slice_in_dim(g.reshape(lead + (D,)), me * dsh, dsh, axis=len(lead))

    dmod_all = gall.reshape(NDEV, -1)[:, goffs[1]:goffs[2]].reshape(NDEV, DEPTH, 3 * 3 * D)
    dmod_cols = lax.dynamic_slice_in_dim(dmod_all, me * wcw, wcw, axis=2).transpose(1, 0, 2)
    dmod_cols = jnp.pad(dmod_cols, ((0, 0), (0, 128 - NDEV), (0, 0))).astype(BF16)
    g_wcond = _cond_bwd(ca_pad, dmod_cols, "cond_bwd")

    out["w_cond"] = _adamw(g_wcond[:, None], w_cond, m_w_cond, v_w_cond, "adamw_wcond")

    small = [
        ("b_cond", b_cond, m_b_cond, v_b_cond, gpiece(1)),
        ("norm_pre", norm_pre, m_norm_pre, v_norm_pre, my_ch(gpiece(2), (DEPTH, 3))),
        ("norm_post", norm_post, m_norm_post, v_norm_post, my_ch(gpiece(3), (DEPTH, 3))),
        ("fox_b_f", fox_b_f, m_fox_b_f, v_fox_b_f, gpiece(4)),
        ("sconv_conv_w", sconv_conv_w, m_sconv_conv_w, v_sconv_conv_w, my_ch(gpiece(5), (1, 3))),
        ("lru_conv_w", lru_conv_w, m_lru_conv_w, v_lru_conv_w, my_ch(gpiece(6), (1, 4))),
        ("lru_conv_b", lru_conv_b, m_lru_conv_b, v_lru_conv_b, my_ch(gpiece(7), (1,))),
        ("lru_w_a", lru_w_a, m_lru_w_a, v_lru_w_a, gpiece(8)),
        ("lru_b_a", lru_b_a, m_lru_b_a, v_lru_b_a, gpiece(9)),
        ("lru_w_x", lru_w_x, m_lru_w_x, v_lru_w_x, gpiece(10)),
        ("lru_b_x", lru_b_x, m_lru_b_x, v_lru_b_x, gpiece(11)),
        ("lru_lambda", lru_lambda, m_lru_lambda, v_lru_lambda, my_ch(gpiece(12), (1,))),
    ]
    pack = lambda ts: jnp.concatenate([t.reshape(-1) for t in ts])
    ssz = [w_.size for _, w_, _, _, _ in small]
    tot = sum(ssz)
    padr = (-tot) % (16 * 1024)
    pk = lambda ts: jnp.pad(pack(ts), (0, padr)).reshape(1, -1, 1024)
    sg, sd, smm, svv = _adamw(pk([t[4] for t in small])[:, None], pk([t[1] for t in small]),
                              pk([t[2] for t in small]), pk([t[3] for t in small]), "adamw_small")
    soff = 0
    for (name, w_, _, _, _), n_ in zip(small, ssz):
        out[name] = tuple(t.reshape(-1)[soff:soff + n_].reshape(w_.shape) for t in (sg, sd, smm, svv))
        soff += n_

    after[0] = sd
    update_slice("wi00")
    out["w_ffn_in"] = tuple(t.reshape(w_ffn_in.shape) for t in bufs["wi"])
    out["w_ffn_out"] = tuple(t.reshape(w_ffn_out.shape) for t in bufs["wo"])

    names =["w_cond", "b_cond", "norm_pre", "norm_post", "w_ffn_in", "w_ffn_out", "fox_w_in", "fox_b_f",
             "fox_w_out", "sconv_w_in", "sconv_conv_w", "sconv_w_out", "lru_w_in", "lru_conv_w", "lru_conv_b",
             "lru_w_a", "lru_b_a", "lru_w_x", "lru_b_x", "lru_lambda", "lru_w_out"]
    return (loss, grad_x, *[out[n_][0] for n_ in names], *[out[n_][1] for n_ in names],
            *[out[n_][2] for n_ in names], *[out[n_][3] for n_ in names])
```
